```python
import math
import jax, jax.numpy as jnp
from jax import lax
import numpy as np

D_MODEL = 1024
BATCH = 8
SEQ = 2048
DEPTH = 1
DEC_BATCH = 128
DEC_SEQ = 4
PAST_LEN = 2048
PAGE_SIZE = 128

D_MIX = D_MODEL
HEAD_DIM = 128
GDN_HEADS = D_MIX // (2 * HEAD_DIM)
FOX_HEADS = D_MIX // (2 * HEAD_DIM)
GDN_DK = HEAD_DIM
GDN_DV = HEAD_DIM
GDN_KEY_W = GDN_HEADS * GDN_DK
GDN_WIDTH = GDN_HEADS * GDN_DV
FOX_WIDTH = FOX_HEADS * HEAD_DIM
GDN_CONV_DIM = 2 * GDN_KEY_W + GDN_WIDTH
CONV_K = 4
GDN_CHUNK = 64
Q_BLOCK = 128
NORM_EPS = 1e-6
L2_EPS = 1e-6
FGATE_BIAS_MEAN = 3.0
SPLIT_SIZES = (GDN_CONV_DIM, GDN_WIDTH, GDN_HEADS, GDN_HEADS, FOX_WIDTH, FOX_WIDTH, FOX_WIDTH, FOX_WIDTH, FOX_HEADS)
D_IN = sum(SPLIT_SIZES)

kernel_name = 'hymba_gdn_fox_decode_step'


def rms_norm(x, w):
    xf = x.astype(jnp.float32)
    y = xf * lax.rsqrt(jnp.mean(xf * xf, axis=-1, keepdims=True) + NORM_EPS)
    return (y * w.astype(jnp.float32)).astype(x.dtype)


def l2_normalize(x):
    return x * lax.rsqrt(jnp.sum(x * x, axis=-1, keepdims=True) + L2_EPS)


def split_proj(p):
    idx = [int(i) for i in np.cumsum(SPLIT_SIZES)[:-1]]
    return jnp.split(p, idx, axis=-1)


def short_conv(x, buf, w):
    L = x.shape[1]
    xx = jnp.concatenate([buf.astype(x.dtype), x], axis=1)
    y = sum(xx[:, j:j + L] * w[j] for j in range(CONV_K))
    return jax.nn.silu(y), xx[:, -(CONV_K - 1):]


def gdn_chunked(q, k, v, g, beta, s0):
    B, L, H, DK = q.shape
    DV = v.shape[-1]
    C = min(GDN_CHUNK, L)
    n = -(-L // C)
    pad = n * C - L
    if pad:
        pad4 = ((0, 0), (0, pad), (0, 0), (0, 0))
        q, k, v = jnp.pad(q, pad4), jnp.pad(k, pad4), jnp.pad(v, pad4)
        g, beta = jnp.pad(g, pad4[:3]), jnp.pad(beta, pad4[:3])
    to_c = lambda t: t.reshape(B, n, C, H, t.shape[-1]).transpose(1, 0, 3, 2, 4)
    q, k, v = to_c(q), to_c(k), to_c(v)
    g = g.reshape(B, n, C, H).transpose(1, 0, 3, 2)
    beta = beta.reshape(B, n, C, H).transpose(1, 0, 3, 2)
    gc = jnp.cumsum(g, axis=-1)
    incl = jnp.tril(jnp.ones((C, C), dtype=bool))
    strict = jnp.tril(jnp.ones((C, C), dtype=bool), -1)
    diff = gc[..., :, None] - gc[..., None, :]
    decay = jnp.where(incl, jnp.exp(jnp.where(incl, diff, 0.0)), 0.0)
    kb = k * beta[..., None]
    a = jnp.where(strict, jnp.einsum('nbhid,nbhjd->nbhij', kb, k) * decay, 0.0)
    m = a + jnp.eye(C, dtype=a.dtype)
    rhs = jnp.concatenate([v * beta[..., None], kb * jnp.exp(gc)[..., None]], axis=-1)
    sol = lax.linalg.triangular_solve(m, rhs, left_side=True, lower=True, unit_diagonal=True)
    u, w = sol[..., :DV], sol[..., DV:]
    qk = jnp.einsum('nbhid,nbhjd->nbhij', q, k) * decay
    q_dec = q * jnp.exp(gc)[..., None]
    k_dec = k * jnp.exp(gc[..., -1:] - gc)[..., None]
    g_last = jnp.exp(gc[..., -1])

    def step(s, xs):
        qk_i, qd_i, kd_i, u_i, w_i, gl_i = xs
        v_new = u_i - jnp.einsum('bhck,bhkv->bhcv', w_i, s)
        o_i = jnp.einsum('bhck,bhkv->bhcv', qd_i, s) + jnp.einsum('bhij,bhjv->bhiv', qk_i, v_new)
        s = s * gl_i[..., None, None] + jnp.einsum('bhck,bhcv->bhkv', kd_i, v_new)
        return s, o_i

    s_fin, o = lax.scan(step, s0, (qk, q_dec, k_dec, u, w, g_last))
    o = o.transpose(1, 0, 3, 2, 4).reshape(B, n * C, H, DV)[:, :L]
    return o, s_fin


def fox_attention(q, k, v, fq, fk, q_pos, k_pos):
    B, Lq, H, D = q.shape
    qb = min(Q_BLOCK, Lq)
    nb = -(-Lq // qb)
    pad = nb * qb - Lq
    if pad:
        q = jnp.pad(q, ((0, 0), (0, pad), (0, 0), (0, 0)))
        fq = jnp.pad(fq, ((0, 0), (0, pad), (0, 0)))
        q_pos = jnp.pad(q_pos, (0, pad), mode='edge')
    qs = q.reshape(B, nb, qb, H, D).swapaxes(0, 1)
    fqs = fq.reshape(B, nb, qb, H).transpose(1, 0, 3, 2)
    ps = q_pos.reshape(nb, qb)
    fk_t = fk.transpose(0, 2, 1)
    scale = D ** -0.5

    def block(xs):
        q_i, fq_i, p_i = xs
        s = jnp.einsum('bqhd,bkhd->bhqk', q_i, k, preferred_element_type=jnp.float32) * scale
        s = s + (fq_i[..., :, None] - fk_t[..., None, :])
        s = jnp.where(k_pos[None, None, None, :] <= p_i[None, None, :, None], s, -jnp.inf)
        p = jax.nn.softmax(s, axis=-1)
        return jnp.einsum('bhqk,bkhd->bqhd', p.astype(v.dtype), v)

    o = lax.map(block, (qs, fqs, ps))
    return o.swapaxes(0, 1).reshape(B, nb * qb, H, D)[:, :Lq]


def mixer_layer(x, conv_buf, ssm0, past, w_in, conv_w, a_log, dt_bias, onorm_w, f_bias, w_out, norm_w):
    B, L, _ = x.shape
    f32 = jnp.float32
    h = rms_norm(x, norm_w)
    proj = h @ w_in
    qkv_g, z_g, b_g, a_g, q_f, k_f, v_f, z_f, f_f = split_proj(proj)

    qkv_c, new_conv = short_conv(qkv_g, conv_buf, conv_w)
    q_g, k_g, v_g = jnp.split(qkv_c, [GDN_KEY_W, 2 * GDN_KEY_W], axis=-1)
    q_g = l2_normalize(q_g.reshape(B, L, GDN_HEADS, GDN_DK).astype(f32)) * (GDN_DK ** -0.5)
    k_g = l2_normalize(k_g.reshape(B, L, GDN_HEADS, GDN_DK).astype(f32))
    v_g = v_g.reshape(B, L, GDN_HEADS, GDN_DV).astype(f32)
    beta = jax.nn.sigmoid(b_g.astype(f32))
    g = -jnp.exp(a_log.astype(f32)) * jax.nn.softplus(a_g.astype(f32) + dt_bias.astype(f32))
    o_g, ssm_new = gdn_chunked(q_g, k_g, v_g, g, beta, ssm0.astype(f32))
    o_g = rms_norm(o_g, onorm_w).reshape(B, L, GDN_WIDTH) * jax.nn.silu(z_g.astype(f32))

    q_f = q_f.reshape(B, L, FOX_HEADS, HEAD_DIM)
    k_f = k_f.reshape(B, L, FOX_HEADS, HEAD_DIM)
    v_f = v_f.reshape(B, L, FOX_HEADS, HEAD_DIM)
    logf = jax.nn.log_sigmoid((f_f + f_bias).astype(f32))
    if past is None:
        fq = jnp.cumsum(logf, axis=1)
        fk, keys, vals = fq, k_f, v_f
        q_pos = jnp.arange(L)
        k_pos = q_pos
    else:
        pk, pv, plogf = past
        P = pk.shape[1]
        f_past = jnp.cumsum(plogf.astype(f32), axis=1)
        fq = f_past[:, -1:] + jnp.cumsum(logf, axis=1)
        fk = jnp.concatenate([f_past, fq], axis=1)
        keys = jnp.concatenate([pk.astype(k_f.dtype), k_f], axis=1)
        vals = jnp.concatenate([pv.astype(v_f.dtype), v_f], axis=1)
        q_pos = P + jnp.arange(L)
        k_pos = jnp.arange(P + L)
    o_f = fox_attention(q_f, keys, vals, fq, fk, q_pos, k_pos)
    o_f = o_f.reshape(B, L, FOX_WIDTH).astype(f32) * jax.nn.silu(z_f.astype(f32))

    o = jnp.concatenate([o_g, o_f], axis=-1).astype(x.dtype) @ w_out
    y = x + o
    return y, new_conv, ssm_new.astype(x.dtype), k_f, v_f, logf.astype(x.dtype)


def setup_inputs(seed: int = 0) -> dict:
    key = jax.random.key(seed)
    ks = jax.random.split(key, 20)
    nrm = jax.random.normal
    n_pages = PAST_LEN // PAGE_SIZE
    n_used = DEC_BATCH * n_pages
    n_pool = (5 * n_used + 3) // 4
    x_prompt = nrm(ks[0], (BATCH, SEQ, D_MODEL), jnp.float32)
    x_sample = nrm(ks[1], (DEC_BATCH, DEC_SEQ, D_MODEL), jnp.float32)
    cache_fox_k = nrm(ks[2], (DEPTH, n_pool, PAGE_SIZE, FOX_HEADS, HEAD_DIM), jnp.float32)
    cache_fox_v = nrm(ks[3], (DEPTH, n_pool, PAGE_SIZE, FOX_HEADS, HEAD_DIM), jnp.float32)
    cache_fox_logf = jax.nn.log_sigmoid(FGATE_BIAS_MEAN + nrm(ks[4], (DEPTH, n_pool, PAGE_SIZE, FOX_HEADS), jnp.float32))
    page_table = jax.random.permutation(ks[5], n_pool)[:n_used].reshape(DEC_BATCH, n_pages).astype(jnp.int32)
    state_gdn_ssm = 0.1 * nrm(ks[6], (DEPTH, DEC_BATCH, GDN_HEADS, GDN_DK, GDN_DV), jnp.float32)
    state_gdn_conv = nrm(ks[7], (DEPTH, DEC_BATCH, CONV_K - 1, GDN_CONV_DIM), jnp.float32)
    w_in = nrm(ks[8], (DEPTH, D_MODEL, D_IN), jnp.float32) * D_MODEL ** -0.5
    gdn_conv_w = nrm(ks[9], (DEPTH, CONV_K, GDN_CONV_DIM), jnp.float32) * CONV_K ** -0.5
    gdn_a_log = jnp.log(jax.random.uniform(ks[10], (DEPTH, GDN_HEADS), jnp.float32, 1.0, 16.0))
    dt = jnp.exp(jax.random.uniform(ks[11], (DEPTH, GDN_HEADS), jnp.float32, math.log(1e-3), math.log(1e-1)))
    gdn_dt_bias = dt + jnp.log(-jnp.expm1(-dt))
    gdn_out_norm_w = 1.0 + 0.02 * nrm(ks[12], (DEPTH, GDN_DV), jnp.float32)
    fox_f_bias = FGATE_BIAS_MEAN + 0.5 * nrm(ks[13], (DEPTH, FOX_HEADS), jnp.float32)
    w_out = nrm(ks[14], (DEPTH, D_MIX, D_MODEL), jnp.float32) * D_MIX ** -0.5
    norm_w = 1.0 + 0.02 * nrm(ks[15], (DEPTH, D_MODEL), jnp.float32)
    final_norm_w = 1.0 + 0.02 * nrm(ks[16], (D_MODEL,), jnp.float32)
    return {'x_prompt': x_prompt, 'x_sample': x_sample,
            'cache_fox_k': cache_fox_k, 'cache_fox_v': cache_fox_v, 'cache_fox_logf': cache_fox_logf,
            'page_table': page_table, 'state_gdn_ssm': state_gdn_ssm, 'state_gdn_conv': state_gdn_conv,
            'w_in': w_in, 'gdn_conv_w': gdn_conv_w, 'gdn_a_log': gdn_a_log, 'gdn_dt_bias': gdn_dt_bias,
            'gdn_out_norm_w': gdn_out_norm_w, 'fox_f_bias': fox_f_bias, 'w_out': w_out,
            'norm_w': norm_w, 'final_norm_w': final_norm_w}


def reference(x_prompt, x_sample, cache_fox_k, cache_fox_v, cache_fox_logf, page_table,
              state_gdn_ssm, state_gdn_conv, w_in, gdn_conv_w, gdn_a_log, gdn_dt_bias,
              gdn_out_norm_w, fox_f_bias, w_out, norm_w, final_norm_w):
    hp, hs = x_prompt, x_sample
    bp = x_prompt.shape[0]
    bs, n_pages = page_table.shape
    pro, sam = [], []
    for l in range(DEPTH):
        lw = (w_in[l], gdn_conv_w[l], gdn_a_log[l], gdn_dt_bias[l], gdn_out_norm_w[l],
              fox_f_bias[l], w_out[l], norm_w[l])
        conv0 = jnp.zeros((bp, CONV_K - 1, GDN_CONV_DIM), hp.dtype)
        ssm0 = jnp.zeros((bp, GDN_HEADS, GDN_DK, GDN_DV), jnp.float32)
        out_p = mixer_layer(hp, conv0, ssm0, None, *lw)
        hp = out_p[0]
        pro.append(out_p[1:])
        past = tuple(c[l][page_table].reshape(bs, n_pages * PAGE_SIZE, *c.shape[3:])
                     for c in (cache_fox_k, cache_fox_v, cache_fox_logf))
        out_s = mixer_layer(hs, state_gdn_conv[l], state_gdn_ssm[l], past, *lw)
        hs = out_s[0]
        sam.append(out_s[1:])
    y_prompt = rms_norm(hp, final_norm_w)
    y_sample = rms_norm(hs, final_norm_w)
    conv_prompt = jnp.stack([r[0] for r in pro])
    ssm_prompt = jnp.stack([r[1] for r in pro])
    k_prompt = jnp.stack([r[2] for r in pro])
    v_prompt = jnp.stack([r[3] for r in pro])
    logf_prompt = jnp.stack([r[4] for r in pro])
    conv_sample = jnp.stack([r[0] for r in sam])
    ssm_sample = jnp.stack([r[1] for r in sam])
    k_sample = jnp.stack([r[2] for r in sam])
    v_sample = jnp.stack([r[3] for r in sam])
    logf_sample = jnp.stack([r[4] for r in sam])
    return (y_prompt, y_sample, k_prompt, v_prompt, logf_prompt, ssm_prompt, conv_prompt,
            k_sample, v_sample, logf_sample, ssm_sample, conv_sample)
```

```python
import functools
import math

import jax
import jax.numpy as jnp
from jax import lax
from jax.experimental import pallas as pl
from jax.experimental.pallas import tpu as pltpu

F32 = jnp.float32
BF16 = jnp.bfloat16

NORM_EPS = 1e-6
L2_EPS = 1e-6
HEAD_DIM = 128
N_HEADS = 4
GROUP_W = N_HEADS * HEAD_DIM
CONV_DIM = 3 * GROUP_W
CONV_K = 4
LANES = 128
SUBLANES = 8
GDN_CHUNK = 64
SM_BETA = 0
SM_DECAY = 4
SM_FORGET = 8
VMEM_LIMIT = 56 * 1024 * 1024


def _sigmoid(x):
    return 1.0 / (1.0 + jnp.exp(-x))


def _softplus(x):
    return jnp.maximum(x, 0.0) + jnp.log(1.0 + jnp.exp(-jnp.abs(x)))


def _bdot(a, b):
    return jnp.dot(a.astype(BF16), b.astype(BF16), preferred_element_type=F32)


def _bdot_nt(a, b):
    return lax.dot_general(a.astype(BF16), b.astype(BF16), (((1,), (1,)), ((), ())),
                           preferred_element_type=F32)


def _bdot_tn(a, b):
    return lax.dot_general(a.astype(BF16), b.astype(BF16), (((0,), (0,)), ((), ())),
                           preferred_element_type=F32)


def _fdot(a, b):
    return jnp.dot(a, b, preferred_element_type=F32, precision=lax.Precision.HIGHEST)


def _iota2(shape, dim):
    return lax.broadcasted_iota(jnp.int32, shape, dim)


_PROJ_SEGS = ((0, CONV_DIM), (CONV_DIM, 2048), (2048, 2560), (2560, 3072), (3072, 3584),
              (3584, 4096), (4096, 4224))


def _proj_kernel(x_ref, nw_ref, w_ref, *out_refs):
    x = x_ref[...]
    var = jnp.mean(x * x, axis=-1, keepdims=True)
    h = (x * lax.rsqrt(var + NORM_EPS) * nw_ref[...]).astype(BF16)
    for ref, (lo, hi) in zip(out_refs, _PROJ_SEGS):
        ref[...] = jnp.dot(h, w_ref[:, lo:hi], preferred_element_type=F32).astype(ref.dtype)


def _proj_call(x2d, norm_w, w_big, tm):
    t, d = x2d.shape
    n = w_big.shape[1]
    widths = [hi - lo for lo, hi in _PROJ_SEGS]
    out_shape = [jax.ShapeDtypeStruct((t, w), F32) for w in widths]
    out_specs = [pl.BlockSpec((tm, w), lambda i: (i, 0)) for w in widths]
    return pl.pallas_call(
        _proj_kernel,
        grid=(t // tm,),
        in_specs=[pl.BlockSpec((tm, d), lambda i: (i, 0)),
                  pl.BlockSpec((1, d), lambda i: (0, 0)),
                  pl.BlockSpec((d, n), lambda i: (0, 0))],
        out_specs=out_specs,
        out_shape=out_shape,
        compiler_params=pltpu.CompilerParams(dimension_semantics=("arbitrary",),
                                             vmem_limit_bytes=VMEM_LIMIT),
        name="proj",
    )(x2d, norm_w, w_big)


def _unit_lower_inverse(a_strict, c):
    eye = (_iota2((c, c), 0) == _iota2((c, c), 1)).astype(F32)
    b = -a_strict
    q = eye + b
    p = b
    n_steps = int(math.log2(c)) - 1
    for _ in range(n_steps):
        p = _bdot(p, p)
        q = q + _bdot(q, p)
    return q


def _gdn_kernel(qkv_ref, zg_ref, sm_ref, cw_ref, alog_ref, dtb_ref, onw_ref, s0_ref, c0_ref,
                og_ref, sout_ref, xbuf, s_scr, *, c, l_valid):
    ci = pl.program_id(1)
    n_c = pl.num_programs(1)

    @pl.when(ci == 0)
    def _():
        xbuf[0:SUBLANES, :] = c0_ref[...]
        s_scr[...] = s0_ref[...]

    x = qkv_ref[...]
    xbuf[SUBLANES:SUBLANES + c, :] = x
    y = jnp.zeros((c, CONV_DIM), F32)
    for j in range(CONV_K):
        y = y + xbuf[pl.ds(SUBLANES - (CONV_K - 1) + j, c), :] * cw_ref[j:j + 1, :]
    y = y * _sigmoid(y)
    xbuf[0:SUBLANES, :] = xbuf[c:c + SUBLANES, :]

    row = _iota2((c, 1), 0) + ci * c
    valid = (row < l_valid).astype(F32)

    sm = sm_ref[...]
    beta_t = _sigmoid(sm) * valid
    g_t = -jnp.exp(alog_ref[...]) * _softplus(sm + dtb_ref[...]) * valid
    tri_incl = (_iota2((c, c), 0) >= _iota2((c, c), 1))
    tri_strict = (_iota2((c, c), 0) > _iota2((c, c), 1))
    gc_t = _fdot(tri_incl.astype(F32), g_t)
    pad_rows = LANES - c
    gc_sq = jnp.concatenate([gc_t, jnp.zeros((pad_rows, LANES), F32)], axis=0) if pad_rows else gc_t
    gc_tr = gc_sq.T

    for h in range(N_HEADS):
        lo = h * HEAD_DIM
        q = y[:, lo:lo + HEAD_DIM]
        k = y[:, GROUP_W + lo:GROUP_W + lo + HEAD_DIM]
        v = y[:, 2 * GROUP_W + lo:2 * GROUP_W + lo + HEAD_DIM]
        q = q * lax.rsqrt(jnp.sum(q * q, axis=-1, keepdims=True) + L2_EPS) * (HEAD_DIM ** -0.5)
        k = k * lax.rsqrt(jnp.sum(k * k, axis=-1, keepdims=True) + L2_EPS) * valid
        beta = beta_t[:, SM_BETA + h:SM_BETA + h + 1]
        gc = gc_t[:, SM_DECAY + h:SM_DECAY + h + 1]
        gc_row = gc_tr[SM_DECAY + h:SM_DECAY + h + 1, 0:c]
        gc_last = gc_t[c - 1:c, SM_DECAY + h:SM_DECAY + h + 1]
        diff = gc - gc_row
        decay = jnp.where(tri_incl, jnp.exp(jnp.where(tri_incl, diff, 0.0)), 0.0)
        kb = k * beta
        kkqk = _bdot_nt(jnp.concatenate([kb, q], axis=0), k)
        a = jnp.where(tri_strict, kkqk[0:c] * decay, 0.0)
        qk = kkqk[c:2 * c] * decay
        t_inv = _unit_lower_inverse(a, c)
        egc = jnp.exp(gc)
        rhs = jnp.concatenate([v * beta, kb * egc], axis=-1)
        sol = _bdot(t_inv, rhs)
        u = sol[:, 0:HEAD_DIM]
        w = sol[:, HEAD_DIM:2 * HEAD_DIM]
        q_dec = q * egc
        k_dec = k * jnp.exp(gc_last - gc)
        s = s_scr[h]
        ws = _bdot(jnp.concatenate([w, q_dec], axis=0), s)
        v_new = u - ws[0:c]
        o = ws[c:2 * c] + _bdot(qk, v_new)
        s_scr[h] = s * jnp.exp(gc_last) + _bdot_tn(k_dec, v_new)
        o = o * lax.rsqrt(jnp.mean(o * o, axis=-1, keepdims=True) + NORM_EPS) * onw_ref[...]
        z = zg_ref[:, lo:lo + HEAD_DIM]
        og_ref[:, lo:lo + HEAD_DIM] = (o * (z * _sigmoid(z))).astype(og_ref.dtype)

    @pl.when(ci == n_c - 1)
    def _():
        sout_ref[...] = s_scr[...]


def _gdn_call(qkv, zg, sm, conv_w, alog_row, dtb_row, onw, s0, c0, *, c, l_valid):
    b, l, _ = qkv.shape
    n_c = l // c
    kern = functools.partial(_gdn_kernel, c=c, l_valid=l_valid)
    blk = lambda w: pl.BlockSpec((None, c, w), lambda bi, ci: (bi, ci, 0))
    full = lambda shape: pl.BlockSpec(shape, lambda bi, ci: (0,) * len(shape))
    return pl.pallas_call(
        kern,
        grid=(b, n_c),
        in_specs=[blk(CONV_DIM), blk(GROUP_W), blk(LANES),
                  full((CONV_K, CONV_DIM)), full((1, LANES)), full((1, LANES)), full((1, HEAD_DIM)),
                  pl.BlockSpec((None, N_HEADS, HEAD_DIM, HEAD_DIM), lambda bi, ci: (bi, 0, 0, 0)),
                  pl.BlockSpec((None, SUBLANES, CONV_DIM), lambda bi, ci: (bi, 0, 0))],
        out_specs=[blk(GROUP_W),
                   pl.BlockSpec((None, N_HEADS, HEAD_DIM, HEAD_DIM), lambda bi, ci: (bi, 0, 0, 0))],
        out_shape=[jax.ShapeDtypeStruct((b, l, GROUP_W), BF16),
                   jax.ShapeDtypeStruct((b, N_HEADS, HEAD_DIM, HEAD_DIM), F32)],
        scratch_shapes=[pltpu.VMEM((c + SUBLANES, CONV_DIM), F32),
                        pltpu.VMEM((N_HEADS, HEAD_DIM, HEAD_DIM), F32)],
        compiler_params=pltpu.CompilerParams(dimension_semantics=("arbitrary", "arbitrary"),
                                             vmem_limit_bytes=VMEM_LIMIT),
        name="gdn",
    )(qkv, zg, sm, conv_w, alog_row, dtb_row, onw, s0, c0)


def _log_sigmoid(x):
    return -_softplus(-x)


def _fox_gates_kernel(sm_ref, fb_ref, logf_ref, fcol_ref, frow_ref, *, l, tk):
    blk = LANES
    tri = (_iota2((blk, blk), 0) >= _iota2((blk, blk), 1)).astype(F32)
    carry = jnp.zeros((1, LANES), F32)
    per = tk // blk
    for i in range(l // blk):
        lf = _log_sigmoid(sm_ref[i * blk:(i + 1) * blk, :] + fb_ref[...])
        f = _fdot(tri, lf) + carry
        carry = f[blk - 1:blk, :]
        logf_ref[i * blk:(i + 1) * blk, :] = lf
        fcol_ref[i * blk:(i + 1) * blk, :] = f
        ft = f.T
        frow_ref[i // per, :, (i % per) * blk:(i % per + 1) * blk] = ft[SM_FORGET:SM_FORGET + SUBLANES, :]


def _fox_gates_call(sm, fb_row, tk):
    b, l, _ = sm.shape
    kern = functools.partial(_fox_gates_kernel, l=l, tk=tk)
    return pl.pallas_call(
        kern,
        grid=(b,),
        in_specs=[pl.BlockSpec((None, l, LANES), lambda i: (i, 0, 0)),
                  pl.BlockSpec((1, LANES), lambda i: (0, 0))],
        out_specs=[pl.BlockSpec((None, l, LANES), lambda i: (i, 0, 0)),
                   pl.BlockSpec((None, l, LANES), lambda i: (i, 0, 0)),
                   pl.BlockSpec((None, l // tk, SUBLANES, tk), lambda i: (i, 0, 0, 0))],
        out_shape=[jax.ShapeDtypeStruct((b, l, LANES), F32),
                   jax.ShapeDtypeStruct((b, l, LANES), F32),
                   jax.ShapeDtypeStruct((b, l // tk, SUBLANES, tk), F32)],
        compiler_params=pltpu.CompilerParams(dimension_semantics=("arbitrary",),
                                             vmem_limit_bytes=VMEM_LIMIT),
        name="fox_gates",
    )(sm, fb_row)


NEG_BIG = -1e30


def _fox_prompt_kernel(q_ref, k_ref, v_ref, fcol_ref, frow_ref, zf_ref, o_ref, *, tq):
    qi = pl.program_id(1)
    scale = HEAD_DIM ** -0.5
    causal = _iota2((tq, tq), 0) >= _iota2((tq, tq), 1)
    for h in range(N_HEADS):
        lo = h * HEAD_DIM
        q = q_ref[:, lo:lo + HEAD_DIM].astype(BF16)
        fq = fcol_ref[:, SM_FORGET + h:SM_FORGET + h + 1]

        def scores(ki):
            start = pl.multiple_of(ki * tq, tq)
            kb = k_ref[pl.ds(start, tq), lo:lo + HEAD_DIM].astype(BF16)
            s = lax.dot_general(q, kb, (((1,), (1,)), ((), ())), preferred_element_type=F32)
            return s * scale + (fq - frow_ref[ki, h:h + 1, :])

        def update(ki, s, carry):
            m, l, acc = carry
            start = pl.multiple_of(ki * tq, tq)
            vb = v_ref[pl.ds(start, tq), lo:lo + HEAD_DIM].astype(BF16)
            m_new = jnp.maximum(m, jnp.max(s, axis=-1, keepdims=True))
            alpha = jnp.exp(m - m_new)
            p = jnp.exp(s - m_new)
            l = l * alpha + jnp.sum(p, axis=-1, keepdims=True)
            acc = acc * alpha + jnp.dot(p.astype(BF16), vb, preferred_element_type=F32)
            return m_new, l, acc

        def body(ki, carry):
            return update(ki, scores(ki), carry)

        init = (jnp.full((tq, 1), NEG_BIG, F32), jnp.zeros((tq, 1), F32), jnp.zeros((tq, HEAD_DIM), F32))
        carry = lax.fori_loop(0, qi, body, init)
        s_diag = jnp.where(causal, scores(qi), NEG_BIG)
        m, l, acc = update(qi, s_diag, carry)
        z = zf_ref[:, lo:lo + HEAD_DIM]
        o_ref[:, lo:lo + HEAD_DIM] = ((acc / l) * (z * _sigmoid(z))).astype(o_ref.dtype)


def _fox_prompt_call(qf, kf, vf, fcol, frow, zf, tq):
    b, l, _ = qf.shape
    kern = functools.partial(_fox_prompt_kernel, tq=tq)
    qblk = lambda w: pl.BlockSpec((None, tq, w), lambda bi, qi: (bi, qi, 0))
    seq = pl.BlockSpec((None, l, GROUP_W), lambda bi, qi: (bi, 0, 0))
    return pl.pallas_call(
        kern,
        grid=(b, l // tq),
        in_specs=[qblk(GROUP_W), seq, seq, qblk(LANES),
                  pl.BlockSpec((None, l // tq, SUBLANES, tq), lambda bi, qi: (bi, 0, 0, 0)),
                  qblk(GROUP_W)],
        out_specs=qblk(GROUP_W),
        out_shape=jax.ShapeDtypeStruct((b, l, GROUP_W), BF16),
        compiler_params=pltpu.CompilerParams(dimension_semantics=("arbitrary", "arbitrary"),
                                             vmem_limit_bytes=VMEM_LIMIT),
        name="fox_prompt",
    )(qf, kf, vf, fcol, frow, zf)


def _out_kernel(og_ref, of_ref, x_ref, w_ref, fnw_ref, y_ref):
    o = jnp.dot(og_ref[...], w_ref[0:GROUP_W, :], preferred_element_type=F32)
    o = o + jnp.dot(of_ref[...], w_ref[GROUP_W:2 * GROUP_W, :], preferred_element_type=F32)
    y = x_ref[...] + o
    var = jnp.mean(y * y, axis=-1, keepdims=True)
    y_ref[...] = y * lax.rsqrt(var + NORM_EPS) * fnw_ref[...]


def _out_call(og, of, x2d, w_out, fnw, tm):
    t, d = x2d.shape
    return pl.pallas_call(
        _out_kernel,
        grid=(t // tm,),
        in_specs=[pl.BlockSpec((tm, GROUP_W), lambda i: (i, 0)),
                  pl.BlockSpec((tm, GROUP_W), lambda i: (i, 0)),
                  pl.BlockSpec((tm, d), lambda i: (i, 0)),
                  pl.BlockSpec((2 * GROUP_W, d), lambda i: (0, 0)),
                  pl.BlockSpec((1, d), lambda i: (0, 0))],
        out_specs=pl.BlockSpec((tm, d), lambda i: (i, 0)),
        out_shape=jax.ShapeDtypeStruct((t, d), F32),
        compiler_params=pltpu.CompilerParams(dimension_semantics=("arbitrary",),
                                             vmem_limit_bytes=VMEM_LIMIT),
        name="out_proj",
    )(og, of, x2d, w_out, fnw)


def _page_copies(pt_ref, kc_ref, vc_ref, lc_ref, kbuf, vbuf, lbuf, sems, bi, slot, n_pages, page):
    copies = []
    for p in range(n_pages):
        pid = pt_ref[bi, p]
        copies.append(pltpu.make_async_copy(kc_ref.at[pid], kbuf.at[slot, pl.ds(p * page, page)], sems.at[0, slot]))
        copies.append(pltpu.make_async_copy(vc_ref.at[pid], vbuf.at[slot, pl.ds(p * page, page)], sems.at[1, slot]))
        copies.append(pltpu.make_async_copy(lc_ref.at[pid], lbuf.at[slot, :, pl.ds(p * page, page)], sems.at[2, slot]))
    return copies


def _fox_decode_kernel(pt_ref, q_ref, kn_ref, vn_ref, zf_ref, sm_ref, ffr_ref, fbrow_ref, fbcol_ref,
                       kc_ref, vc_ref, lc_ref, o_ref, logf_ref, kbuf, vbuf, lbuf, fpast, sems,
                       *, n_pages, page, l_new):
    bi = pl.program_id(0)
    nb = pl.num_programs(0)
    slot = bi % 2
    copies = functools.partial(_page_copies, pt_ref, kc_ref, vc_ref, lc_ref, kbuf, vbuf, lbuf, sems,
                               n_pages=n_pages, page=page)

    @pl.when(bi == 0)
    def _():
        for cp in copies(bi=bi, slot=slot):
            cp.start()

    @pl.when(bi + 1 < nb)
    def _():
        for cp in copies(bi=bi + 1, slot=1 - slot):
            cp.start()

    for cp in copies(bi=bi, slot=slot):
        cp.wait()

    scale = HEAD_DIM ** -0.5
    r8 = _iota2((SUBLANES, SUBLANES), 0)
    c8 = _iota2((SUBLANES, SUBLANES), 1)
    upper = (_iota2((page, page), 0) <= _iota2((page, page), 1)).astype(F32)
    carry = jnp.zeros((SUBLANES, 1), F32)
    for p in range(n_pages):
        f = _fdot(lbuf[slot, :, p * page:(p + 1) * page], upper) + carry
        fpast[:, p * page:(p + 1) * page] = f
        carry = f[:, page - 1:page]
    f_tot = carry

    tok_valid = (_iota2((SUBLANES, 1), 0) < l_new).astype(F32)
    lf_col = _log_sigmoid(sm_ref[...] + fbrow_ref[...]) * tok_valid
    logf_ref[...] = lf_col
    fq_col_all = _fdot((r8 >= c8).astype(F32), lf_col)
    lane_valid = (_iota2((1, LANES), 1) < l_new).astype(F32)
    lf_row = _log_sigmoid(ffr_ref[...] + fbcol_ref[...]) * lane_valid
    fq_row_all = _fdot(lf_row, (_iota2((LANES, LANES), 0) <= _iota2((LANES, LANES), 1)).astype(F32)) + f_tot
    new_mask = (c8 <= r8) & (c8 < l_new)

    for h in range(N_HEADS):
        lo = h * HEAD_DIM
        q = q_ref[:, lo:lo + HEAD_DIM].astype(BF16)
        fq = fq_col_all[:, SM_FORGET + h:SM_FORGET + h + 1] + f_tot[h:h + 1, :]
        kh = kbuf[slot, :, lo:lo + HEAD_DIM].astype(BF16)
        s = lax.dot_general(q, kh, (((1,), (1,)), ((), ())), preferred_element_type=F32)
        s = s * scale + (fq - fpast[h:h + 1, :])
        kn = kn_ref[:, lo:lo + HEAD_DIM].astype(BF16)
        s_new = lax.dot_general(q, kn, (((1,), (1,)), ((), ())), preferred_element_type=F32)
        s_new = s_new * scale + (fq - fq_row_all[h:h + 1, 0:SUBLANES])
        s_new = jnp.where(new_mask, s_new, NEG_BIG)
        m = jnp.maximum(jnp.max(s, axis=-1, keepdims=True), jnp.max(s_new, axis=-1, keepdims=True))
        p_past = jnp.exp(s - m)
        p_new = jnp.exp(s_new - m)
        l = jnp.sum(p_past, axis=-1, keepdims=True) + jnp.sum(p_new, axis=-1, keepdims=True)
        vh = vbuf[slot, :, lo:lo + HEAD_DIM].astype(BF16)
        acc = jnp.dot(p_past.astype(BF16), vh, preferred_element_type=F32)
        acc = acc + jnp.dot(p_new.astype(BF16), vn_ref[:, lo:lo + HEAD_DIM].astype(BF16),
                            preferred_element_type=F32)
        z = zf_ref[:, lo:lo + HEAD_DIM]
        o_ref[:, lo:lo + HEAD_DIM] = ((acc / l) * (z * _sigmoid(z))).astype(o_ref.dtype)


def _fox_decode_call(page_table, q8, kn8, vn8, zf8, sm8, ffr8, fb_row, fb_col, kcache, vcache, lcache, l_new):
    b, n_pages = page_table.shape
    page = kcache.shape[1]
    past = n_pages * page
    kern = functools.partial(_fox_decode_kernel, n_pages=n_pages, page=page, l_new=l_new)
    row = lambda w: pl.BlockSpec((None, SUBLANES, w), lambda i, pt: (i, 0, 0))
    const = lambda shape: pl.BlockSpec(shape, lambda i, pt: (0,) * len(shape))
    any_spec = pl.BlockSpec(memory_space=pl.ANY)
    grid_spec = pltpu.PrefetchScalarGridSpec(
        num_scalar_prefetch=1,
        grid=(b,),
        in_specs=[row(GROUP_W), row(GROUP_W), row(GROUP_W), row(GROUP_W), row(LANES), row(LANES),
                  const((1, LANES)), const((SUBLANES, LANES)), any_spec, any_spec, any_spec],
        out_specs=[row(GROUP_W), row(LANES)],
        scratch_shapes=[pltpu.VMEM((2, past, GROUP_W), F32),
                        pltpu.VMEM((2, past, GROUP_W), F32),
                        pltpu.VMEM((2, SUBLANES, past), F32),
                        pltpu.VMEM((SUBLANES, past), F32),
                        pltpu.SemaphoreType.DMA((3, 2))],
    )
    return pl.pallas_call(
        kern,
        grid_spec=grid_spec,
        out_shape=[jax.ShapeDtypeStruct((b, SUBLANES, GROUP_W), BF16),
                   jax.ShapeDtypeStruct((b, SUBLANES, LANES), F32)],
        compiler_params=pltpu.CompilerParams(dimension_semantics=("arbitrary",),
                                             vmem_limit_bytes=VMEM_LIMIT),
        name="fox_decode",
    )(page_table, q8, kn8, vn8, zf8, sm8, ffr8, fb_row, fb_col, kcache, vcache, lcache)


def _pack_w_in(w):
    small = jnp.concatenate([w[:, 2048:2056], w[:, 4104:4108]], axis=1)
    small = jnp.pad(small, ((0, 0), (0, LANES - small.shape[1])))
    return jnp.concatenate([w[:, 0:2048], w[:, 2056:4104], small], axis=1).astype(BF16)


def _gate_row(vals, offset):
    return jnp.zeros((1, LANES), F32).at[0, offset:offset + N_HEADS].set(vals.astype(F32))


def _pad_rows(t, rows):
    return jnp.pad(t, ((0, 0), (0, rows - t.shape[1]), (0, 0)))


def kernel(x_prompt, x_sample, cache_fox_k, cache_fox_v, cache_fox_logf, page_table, state_gdn_ssm,
           state_gdn_conv, w_in, gdn_conv_w, gdn_a_log, gdn_dt_bias, gdn_out_norm_w, fox_f_bias, w_out,
           norm_w, final_norm_w):
    bp, lp, d = x_prompt.shape
    bs, ls, _ = x_sample.shape
    depth = w_in.shape[0]
    assert depth == 1, "single-layer trunk"
    n_pool, page = cache_fox_k.shape[1], cache_fox_k.shape[2]

    w_big = _pack_w_in(w_in[0])
    w_o = w_out[0].astype(BF16)
    nw = norm_w[0].reshape(1, d)
    fnw = final_norm_w.reshape(1, d)
    conv_w = gdn_conv_w[0]
    alog_row = _gate_row(gdn_a_log[0], SM_DECAY)
    dtb_row = _gate_row(gdn_dt_bias[0], SM_DECAY)
    fb_row = _gate_row(fox_f_bias[0], SM_FORGET)
    onw = gdn_out_norm_w[0].reshape(1, HEAD_DIM)

    xp2 = x_prompt.reshape(bp * lp, d)
    qkv, zg, qf, kf, vf, zf, sm = _proj_call(xp2, nw, w_big, tm=256)
    r3 = lambda t: t.reshape(bp, lp, t.shape[-1])
    s0 = jnp.zeros((bp, N_HEADS, HEAD_DIM, HEAD_DIM), F32)
    c0 = jnp.zeros((bp, SUBLANES, CONV_DIM), F32)
    og_p, ssm_p = _gdn_call(r3(qkv), r3(zg), r3(sm), conv_w, alog_row, dtb_row, onw, s0, c0,
                            c=GDN_CHUNK, l_valid=lp)
    tq = 256
    logf_p, fcol, frow = _fox_gates_call(r3(sm), fb_row, tk=tq)
    of_p = _fox_prompt_call(r3(qf), r3(kf), r3(vf), fcol, frow, r3(zf), tq=tq)
    y_p = _out_call(og_p.reshape(bp * lp, GROUP_W), of_p.reshape(bp * lp, GROUP_W), xp2, w_o, fnw, tm=512)

    y_prompt = y_p.reshape(bp, lp, d)
    k_prompt = kf.reshape(1, bp, lp, N_HEADS, HEAD_DIM)
    v_prompt = vf.reshape(1, bp, lp, N_HEADS, HEAD_DIM)
    logf_prompt = logf_p[:, :, SM_FORGET:SM_FORGET + N_HEADS].reshape(1, bp, lp, N_HEADS)
    ssm_prompt = ssm_p.reshape(1, bp, N_HEADS, HEAD_DIM, HEAD_DIM)
    conv_prompt = r3(qkv)[:, lp - (CONV_K - 1):, :].reshape(1, bp, CONV_K - 1, CONV_DIM)

    xs2 = x_sample.reshape(bs * ls, d)
    qkv_s, zg_s, qf_s, kf_s, vf_s, zf_s, sm_s = _proj_call(xs2, nw, w_big, tm=256)
    r3s = lambda t: t.reshape(bs, ls, t.shape[-1])
    p8 = lambda t: _pad_rows(r3s(t), SUBLANES)
    c0_s = jnp.pad(state_gdn_conv[0], ((0, 0), (SUBLANES - (CONV_K - 1), 0), (0, 0)))
    og_s, ssm_s = _gdn_call(p8(qkv_s), p8(zg_s), p8(sm_s), conv_w, alog_row, dtb_row, onw,
                            state_gdn_ssm[0], c0_s, c=SUBLANES, l_valid=ls)
    ff = r3s(sm_s)[:, :, SM_FORGET:SM_FORGET + N_HEADS]
    ffr8 = jnp.pad(ff.transpose(0, 2, 1), ((0, 0), (0, SUBLANES - N_HEADS), (0, LANES - ls)))
    fb_col = jnp.broadcast_to(jnp.pad(fox_f_bias[0].astype(F32), (0, SUBLANES - N_HEADS))[:, None],
                              (SUBLANES, LANES))
    kcache = cache_fox_k[0].reshape(n_pool, page, GROUP_W)
    vcache = cache_fox_v[0].reshape(n_pool, page, GROUP_W)
    lcache = jnp.pad(cache_fox_logf[0].transpose(0, 2, 1), ((0, 0), (0, SUBLANES - N_HEADS), (0, 0)))
    of_s, logf_s = _fox_decode_call(page_table, p8(qf_s), p8(kf_s), p8(vf_s), p8(zf_s), p8(sm_s), ffr8,
                                    fb_row, fb_col, kcache, vcache, lcache, l_new=ls)
    og_s2 = og_s[:, :ls].reshape(bs * ls, GROUP_W)
    of_s2 = of_s[:, :ls].reshape(bs * ls, GROUP_W)
    y_s = _out_call(og_s2, of_s2, xs2, w_o, fnw, tm=256)

    y_sample = y_s.reshape(bs, ls, d)
    k_sample = kf_s.reshape(1, bs, ls, N_HEADS, HEAD_DIM)
    v_sample = vf_s.reshape(1, bs, ls, N_HEADS, HEAD_DIM)
    logf_sample = logf_s[:, :ls, SM_FORGET:SM_FORGET + N_HEADS].reshape(1, bs, ls, N_HEADS)
    ssm_sample = ssm_s.reshape(1, bs, N_HEADS, HEAD_DIM, HEAD_DIM)
    if ls >= CONV_K - 1:
        conv_sample = r3s(qkv_s)[:, ls - (CONV_K - 1):, :]
    else:
        conv_sample = jnp.concatenate([state_gdn_conv[0], r3s(qkv_s)], axis=1)[:, -(CONV_K - 1):, :]
    conv_sample = conv_sample.reshape(1, bs, CONV_K - 1, CONV_DIM)

    return (y_prompt, y_sample, k_prompt, v_prompt, logf_prompt, ssm_prompt, conv_prompt,
            k_sample, v_sample, logf_sample, ssm_sample, conv_sample)
```

```python
import functools
import math

import jax
import jax.numpy as jnp
from jax import lax
from jax.experimental import pallas as pl
from jax.experimental.pallas import tpu as pltpu

F32 = jnp.float32
BF16 = jnp.bfloat16

NORM_EPS = 1e-6
L2_EPS = 1e-6
HEAD_DIM = 128
N_HEADS = 4
GROUP_W = N_HEADS * HEAD_DIM
CONV_DIM = 3 * GROUP_W
CONV_K = 4
LANES = 128
SUBLANES = 8
GDN_CHUNK = 64
SM_BETA = 0
SM_DECAY = 4
SM_FORGET = 8
VMEM_LIMIT = 56 * 1024 * 1024


def _sigmoid(x):
    return 1.0 / (1.0 + jnp.exp(-x))


def _softplus(x):
    return jnp.maximum(x, 0.0) + jnp.log(1.0 + jnp.exp(-jnp.abs(x)))


def _bdot(a, b):
    return jnp.dot(a.astype(BF16), b.astype(BF16), preferred_element_type=F32)


def _bdot_nt(a, b):
    return lax.dot_general(a.astype(BF16), b.astype(BF16), (((1,), (1,)), ((), ())),
                           preferred_element_type=F32)


def _bdot_tn(a, b):
    return lax.dot_general(a.astype(BF16), b.astype(BF16), (((0,), (0,)), ((), ())),
                           preferred_element_type=F32)


def _fdot(a, b):
    return jnp.dot(a, b, preferred_element_type=F32, precision=lax.Precision.HIGHEST)


def _iota2(shape, dim):
    return lax.broadcasted_iota(jnp.int32, shape, dim)


W_QKV, W_ZG, W_QF, W_KF, W_VF, W_ZF, W_SM, W_END = 0, 1536, 2048, 2560, 3072, 3584, 4096, 4224


def _store_head_rows(ref, val, tm):
    for h in range(N_HEADS):
        ref[pl.ds(h, tm, stride=N_HEADS), :] = val[:, h * HEAD_DIM:(h + 1) * HEAD_DIM].astype(ref.dtype)


def _proj_kernel(x_ref, nw_ref, w_ref, *out_refs, tm, sample):
    x = x_ref[...]
    var = jnp.mean(x * x, axis=-1, keepdims=True)
    h = (x * lax.rsqrt(var + NORM_EPS) * nw_ref[...]).astype(BF16)
    seg = lambda lo, hi: jnp.dot(h, w_ref[:, lo:hi], preferred_element_type=F32)
    if sample:
        qkv_ref, zg_ref, sm_ref, q4_ref, k4_ref, v4_ref, z4_ref = out_refs
    else:
        qkv_ref, zg_ref, sm_ref, qb_ref, kb_ref, vb_ref, zf_ref, k4_ref, v4_ref = out_refs
    qkv_ref[...] = seg(W_QKV, W_ZG)
    zg_ref[...] = seg(W_ZG, W_QF)
    sm_ref[...] = seg(W_SM, W_END)
    qf, kf, vf, zf = seg(W_QF, W_KF), seg(W_KF, W_VF), seg(W_VF, W_ZF), seg(W_ZF, W_SM)
    _store_head_rows(k4_ref, kf, tm)
    _store_head_rows(v4_ref, vf, tm)
    if sample:
        _store_head_rows(q4_ref, qf, tm)
        _store_head_rows(z4_ref, zf, tm)
    else:
        qb_ref[...] = (qf * (HEAD_DIM ** -0.5)).astype(BF16)
        kb_ref[...] = kf.astype(BF16)
        vb_ref[...] = vf.astype(BF16)
        zf_ref[...] = zf


def _proj_call(x2d, norm_w, w_big, tm, sample):
    t, d = x2d.shape
    n = w_big.shape[1]
    wide = lambda w, dt: (jax.ShapeDtypeStruct((t, w), dt), pl.BlockSpec((tm, w), lambda i: (i, 0)))
    rows4 = (jax.ShapeDtypeStruct((t * N_HEADS, HEAD_DIM), F32),
             pl.BlockSpec((tm * N_HEADS, HEAD_DIM), lambda i: (i, 0)))
    outs = [wide(CONV_DIM, F32), wide(GROUP_W, F32), wide(LANES, F32)]
    if sample:
        outs += [rows4, rows4, rows4, rows4]
    else:
        outs += [wide(GROUP_W, BF16), wide(GROUP_W, BF16), wide(GROUP_W, BF16), wide(GROUP_W, F32), rows4, rows4]
    out_shape = [o[0] for o in outs]
    out_specs = [o[1] for o in outs]
    return pl.pallas_call(
        functools.partial(_proj_kernel, tm=tm, sample=sample),
        grid=(t // tm,),
        in_specs=[pl.BlockSpec((tm, d), lambda i: (i, 0)),
                  pl.BlockSpec((1, d), lambda i: (0, 0)),
                  pl.BlockSpec((d, n), lambda i: (0, 0))],
        out_specs=out_specs,
        out_shape=out_shape,
        compiler_params=pltpu.CompilerParams(dimension_semantics=("arbitrary",),
                                             vmem_limit_bytes=VMEM_LIMIT),
        name="proj",
    )(x2d, norm_w, w_big)


def _gdn_kernel(qkv_ref, zg_ref, sm_ref, cw_ref, alog_ref, dtb_ref, onw_ref, s0_ref, c0_ref,
                og_ref, sout_ref, xbuf, s_scr, *, c, l_valid, nb):
    ci = pl.program_id(1)
    n_c = pl.num_programs(1)

    @pl.when(ci == 0)
    def _():
        xbuf[:, 0:SUBLANES, :] = c0_ref[...]
        s_scr[...] = s0_ref[...]

    row = _iota2((c, 1), 0) + ci * c
    valid = jnp.broadcast_to((row < l_valid).astype(F32), (c, LANES))
    tri_incl = (_iota2((c, c), 0) >= _iota2((c, c), 1))
    tri_strict = (_iota2((c, c), 0) > _iota2((c, c), 1))
    eye = (_iota2((c, c), 0) == _iota2((c, c), 1)).astype(F32)
    pad_rows = LANES - c
    sl = lambda base, h: slice(base + h * HEAD_DIM, base + (h + 1) * HEAD_DIM)

    q, k, v, beta, gc, gc_row, gc_last = [], [], [], [], [], [], []
    for bb in range(nb):
        xbuf[bb, SUBLANES:SUBLANES + c, :] = qkv_ref[bb]
        y = jnp.zeros((c, CONV_DIM), F32)
        for j in range(CONV_K):
            y = y + xbuf[bb, pl.ds(SUBLANES - (CONV_K - 1) + j, c), :] * cw_ref[j:j + 1, :]
        y = y * _sigmoid(y)
        xbuf[bb, 0:SUBLANES, :] = xbuf[bb, c:c + SUBLANES, :]
        sm = sm_ref[bb]
        beta_t = _sigmoid(sm) * valid
        g_t = -jnp.exp(alog_ref[...]) * _softplus(sm + dtb_ref[...]) * valid
        gc_t = _fdot(tri_incl.astype(F32), g_t)
        gc_sq = jnp.concatenate([gc_t, jnp.zeros((pad_rows, LANES), F32)], axis=0) if pad_rows else gc_t
        gc_tr = gc_sq.T
        for h in range(N_HEADS):
            qh, kh = y[:, sl(0, h)], y[:, sl(GROUP_W, h)]
            q.append(qh * lax.rsqrt(jnp.sum(qh * qh, axis=-1, keepdims=True) + L2_EPS) * (HEAD_DIM ** -0.5))
            k.append(kh * lax.rsqrt(jnp.sum(kh * kh, axis=-1, keepdims=True) + L2_EPS) * valid)
            v.append(y[:, sl(2 * GROUP_W, h)])
            beta.append(jnp.broadcast_to(beta_t[:, SM_BETA + h:SM_BETA + h + 1], (c, HEAD_DIM)))
            gc.append(jnp.broadcast_to(gc_t[:, SM_DECAY + h:SM_DECAY + h + 1], (c, HEAD_DIM)))
            gc_row.append(gc_tr[SM_DECAY + h:SM_DECAY + h + 1, 0:c])
            gc_last.append(jnp.broadcast_to(gc_t[c - 1:c, SM_DECAY + h:SM_DECAY + h + 1], (1, HEAD_DIM)))

    chains = range(nb * N_HEADS)
    decay = [jnp.where(tri_incl, jnp.exp(jnp.where(tri_incl, gc[i][:, 0:c] - gc_row[i], 0.0)), 0.0)
             for i in chains]
    kb = [k[i] * beta[i] for i in chains]
    kkqk = [_bdot_nt(jnp.concatenate([kb[i], q[i]], axis=0), k[i]) for i in chains]
    qk = [kkqk[i][c:2 * c] * decay[i] for i in chains]
    neg_a = [-jnp.where(tri_strict, kkqk[i][0:c] * decay[i], 0.0) for i in chains]
    t_inv = [eye + neg_a[i] for i in chains]
    pw = [_bdot(neg_a[i], neg_a[i]) for i in chains]
    n_sq = int(math.log2(c))
    for j in range(1, n_sq):
        if j < n_sq - 1:
            both = [_bdot(jnp.concatenate([t_inv[i], pw[i]], axis=0), pw[i]) for i in chains]
            t_inv = [t_inv[i] + both[i][0:c] for i in chains]
            pw = [both[i][c:2 * c] for i in chains]
        else:
            t_inv = [t_inv[i] + _bdot(t_inv[i], pw[i]) for i in chains]
    egc = [jnp.exp(gc[i]) for i in chains]
    sol = [_bdot(t_inv[i], jnp.concatenate([v[i] * beta[i], kb[i] * egc[i]], axis=-1)) for i in chains]
    s = [s_scr[i // N_HEADS, i % N_HEADS] for i in chains]
    ws = [_bdot(jnp.concatenate([sol[i][:, HEAD_DIM:2 * HEAD_DIM], q[i] * egc[i]], axis=0), s[i])
          for i in chains]
    v_new = [sol[i][:, 0:HEAD_DIM] - ws[i][0:c] for i in chains]
    o = [ws[i][c:2 * c] + _bdot(qk[i], v_new[i]) for i in chains]
    k_dec = [k[i] * jnp.exp(gc_last[i] - gc[i]) for i in chains]
    s_new = [s[i] * jnp.exp(gc_last[i]) + _bdot_tn(k_dec[i], v_new[i]) for i in chains]
    for i in chains:
        bb, h = i // N_HEADS, i % N_HEADS
        s_scr[bb, h] = s_new[i]
        oh = o[i] * lax.rsqrt(jnp.mean(o[i] * o[i], axis=-1, keepdims=True) + NORM_EPS) * onw_ref[...]
        z = zg_ref[bb, :, sl(0, h)]
        og_ref[bb, :, sl(0, h)] = (oh * (z * _sigmoid(z))).astype(og_ref.dtype)

    @pl.when(ci == n_c - 1)
    def _():
        sout_ref[...] = s_scr[...]


def _gdn_call(qkv, zg, sm, conv_w, alog_row, dtb_row, onw, s0, c0, *, c, l_valid, nb):
    b, l, _ = qkv.shape
    n_c = l // c
    kern = functools.partial(_gdn_kernel, c=c, l_valid=l_valid, nb=nb)
    blk = lambda w: pl.BlockSpec((nb, c, w), lambda bi, ci: (bi, ci, 0))
    full = lambda shape: pl.BlockSpec(shape, lambda bi, ci: (0,) * len(shape))
    state = pl.BlockSpec((nb, N_HEADS, HEAD_DIM, HEAD_DIM), lambda bi, ci: (bi, 0, 0, 0))
    return pl.pallas_call(
        kern,
        grid=(b // nb, n_c),
        in_specs=[blk(CONV_DIM), blk(GROUP_W), blk(LANES),
                  full((CONV_K, CONV_DIM)), full((1, LANES)), full((1, LANES)), full((1, HEAD_DIM)),
                  state, pl.BlockSpec((nb, SUBLANES, CONV_DIM), lambda bi, ci: (bi, 0, 0))],
        out_specs=[blk(GROUP_W), state],
        out_shape=[jax.ShapeDtypeStruct((b, l, GROUP_W), BF16),
                   jax.ShapeDtypeStruct((b, N_HEADS, HEAD_DIM, HEAD_DIM), F32)],
        scratch_shapes=[pltpu.VMEM((nb, c + SUBLANES, CONV_DIM), F32),
                        pltpu.VMEM((nb, N_HEADS, HEAD_DIM, HEAD_DIM), F32)],
        compiler_params=pltpu.CompilerParams(dimension_semantics=("arbitrary", "arbitrary"),
                                             vmem_limit_bytes=VMEM_LIMIT),
        name="gdn",
    )(qkv, zg, sm, conv_w, alog_row, dtb_row, onw, s0, c0)


def _log_sigmoid(x):
    return -_softplus(-x)


def _fox_gates_kernel(sm_ref, fb_ref, logf_ref, fcol_ref, *, l):
    blk = LANES
    tri = (_iota2((blk, blk), 0) >= _iota2((blk, blk), 1)).astype(F32)
    carry = jnp.zeros((1, LANES), F32)
    for i in range(l // blk):
        lf = _log_sigmoid(sm_ref[i * blk:(i + 1) * blk, :] + fb_ref[...])
        f = _fdot(tri, lf) + carry
        carry = f[blk - 1:blk, :]
        logf_ref[i * blk:(i + 1) * blk, :] = lf
        fcol_ref[i * blk:(i + 1) * blk, :] = f


def _fox_gates_call(sm, fb_row):
    b, l, _ = sm.shape
    kern = functools.partial(_fox_gates_kernel, l=l)
    seq = pl.BlockSpec((None, l, LANES), lambda i: (i, 0, 0))
    return pl.pallas_call(
        kern,
        grid=(b,),
        in_specs=[seq, pl.BlockSpec((1, LANES), lambda i: (0, 0))],
        out_specs=[seq, seq],
        out_shape=[jax.ShapeDtypeStruct((b, l, LANES), F32), jax.ShapeDtypeStruct((b, l, LANES), F32)],
        compiler_params=pltpu.CompilerParams(dimension_semantics=("arbitrary",),
                                             vmem_limit_bytes=VMEM_LIMIT),
        name="fox_gates",
    )(sm, fb_row)


NEG_BIG = -1e30


def _forget_columns(f_tile, h, rows, for_keys):
    f = jnp.broadcast_to(f_tile[:, SM_FORGET + h:SM_FORGET + h + 1], (rows, LANES))
    f1, f2, f3 = (t.astype(F32) for t in _split3(-f if for_keys else f))
    lane = _iota2((rows, LANES), 1)
    base = 3 if for_keys else 0
    ones = ((lane >= 3 - base) & (lane < 6 - base)).astype(F32)
    cols = jnp.where(lane == base, f1, jnp.where(lane == base + 1, f2, jnp.where(lane == base + 2, f3, ones)))
    return cols.astype(BF16)


def _fox_prompt_kernel(q_ref, k_ref, v_ref, fcol_ref, zf_ref, o_ref, kx_ref, *, tq, l):
    qi = pl.program_id(1)
    heads = range(N_HEADS)
    sl = lambda h: slice(h * HEAD_DIM, (h + 1) * HEAD_DIM)
    nt = (((1,), (1,)), ((), ()))

    @pl.when(qi == 0)
    def _():
        for r in range(l // tq):
            for h in heads:
                kx_ref[r * tq:(r + 1) * tq, sl(h)] = _forget_columns(fcol_ref[r * tq:(r + 1) * tq, :], h, tq, True)

    f_q = fcol_ref[pl.ds(pl.multiple_of(qi * tq, tq), tq), :]
    qa = [jnp.concatenate([q_ref[:, sl(h)], _forget_columns(f_q, h, tq, False)], axis=1) for h in heads]
    causal = _iota2((tq, tq), 0) >= _iota2((tq, tq), 1)

    def block(ki, carry, masked):
        m, lsum, acc = carry
        rows = pl.ds(pl.multiple_of(ki * tq, tq), tq)
        ka = [jnp.concatenate([k_ref[rows, sl(h)], kx_ref[rows, sl(h)]], axis=1) for h in heads]
        s = [lax.dot_general(qa[h], ka[h], nt, preferred_element_type=F32) for h in heads]
        if masked:
            s = [jnp.where(causal, t, NEG_BIG) for t in s]
        m_new = [jnp.maximum(m[h], jnp.max(s[h], axis=-1, keepdims=True)) for h in heads]
        alpha = [jnp.exp(m[h] - m_new[h]) for h in heads]
        p = [jnp.exp(s[h] - m_new[h]) for h in heads]
        lsum = [lsum[h] * alpha[h] + jnp.sum(p[h], axis=-1, keepdims=True) for h in heads]
        pv = [jnp.dot(p[h].astype(BF16), v_ref[rows, sl(h)], preferred_element_type=F32) for h in heads]
        acc = [acc[h] * alpha[h] + pv[h] for h in heads]
        return m_new, lsum, acc

    init = ([jnp.full((tq, 1), NEG_BIG, F32) for _ in heads], [jnp.zeros((tq, 1), F32) for _ in heads],
            [jnp.zeros((tq, HEAD_DIM), F32) for _ in heads])
    carry = lax.fori_loop(0, qi, lambda ki, c: block(ki, c, False), init)
    m, lsum, acc = block(qi, carry, True)
    for h in heads:
        z = zf_ref[:, sl(h)]
        o_ref[:, sl(h)] = ((acc[h] / lsum[h]) * (z * _sigmoid(z))).astype(o_ref.dtype)


def _fox_prompt_call(qf, kf, vf, fcol, zf, tq):
    b, l, _ = qf.shape
    kern = functools.partial(_fox_prompt_kernel, tq=tq, l=l)
    qblk = lambda w: pl.BlockSpec((None, tq, w), lambda bi, qi: (bi, qi, 0))
    seq = lambda w: pl.BlockSpec((None, l, w), lambda bi, qi: (bi, 0, 0))
    return pl.pallas_call(
        kern,
        grid=(b, l // tq),
        in_specs=[qblk(GROUP_W), seq(GROUP_W), seq(GROUP_W), seq(LANES), qblk(GROUP_W)],
        out_specs=qblk(GROUP_W),
        out_shape=jax.ShapeDtypeStruct((b, l, GROUP_W), BF16),
        scratch_shapes=[pltpu.VMEM((l, GROUP_W), BF16)],
        compiler_params=pltpu.CompilerParams(dimension_semantics=("arbitrary", "arbitrary"),
                                             vmem_limit_bytes=VMEM_LIMIT),
        name="fox_prompt",
    )(qf, kf, vf, fcol, zf)


def _out_kernel(og_ref, of_ref, x_ref, w_ref, fnw_ref, y_ref):
    o = jnp.dot(og_ref[...], w_ref[0:GROUP_W, :], preferred_element_type=F32)
    o = o + jnp.dot(of_ref[...], w_ref[GROUP_W:2 * GROUP_W, :], preferred_element_type=F32)
    y = x_ref[...] + o
    var = jnp.mean(y * y, axis=-1, keepdims=True)
    y_ref[...] = y * lax.rsqrt(var + NORM_EPS) * fnw_ref[...]


def _out_call(og, of, x2d, w_out, fnw, tm):
    t, d = x2d.shape
    return pl.pallas_call(
        _out_kernel,
        grid=(t // tm,),
        in_specs=[pl.BlockSpec((tm, GROUP_W), lambda i: (i, 0)),
                  pl.BlockSpec((tm, GROUP_W), lambda i: (i, 0)),
                  pl.BlockSpec((tm, d), lambda i: (i, 0)),
                  pl.BlockSpec((2 * GROUP_W, d), lambda i: (0, 0)),
                  pl.BlockSpec((1, d), lambda i: (0, 0))],
        out_specs=pl.BlockSpec((tm, d), lambda i: (i, 0)),
        out_shape=jax.ShapeDtypeStruct((t, d), F32),
        compiler_params=pltpu.CompilerParams(dimension_semantics=("arbitrary",),
                                             vmem_limit_bytes=VMEM_LIMIT),
        name="out_proj",
    )(og, of, x2d, w_out, fnw)


def _page_copies(pt_ref, kc_ref, vc_ref, lc_ref, kbuf, vbuf, lbuf, sems, bi, slot, n_pages, pg):
    copies = []
    for p in range(n_pages):
        pid = pt_ref[bi, p]
        copies.append(pltpu.make_async_copy(kc_ref.at[pid], kbuf.at[slot, pl.ds(p * pg, pg)], sems.at[0, slot]))
        copies.append(pltpu.make_async_copy(vc_ref.at[pid], vbuf.at[slot, pl.ds(p * pg, pg)], sems.at[1, slot]))
        copies.append(pltpu.make_async_copy(lc_ref.at[pid], lbuf.at[slot, pl.ds(p, 1)], sems.at[2, slot]))
    return copies


def _split3(x):
    x1 = x.astype(BF16)
    r1 = x - x1.astype(F32)
    x2 = r1.astype(BF16)
    x3 = (r1 - x2.astype(F32)).astype(BF16)
    return x1, x2, x3


def _fox_decode_kernel(pt_ref, q_ref, kn_ref, vn_ref, zf_ref, sm_ref, fbrow_ref, cums_ref,
                       kc_ref, vc_ref, lc_ref, o_ref, logf_ref, kbuf, vbuf, lbuf, sems,
                       *, n_pages, pg, l_new):
    bi = pl.program_id(0)
    nb = pl.num_programs(0)
    slot = bi % 2
    copies = functools.partial(_page_copies, pt_ref, kc_ref, vc_ref, lc_ref, kbuf, vbuf, lbuf, sems,
                               n_pages=n_pages, pg=pg)
    nr = l_new * N_HEADS

    @pl.when(bi == 0)
    def _():
        for cp in copies(bi=bi, slot=slot):
            cp.start()

    @pl.when(bi + 1 < nb)
    def _():
        for cp in copies(bi=bi + 1, slot=1 - slot):
            cp.start()

    for cp in copies(bi=bi, slot=slot):
        cp.wait()

    scale = HEAD_DIM ** -0.5
    lf1, lf2, lf3 = _split3(lbuf[slot])
    res = jnp.dot(jnp.concatenate([lf1, lf2, lf3], axis=0), cums_ref[...], preferred_element_type=F32)
    res = res[0:n_pages] + res[n_pages:2 * n_pages] + res[2 * n_pages:3 * n_pages]
    within, tot = res[:, 0:pg], res[:, pg:2 * pg]
    earlier = (_iota2((n_pages, n_pages), 0) > _iota2((n_pages, n_pages), 1)).astype(F32)
    carry = _fdot(earlier, tot)
    f_past = within + carry
    f_tot_row = carry[n_pages - 1:n_pages, :] + tot[n_pages - 1:n_pages, :]

    tok_valid = (_iota2((SUBLANES, 1), 0) < l_new).astype(F32)
    lf_col = _log_sigmoid(sm_ref[...] + fbrow_ref[...]) * tok_valid
    logf_ref[...] = lf_col
    r_tok = _iota2((nr, SUBLANES), 0) // N_HEADS
    csum = _fdot((_iota2((nr, SUBLANES), 1) <= r_tok).astype(F32), lf_col)
    own_lane = _iota2((nr, LANES), 1) == SM_FORGET + _iota2((nr, LANES), 0) % N_HEADS
    fq_new = jnp.sum(jnp.where(own_lane, csum, 0.0), axis=-1, keepdims=True)
    eye = _iota2((nr, nr), 0) == _iota2((nr, nr), 1)
    f_tot_col = jnp.sum(jnp.where(eye, jnp.broadcast_to(f_tot_row[:, 0:nr], (nr, nr)), 0.0),
                        axis=-1, keepdims=True)
    fq = fq_new + f_tot_col
    fq_row = jnp.sum(jnp.where(eye, jnp.broadcast_to(fq, (nr, nr)), 0.0), axis=0, keepdims=True)

    q = q_ref[...].astype(BF16)
    s_all = lax.dot_general(q, kbuf[slot].astype(BF16), (((1,), (1,)), ((), ())),
                            preferred_element_type=F32)
    same_head = (_iota2((nr, pg), 1) % N_HEADS) == (_iota2((nr, pg), 0) % N_HEADS)
    sp = [jnp.where(same_head, s_all[:, p * pg:(p + 1) * pg] * scale + (fq - f_past[p:p + 1, :]), NEG_BIG)
          for p in range(n_pages)]
    s_new = lax.dot_general(q, kn_ref[...].astype(BF16), (((1,), (1,)), ((), ())),
                            preferred_element_type=F32)
    rr, cc = _iota2((nr, nr), 0), _iota2((nr, nr), 1)
    new_ok = (rr % N_HEADS == cc % N_HEADS) & (cc // N_HEADS <= rr // N_HEADS)
    s_new = jnp.where(new_ok, s_new * scale + (fq - fq_row), NEG_BIG)
    m_el = sp[0]
    for p in range(1, n_pages):
        m_el = jnp.maximum(m_el, sp[p])
    m = jnp.maximum(jnp.max(m_el, axis=-1, keepdims=True), jnp.max(s_new, axis=-1, keepdims=True))
    pp = [jnp.exp(t - m) for t in sp]
    p_new = jnp.exp(s_new - m)
    l_el = pp[0]
    for p in range(1, n_pages):
        l_el = l_el + pp[p]
    l = jnp.sum(l_el, axis=-1, keepdims=True) + jnp.sum(p_new, axis=-1, keepdims=True)
    p_all = jnp.concatenate([t.astype(BF16) for t in pp], axis=-1)
    acc = jnp.dot(p_all, vbuf[slot].astype(BF16), preferred_element_type=F32)
    acc = acc + jnp.dot(p_new.astype(BF16), vn_ref[...].astype(BF16), preferred_element_type=F32)
    z = zf_ref[...]
    o_ref[...] = ((acc / l) * (z * _sigmoid(z))).astype(o_ref.dtype)


def _head_cumsum_matrix(pg):
    i = jnp.arange(pg)
    same = (i[:, None] % N_HEADS) == (i[None, :] % N_HEADS)
    c = same & (i[:, None] // N_HEADS <= i[None, :] // N_HEADS)
    return jnp.concatenate([c, same], axis=1).astype(BF16)


def _fox_decode_call(page_table, q4, kn4, vn4, zf4, sm8, fb_row, kcache, vcache, lcache, l_new):
    b, n_pages = page_table.shape
    pg = kcache.shape[1]
    nr = l_new * N_HEADS
    assert nr % (2 * SUBLANES) == 0 and l_new <= SUBLANES
    kern = functools.partial(_fox_decode_kernel, n_pages=n_pages, pg=pg, l_new=l_new)
    rows = pl.BlockSpec((nr, HEAD_DIM), lambda i, pt: (i, 0))
    tok = pl.BlockSpec((None, SUBLANES, LANES), lambda i, pt: (i, 0, 0))
    const = lambda shape: pl.BlockSpec(shape, lambda i, pt: (0,) * len(shape))
    any_spec = pl.BlockSpec(memory_space=pl.ANY)
    grid_spec = pltpu.PrefetchScalarGridSpec(
        num_scalar_prefetch=1,
        grid=(b,),
        in_specs=[rows, rows, rows, rows, tok, const((1, LANES)), const((pg, 2 * pg)),
                  any_spec, any_spec, any_spec],
        out_specs=[rows, tok],
        scratch_shapes=[pltpu.VMEM((2, n_pages * pg, HEAD_DIM), F32),
                        pltpu.VMEM((2, n_pages * pg, HEAD_DIM), F32),
                        pltpu.VMEM((2, n_pages, pg), F32),
                        pltpu.SemaphoreType.DMA((3, 2))],
    )
    return pl.pallas_call(
        kern,
        grid_spec=grid_spec,
        out_shape=[jax.ShapeDtypeStruct((b * nr, HEAD_DIM), BF16),
                   jax.ShapeDtypeStruct((b, SUBLANES, LANES), F32)],
        compiler_params=pltpu.CompilerParams(dimension_semantics=("arbitrary",),
                                             vmem_limit_bytes=VMEM_LIMIT),
        name="fox_decode",
    )(page_table, q4, kn4, vn4, zf4, sm8, fb_row, _head_cumsum_matrix(pg), kcache, vcache, lcache)


def _pack_w_in(w):
    small = jnp.concatenate([w[:, 2048:2056], w[:, 4104:4108]], axis=1)
    small = jnp.pad(small, ((0, 0), (0, LANES - small.shape[1])))
    return jnp.concatenate([w[:, 0:2048], w[:, 2056:4104], small], axis=1).astype(BF16)


def _gate_row(vals, offset):
    return jnp.zeros((1, LANES), F32).at[0, offset:offset + N_HEADS].set(vals.astype(F32))


def _pad_rows(t, rows):
    return jnp.pad(t, ((0, 0), (0, rows - t.shape[1]), (0, 0)))


def kernel(x_prompt, x_sample, cache_fox_k, cache_fox_v, cache_fox_logf, page_table, state_gdn_ssm,
           state_gdn_conv, w_in, gdn_conv_w, gdn_a_log, gdn_dt_bias, gdn_out_norm_w, fox_f_bias, w_out,
           norm_w, final_norm_w):
    bp, lp, d = x_prompt.shape
    bs, ls, _ = x_sample.shape
    depth = w_in.shape[0]
    assert depth == 1, "single-layer trunk"
    n_pool, page = cache_fox_k.shape[1], cache_fox_k.shape[2]

    w_big = _pack_w_in(w_in[0])
    w_o = w_out[0].astype(BF16)
    nw = norm_w[0].reshape(1, d)
    fnw = final_norm_w.reshape(1, d)
    conv_w = gdn_conv_w[0]
    alog_row = _gate_row(gdn_a_log[0], SM_DECAY)
    dtb_row = _gate_row(gdn_dt_bias[0], SM_DECAY)
    fb_row = _gate_row(fox_f_bias[0], SM_FORGET)
    onw = gdn_out_norm_w[0].reshape(1, HEAD_DIM)

    xp2 = x_prompt.reshape(bp * lp, d)
    qkv, zg, sm, qf, kf, vf, zf, k4, v4 = _proj_call(xp2, nw, w_big, tm=256, sample=False)
    r3 = lambda t: t.reshape(bp, lp, t.shape[-1])
    s0 = jnp.zeros((bp, N_HEADS, HEAD_DIM, HEAD_DIM), F32)
    c0 = jnp.zeros((bp, SUBLANES, CONV_DIM), F32)
    og_p, ssm_p = _gdn_call(r3(qkv), r3(zg), r3(sm), conv_w, alog_row, dtb_row, onw, s0, c0,
                            c=GDN_CHUNK, l_valid=lp, nb=4)
    logf_p, fcol = _fox_gates_call(r3(sm), fb_row)
    of_p = _fox_prompt_call(r3(qf), r3(kf), r3(vf), fcol, r3(zf), tq=512)
    y_p = _out_call(og_p.reshape(bp * lp, GROUP_W), of_p.reshape(bp * lp, GROUP_W), xp2, w_o, fnw, tm=512)

    y_prompt = y_p.reshape(bp, lp, d)
    k_prompt = k4.reshape(1, bp, lp, N_HEADS, HEAD_DIM)
    v_prompt = v4.reshape(1, bp, lp, N_HEADS, HEAD_DIM)
    logf_prompt = logf_p[:, :, SM_FORGET:SM_FORGET + N_HEADS].reshape(1, bp, lp, N_HEADS)
    ssm_prompt = ssm_p.reshape(1, bp, N_HEADS, HEAD_DIM, HEAD_DIM)
    conv_prompt = r3(qkv)[:, lp - (CONV_K - 1):, :].reshape(1, bp, CONV_K - 1, CONV_DIM)

    xs2 = x_sample.reshape(bs * ls, d)
    qkv_s, zg_s, sm_s, q4_s, k4_s, v4_s, z4_s = _proj_call(xs2, nw, w_big, tm=256, sample=True)
    r3s = lambda t: t.reshape(bs, ls, t.shape[-1])
    p8 = lambda t: _pad_rows(r3s(t), SUBLANES)
    c0_s = jnp.pad(state_gdn_conv[0], ((0, 0), (SUBLANES - (CONV_K - 1), 0), (0, 0)))
    og_s, ssm_s = _gdn_call(p8(qkv_s), p8(zg_s), p8(sm_s), conv_w, alog_row, dtb_row, onw,
                            state_gdn_ssm[0], c0_s, c=SUBLANES, l_valid=ls, nb=16)
    kcache = cache_fox_k[0].reshape(n_pool, page * N_HEADS, HEAD_DIM)
    vcache = cache_fox_v[0].reshape(n_pool, page * N_HEADS, HEAD_DIM)
    lcache = cache_fox_logf[0].reshape(n_pool, 1, page * N_HEADS)
    of_s, logf_s = _fox_decode_call(page_table, q4_s, k4_s, v4_s, z4_s, p8(sm_s), fb_row,
                                    kcache, vcache, lcache, l_new=ls)
    og_s2 = og_s[:, :ls].reshape(bs * ls, GROUP_W)
    of_s2 = of_s.reshape(bs * ls, GROUP_W)
    y_s = _out_call(og_s2, of_s2, xs2, w_o, fnw, tm=256)

    y_sample = y_s.reshape(bs, ls, d)
    k_sample = k4_s.reshape(1, bs, ls, N_HEADS, HEAD_DIM)
    v_sample = v4_s.reshape(1, bs, ls, N_HEADS, HEAD_DIM)
    logf_sample = logf_s[:, :ls, SM_FORGET:SM_FORGET + N_HEADS].reshape(1, bs, ls, N_HEADS)
    ssm_sample = ssm_s.reshape(1, bs, N_HEADS, HEAD_DIM, HEAD_DIM)
    if ls >= CONV_K - 1:
        conv_sample = r3s(qkv_s)[:, ls - (CONV_K - 1):, :]
    else:
        conv_sample = jnp.concatenate([state_gdn_conv[0], r3s(qkv_s)], axis=1)[:, -(CONV_K - 1):, :]
    conv_sample = conv_sample.reshape(1, bs, CONV_K - 1, CONV_DIM)

    return (y_prompt, y_sample, k_prompt, v_prompt, logf_prompt, ssm_prompt, conv_prompt,
            k_sample, v_sample, logf_sample, ssm_sample, conv_sample)
```

```python
import functools
import math

import jax
import jax.numpy as jnp
from jax import lax
from jax.experimental import pallas as pl
from jax.experimental.pallas import tpu as pltpu

F32 = jnp.float32
BF16 = jnp.bfloat16

NORM_EPS = 1e-6
L2_EPS = 1e-6
HEAD_DIM = 128
N_HEADS = 4
GROUP_W = N_HEADS * HEAD_DIM
CONV_DIM = 3 * GROUP_W
CONV_K = 4
LANES = 128
SUBLANES = 8
GDN_CHUNK = 64
SM_BETA = 0
SM_DECAY = 4
SM_FORGET = 8
VMEM_LIMIT = 56 * 1024 * 1024


def _sigmoid(x):
    return 1.0 / (1.0 + jnp.exp(-x))


def _softplus(x):
    return jnp.maximum(x, 0.0) + jnp.log(1.0 + jnp.exp(-jnp.abs(x)))


def _bdot(a, b):
    return jnp.dot(a.astype(BF16), b.astype(BF16), preferred_element_type=F32)


def _bdot_nt(a, b):
    return lax.dot_general(a.astype(BF16), b.astype(BF16), (((1,), (1,)), ((), ())),
                           preferred_element_type=F32)


def _bdot_tn(a, b):
    return lax.dot_general(a.astype(BF16), b.astype(BF16), (((0,), (0,)), ((), ())),
                           preferred_element_type=F32)


def _fdot(a, b):
    return jnp.dot(a, b, preferred_element_type=F32, precision=lax.Precision.HIGHEST)


def _iota2(shape, dim):
    return lax.broadcasted_iota(jnp.int32, shape, dim)


W_QKV, W_ZG, W_QF, W_KF, W_VF, W_ZF, W_SM, W_END = 0, 1536, 2048, 2560, 3072, 3584, 4096, 4224
SRC_GATES_G, SRC_FOX, SRC_GATE_F, SRC_END = 2048, 2056, 4104, 4108


def _pack_w_kernel(w_ref, o_ref):
    o_ref[:, W_QKV:W_QF] = w_ref[:, 0:SRC_GATES_G].astype(BF16)
    o_ref[:, W_QF:W_SM] = w_ref[:, SRC_FOX:SRC_GATE_F].astype(BF16)
    rows = w_ref.shape[0]
    gates = jnp.concatenate([w_ref[:, SRC_GATES_G:SRC_FOX], w_ref[:, SRC_GATE_F:SRC_END],
                             jnp.zeros((rows, LANES - (SRC_FOX - SRC_GATES_G) - (SRC_END - SRC_GATE_F)), F32)],
                            axis=1)
    o_ref[:, W_SM:W_END] = gates.astype(BF16)


def _pack_w_call(w):
    rows = w.shape[0]
    return pl.pallas_call(
        _pack_w_kernel,
        out_shape=jax.ShapeDtypeStruct((rows, W_END), BF16),
        compiler_params=pltpu.CompilerParams(vmem_limit_bytes=VMEM_LIMIT),
        name="pack_w",
    )(w)


def _store_head_rows(ref, val, tm):
    for h in range(N_HEADS):
        ref[pl.ds(h, tm, stride=N_HEADS), :] = val[:, h * HEAD_DIM:(h + 1) * HEAD_DIM].astype(ref.dtype)


def _proj_kernel(x_ref, nw_ref, w_ref, *out_refs, tm, sample):
    x = x_ref[...]
    var = jnp.mean(x * x, axis=-1, keepdims=True)
    h = (x * lax.rsqrt(var + NORM_EPS) * nw_ref[...]).astype(BF16)
    seg = lambda lo, hi: jnp.dot(h, w_ref[:, lo:hi], preferred_element_type=F32)
    if sample:
        qkv_ref, zg_ref, sm_ref, q4_ref, k4_ref, v4_ref, z4_ref = out_refs
    else:
        qkv_ref, zg_ref, sm_ref, qb_ref, kb_ref, vb_ref, zf_ref, k4_ref, v4_ref = out_refs
    qkv_ref[...] = seg(W_QKV, W_ZG)
    zg_ref[...] = seg(W_ZG, W_QF)
    sm_ref[...] = seg(W_SM, W_END)
    qf, kf, vf, zf = seg(W_QF, W_KF), seg(W_KF, W_VF), seg(W_VF, W_ZF), seg(W_ZF, W_SM)
    _store_head_rows(k4_ref, kf, tm)
    _store_head_rows(v4_ref, vf, tm)
    if sample:
        _store_head_rows(q4_ref, qf, tm)
        _store_head_rows(z4_ref, zf, tm)
    else:
        qb_ref[...] = (qf * (HEAD_DIM ** -0.5)).astype(BF16)
        kb_ref[...] = kf.astype(BF16)
        vb_ref[...] = vf.astype(BF16)
        zf_ref[...] = zf


def _proj_call(x2d, norm_w, w_big, tm, sample):
    t, d = x2d.shape
    n = w_big.shape[1]
    wide = lambda w, dt: (jax.ShapeDtypeStruct((t, w), dt), pl.BlockSpec((tm, w), lambda i: (i, 0)))
    rows4 = (jax.ShapeDtypeStruct((t * N_HEADS, HEAD_DIM), F32),
             pl.BlockSpec((tm * N_HEADS, HEAD_DIM), lambda i: (i, 0)))
    outs = [wide(CONV_DIM, F32), wide(GROUP_W, F32), wide(LANES, F32)]
    if sample:
        outs += [rows4, rows4, rows4, rows4]
    else:
        outs += [wide(GROUP_W, BF16), wide(GROUP_W, BF16), wide(GROUP_W, BF16), wide(GROUP_W, F32), rows4, rows4]
    out_shape = [o[0] for o in outs]
    out_specs = [o[1] for o in outs]
    return pl.pallas_call(
        functools.partial(_proj_kernel, tm=tm, sample=sample),
        grid=(t // tm,),
        in_specs=[pl.BlockSpec((tm, d), lambda i: (i, 0)),
                  pl.BlockSpec((1, d), lambda i: (0, 0)),
                  pl.BlockSpec((d, n), lambda i: (0, 0))],
        out_specs=out_specs,
        out_shape=out_shape,
        compiler_params=pltpu.CompilerParams(dimension_semantics=("arbitrary",),
                                             vmem_limit_bytes=VMEM_LIMIT),
        name="proj",
    )(x2d, norm_w, w_big)


def _gdn_kernel(qkv_ref, zg_ref, sm_ref, cw_ref, alog_ref, dtb_ref, onw_ref, s0_ref, c0_ref,
                og_ref, sout_ref, xbuf, s_scr, *, c, l_valid, nb):
    ci = pl.program_id(1)
    n_c = pl.num_programs(1)

    @pl.when(ci == 0)
    def _():
        xbuf[:, 0:SUBLANES, :] = c0_ref[...]
        s_scr[...] = s0_ref[...]

    row = _iota2((c, 1), 0) + ci * c
    valid = jnp.broadcast_to((row < l_valid).astype(F32), (c, LANES))
    tri_incl = (_iota2((c, c), 0) >= _iota2((c, c), 1))
    tri_strict = (_iota2((c, c), 0) > _iota2((c, c), 1))
    eye = (_iota2((c, c), 0) == _iota2((c, c), 1)).astype(F32)
    pad_rows = LANES - c
    sl = lambda base, h: slice(base + h * HEAD_DIM, base + (h + 1) * HEAD_DIM)

    q, k, v, beta, gc, gc_row, gc_last = [], [], [], [], [], [], []
    for bb in range(nb):
        xbuf[bb, SUBLANES:SUBLANES + c, :] = qkv_ref[bb]
        y = jnp.zeros((c, CONV_DIM), F32)
        for j in range(CONV_K):
            y = y + xbuf[bb, pl.ds(SUBLANES - (CONV_K - 1) + j, c), :] * cw_ref[j:j + 1, :]
        y = y * _sigmoid(y)
        xbuf[bb, 0:SUBLANES, :] = xbuf[bb, c:c + SUBLANES, :]
        sm = sm_ref[bb]
        beta_t = _sigmoid(sm) * valid
        g_t = -jnp.exp(alog_ref[...]) * _softplus(sm + dtb_ref[...]) * valid
        gc_t = _fdot(tri_incl.astype(F32), g_t)
        gc_sq = jnp.concatenate([gc_t, jnp.zeros((pad_rows, LANES), F32)], axis=0) if pad_rows else gc_t
        gc_tr = gc_sq.T
        for h in range(N_HEADS):
            qh, kh = y[:, sl(0, h)], y[:, sl(GROUP_W, h)]
            q.append(qh * lax.rsqrt(jnp.sum(qh * qh, axis=-1, keepdims=True) + L2_EPS) * (HEAD_DIM ** -0.5))
            k.append(kh * lax.rsqrt(jnp.sum(kh * kh, axis=-1, keepdims=True) + L2_EPS) * valid)
            v.append(y[:, sl(2 * GROUP_W, h)])
            beta.append(jnp.broadcast_to(beta_t[:, SM_BETA + h:SM_BETA + h + 1], (c, HEAD_DIM)))
            gc.append(jnp.broadcast_to(gc_t[:, SM_DECAY + h:SM_DECAY + h + 1], (c, HEAD_DIM)))
            gc_row.append(gc_tr[SM_DECAY + h:SM_DECAY + h + 1, 0:c])
            gc_last.append(jnp.broadcast_to(gc_t[c - 1:c, SM_DECAY + h:SM_DECAY + h + 1], (1, HEAD_DIM)))

    chains = range(nb * N_HEADS)
    decay = [jnp.where(tri_incl, jnp.exp(jnp.where(tri_incl, gc[i][:, 0:c] - gc_row[i], 0.0)), 0.0)
             for i in chains]
    kb = [k[i] * beta[i] for i in chains]
    kkqk = [_bdot_nt(jnp.concatenate([kb[i], q[i]], axis=0), k[i]) for i in chains]
    qk = [kkqk[i][c:2 * c] * decay[i] for i in chains]
    neg_a = [-jnp.where(tri_strict, kkqk[i][0:c] * decay[i], 0.0) for i in chains]
    t_inv = [eye + neg_a[i] for i in chains]
    pw = [_bdot(neg_a[i], neg_a[i]) for i in chains]
    n_sq = int(math.log2(c))
    for j in range(1, n_sq):
        if j < n_sq - 1:
            both = [_bdot(jnp.concatenate([t_inv[i], pw[i]], axis=0), pw[i]) for i in chains]
            t_inv = [t_inv[i] + both[i][0:c] for i in chains]
            pw = [both[i][c:2 * c] for i in chains]
        else:
            t_inv = [t_inv[i] + _bdot(t_inv[i], pw[i]) for i in chains]
    egc = [jnp.exp(gc[i]) for i in chains]
    sol = [_bdot(t_inv[i], jnp.concatenate([v[i] * beta[i], kb[i] * egc[i]], axis=-1)) for i in chains]
    s = [s_scr[i // N_HEADS, i % N_HEADS] for i in chains]
    ws = [_bdot(jnp.concatenate([sol[i][:, HEAD_DIM:2 * HEAD_DIM], q[i] * egc[i]], axis=0), s[i])
          for i in chains]
    v_new = [sol[i][:, 0:HEAD_DIM] - ws[i][0:c] for i in chains]
    o = [ws[i][c:2 * c] + _bdot(qk[i], v_new[i]) for i in chains]
    k_dec = [k[i] * jnp.exp(gc_last[i] - gc[i]) for i in chains]
    s_new = [s[i] * jnp.exp(gc_last[i]) + _bdot_tn(k_dec[i], v_new[i]) for i in chains]
    for i in chains:
        bb, h = i // N_HEADS, i % N_HEADS
        s_scr[bb, h] = s_new[i]
        oh = o[i] * lax.rsqrt(jnp.mean(o[i] * o[i], axis=-1, keepdims=True) + NORM_EPS) * onw_ref[...]
        z = zg_ref[bb, :, sl(0, h)]
        og_ref[bb, :, sl(0, h)] = (oh * (z * _sigmoid(z))).astype(og_ref.dtype)

    @pl.when(ci == n_c - 1)
    def _():
        sout_ref[...] = s_scr[...]


def _gdn_call(qkv, zg, sm, conv_w, alog_row, dtb_row, onw, s0, c0, *, c, l_valid, nb):
    b, l, _ = qkv.shape
    n_c = l // c
    kern = functools.partial(_gdn_kernel, c=c, l_valid=l_valid, nb=nb)
    blk = lambda w: pl.BlockSpec((nb, c, w), lambda bi, ci: (bi, ci, 0))
    full = lambda shape: pl.BlockSpec(shape, lambda bi, ci: (0,) * len(shape))
    state = pl.BlockSpec((nb, N_HEADS, HEAD_DIM, HEAD_DIM), lambda bi, ci: (bi, 0, 0, 0))
    return pl.pallas_call(
        kern,
        grid=(b // nb, n_c),
        in_specs=[blk(CONV_DIM), blk(GROUP_W), blk(LANES),
                  full((CONV_K, CONV_DIM)), full((1, LANES)), full((1, LANES)), full((1, HEAD_DIM)),
                  state, pl.BlockSpec((nb, SUBLANES, CONV_DIM), lambda bi, ci: (bi, 0, 0))],
        out_specs=[blk(GROUP_W), state],
        out_shape=[jax.ShapeDtypeStruct((b, l, GROUP_W), BF16),
                   jax.ShapeDtypeStruct((b, N_HEADS, HEAD_DIM, HEAD_DIM), F32)],
        scratch_shapes=[pltpu.VMEM((nb, c + SUBLANES, CONV_DIM), F32),
                        pltpu.VMEM((nb, N_HEADS, HEAD_DIM, HEAD_DIM), F32)],
        compiler_params=pltpu.CompilerParams(dimension_semantics=("arbitrary", "arbitrary"),
                                             vmem_limit_bytes=VMEM_LIMIT),
        name="gdn",
    )(qkv, zg, sm, conv_w, alog_row, dtb_row, onw, s0, c0)


def _log_sigmoid(x):
    return -_softplus(-x)


def _fox_gates_kernel(sm_ref, fb_ref, logf_ref, fcol_ref, *, l):
    blk = LANES
    tri = (_iota2((blk, blk), 0) >= _iota2((blk, blk), 1)).astype(F32)
    carry = jnp.zeros((1, LANES), F32)
    for i in range(l // blk):
        lf = _log_sigmoid(sm_ref[i * blk:(i + 1) * blk, :] + fb_ref[...])
        f = _fdot(tri, lf) + carry
        carry = f[blk - 1:blk, :]
        logf_ref[i * blk:(i + 1) * blk, :] = lf
        fcol_ref[i * blk:(i + 1) * blk, :] = f


def _fox_gates_call(sm, fb_row):
    b, l, _ = sm.shape
    kern = functools.partial(_fox_gates_kernel, l=l)
    seq = pl.BlockSpec((None, l, LANES), lambda i: (i, 0, 0))
    return pl.pallas_call(
        kern,
        grid=(b,),
        in_specs=[seq, pl.BlockSpec((1, LANES), lambda i: (0, 0))],
        out_specs=[seq, seq],
        out_shape=[jax.ShapeDtypeStruct((b, l, LANES), F32), jax.ShapeDtypeStruct((b, l, LANES), F32)],
        compiler_params=pltpu.CompilerParams(dimension_semantics=("arbitrary",),
                                             vmem_limit_bytes=VMEM_LIMIT),
        name="fox_gates",
    )(sm, fb_row)


NEG_BIG = -1e30


def _forget_columns(f_tile, h, rows, for_keys):
    f = jnp.broadcast_to(f_tile[:, SM_FORGET + h:SM_FORGET + h + 1], (rows, LANES))
    f1, f2, f3 = (t.astype(F32) for t in _split3(-f if for_keys else f))
    lane = _iota2((rows, LANES), 1)
    base = 3 if for_keys else 0
    ones = ((lane >= 3 - base) & (lane < 6 - base)).astype(F32)
    cols = jnp.where(lane == base, f1, jnp.where(lane == base + 1, f2, jnp.where(lane == base + 2, f3, ones)))
    return cols.astype(BF16)


ROW_GROUP = 32


def _fox_prompt_kernel(q_ref, k_ref, v_ref, fcol_ref, zf_ref, o_ref,
                       kx_ref, qa_ref, s_ref, p_ref, acc_ref, m_ref, a_ref, *, tq, l):
    qi = pl.program_id(1)
    heads = range(N_HEADS)
    sl = lambda h: slice(h * HEAD_DIM, (h + 1) * HEAD_DIM)
    nt = (((1,), (1,)), ((), ()))

    @pl.when(qi == 0)
    def _():
        for r in range(l // tq):
            for h in heads:
                kx_ref[r * tq:(r + 1) * tq, sl(h)] = _forget_columns(fcol_ref[r * tq:(r + 1) * tq, :], h, tq, True)

    f_q = fcol_ref[pl.ds(pl.multiple_of(qi * tq, tq), tq), :]
    for h in heads:
        qa_ref[h, :, 0:HEAD_DIM] = q_ref[:, sl(h)]
        qa_ref[h, :, HEAD_DIM:2 * HEAD_DIM] = _forget_columns(f_q, h, tq, False)
    acc_ref[...] = jnp.zeros(acc_ref.shape, F32)
    m_ref[...] = jnp.full(m_ref.shape, NEG_BIG, F32)
    ones = jnp.ones((tq, HEAD_DIM), BF16)

    def block(ki, masked):
        rows = pl.ds(pl.multiple_of(ki * tq, tq), tq)
        for h in heads:
            ka = jnp.concatenate([k_ref[rows, sl(h)], kx_ref[rows, sl(h)]], axis=1)
            s_ref[h] = lax.dot_general(qa_ref[h], ka, nt, preferred_element_type=F32)
        for h in heads:
            for r in range(0, tq, ROW_GROUP):
                rg = slice(r, r + ROW_GROUP)
                s = s_ref[h, rg, :]
                if masked:
                    keep = _iota2((ROW_GROUP, tq), 1) <= _iota2((ROW_GROUP, tq), 0) + r
                    s = jnp.where(keep, s, NEG_BIG)
                m_old = m_ref[h, rg, :]
                m_new = jnp.maximum(m_old, jnp.max(s, axis=-1, keepdims=True))
                a_ref[h, rg, :] = jnp.exp(m_old - m_new)
                m_ref[h, rg, :] = m_new
                p_ref[h, rg, :] = jnp.exp(s - jnp.concatenate([m_new] * (tq // LANES), axis=1)).astype(BF16)
        for h in heads:
            pv = jnp.dot(p_ref[h], jnp.concatenate([v_ref[rows, sl(h)], ones], axis=1),
                         preferred_element_type=F32)
            alpha = a_ref[h]
            acc_ref[h] = acc_ref[h] * jnp.concatenate([alpha, alpha], axis=1) + pv

    def body(ki, carry):
        block(ki, False)
        return carry

    lax.fori_loop(0, qi, body, 0)
    block(qi, True)
    for h in heads:
        z = zf_ref[:, sl(h)]
        o = acc_ref[h, :, 0:HEAD_DIM] / acc_ref[h, :, HEAD_DIM:2 * HEAD_DIM]
        o_ref[:, sl(h)] = (o * (z * _sigmoid(z))).astype(o_ref.dtype)


def _fox_prompt_call(qf, kf, vf, fcol, zf, tq):
    b, l, _ = qf.shape
    kern = functools.partial(_fox_prompt_kernel, tq=tq, l=l)
    qblk = lambda w: pl.BlockSpec((None, tq, w), lambda bi, qi: (bi, qi, 0))
    seq = lambda w: pl.BlockSpec((None, l, w), lambda bi, qi: (bi, 0, 0))
    return pl.pallas_call(
        kern,
        grid=(b, l // tq),
        in_specs=[qblk(GROUP_W), seq(GROUP_W), seq(GROUP_W), seq(LANES), qblk(GROUP_W)],
        out_specs=qblk(GROUP_W),
        out_shape=jax.ShapeDtypeStruct((b, l, GROUP_W), BF16),
        scratch_shapes=[pltpu.VMEM((l, GROUP_W), BF16),
                        pltpu.VMEM((N_HEADS, tq, 2 * HEAD_DIM), BF16),
                        pltpu.VMEM((N_HEADS, tq, tq), F32),
                        pltpu.VMEM((N_HEADS, tq, tq), BF16),
                        pltpu.VMEM((N_HEADS, tq, 2 * HEAD_DIM), F32),
                        pltpu.VMEM((N_HEADS, tq, LANES), F32),
                        pltpu.VMEM((N_HEADS, tq, LANES), F32)],
        compiler_params=pltpu.CompilerParams(dimension_semantics=("arbitrary", "arbitrary"),
                                             vmem_limit_bytes=VMEM_LIMIT),
        name="fox_prompt",
    )(qf, kf, vf, fcol, zf)


def _out_kernel(og_ref, of_ref, x_ref, w_ref, fnw_ref, y_ref):
    o = jnp.dot(og_ref[...], w_ref[0:GROUP_W, :], preferred_element_type=F32)
    o = o + jnp.dot(of_ref[...], w_ref[GROUP_W:2 * GROUP_W, :], preferred_element_type=F32)
    y = x_ref[...] + o
    var = jnp.mean(y * y, axis=-1, keepdims=True)
    y_ref[...] = y * lax.rsqrt(var + NORM_EPS) * fnw_ref[...]


def _out_call(og, of, x2d, w_out, fnw, tm):
    t, d = x2d.shape
    return pl.pallas_call(
        _out_kernel,
        grid=(t // tm,),
        in_specs=[pl.BlockSpec((tm, GROUP_W), lambda i: (i, 0)),
                  pl.BlockSpec((tm, GROUP_W), lambda i: (i, 0)),
                  pl.BlockSpec((tm, d), lambda i: (i, 0)),
                  pl.BlockSpec((2 * GROUP_W, d), lambda i: (0, 0)),
                  pl.BlockSpec((1, d), lambda i: (0, 0))],
        out_specs=pl.BlockSpec((tm, d), lambda i: (i, 0)),
        out_shape=jax.ShapeDtypeStruct((t, d), F32),
        compiler_params=pltpu.CompilerParams(dimension_semantics=("arbitrary",),
                                             vmem_limit_bytes=VMEM_LIMIT),
        name="out_proj",
    )(og, of, x2d, w_out, fnw)


def _page_copies(pt_ref, kc_ref, vc_ref, lc_ref, kbuf, vbuf, lbuf, sems, bi, slot, n_pages, pg):
    copies = []
    for p in range(n_pages):
        pid = pt_ref[bi, p]
        copies.append(pltpu.make_async_copy(kc_ref.at[pid], kbuf.at[slot, pl.ds(p * pg, pg)], sems.at[0, slot]))
        copies.append(pltpu.make_async_copy(vc_ref.at[pid], vbuf.at[slot, pl.ds(p * pg, pg)], sems.at[1, slot]))
        copies.append(pltpu.make_async_copy(lc_ref.at[pid], lbuf.at[slot, :, p, :], sems.at[2, slot]))
    return copies


def _split3(x):
    x1 = x.astype(BF16)
    r1 = x - x1.astype(F32)
    x2 = r1.astype(BF16)
    x3 = (r1 - x2.astype(F32)).astype(BF16)
    return x1, x2, x3


def _fox_decode_kernel(pt_ref, q_ref, kn_ref, vn_ref, zf_ref, sm_ref, fbrow_ref, cums_ref,
                       kc_ref, vc_ref, lc_ref, o_ref, logf_ref, kbuf, vbuf, lbuf, sems,
                       *, n_pages, pg, l_new):
    bi = pl.program_id(0)
    nb = pl.num_programs(0)
    slot = bi % 2
    copies = functools.partial(_page_copies, pt_ref, kc_ref, vc_ref, lc_ref, kbuf, vbuf, lbuf, sems,
                               n_pages=n_pages, pg=pg)
    nr = l_new * N_HEADS

    @pl.when(bi == 0)
    def _():
        for cp in copies(bi=bi, slot=slot):
            cp.start()

    @pl.when(bi + 1 < nb)
    def _():
        for cp in copies(bi=bi + 1, slot=1 - slot):
            cp.start()

    for cp in copies(bi=bi, slot=slot):
        cp.wait()

    scale = HEAD_DIM ** -0.5
    res = jnp.zeros((3 * n_pages, 2 * pg), F32)
    for h in range(N_HEADS):
        res = res + jnp.dot(jnp.concatenate(_split3(lbuf[slot, h]), axis=0), cums_ref[h],
                            preferred_element_type=F32)
    res = res[0:n_pages] + res[n_pages:2 * n_pages] + res[2 * n_pages:3 * n_pages]
    within, tot = res[:, 0:pg], res[:, pg:2 * pg]
    earlier = (_iota2((n_pages, n_pages), 0) > _iota2((n_pages, n_pages), 1)).astype(F32)
    carry = _fdot(earlier, tot)
    f_past = within + carry
    f_tot_row = carry[n_pages - 1:n_pages, :] + tot[n_pages - 1:n_pages, :]

    tok_valid = (_iota2((SUBLANES, 1), 0) < l_new).astype(F32)
    lf_col = _log_sigmoid(sm_ref[...] + fbrow_ref[...]) * tok_valid
    logf_ref[...] = lf_col
    r_tok = _iota2((nr, SUBLANES), 0) // N_HEADS
    csum = _fdot((_iota2((nr, SUBLANES), 1) <= r_tok).astype(F32), lf_col)
    own_lane = _iota2((nr, LANES), 1) == SM_FORGET + _iota2((nr, LANES), 0) % N_HEADS
    fq_new = jnp.sum(jnp.where(own_lane, csum, 0.0), axis=-1, keepdims=True)
    eye = _iota2((nr, nr), 0) == _iota2((nr, nr), 1)
    f_tot_col = jnp.sum(jnp.where(eye, jnp.broadcast_to(f_tot_row[:, 0:nr], (nr, nr)), 0.0),
                        axis=-1, keepdims=True)
    fq = fq_new + f_tot_col
    fq_row = jnp.sum(jnp.where(eye, jnp.broadcast_to(fq, (nr, nr)), 0.0), axis=0, keepdims=True)

    q = q_ref[...].astype(BF16)
    s_all = lax.dot_general(q, kbuf[slot].astype(BF16), (((1,), (1,)), ((), ())),
                            preferred_element_type=F32)
    same_head = (_iota2((nr, pg), 1) % N_HEADS) == (_iota2((nr, pg), 0) % N_HEADS)
    sp = [jnp.where(same_head, s_all[:, p * pg:(p + 1) * pg] * scale + (fq - f_past[p:p + 1, :]), NEG_BIG)
          for p in range(n_pages)]
    s_new = lax.dot_general(q, kn_ref[...].astype(BF16), (((1,), (1,)), ((), ())),
                            preferred_element_type=F32)
    rr, cc = _iota2((nr, nr), 0), _iota2((nr, nr), 1)
    new_ok = (rr % N_HEADS == cc % N_HEADS) & (cc // N_HEADS <= rr // N_HEADS)
    s_new = jnp.where(new_ok, s_new * scale + (fq - fq_row), NEG_BIG)
    m_el = sp[0]
    for p in range(1, n_pages):
        m_el = jnp.maximum(m_el, sp[p])
    m = jnp.maximum(jnp.max(m_el, axis=-1, keepdims=True), jnp.max(s_new, axis=-1, keepdims=True))
    pp = [jnp.exp(t - m) for t in sp]
    p_new = jnp.exp(s_new - m)
    l_el = pp[0]
    for p in range(1, n_pages):
        l_el = l_el + pp[p]
    l = jnp.sum(l_el, axis=-1, keepdims=True) + jnp.sum(p_new, axis=-1, keepdims=True)
    p_all = jnp.concatenate([t.astype(BF16) for t in pp], axis=-1)
    acc = jnp.dot(p_all, vbuf[slot].astype(BF16), preferred_element_type=F32)
    acc = acc + jnp.dot(p_new.astype(BF16), vn_ref[...].astype(BF16), preferred_element_type=F32)
    z = zf_ref[...]
    o_ref[...] = ((acc / l) * (z * _sigmoid(z))).astype(o_ref.dtype)


def _head_cumsum_matrix(page):
    t = jnp.arange(page)[None, :, None]
    j = jnp.arange(page * N_HEADS)[None, None, :]
    h = jnp.arange(N_HEADS)[:, None, None]
    own = (j % N_HEADS) == h
    c = own & (t <= j // N_HEADS)
    b = jnp.broadcast_to(own, c.shape)
    return jnp.concatenate([c, b], axis=2).astype(BF16)


def _fox_decode_call(page_table, q4, kn4, vn4, zf4, sm8, fb_row, kcache, vcache, lcache, l_new):
    b, n_pages = page_table.shape
    pg = kcache.shape[1]
    page = lcache.shape[2]
    nr = l_new * N_HEADS
    assert nr % (2 * SUBLANES) == 0 and l_new <= SUBLANES
    kern = functools.partial(_fox_decode_kernel, n_pages=n_pages, pg=pg, l_new=l_new)
    rows = pl.BlockSpec((nr, HEAD_DIM), lambda i, pt: (i, 0))
    tok = pl.BlockSpec((None, SUBLANES, LANES), lambda i, pt: (i, 0, 0))
    const = lambda shape: pl.BlockSpec(shape, lambda i, pt: (0,) * len(shape))
    any_spec = pl.BlockSpec(memory_space=pl.ANY)
    grid_spec = pltpu.PrefetchScalarGridSpec(
        num_scalar_prefetch=1,
        grid=(b,),
        in_specs=[rows, rows, rows, rows, tok, const((1, LANES)), const((N_HEADS, page, 2 * pg)),
                  any_spec, any_spec, any_spec],
        out_specs=[rows, tok],
        scratch_shapes=[pltpu.VMEM((2, n_pages * pg, HEAD_DIM), F32),
                        pltpu.VMEM((2, n_pages * pg, HEAD_DIM), F32),
                        pltpu.VMEM((2, N_HEADS, n_pages, page), F32),
                        pltpu.SemaphoreType.DMA((3, 2))],
    )
    return pl.pallas_call(
        kern,
        grid_spec=grid_spec,
        out_shape=[jax.ShapeDtypeStruct((b * nr, HEAD_DIM), BF16),
                   jax.ShapeDtypeStruct((b, SUBLANES, LANES), F32)],
        compiler_params=pltpu.CompilerParams(dimension_semantics=("arbitrary",),
                                             vmem_limit_bytes=VMEM_LIMIT),
        name="fox_decode",
    )(page_table, q4, kn4, vn4, zf4, sm8, fb_row, _head_cumsum_matrix(page), kcache, vcache, lcache)


def _gate_row(vals, offset):
    return jnp.zeros((1, LANES), F32).at[0, offset:offset + N_HEADS].set(vals.astype(F32))


def _pad_rows(t, rows):
    return jnp.pad(t, ((0, 0), (0, rows - t.shape[1]), (0, 0)))


def kernel(x_prompt, x_sample, cache_fox_k, cache_fox_v, cache_fox_logf, page_table, state_gdn_ssm,
           state_gdn_conv, w_in, gdn_conv_w, gdn_a_log, gdn_dt_bias, gdn_out_norm_w, fox_f_bias, w_out,
           norm_w, final_norm_w):
    bp, lp, d = x_prompt.shape
    bs, ls, _ = x_sample.shape
    depth = w_in.shape[0]
    assert depth == 1, "single-layer trunk"
    n_pool, page = cache_fox_k.shape[1], cache_fox_k.shape[2]

    w_big = _pack_w_call(w_in[0])
    w_o = w_out[0].astype(BF16)
    nw = norm_w[0].reshape(1, d)
    fnw = final_norm_w.reshape(1, d)
    conv_w = gdn_conv_w[0]
    alog_row = _gate_row(gdn_a_log[0], SM_DECAY)
    dtb_row = _gate_row(gdn_dt_bias[0], SM_DECAY)
    fb_row = _gate_row(fox_f_bias[0], SM_FORGET)
    onw = gdn_out_norm_w[0].reshape(1, HEAD_DIM)

    xp2 = x_prompt.reshape(bp * lp, d)
    qkv, zg, sm, qf, kf, vf, zf, k4, v4 = _proj_call(xp2, nw, w_big, tm=512, sample=False)
    r3 = lambda t: t.reshape(bp, lp, t.shape[-1])
    s0 = jnp.zeros((bp, N_HEADS, HEAD_DIM, HEAD_DIM), F32)
    c0 = jnp.zeros((bp, SUBLANES, CONV_DIM), F32)
    og_p, ssm_p = _gdn_call(r3(qkv), r3(zg), r3(sm), conv_w, alog_row, dtb_row, onw, s0, c0,
                            c=GDN_CHUNK, l_valid=lp, nb=4)
    logf_p, fcol = _fox_gates_call(r3(sm), fb_row)
    of_p = _fox_prompt_call(r3(qf), r3(kf), r3(vf), fcol, r3(zf), tq=512)
    y_p = _out_call(og_p.reshape(bp * lp, GROUP_W), of_p.reshape(bp * lp, GROUP_W), xp2, w_o, fnw, tm=512)

    y_prompt = y_p.reshape(bp, lp, d)
    k_prompt = k4.reshape(1, bp, lp, N_HEADS, HEAD_DIM)
    v_prompt = v4.reshape(1, bp, lp, N_HEADS, HEAD_DIM)
    logf_prompt = logf_p[:, :, SM_FORGET:SM_FORGET + N_HEADS].reshape(1, bp, lp, N_HEADS)
    ssm_prompt = ssm_p.reshape(1, bp, N_HEADS, HEAD_DIM, HEAD_DIM)
    conv_prompt = r3(qkv)[:, lp - (CONV_K - 1):, :].reshape(1, bp, CONV_K - 1, CONV_DIM)

    xs2 = x_sample.reshape(bs * ls, d)
    qkv_s, zg_s, sm_s, q4_s, k4_s, v4_s, z4_s = _proj_call(xs2, nw, w_big, tm=256, sample=True)
    r3s = lambda t: t.reshape(bs, ls, t.shape[-1])
    p8 = lambda t: _pad_rows(r3s(t), SUBLANES)
    c0_s = jnp.pad(state_gdn_conv[0], ((0, 0), (SUBLANES - (CONV_K - 1), 0), (0, 0)))
    og_s, ssm_s = _gdn_call(p8(qkv_s), p8(zg_s), p8(sm_s), conv_w, alog_row, dtb_row, onw,
                            state_gdn_ssm[0], c0_s, c=SUBLANES, l_valid=ls, nb=16)
    kcache = cache_fox_k[0].reshape(n_pool, page * N_HEADS, HEAD_DIM)
    vcache = cache_fox_v[0].reshape(n_pool, page * N_HEADS, HEAD_DIM)
    lcache = cache_fox_logf[0].transpose(0, 2, 1)
    of_s, logf_s = _fox_decode_call(page_table, q4_s, k4_s, v4_s, z4_s, p8(sm_s), fb_row,
                                    kcache, vcache, lcache, l_new=ls)
    og_s2 = og_s[:, :ls].reshape(bs * ls, GROUP_W)
    of_s2 = of_s.reshape(bs * ls, GROUP_W)
    y_s = _out_call(og_s2, of_s2, xs2, w_o, fnw, tm=256)

    y_sample = y_s.reshape(bs, ls, d)
    k_sample = k4_s.reshape(1, bs, ls, N_HEADS, HEAD_DIM)
    v_sample = v4_s.reshape(1, bs, ls, N_HEADS, HEAD_DIM)
    logf_sample = logf_s[:, :ls, SM_FORGET:SM_FORGET + N_HEADS].reshape(1, bs, ls, N_HEADS)
    ssm_sample = ssm_s.reshape(1, bs, N_HEADS, HEAD_DIM, HEAD_DIM)
    if ls >= CONV_K - 1:
        conv_sample = r3s(qkv_s)[:, ls - (CONV_K - 1):, :]
    else:
        conv_sample = jnp.concatenate([state_gdn_conv[0], r3s(qkv_s)], axis=1)[:, -(CONV_K - 1):, :]
    conv_sample = conv_sample.reshape(1, bs, CONV_K - 1, CONV_DIM)

    return (y_prompt, y_sample, k_prompt, v_prompt, logf_prompt, ssm_prompt, conv_prompt,
            k_sample, v_sample, logf_sample, ssm_sample, conv_sample)
```

```python
import functools
import math

import jax
import jax.numpy as jnp
from jax import lax
from jax.experimental import pallas as pl
from jax.experimental.pallas import tpu as pltpu

F32 = jnp.float32
BF16 = jnp.bfloat16

NORM_EPS = 1e-6
L2_EPS = 1e-6
HEAD_DIM = 128
N_HEADS = 4
GROUP_W = N_HEADS * HEAD_DIM
CONV_DIM = 3 * GROUP_W
CONV_K = 4
LANES = 128
SUBLANES = 8
GDN_CHUNK = 64
SM_BETA = 0
SM_DECAY = 4
SM_FORGET = 8
VMEM_LIMIT = 56 * 1024 * 1024


def _sigmoid(x):
    return 1.0 / (1.0 + jnp.exp(-x))


def _softplus(x):
    return jnp.maximum(x, 0.0) + jnp.log(1.0 + jnp.exp(-jnp.abs(x)))


def _bdot(a, b):
    return jnp.dot(a.astype(BF16), b.astype(BF16), preferred_element_type=F32)


def _bdot_nt(a, b):
    return lax.dot_general(a.astype(BF16), b.astype(BF16), (((1,), (1,)), ((), ())),
                           preferred_element_type=F32)


def _bdot_tn(a, b):
    return lax.dot_general(a.astype(BF16), b.astype(BF16), (((0,), (0,)), ((), ())),
                           preferred_element_type=F32)


def _fdot(a, b):
    return jnp.dot(a, b, preferred_element_type=F32, precision=lax.Precision.HIGHEST)


def _iota2(shape, dim):
    return lax.broadcasted_iota(jnp.int32, shape, dim)


W_QKV, W_ZG, W_QF, W_KF, W_VF, W_ZF, W_SM, W_END = 0, 1536, 2048, 2560, 3072, 3584, 4096, 4224
SRC_GATES_G, SRC_FOX, SRC_GATE_F, SRC_END = 2048, 2056, 4104, 4108


def _pack_w_kernel(w_ref, o_ref):
    o_ref[W_QKV:W_QF, :] = w_ref[0:SRC_GATES_G, :].astype(BF16)
    o_ref[W_QF:W_SM, :] = w_ref[SRC_FOX:SRC_GATE_F, :].astype(BF16)
    n_gate = (SRC_FOX - SRC_GATES_G) + (SRC_END - SRC_GATE_F)
    gates = jnp.concatenate([w_ref[SRC_GATES_G:SRC_FOX, :], w_ref[SRC_GATE_F:SRC_END, :],
                             jnp.zeros((W_END - W_SM - n_gate, w_ref.shape[1]), F32)], axis=0)
    o_ref[W_SM:W_END, :] = gates.astype(BF16)


def _pack_w_call(w_t):
    return pl.pallas_call(
        _pack_w_kernel,
        out_shape=jax.ShapeDtypeStruct((W_END, w_t.shape[1]), BF16),
        compiler_params=pltpu.CompilerParams(vmem_limit_bytes=VMEM_LIMIT),
        name="pack_w",
    )(w_t)


def _store_head_rows(ref, val, tm):
    for h in range(N_HEADS):
        ref[pl.ds(h, tm, stride=N_HEADS), :] = val[:, h * HEAD_DIM:(h + 1) * HEAD_DIM].astype(ref.dtype)


def _proj_kernel(x_ref, nw_ref, w_ref, *out_refs, tm, sample):
    x = x_ref[...]
    var = jnp.mean(x * x, axis=-1, keepdims=True)
    h = (x * lax.rsqrt(var + NORM_EPS) * nw_ref[...]).astype(BF16)
    seg = lambda lo, hi: lax.dot_general(h, w_ref[lo:hi, :], (((1,), (1,)), ((), ())),
                                         preferred_element_type=F32)
    if sample:
        qkv_ref, zg_ref, sm_ref, q4_ref, k4_ref, v4_ref, z4_ref = out_refs
    else:
        qkv_ref, zg_ref, sm_ref, qb_ref, kb_ref, vb_ref, zf_ref, k4_ref, v4_ref = out_refs
    qkv_ref[...] = seg(W_QKV, W_ZG)
    zg_ref[...] = seg(W_ZG, W_QF)
    sm_ref[...] = seg(W_SM, W_END)
    qf, kf, vf, zf = seg(W_QF, W_KF), seg(W_KF, W_VF), seg(W_VF, W_ZF), seg(W_ZF, W_SM)
    _store_head_rows(k4_ref, kf, tm)
    _store_head_rows(v4_ref, vf, tm)
    if sample:
        _store_head_rows(q4_ref, qf, tm)
        _store_head_rows(z4_ref, zf, tm)
    else:
        qb_ref[...] = (qf * (HEAD_DIM ** -0.5)).astype(BF16)
        kb_ref[...] = kf.astype(BF16)
        vb_ref[...] = vf.astype(BF16)
        zf_ref[...] = zf


def _proj_call(x2d, norm_w, w_big, tm, sample):
    t, d = x2d.shape
    n = w_big.shape[0]
    wide = lambda w, dt: (jax.ShapeDtypeStruct((t, w), dt), pl.BlockSpec((tm, w), lambda i: (i, 0)))
    rows4 = (jax.ShapeDtypeStruct((t * N_HEADS, HEAD_DIM), F32),
             pl.BlockSpec((tm * N_HEADS, HEAD_DIM), lambda i: (i, 0)))
    outs = [wide(CONV_DIM, F32), wide(GROUP_W, F32), wide(LANES, F32)]
    if sample:
        outs += [rows4, rows4, rows4, rows4]
    else:
        outs += [wide(GROUP_W, BF16), wide(GROUP_W, BF16), wide(GROUP_W, BF16), wide(GROUP_W, F32), rows4, rows4]
    out_shape = [o[0] for o in outs]
    out_specs = [o[1] for o in outs]
    return pl.pallas_call(
        functools.partial(_proj_kernel, tm=tm, sample=sample),
        grid=(t // tm,),
        in_specs=[pl.BlockSpec((tm, d), lambda i: (i, 0)),
                  pl.BlockSpec((1, d), lambda i: (0, 0)),
                  pl.BlockSpec((n, d), lambda i: (0, 0))],
        out_specs=out_specs,
        out_shape=out_shape,
        compiler_params=pltpu.CompilerParams(dimension_semantics=("arbitrary",),
                                             vmem_limit_bytes=VMEM_LIMIT),
        name="proj",
    )(x2d, norm_w, w_big)


def _gdn_kernel(qkv_ref, zg_ref, sm_ref, cw_ref, alog_ref, dtb_ref, onw_ref, s0_ref, c0_ref,
                og_ref, sout_ref, xbuf, s_scr, *, c, l_valid, nb):
    ci = pl.program_id(1)
    n_c = pl.num_programs(1)

    @pl.when(ci == 0)
    def _():
        xbuf[:, 0:SUBLANES, :] = c0_ref[...]
        s_scr[...] = s0_ref[...]

    row = _iota2((c, 1), 0) + ci * c
    valid = jnp.broadcast_to((row < l_valid).astype(F32), (c, LANES))
    tri_incl = (_iota2((c, c), 0) >= _iota2((c, c), 1))
    tri_strict = (_iota2((c, c), 0) > _iota2((c, c), 1))
    eye = (_iota2((c, c), 0) == _iota2((c, c), 1)).astype(F32)
    pad_rows = LANES - c
    sl = lambda base, h: slice(base + h * HEAD_DIM, base + (h + 1) * HEAD_DIM)

    q, k, v, beta, gc, gc_row, gc_last = [], [], [], [], [], [], []
    for bb in range(nb):
        xbuf[bb, SUBLANES:SUBLANES + c, :] = qkv_ref[bb]
        y = jnp.zeros((c, CONV_DIM), F32)
        for j in range(CONV_K):
            y = y + xbuf[bb, pl.ds(SUBLANES - (CONV_K - 1) + j, c), :] * cw_ref[j:j + 1, :]
        y = y * _sigmoid(y)
        xbuf[bb, 0:SUBLANES, :] = xbuf[bb, c:c + SUBLANES, :]
        sm = sm_ref[bb]
        beta_t = _sigmoid(sm) * valid
        g_t = -jnp.exp(alog_ref[...]) * _softplus(sm + dtb_ref[...]) * valid
        gc_t = _fdot(tri_incl.astype(F32), g_t)
        gc_sq = jnp.concatenate([gc_t, jnp.zeros((pad_rows, LANES), F32)], axis=0) if pad_rows else gc_t
        gc_tr = gc_sq.T
        for h in range(N_HEADS):
            qh, kh = y[:, sl(0, h)], y[:, sl(GROUP_W, h)]
            q.append(qh * lax.rsqrt(jnp.sum(qh * qh, axis=-1, keepdims=True) + L2_EPS) * (HEAD_DIM ** -0.5))
            k.append(kh * lax.rsqrt(jnp.sum(kh * kh, axis=-1, keepdims=True) + L2_EPS) * valid)
            v.append(y[:, sl(2 * GROUP_W, h)])
            beta.append(jnp.broadcast_to(beta_t[:, SM_BETA + h:SM_BETA + h + 1], (c, HEAD_DIM)))
            gc.append(jnp.broadcast_to(gc_t[:, SM_DECAY + h:SM_DECAY + h + 1], (c, HEAD_DIM)))
            gc_row.append(gc_tr[SM_DECAY + h:SM_DECAY + h + 1, 0:c])
            gc_last.append(jnp.broadcast_to(gc_t[c - 1:c, SM_DECAY + h:SM_DECAY + h + 1], (1, HEAD_DIM)))

    chains = range(nb * N_HEADS)
    decay = [jnp.where(tri_incl, jnp.exp(jnp.where(tri_incl, gc[i][:, 0:c] - gc_row[i], 0.0)), 0.0)
             for i in chains]
    kb = [k[i] * beta[i] for i in chains]
    kkqk = [_bdot_nt(jnp.concatenate([kb[i], q[i]], axis=0), k[i]) for i in chains]
    qk = [kkqk[i][c:2 * c] * decay[i] for i in chains]
    neg_a = [-jnp.where(tri_strict, kkqk[i][0:c] * decay[i], 0.0) for i in chains]
    t_inv = [eye + neg_a[i] for i in chains]
    pw = [_bdot(neg_a[i], neg_a[i]) for i in chains]
    n_sq = int(math.log2(c))
    for j in range(1, n_sq):
        if j < n_sq - 1:
            both = [_bdot(jnp.concatenate([t_inv[i], pw[i]], axis=0), pw[i]) for i in chains]
            t_inv = [t_inv[i] + both[i][0:c] for i in chains]
            pw = [both[i][c:2 * c] for i in chains]
        else:
            t_inv = [t_inv[i] + _bdot(t_inv[i], pw[i]) for i in chains]
    egc = [jnp.exp(gc[i]) for i in chains]
    sol = [_bdot(t_inv[i], jnp.concatenate([v[i] * beta[i], kb[i] * egc[i]], axis=-1)) for i in chains]
    s = [s_scr[i // N_HEADS, i % N_HEADS] for i in chains]
    ws = [_bdot(jnp.concatenate([sol[i][:, HEAD_DIM:2 * HEAD_DIM], q[i] * egc[i]], axis=0), s[i])
          for i in chains]
    v_new = [sol[i][:, 0:HEAD_DIM] - ws[i][0:c] for i in chains]
    o = [ws[i][c:2 * c] + _bdot(qk[i], v_new[i]) for i in chains]
    k_dec = [k[i] * jnp.exp(gc_last[i] - gc[i]) for i in chains]
    s_new = [s[i] * jnp.exp(gc_last[i]) + _bdot_tn(k_dec[i], v_new[i]) for i in chains]
    for i in chains:
        bb, h = i // N_HEADS, i % N_HEADS
        s_scr[bb, h] = s_new[i]
        oh = o[i] * lax.rsqrt(jnp.mean(o[i] * o[i], axis=-1, keepdims=True) + NORM_EPS) * onw_ref[...]
        z = zg_ref[bb, :, sl(0, h)]
        og_ref[bb, :, sl(0, h)] = (oh * (z * _sigmoid(z))).astype(og_ref.dtype)

    @pl.when(ci == n_c - 1)
    def _():
        sout_ref[...] = s_scr[...]


def _gdn_call(qkv, zg, sm, conv_w, alog_row, dtb_row, onw, s0, c0, *, c, l_valid, nb):
    b, l, _ = qkv.shape
    n_c = l // c
    kern = functools.partial(_gdn_kernel, c=c, l_valid=l_valid, nb=nb)
    blk = lambda w: pl.BlockSpec((nb, c, w), lambda bi, ci: (bi, ci, 0))
    full = lambda shape: pl.BlockSpec(shape, lambda bi, ci: (0,) * len(shape))
    state = pl.BlockSpec((nb, N_HEADS, HEAD_DIM, HEAD_DIM), lambda bi, ci: (bi, 0, 0, 0))
    return pl.pallas_call(
        kern,
        grid=(b // nb, n_c),
        in_specs=[blk(CONV_DIM), blk(GROUP_W), blk(LANES),
                  full((CONV_K, CONV_DIM)), full((1, LANES)), full((1, LANES)), full((1, HEAD_DIM)),
                  state, pl.BlockSpec((nb, SUBLANES, CONV_DIM), lambda bi, ci: (bi, 0, 0))],
        out_specs=[blk(GROUP_W), state],
        out_shape=[jax.ShapeDtypeStruct((b, l, GROUP_W), BF16),
                   jax.ShapeDtypeStruct((b, N_HEADS, HEAD_DIM, HEAD_DIM), F32)],
        scratch_shapes=[pltpu.VMEM((nb, c + SUBLANES, CONV_DIM), F32),
                        pltpu.VMEM((nb, N_HEADS, HEAD_DIM, HEAD_DIM), F32)],
        compiler_params=pltpu.CompilerParams(dimension_semantics=("arbitrary", "arbitrary"),
                                             vmem_limit_bytes=VMEM_LIMIT),
        name="gdn",
    )(qkv, zg, sm, conv_w, alog_row, dtb_row, onw, s0, c0)


def _log_sigmoid(x):
    return -_softplus(-x)


def _fox_gates_kernel(sm_ref, fb_ref, logf_ref, fcol_ref, *, l):
    blk = LANES
    tri = (_iota2((blk, blk), 0) >= _iota2((blk, blk), 1)).astype(F32)
    carry = jnp.zeros((1, LANES), F32)
    for i in range(l // blk):
        lf = _log_sigmoid(sm_ref[i * blk:(i + 1) * blk, :] + fb_ref[...])
        f = _fdot(tri, lf) + carry
        carry = f[blk - 1:blk, :]
        logf_ref[i * blk:(i + 1) * blk, :] = lf
        fcol_ref[i * blk:(i + 1) * blk, :] = f


def _fox_gates_call(sm, fb_row):
    b, l, _ = sm.shape
    kern = functools.partial(_fox_gates_kernel, l=l)
    seq = pl.BlockSpec((None, l, LANES), lambda i: (i, 0, 0))
    return pl.pallas_call(
        kern,
        grid=(b,),
        in_specs=[seq, pl.BlockSpec((1, LANES), lambda i: (0, 0))],
        out_specs=[seq, seq],
        out_shape=[jax.ShapeDtypeStruct((b, l, LANES), F32), jax.ShapeDtypeStruct((b, l, LANES), F32)],
        compiler_params=pltpu.CompilerParams(dimension_semantics=("arbitrary",),
                                             vmem_limit_bytes=VMEM_LIMIT),
        name="fox_gates",
    )(sm, fb_row)


NEG_BIG = -1e30


def _forget_columns(f_tile, h, rows, for_keys):
    f = jnp.broadcast_to(f_tile[:, SM_FORGET + h:SM_FORGET + h + 1], (rows, LANES))
    f1, f2, f3 = (t.astype(F32) for t in _split3(-f if for_keys else f))
    lane = _iota2((rows, LANES), 1)
    base = 3 if for_keys else 0
    ones = ((lane >= 3 - base) & (lane < 6 - base)).astype(F32)
    cols = jnp.where(lane == base, f1, jnp.where(lane == base + 1, f2, jnp.where(lane == base + 2, f3, ones)))
    return cols.astype(BF16)


ROW_GROUP = 32


def _fox_prompt_kernel(q_ref, k_ref, v_ref, fcol_ref, zf_ref, o_ref,
                       kx_ref, qa_ref, s_ref, p_ref, acc_ref, m_ref, a_ref, *, tq, l):
    qi = pl.program_id(1)
    heads = range(N_HEADS)
    sl = lambda h: slice(h * HEAD_DIM, (h + 1) * HEAD_DIM)
    nt = (((1,), (1,)), ((), ()))

    @pl.when(qi == 0)
    def _():
        for r in range(l // tq):
            for h in heads:
                kx_ref[r * tq:(r + 1) * tq, sl(h)] = _forget_columns(fcol_ref[r * tq:(r + 1) * tq, :], h, tq, True)

    f_q = fcol_ref[pl.ds(pl.multiple_of(qi * tq, tq), tq), :]
    for h in heads:
        qa_ref[h, :, 0:HEAD_DIM] = q_ref[:, sl(h)]
        qa_ref[h, :, HEAD_DIM:2 * HEAD_DIM] = _forget_columns(f_q, h, tq, False)
    acc_ref[...] = jnp.zeros(acc_ref.shape, F32)
    m_ref[...] = jnp.full(m_ref.shape, NEG_BIG, F32)
    ones = jnp.ones((tq, HEAD_DIM), BF16)

    def block(ki, masked):
        rows = pl.ds(pl.multiple_of(ki * tq, tq), tq)
        for h in heads:
            ka = jnp.concatenate([k_ref[rows, sl(h)], kx_ref[rows, sl(h)]], axis=1)
            s_ref[h] = lax.dot_general(qa_ref[h], ka, nt, preferred_element_type=F32)
        for h in heads:
            for r in range(0, tq, ROW_GROUP):
                rg = slice(r, r + ROW_GROUP)
                s = s_ref[h, rg, :]
                if masked:
                    keep = _iota2((ROW_GROUP, tq), 1) <= _iota2((ROW_GROUP, tq), 0) + r
                    s = jnp.where(keep, s, NEG_BIG)
                m_old = m_ref[h, rg, :]
                m_new = jnp.maximum(m_old, jnp.max(s, axis=-1, keepdims=True))
                a_ref[h, rg, :] = jnp.exp(m_old - m_new)
                m_ref[h, rg, :] = m_new
                p_ref[h, rg, :] = jnp.exp(s - jnp.concatenate([m_new] * (tq // LANES), axis=1)).astype(BF16)
        for h in heads:
            pv = jnp.dot(p_ref[h], jnp.concatenate([v_ref[rows, sl(h)], ones], axis=1),
                         preferred_element_type=F32)
            alpha = a_ref[h]
            acc_ref[h] = acc_ref[h] * jnp.concatenate([alpha, alpha], axis=1) + pv

    def body(ki, carry):
        block(ki, False)
        return carry

    lax.fori_loop(0, qi, body, 0)
    block(qi, True)
    for h in heads:
        z = zf_ref[:, sl(h)]
        o = acc_ref[h, :, 0:HEAD_DIM] / acc_ref[h, :, HEAD_DIM:2 * HEAD_DIM]
        o_ref[:, sl(h)] = (o * (z * _sigmoid(z))).astype(o_ref.dtype)


def _fox_prompt_call(qf, kf, vf, fcol, zf, tq):
    b, l, _ = qf.shape
    kern = functools.partial(_fox_prompt_kernel, tq=tq, l=l)
    qblk = lambda w: pl.BlockSpec((None, tq, w), lambda bi, qi: (bi, qi, 0))
    seq = lambda w: pl.BlockSpec((None, l, w), lambda bi, qi: (bi, 0, 0))
    return pl.pallas_call(
        kern,
        grid=(b, l // tq),
        in_specs=[qblk(GROUP_W), seq(GROUP_W), seq(GROUP_W), seq(LANES), qblk(GROUP_W)],
        out_specs=qblk(GROUP_W),
        out_shape=jax.ShapeDtypeStruct((b, l, GROUP_W), BF16),
        scratch_shapes=[pltpu.VMEM((l, GROUP_W), BF16),
                        pltpu.VMEM((N_HEADS, tq, 2 * HEAD_DIM), BF16),
                        pltpu.VMEM((N_HEADS, tq, tq), F32),
                        pltpu.VMEM((N_HEADS, tq, tq), BF16),
                        pltpu.VMEM((N_HEADS, tq, 2 * HEAD_DIM), F32),
                        pltpu.VMEM((N_HEADS, tq, LANES), F32),
                        pltpu.VMEM((N_HEADS, tq, LANES), F32)],
        compiler_params=pltpu.CompilerParams(dimension_semantics=("arbitrary", "arbitrary"),
                                             vmem_limit_bytes=VMEM_LIMIT),
        name="fox_prompt",
    )(qf, kf, vf, fcol, zf)


def _out_kernel(og_ref, of_ref, x_ref, w_ref, fnw_ref, y_ref):
    o = jnp.dot(og_ref[...], w_ref[0:GROUP_W, :], preferred_element_type=F32)
    o = o + jnp.dot(of_ref[...], w_ref[GROUP_W:2 * GROUP_W, :], preferred_element_type=F32)
    y = x_ref[...] + o
    var = jnp.mean(y * y, axis=-1, keepdims=True)
    y_ref[...] = y * lax.rsqrt(var + NORM_EPS) * fnw_ref[...]


def _out_call(og, of, x2d, w_out, fnw, tm):
    t, d = x2d.shape
    return pl.pallas_call(
        _out_kernel,
        grid=(t // tm,),
        in_specs=[pl.BlockSpec((tm, GROUP_W), lambda i: (i, 0)),
                  pl.BlockSpec((tm, GROUP_W), lambda i: (i, 0)),
                  pl.BlockSpec((tm, d), lambda i: (i, 0)),
                  pl.BlockSpec((2 * GROUP_W, d), lambda i: (0, 0)),
                  pl.BlockSpec((1, d), lambda i: (0, 0))],
        out_specs=pl.BlockSpec((tm, d), lambda i: (i, 0)),
        out_shape=jax.ShapeDtypeStruct((t, d), F32),
        compiler_params=pltpu.CompilerParams(dimension_semantics=("arbitrary",),
                                             vmem_limit_bytes=VMEM_LIMIT),
        name="out_proj",
    )(og, of, x2d, w_out, fnw)


def _page_copies(pt_ref, kc_ref, vc_ref, lc_ref, kbuf, vbuf, lbuf, sems, bi, slot, n_pages, pg):
    copies = []
    for p in range(n_pages):
        pid = pt_ref[bi, p]
        copies.append(pltpu.make_async_copy(kc_ref.at[pid], kbuf.at[slot, pl.ds(p * pg, pg)], sems.at[0, slot]))
        copies.append(pltpu.make_async_copy(vc_ref.at[pid], vbuf.at[slot, pl.ds(p * pg, pg)], sems.at[1, slot]))
        copies.append(pltpu.make_async_copy(lc_ref.at[pid], lbuf.at[slot, :, p, :], sems.at[2, slot]))
    return copies


def _split3(x):
    x1 = x.astype(BF16)
    r1 = x - x1.astype(F32)
    x2 = r1.astype(BF16)
    x3 = (r1 - x2.astype(F32)).astype(BF16)
    return x1, x2, x3


def _fox_decode_kernel(pt_ref, q_ref, kn_ref, vn_ref, zf_ref, sm_ref, fbrow_ref, cums_ref,
                       kc_ref, vc_ref, lc_ref, o_ref, logf_ref, kbuf, vbuf, lbuf, sems,
                       *, n_pages, pg, l_new):
    bi = pl.program_id(0)
    nb = pl.num_programs(0)
    slot = bi % 2
    copies = functools.partial(_page_copies, pt_ref, kc_ref, vc_ref, lc_ref, kbuf, vbuf, lbuf, sems,
                               n_pages=n_pages, pg=pg)
    nr = l_new * N_HEADS

    @pl.when(bi == 0)
    def _():
        for cp in copies(bi=bi, slot=slot):
            cp.start()

    @pl.when(bi + 1 < nb)
    def _():
        for cp in copies(bi=bi + 1, slot=1 - slot):
            cp.start()

    for cp in copies(bi=bi, slot=slot):
        cp.wait()

    scale = HEAD_DIM ** -0.5
    res = jnp.zeros((3 * n_pages, 2 * pg), F32)
    for h in range(N_HEADS):
        res = res + jnp.dot(jnp.concatenate(_split3(lbuf[slot, h]), axis=0), cums_ref[h],
                            preferred_element_type=F32)
    res = res[0:n_pages] + res[n_pages:2 * n_pages] + res[2 * n_pages:3 * n_pages]
    within, tot = res[:, 0:pg], res[:, pg:2 * pg]
    earlier = (_iota2((n_pages, n_pages), 0) > _iota2((n_pages, n_pages), 1)).astype(F32)
    carry = _fdot(earlier, tot)
    f_past = within + carry
    f_tot_row = carry[n_pages - 1:n_pages, :] + tot[n_pages - 1:n_pages, :]

    tok_valid = (_iota2((SUBLANES, 1), 0) < l_new).astype(F32)
    lf_col = _log_sigmoid(sm_ref[...] + fbrow_ref[...]) * tok_valid
    logf_ref[...] = lf_col
    r_tok = _iota2((nr, SUBLANES), 0) // N_HEADS
    csum = _fdot((_iota2((nr, SUBLANES), 1) <= r_tok).astype(F32), lf_col)
    own_lane = _iota2((nr, LANES), 1) == SM_FORGET + _iota2((nr, LANES), 0) % N_HEADS
    fq_new = jnp.sum(jnp.where(own_lane, csum, 0.0), axis=-1, keepdims=True)
    eye = _iota2((nr, nr), 0) == _iota2((nr, nr), 1)
    f_tot_col = jnp.sum(jnp.where(eye, jnp.broadcast_to(f_tot_row[:, 0:nr], (nr, nr)), 0.0),
                        axis=-1, keepdims=True)
    fq = fq_new + f_tot_col
    fq_row = jnp.sum(jnp.where(eye, jnp.broadcast_to(fq, (nr, nr)), 0.0), axis=0, keepdims=True)

    q = q_ref[...].astype(BF16)
    s_all = lax.dot_general(q, kbuf[slot].astype(BF16), (((1,), (1,)), ((), ())),
                            preferred_element_type=F32)
    same_head = (_iota2((nr, pg), 1) % N_HEADS) == (_iota2((nr, pg), 0) % N_HEADS)
    sp = [jnp.where(same_head, s_all[:, p * pg:(p + 1) * pg] * scale + (fq - f_past[p:p + 1, :]), NEG_BIG)
          for p in range(n_pages)]
    s_new = lax.dot_general(q, kn_ref[...].astype(BF16), (((1,), (1,)), ((), ())),
                            preferred_element_type=F32)
    rr, cc = _iota2((nr, nr), 0), _iota2((nr, nr), 1)
    new_ok = (rr % N_HEADS == cc % N_HEADS) & (cc // N_HEADS <= rr // N_HEADS)
    s_new = jnp.where(new_ok, s_new * scale + (fq - fq_row), NEG_BIG)
    m_el = sp[0]
    for p in range(1, n_pages):
        m_el = jnp.maximum(m_el, sp[p])
    m = jnp.maximum(jnp.max(m_el, axis=-1, keepdims=True), jnp.max(s_new, axis=-1, keepdims=True))
    pp = [jnp.exp(t - m) for t in sp]
    p_new = jnp.exp(s_new - m)
    l_el = pp[0]
    for p in range(1, n_pages):
        l_el = l_el + pp[p]
    l = jnp.sum(l_el, axis=-1, keepdims=True) + jnp.sum(p_new, axis=-1, keepdims=True)
    p_all = jnp.concatenate([t.astype(BF16) for t in pp], axis=-1)
    acc = jnp.dot(p_all, vbuf[slot].astype(BF16), preferred_element_type=F32)
    acc = acc + jnp.dot(p_new.astype(BF16), vn_ref[...].astype(BF16), preferred_element_type=F32)
    z = zf_ref[...]
    o_ref[...] = ((acc / l) * (z * _sigmoid(z))).astype(o_ref.dtype)


def _head_cumsum_matrix(page):
    t = jnp.arange(page)[None, :, None]
    j = jnp.arange(page * N_HEADS)[None, None, :]
    h = jnp.arange(N_HEADS)[:, None, None]
    own = (j % N_HEADS) == h
    c = own & (t <= j // N_HEADS)
    b = jnp.broadcast_to(own, c.shape)
    return jnp.concatenate([c, b], axis=2).astype(BF16)


def _fox_decode_call(page_table, q4, kn4, vn4, zf4, sm8, fb_row, kcache, vcache, lcache, l_new):
    b, n_pages = page_table.shape
    pg = kcache.shape[1]
    page = lcache.shape[2]
    nr = l_new * N_HEADS
    assert nr % (2 * SUBLANES) == 0 and l_new <= SUBLANES
    kern = functools.partial(_fox_decode_kernel, n_pages=n_pages, pg=pg, l_new=l_new)
    rows = pl.BlockSpec((nr, HEAD_DIM), lambda i, pt: (i, 0))
    tok = pl.BlockSpec((None, SUBLANES, LANES), lambda i, pt: (i, 0, 0))
    const = lambda shape: pl.BlockSpec(shape, lambda i, pt: (0,) * len(shape))
    any_spec = pl.BlockSpec(memory_space=pl.ANY)
    grid_spec = pltpu.PrefetchScalarGridSpec(
        num_scalar_prefetch=1,
        grid=(b,),
        in_specs=[rows, rows, rows, rows, tok, const((1, LANES)), const((N_HEADS, page, 2 * pg)),
                  any_spec, any_spec, any_spec],
        out_specs=[rows, tok],
        scratch_shapes=[pltpu.VMEM((2, n_pages * pg, HEAD_DIM), F32),
                        pltpu.VMEM((2, n_pages * pg, HEAD_DIM), F32),
                        pltpu.VMEM((2, N_HEADS, n_pages, page), F32),
                        pltpu.SemaphoreType.DMA((3, 2))],
    )
    return pl.pallas_call(
        kern,
        grid_spec=grid_spec,
        out_shape=[jax.ShapeDtypeStruct((b * nr, HEAD_DIM), BF16),
                   jax.ShapeDtypeStruct((b, SUBLANES, LANES), F32)],
        compiler_params=pltpu.CompilerParams(dimension_semantics=("arbitrary",),
                                             vmem_limit_bytes=VMEM_LIMIT),
        name="fox_decode",
    )(page_table, q4, kn4, vn4, zf4, sm8, fb_row, _head_cumsum_matrix(page), kcache, vcache, lcache)


def _gate_row(vals, offset):
    return jnp.zeros((1, LANES), F32).at[0, offset:offset + N_HEADS].set(vals.astype(F32))


def _pad_rows(t, rows):
    return jnp.pad(t, ((0, 0), (0, rows - t.shape[1]), (0, 0)))


def kernel(x_prompt, x_sample, cache_fox_k, cache_fox_v, cache_fox_logf, page_table, state_gdn_ssm,
           state_gdn_conv, w_in, gdn_conv_w, gdn_a_log, gdn_dt_bias, gdn_out_norm_w, fox_f_bias, w_out,
           norm_w, final_norm_w):
    bp, lp, d = x_prompt.shape
    bs, ls, _ = x_sample.shape
    depth = w_in.shape[0]
    assert depth == 1, "single-layer trunk"
    n_pool, page = cache_fox_k.shape[1], cache_fox_k.shape[2]

    w_big = _pack_w_call(w_in[0].T)
    w_o = w_out[0].astype(BF16)
    nw = norm_w[0].reshape(1, d)
    fnw = final_norm_w.reshape(1, d)
    conv_w = gdn_conv_w[0]
    alog_row = _gate_row(gdn_a_log[0], SM_DECAY)
    dtb_row = _gate_row(gdn_dt_bias[0], SM_DECAY)
    fb_row = _gate_row(fox_f_bias[0], SM_FORGET)
    onw = gdn_out_norm_w[0].reshape(1, HEAD_DIM)

    xp2 = x_prompt.reshape(bp * lp, d)
    qkv, zg, sm, qf, kf, vf, zf, k4, v4 = _proj_call(xp2, nw, w_big, tm=512, sample=False)
    r3 = lambda t: t.reshape(bp, lp, t.shape[-1])
    s0 = jnp.zeros((bp, N_HEADS, HEAD_DIM, HEAD_DIM), F32)
    c0 = jnp.zeros((bp, SUBLANES, CONV_DIM), F32)
    og_p, ssm_p = _gdn_call(r3(qkv), r3(zg), r3(sm), conv_w, alog_row, dtb_row, onw, s0, c0,
                            c=GDN_CHUNK, l_valid=lp, nb=4)
    logf_p, fcol = _fox_gates_call(r3(sm), fb_row)
    of_p = _fox_prompt_call(r3(qf), r3(kf), r3(vf), fcol, r3(zf), tq=512)
    y_p = _out_call(og_p.reshape(bp * lp, GROUP_W), of_p.reshape(bp * lp, GROUP_W), xp2, w_o, fnw, tm=512)

    y_prompt = y_p.reshape(bp, lp, d)
    k_prompt = k4.reshape(1, bp, lp, N_HEADS, HEAD_DIM)
    v_prompt = v4.reshape(1, bp, lp, N_HEADS, HEAD_DIM)
    logf_prompt = logf_p[:, :, SM_FORGET:SM_FORGET + N_HEADS].reshape(1, bp, lp, N_HEADS)
    ssm_prompt = ssm_p.reshape(1, bp, N_HEADS, HEAD_DIM, HEAD_DIM)
    conv_prompt = r3(qkv)[:, lp - (CONV_K - 1):, :].reshape(1, bp, CONV_K - 1, CONV_DIM)

    xs2 = x_sample.reshape(bs * ls, d)
    qkv_s, zg_s, sm_s, q4_s, k4_s, v4_s, z4_s = _proj_call(xs2, nw, w_big, tm=256, sample=True)
    r3s = lambda t: t.reshape(bs, ls, t.shape[-1])
    p8 = lambda t: _pad_rows(r3s(t), SUBLANES)
    c0_s = jnp.pad(state_gdn_conv[0], ((0, 0), (SUBLANES - (CONV_K - 1), 0), (0, 0)))
    og_s, ssm_s = _gdn_call(p8(qkv_s), p8(zg_s), p8(sm_s), conv_w, alog_row, dtb_row, onw,
                            state_gdn_ssm[0], c0_s, c=SUBLANES, l_valid=ls, nb=16)
    kcache = cache_fox_k[0].reshape(n_pool, page * N_HEADS, HEAD_DIM)
    vcache = cache_fox_v[0].reshape(n_pool, page * N_HEADS, HEAD_DIM)
    lcache = cache_fox_logf[0].transpose(0, 2, 1)
    of_s, logf_s = _fox_decode_call(page_table, q4_s, k4_s, v4_s, z4_s, p8(sm_s), fb_row,
                                    kcache, vcache, lcache, l_new=ls)
    og_s2 = og_s[:, :ls].reshape(bs * ls, GROUP_W)
    of_s2 = of_s.reshape(bs * ls, GROUP_W)
    y_s = _out_call(og_s2, of_s2, xs2, w_o, fnw, tm=256)

    y_sample = y_s.reshape(bs, ls, d)
    k_sample = k4_s.reshape(1, bs, ls, N_HEADS, HEAD_DIM)
    v_sample = v4_s.reshape(1, bs, ls, N_HEADS, HEAD_DIM)
    logf_sample = logf_s[:, :ls, SM_FORGET:SM_FORGET + N_HEADS].reshape(1, bs, ls, N_HEADS)
    ssm_sample = ssm_s.reshape(1, bs, N_HEADS, HEAD_DIM, HEAD_DIM)
    if ls >= CONV_K - 1:
        conv_sample = r3s(qkv_s)[:, ls - (CONV_K - 1):, :]
    else:
        conv_sample = jnp.concatenate([state_gdn_conv[0], r3s(qkv_s)], axis=1)[:, -(CONV_K - 1):, :]
    conv_sample = conv_sample.reshape(1, bs, CONV_K - 1, CONV_DIM)

    return (y_prompt, y_sample, k_prompt, v_prompt, logf_prompt, ssm_prompt, conv_prompt,
            k_sample, v_sample, logf_sample, ssm_sample, conv_sample)
```

```python
import functools
import math

import jax
import jax.numpy as jnp
from jax import lax
from jax.experimental import pallas as pl
from jax.experimental.pallas import tpu as pltpu

F32 = jnp.float32
BF16 = jnp.bfloat16

NORM_EPS = 1e-6
L2_EPS = 1e-6
HEAD_DIM = 128
N_HEADS = 4
GROUP_W = N_HEADS * HEAD_DIM
CONV_DIM = 3 * GROUP_W
CONV_K = 4
LANES = 128
SUBLANES = 8
GDN_CHUNK = 64
SM_BETA = 0
SM_DECAY = 4
SM_FORGET = 8
VMEM_LIMIT = 56 * 1024 * 1024


def _sigmoid(x):
    return 1.0 / (1.0 + jnp.exp(-x))


def _softplus(x):
    return jnp.maximum(x, 0.0) + jnp.log(1.0 + jnp.exp(-jnp.abs(x)))


def _bdot(a, b):
    return jnp.dot(a.astype(BF16), b.astype(BF16), preferred_element_type=F32)


def _bdot_nt(a, b):
    return lax.dot_general(a.astype(BF16), b.astype(BF16), (((1,), (1,)), ((), ())),
                           preferred_element_type=F32)


def _bdot_tn(a, b):
    return lax.dot_general(a.astype(BF16), b.astype(BF16), (((0,), (0,)), ((), ())),
                           preferred_element_type=F32)


def _fdot(a, b):
    return jnp.dot(a, b, preferred_element_type=F32, precision=lax.Precision.HIGHEST)


def _iota2(shape, dim):
    return lax.broadcasted_iota(jnp.int32, shape, dim)


W_QKV, W_ZG, W_QF, W_KF, W_VF, W_ZF, W_SM, W_END = 0, 1536, 2048, 2560, 3072, 3584, 4096, 4224
SRC_GATES_G, SRC_FOX, SRC_GATE_F, SRC_END = 2048, 2056, 4104, 4108


def _pack_w_kernel(w_ref, o_ref):
    o_ref[W_QKV:W_QF, :] = w_ref[0:SRC_GATES_G, :].astype(BF16)
    o_ref[W_QF:W_SM, :] = w_ref[SRC_FOX:SRC_GATE_F, :].astype(BF16)
    n_gate = (SRC_FOX - SRC_GATES_G) + (SRC_END - SRC_GATE_F)
    gates = jnp.concatenate([w_ref[SRC_GATES_G:SRC_FOX, :], w_ref[SRC_GATE_F:SRC_END, :],
                             jnp.zeros((W_END - W_SM - n_gate, w_ref.shape[1]), F32)], axis=0)
    o_ref[W_SM:W_END, :] = gates.astype(BF16)


def _pack_w_call(w_t):
    return pl.pallas_call(
        _pack_w_kernel,
        out_shape=jax.ShapeDtypeStruct((W_END, w_t.shape[1]), BF16),
        compiler_params=pltpu.CompilerParams(vmem_limit_bytes=VMEM_LIMIT),
        name="pack_w",
    )(w_t)


def _store_head_rows(ref, val, tm):
    for h in range(N_HEADS):
        ref[pl.ds(h, tm, stride=N_HEADS), :] = val[:, h * HEAD_DIM:(h + 1) * HEAD_DIM].astype(ref.dtype)


def _conv_silu_qkv(xbuf, cw_ref, g, rows):
    cols = slice(g * GROUP_W, (g + 1) * GROUP_W)
    base = SUBLANES - (CONV_K - 1)
    y = xbuf[pl.ds(base, rows), cols] * cw_ref[0:1, cols]
    for j in range(1, CONV_K):
        y = y + xbuf[pl.ds(base + j, rows), cols] * cw_ref[j:j + 1, cols]
    return y * _sigmoid(y)


def _l2_normalize(t, scale):
    return t * (lax.rsqrt(jnp.sum(t * t, axis=-1, keepdims=True) + L2_EPS) * scale)


def _proj_kernel(x_ref, nw_ref, w_ref, *refs, tm, sample):
    x = x_ref[...]
    var = jnp.mean(x * x, axis=-1, keepdims=True)
    h = (x * lax.rsqrt(var + NORM_EPS) * nw_ref[...]).astype(BF16)
    seg = lambda lo, hi: lax.dot_general(h, w_ref[lo:hi, :], (((1,), (1,)), ((), ())),
                                         preferred_element_type=F32)
    if sample:
        qkv_ref, zg_ref, sm_ref, q4_ref, k4_ref, v4_ref, z4_ref = refs
        qkv_ref[...] = seg(W_QKV, W_ZG)
        zg_ref[...] = seg(W_ZG, W_QF)
        sm_ref[...] = seg(W_SM, W_END)
        _store_head_rows(q4_ref, seg(W_QF, W_KF), tm)
        _store_head_rows(k4_ref, seg(W_KF, W_VF), tm)
        _store_head_rows(v4_ref, seg(W_VF, W_ZF), tm)
        _store_head_rows(z4_ref, seg(W_ZF, W_SM), tm)
        return
    h_ref, zg_ref, sm_ref, qb_ref, kb_ref, vb_ref, zf_ref, k4_ref, v4_ref = refs
    h_ref[...] = h
    zg_ref[...] = seg(W_ZG, W_QF)
    sm_ref[...] = seg(W_SM, W_END)
    qb_ref[...] = (seg(W_QF, W_KF) * (HEAD_DIM ** -0.5)).astype(BF16)
    kf = seg(W_KF, W_VF)
    _store_head_rows(k4_ref, kf, tm)
    kb_ref[...] = kf.astype(BF16)
    vf = seg(W_VF, W_ZF)
    _store_head_rows(v4_ref, vf, tm)
    vb_ref[...] = vf.astype(BF16)
    zf_ref[...] = seg(W_ZF, W_SM)


def _proj_call(x2d, norm_w, w_big, tm, sample):
    t, d = x2d.shape
    n = w_big.shape[0]
    wide = lambda w, dt: (jax.ShapeDtypeStruct((t, w), dt), pl.BlockSpec((tm, w), lambda i: (i, 0)))
    rows4 = (jax.ShapeDtypeStruct((t * N_HEADS, HEAD_DIM), F32),
             pl.BlockSpec((tm * N_HEADS, HEAD_DIM), lambda i: (i, 0)))
    if sample:
        outs = [wide(CONV_DIM, F32), wide(GROUP_W, F32), wide(LANES, F32), rows4, rows4, rows4, rows4]
    else:
        outs = [wide(d, BF16), wide(GROUP_W, F32), wide(LANES, F32), wide(GROUP_W, BF16), wide(GROUP_W, BF16),
                wide(GROUP_W, BF16), wide(GROUP_W, F32), rows4, rows4]
    out_shape = [o[0] for o in outs]
    out_specs = [o[1] for o in outs]
    return pl.pallas_call(
        functools.partial(_proj_kernel, tm=tm, sample=sample),
        grid=(t // tm,),
        in_specs=[pl.BlockSpec((tm, d), lambda i: (i, 0)),
                  pl.BlockSpec((1, d), lambda i: (0, 0)),
                  pl.BlockSpec((n, d), lambda i: (0, 0))],
        out_specs=out_specs,
        out_shape=out_shape,
        compiler_params=pltpu.CompilerParams(dimension_semantics=("arbitrary",),
                                             vmem_limit_bytes=VMEM_LIMIT),
        name="proj",
    )(x2d, norm_w, w_big)


def _gdn_kernel(*refs, c, l_valid, nb, project):
    if project:
        (h_ref, wqkv_ref, zg_ref, sm_ref, cw_ref, alog_ref, dtb_ref, onw_ref,
         og_ref, sout_ref, tail_ref, xbuf, s_scr) = refs
    else:
        (qkv_ref, zg_ref, sm_ref, cw_ref, alog_ref, dtb_ref, onw_ref, s0_ref, c0_ref,
         og_ref, sout_ref, xbuf, s_scr) = refs
    ci = pl.program_id(1)
    n_c = pl.num_programs(1)

    @pl.when(ci == 0)
    def _():
        if project:
            xbuf[:, 0:SUBLANES, :] = jnp.zeros((nb, SUBLANES, CONV_DIM), F32)
            s_scr[...] = jnp.zeros(s_scr.shape, F32)
        else:
            xbuf[:, 0:SUBLANES, :] = c0_ref[...]
            s_scr[...] = s0_ref[...]

    row = _iota2((c, 1), 0) + ci * c
    valid = jnp.broadcast_to((row < l_valid).astype(F32), (c, LANES))
    tri_incl = (_iota2((c, c), 0) >= _iota2((c, c), 1))
    tri_strict = (_iota2((c, c), 0) > _iota2((c, c), 1))
    eye = (_iota2((c, c), 0) == _iota2((c, c), 1)).astype(F32)
    pad_rows = LANES - c
    sl = lambda base, h: slice(base + h * HEAD_DIM, base + (h + 1) * HEAD_DIM)

    if project:
        raw = lax.dot_general(h_ref[...].reshape(nb * c, h_ref.shape[-1]), wqkv_ref[...],
                              (((1,), (1,)), ((), ())), preferred_element_type=F32)
    q, k, v, beta, gc, gc_row, gc_last = [], [], [], [], [], [], []
    for bb in range(nb):
        xbuf[bb, SUBLANES:SUBLANES + c, :] = raw[bb * c:(bb + 1) * c] if project else qkv_ref[bb]
        yq, yk, yv = (_conv_silu_qkv(xbuf.at[bb], cw_ref, g, c) for g in range(3))
        if project:
            tail_ref[bb] = xbuf[bb, c:c + SUBLANES, :]
        xbuf[bb, 0:SUBLANES, :] = xbuf[bb, c:c + SUBLANES, :]
        sm = sm_ref[bb]
        beta_t = _sigmoid(sm) * valid
        g_t = -jnp.exp(alog_ref[...]) * _softplus(sm + dtb_ref[...]) * valid
        gc_t = _fdot(tri_incl.astype(F32), g_t)
        gc_sq = jnp.concatenate([gc_t, jnp.zeros((pad_rows, LANES), F32)], axis=0) if pad_rows else gc_t
        gc_tr = gc_sq.T
        for h in range(N_HEADS):
            q.append(_l2_normalize(yq[:, sl(0, h)], HEAD_DIM ** -0.5))
            k.append(_l2_normalize(yk[:, sl(0, h)], 1.0) * valid)
            v.append(yv[:, sl(0, h)])
            beta.append(jnp.broadcast_to(beta_t[:, SM_BETA + h:SM_BETA + h + 1], (c, HEAD_DIM)))
            gc.append(jnp.broadcast_to(gc_t[:, SM_DECAY + h:SM_DECAY + h + 1], (c, HEAD_DIM)))
            gc_row.append(gc_tr[SM_DECAY + h:SM_DECAY + h + 1, 0:c])
            gc_last.append(jnp.broadcast_to(gc_t[c - 1:c, SM_DECAY + h:SM_DECAY + h + 1], (1, HEAD_DIM)))

    chains = range(nb * N_HEADS)
    decay = [jnp.where(tri_incl, jnp.exp(jnp.where(tri_incl, gc[i][:, 0:c] - gc_row[i], 0.0)), 0.0)
             for i in chains]
    kb = [k[i] * beta[i] for i in chains]
    kkqk = [_bdot_nt(jnp.concatenate([kb[i], q[i]], axis=0), k[i]) for i in chains]
    qk = [kkqk[i][c:2 * c] * decay[i] for i in chains]
    neg_a = [-jnp.where(tri_strict, kkqk[i][0:c] * decay[i], 0.0) for i in chains]
    t_inv = [eye + neg_a[i] for i in chains]
    pw = [_bdot(neg_a[i], neg_a[i]) for i in chains]
    n_sq = int(math.log2(c))
    for j in range(1, n_sq):
        if j < n_sq - 1:
            both = [_bdot(jnp.concatenate([t_inv[i], pw[i]], axis=0), pw[i]) for i in chains]
            t_inv = [t_inv[i] + both[i][0:c] for i in chains]
            pw = [both[i][c:2 * c] for i in chains]
        else:
            t_inv = [t_inv[i] + _bdot(t_inv[i], pw[i]) for i in chains]
    egc = [jnp.exp(gc[i]) for i in chains]
    sol = [_bdot(t_inv[i], jnp.concatenate([v[i] * beta[i], kb[i] * egc[i]], axis=-1)) for i in chains]
    s = [s_scr[i // N_HEADS, i % N_HEADS] for i in chains]
    ws = [_bdot(jnp.concatenate([sol[i][:, HEAD_DIM:2 * HEAD_DIM], q[i] * egc[i]], axis=0), s[i])
          for i in chains]
    v_new = [sol[i][:, 0:HEAD_DIM] - ws[i][0:c] for i in chains]
    o = [ws[i][c:2 * c] + _bdot(qk[i], v_new[i]) for i in chains]
    k_dec = [k[i] * jnp.exp(gc_last[i] - gc[i]) for i in chains]
    s_new = [s[i] * jnp.exp(gc_last[i]) + _bdot_tn(k_dec[i], v_new[i]) for i in chains]
    for i in chains:
        bb, h = i // N_HEADS, i % N_HEADS
        s_scr[bb, h] = s_new[i]
        oh = o[i] * lax.rsqrt(jnp.mean(o[i] * o[i], axis=-1, keepdims=True) + NORM_EPS) * onw_ref[...]
        z = zg_ref[bb, :, sl(0, h)]
        og_ref[bb, :, sl(0, h)] = (oh * (z * _sigmoid(z))).astype(og_ref.dtype)

    @pl.when(ci == n_c - 1)
    def _():
        sout_ref[...] = s_scr[...]


def _gdn_call(src, zg, sm, conv_w, alog_row, dtb_row, onw, *, c, l_valid, nb, w_big=None, s0=None, c0=None):
    project = w_big is not None
    b, l, _ = zg.shape
    n_c = l // c
    assert not project or l_valid == l
    kern = functools.partial(_gdn_kernel, c=c, l_valid=l_valid, nb=nb, project=project)
    blk = lambda w: pl.BlockSpec((nb, c, w), lambda bi, ci: (bi, ci, 0))
    full = lambda shape: pl.BlockSpec(shape, lambda bi, ci: (0,) * len(shape))
    state = pl.BlockSpec((nb, N_HEADS, HEAD_DIM, HEAD_DIM), lambda bi, ci: (bi, 0, 0, 0))
    rows8 = pl.BlockSpec((nb, SUBLANES, CONV_DIM), lambda bi, ci: (bi, 0, 0))
    common = [blk(GROUP_W), blk(LANES), full((CONV_K, CONV_DIM)), full((1, LANES)), full((1, LANES)),
              full((1, HEAD_DIM))]
    out_specs = [blk(GROUP_W), state]
    out_shape = [jax.ShapeDtypeStruct((b, l, GROUP_W), BF16),
                 jax.ShapeDtypeStruct((b, N_HEADS, HEAD_DIM, HEAD_DIM), F32)]
    if project:
        d = src.shape[-1]
        operands = (src, w_big, zg, sm, conv_w, alog_row, dtb_row, onw)
        in_specs = [blk(d), pl.BlockSpec((CONV_DIM, d), lambda bi, ci: (0, 0))] + common
        out_specs.append(rows8)
        out_shape.append(jax.ShapeDtypeStruct((b, SUBLANES, CONV_DIM), F32))
    else:
        operands = (src, zg, sm, conv_w, alog_row, dtb_row, onw, s0, c0)
        in_specs = [blk(CONV_DIM)] + common + [state, rows8]
    return pl.pallas_call(
        kern,
        grid=(b // nb, n_c),
        in_specs=in_specs,
        out_specs=out_specs,
        out_shape=out_shape,
        scratch_shapes=[pltpu.VMEM((nb, c + SUBLANES, CONV_DIM), F32),
                        pltpu.VMEM((nb, N_HEADS, HEAD_DIM, HEAD_DIM), F32)],
        compiler_params=pltpu.CompilerParams(dimension_semantics=("arbitrary", "arbitrary"),
                                             vmem_limit_bytes=VMEM_LIMIT),
        name="gdn",
    )(*operands)


def _log_sigmoid(x):
    return -_softplus(-x)


def _fox_gates_kernel(sm_ref, fb_ref, logf_ref, fcol_ref, *, l):
    blk = LANES
    tri = (_iota2((blk, blk), 0) >= _iota2((blk, blk), 1)).astype(F32)
    carry = jnp.zeros((1, LANES), F32)
    for i in range(l // blk):
        lf = _log_sigmoid(sm_ref[i * blk:(i + 1) * blk, :] + fb_ref[...])
        f = _fdot(tri, lf) + carry
        carry = f[blk - 1:blk, :]
        logf_ref[i * blk:(i + 1) * blk, :] = lf
        fcol_ref[i * blk:(i + 1) * blk, :] = f


def _fox_gates_call(sm, fb_row):
    b, l, _ = sm.shape
    kern = functools.partial(_fox_gates_kernel, l=l)
    seq = pl.BlockSpec((None, l, LANES), lambda i: (i, 0, 0))
    return pl.pallas_call(
        kern,
        grid=(b,),
        in_specs=[seq, pl.BlockSpec((1, LANES), lambda i: (0, 0))],
        out_specs=[seq, seq],
        out_shape=[jax.ShapeDtypeStruct((b, l, LANES), F32), jax.ShapeDtypeStruct((b, l, LANES), F32)],
        compiler_params=pltpu.CompilerParams(dimension_semantics=("arbitrary",),
                                             vmem_limit_bytes=VMEM_LIMIT),
        name="fox_gates",
    )(sm, fb_row)


NEG_BIG = -1e30


def _forget_columns(f_tile, h, rows, for_keys):
    f = jnp.broadcast_to(f_tile[:, SM_FORGET + h:SM_FORGET + h + 1], (rows, LANES))
    f1, f2, f3 = (t.astype(F32) for t in _split3(-f if for_keys else f))
    lane = _iota2((rows, LANES), 1)
    base = 3 if for_keys else 0
    ones = ((lane >= 3 - base) & (lane < 6 - base)).astype(F32)
    cols = jnp.where(lane == base, f1, jnp.where(lane == base + 1, f2, jnp.where(lane == base + 2, f3, ones)))
    return cols.astype(BF16)


ROW_GROUP = 32


def _fox_prompt_kernel(q_ref, k_ref, v_ref, fcol_ref, zf_ref, o_ref,
                       kx_ref, qa_ref, s_ref, p_ref, acc_ref, m_ref, a_ref, *, tq, l):
    qi = pl.program_id(1)
    heads = range(N_HEADS)
    sl = lambda h: slice(h * HEAD_DIM, (h + 1) * HEAD_DIM)
    nt = (((1,), (1,)), ((), ()))

    @pl.when(qi == 0)
    def _():
        for r in range(l // tq):
            for h in heads:
                kx_ref[r * tq:(r + 1) * tq, sl(h)] = _forget_columns(fcol_ref[r * tq:(r + 1) * tq, :], h, tq, True)

    f_q = fcol_ref[pl.ds(pl.multiple_of(qi * tq, tq), tq), :]
    for h in heads:
        qa_ref[h, :, 0:HEAD_DIM] = q_ref[:, sl(h)]
        qa_ref[h, :, HEAD_DIM:2 * HEAD_DIM] = _forget_columns(f_q, h, tq, False)
    acc_ref[...] = jnp.zeros(acc_ref.shape, F32)
    m_ref[...] = jnp.full(m_ref.shape, NEG_BIG, F32)
    ones = jnp.ones((tq, HEAD_DIM), BF16)

    def block(ki, masked):
        rows = pl.ds(pl.multiple_of(ki * tq, tq), tq)
        for h in heads:
            ka = jnp.concatenate([k_ref[rows, sl(h)], kx_ref[rows, sl(h)]], axis=1)
            s_ref[h] = lax.dot_general(qa_ref[h], ka, nt, preferred_element_type=F32)
        for h in heads:
            for r in range(0, tq, ROW_GROUP):
                rg = slice(r, r + ROW_GROUP)
                s = s_ref[h, rg, :]
                if masked:
                    keep = _iota2((ROW_GROUP, tq), 1) <= _iota2((ROW_GROUP, tq), 0) + r
                    s = jnp.where(keep, s, NEG_BIG)
                m_old = m_ref[h, rg, :]
                m_new = jnp.maximum(m_old, jnp.max(s, axis=-1, keepdims=True))
                a_ref[h, rg, :] = jnp.exp(m_old - m_new)
                m_ref[h, rg, :] = m_new
                p_ref[h, rg, :] = jnp.exp(s - jnp.concatenate([m_new] * (tq // LANES), axis=1)).astype(BF16)
        for h in heads:
            pv = jnp.dot(p_ref[h], jnp.concatenate([v_ref[rows, sl(h)], ones], axis=1),
                         preferred_element_type=F32)
            alpha = a_ref[h]
            acc_ref[h] = acc_ref[h] * jnp.concatenate([alpha, alpha], axis=1) + pv

    def body(ki, carry):
        block(ki, False)
        return carry

    lax.fori_loop(0, qi, body, 0)
    block(qi, True)
    for h in heads:
        z = zf_ref[:, sl(h)]
        o = acc_ref[h, :, 0:HEAD_DIM] / acc_ref[h, :, HEAD_DIM:2 * HEAD_DIM]
        o_ref[:, sl(h)] = (o * (z * _sigmoid(z))).astype(o_ref.dtype)


def _fox_prompt_call(qf, kf, vf, fcol, zf, tq):
    b, l, _ = qf.shape
    kern = functools.partial(_fox_prompt_kernel, tq=tq, l=l)
    qblk = lambda w: pl.BlockSpec((None, tq, w), lambda bi, qi: (bi, qi, 0))
    seq = lambda w: pl.BlockSpec((None, l, w), lambda bi, qi: (bi, 0, 0))
    return pl.pallas_call(
        kern,
        grid=(b, l // tq),
        in_specs=[qblk(GROUP_W), seq(GROUP_W), seq(GROUP_W), seq(LANES), qblk(GROUP_W)],
        out_specs=qblk(GROUP_W),
        out_shape=jax.ShapeDtypeStruct((b, l, GROUP_W), BF16),
        scratch_shapes=[pltpu.VMEM((l, GROUP_W), BF16),
                        pltpu.VMEM((N_HEADS, tq, 2 * HEAD_DIM), BF16),
                        pltpu.VMEM((N_HEADS, tq, tq), F32),
                        pltpu.VMEM((N_HEADS, tq, tq), BF16),
                        pltpu.VMEM((N_HEADS, tq, 2 * HEAD_DIM), F32),
                        pltpu.VMEM((N_HEADS, tq, LANES), F32),
                        pltpu.VMEM((N_HEADS, tq, LANES), F32)],
        compiler_params=pltpu.CompilerParams(dimension_semantics=("arbitrary", "arbitrary"),
                                             vmem_limit_bytes=VMEM_LIMIT),
        name="fox_prompt",
    )(qf, kf, vf, fcol, zf)


def _out_kernel(og_ref, of_ref, x_ref, w_ref, fnw_ref, y_ref):
    o = jnp.dot(og_ref[...], w_ref[0:GROUP_W, :], preferred_element_type=F32)
    o = o + jnp.dot(of_ref[...], w_ref[GROUP_W:2 * GROUP_W, :], preferred_element_type=F32)
    y = x_ref[...] + o
    var = jnp.mean(y * y, axis=-1, keepdims=True)
    y_ref[...] = y * lax.rsqrt(var + NORM_EPS) * fnw_ref[...]


def _out_call(og, of, x2d, w_out, fnw, tm):
    t, d = x2d.shape
    return pl.pallas_call(
        _out_kernel,
        grid=(t // tm,),
        in_specs=[pl.BlockSpec((tm, GROUP_W), lambda i: (i, 0)),
                  pl.BlockSpec((tm, GROUP_W), lambda i: (i, 0)),
                  pl.BlockSpec((tm, d), lambda i: (i, 0)),
                  pl.BlockSpec((2 * GROUP_W, d), lambda i: (0, 0)),
                  pl.BlockSpec((1, d), lambda i: (0, 0))],
        out_specs=pl.BlockSpec((tm, d), lambda i: (i, 0)),
        out_shape=jax.ShapeDtypeStruct((t, d), F32),
        compiler_params=pltpu.CompilerParams(dimension_semantics=("arbitrary",),
                                             vmem_limit_bytes=VMEM_LIMIT),
        name="out_proj",
    )(og, of, x2d, w_out, fnw)


def _page_copies(pt_ref, kc_ref, vc_ref, lc_ref, kbuf, vbuf, lbuf, sems, bi, slot, n_pages, pg):
    copies = []
    for p in range(n_pages):
        pid = pt_ref[bi, p]
        copies.append(pltpu.make_async_copy(kc_ref.at[pid], kbuf.at[slot, pl.ds(p * pg, pg)], sems.at[0, slot]))
        copies.append(pltpu.make_async_copy(vc_ref.at[pid], vbuf.at[slot, pl.ds(p * pg, pg)], sems.at[1, slot]))
        copies.append(pltpu.make_async_copy(lc_ref.at[pid], lbuf.at[slot, :, p, :], sems.at[2, slot]))
    return copies


def _split3(x):
    x1 = x.astype(BF16)
    r1 = x - x1.astype(F32)
    x2 = r1.astype(BF16)
    x3 = (r1 - x2.astype(F32)).astype(BF16)
    return x1, x2, x3


def _fox_decode_kernel(pt_ref, q_ref, kn_ref, vn_ref, zf_ref, sm_ref, fbrow_ref, cums_ref,
                       kc_ref, vc_ref, lc_ref, o_ref, logf_ref, kbuf, vbuf, lbuf, sems,
                       *, n_pages, pg, l_new):
    bi = pl.program_id(0)
    nb = pl.num_programs(0)
    slot = bi % 2
    copies = functools.partial(_page_copies, pt_ref, kc_ref, vc_ref, lc_ref, kbuf, vbuf, lbuf, sems,
                               n_pages=n_pages, pg=pg)
    nr = l_new * N_HEADS

    @pl.when(bi == 0)
    def _():
        for cp in copies(bi=bi, slot=slot):
            cp.start()

    @pl.when(bi + 1 < nb)
    def _():
        for cp in copies(bi=bi + 1, slot=1 - slot):
            cp.start()

    for cp in copies(bi=bi, slot=slot):
        cp.wait()

    scale = HEAD_DIM ** -0.5
    res = jnp.zeros((3 * n_pages, 2 * pg), F32)
    for h in range(N_HEADS):
        res = res + jnp.dot(jnp.concatenate(_split3(lbuf[slot, h]), axis=0), cums_ref[h],
                            preferred_element_type=F32)
    res = res[0:n_pages] + res[n_pages:2 * n_pages] + res[2 * n_pages:3 * n_pages]
    within, tot = res[:, 0:pg], res[:, pg:2 * pg]
    earlier = (_iota2((n_pages, n_pages), 0) > _iota2((n_pages, n_pages), 1)).astype(F32)
    carry = _fdot(earlier, tot)
    f_past = within + carry
    f_tot_row = carry[n_pages - 1:n_pages, :] + tot[n_pages - 1:n_pages, :]

    tok_valid = (_iota2((SUBLANES, 1), 0) < l_new).astype(F32)
    lf_col = _log_sigmoid(sm_ref[...] + fbrow_ref[...]) * tok_valid
    logf_ref[...] = lf_col
    r_tok = _iota2((nr, SUBLANES), 0) // N_HEADS
    csum = _fdot((_iota2((nr, SUBLANES), 1) <= r_tok).astype(F32), lf_col)
    own_lane = _iota2((nr, LANES), 1) == SM_FORGET + _iota2((nr, LANES), 0) % N_HEADS
    fq_new = jnp.sum(jnp.where(own_lane, csum, 0.0), axis=-1, keepdims=True)
    eye = _iota2((nr, nr), 0) == _iota2((nr, nr), 1)
    f_tot_col = jnp.sum(jnp.where(eye, jnp.broadcast_to(f_tot_row[:, 0:nr], (nr, nr)), 0.0),
                        axis=-1, keepdims=True)
    fq = fq_new + f_tot_col
    fq_row = jnp.sum(jnp.where(eye, jnp.broadcast_to(fq, (nr, nr)), 0.0), axis=0, keepdims=True)

    q = q_ref[...].astype(BF16)
    s_all = lax.dot_general(q, kbuf[slot].astype(BF16), (((1,), (1,)), ((), ())),
                            preferred_element_type=F32)
    same_head = (_iota2((nr, pg), 1) % N_HEADS) == (_iota2((nr, pg), 0) % N_HEADS)
    sp = [jnp.where(same_head, s_all[:, p * pg:(p + 1) * pg] * scale + (fq - f_past[p:p + 1, :]), NEG_BIG)
          for p in range(n_pages)]
    s_new = lax.dot_general(q, kn_ref[...].astype(BF16), (((1,), (1,)), ((), ())),
                            preferred_element_type=F32)
    rr, cc = _iota2((nr, nr), 0), _iota2((nr, nr), 1)
    new_ok = (rr % N_HEADS == cc % N_HEADS) & (cc // N_HEADS <= rr // N_HEADS)
    s_new = jnp.where(new_ok, s_new * scale + (fq - fq_row), NEG_BIG)
    m_el = sp[0]
    for p in range(1, n_pages):
        m_el = jnp.maximum(m_el, sp[p])
    m = jnp.maximum(jnp.max(m_el, axis=-1, keepdims=True), jnp.max(s_new, axis=-1, keepdims=True))
    pp = [jnp.exp(t - m) for t in sp]
    p_new = jnp.exp(s_new - m)
    l_el = pp[0]
    for p in range(1, n_pages):
        l_el = l_el + pp[p]
    l = jnp.sum(l_el, axis=-1, keepdims=True) + jnp.sum(p_new, axis=-1, keepdims=True)
    p_all = jnp.concatenate([t.astype(BF16) for t in pp], axis=-1)
    acc = jnp.dot(p_all, vbuf[slot].astype(BF16), preferred_element_type=F32)
    acc = acc + jnp.dot(p_new.astype(BF16), vn_ref[...].astype(BF16), preferred_element_type=F32)
    z = zf_ref[...]
    o_ref[...] = ((acc / l) * (z * _sigmoid(z))).astype(o_ref.dtype)


def _head_cumsum_matrix(page):
    t = jnp.arange(page)[None, :, None]
    j = jnp.arange(page * N_HEADS)[None, None, :]
    h = jnp.arange(N_HEADS)[:, None, None]
    own = (j % N_HEADS) == h
    c = own & (t <= j // N_HEADS)
    b = jnp.broadcast_to(own, c.shape)
    return jnp.concatenate([c, b], axis=2).astype(BF16)


def _fox_decode_call(page_table, q4, kn4, vn4, zf4, sm8, fb_row, kcache, vcache, lcache, l_new):
    b, n_pages = page_table.shape
    pg = kcache.shape[1]
    page = lcache.shape[2]
    nr = l_new * N_HEADS
    assert nr % (2 * SUBLANES) == 0 and l_new <= SUBLANES
    kern = functools.partial(_fox_decode_kernel, n_pages=n_pages, pg=pg, l_new=l_new)
    rows = pl.BlockSpec((nr, HEAD_DIM), lambda i, pt: (i, 0))
    tok = pl.BlockSpec((None, SUBLANES, LANES), lambda i, pt: (i, 0, 0))
    const = lambda shape: pl.BlockSpec(shape, lambda i, pt: (0,) * len(shape))
    any_spec = pl.BlockSpec(memory_space=pl.ANY)
    grid_spec = pltpu.PrefetchScalarGridSpec(
        num_scalar_prefetch=1,
        grid=(b,),
        in_specs=[rows, rows, rows, rows, tok, const((1, LANES)), const((N_HEADS, page, 2 * pg)),
                  any_spec, any_spec, any_spec],
        out_specs=[rows, tok],
        scratch_shapes=[pltpu.VMEM((2, n_pages * pg, HEAD_DIM), F32),
                        pltpu.VMEM((2, n_pages * pg, HEAD_DIM), F32),
                        pltpu.VMEM((2, N_HEADS, n_pages, page), F32),
                        pltpu.SemaphoreType.DMA((3, 2))],
    )
    return pl.pallas_call(
        kern,
        grid_spec=grid_spec,
        out_shape=[jax.ShapeDtypeStruct((b * nr, HEAD_DIM), BF16),
                   jax.ShapeDtypeStruct((b, SUBLANES, LANES), F32)],
        compiler_params=pltpu.CompilerParams(dimension_semantics=("arbitrary",),
                                             vmem_limit_bytes=VMEM_LIMIT),
        name="fox_decode",
    )(page_table, q4, kn4, vn4, zf4, sm8, fb_row, _head_cumsum_matrix(page), kcache, vcache, lcache)


def _gate_row(vals, offset):
    return jnp.zeros((1, LANES), F32).at[0, offset:offset + N_HEADS].set(vals.astype(F32))


def _pad_rows(t, rows):
    return jnp.pad(t, ((0, 0), (0, rows - t.shape[1]), (0, 0)))


def kernel(x_prompt, x_sample, cache_fox_k, cache_fox_v, cache_fox_logf, page_table, state_gdn_ssm,
           state_gdn_conv, w_in, gdn_conv_w, gdn_a_log, gdn_dt_bias, gdn_out_norm_w, fox_f_bias, w_out,
           norm_w, final_norm_w):
    bp, lp, d = x_prompt.shape
    bs, ls, _ = x_sample.shape
    depth = w_in.shape[0]
    assert depth == 1, "single-layer trunk"
    n_pool, page = cache_fox_k.shape[1], cache_fox_k.shape[2]

    w_big = _pack_w_call(w_in[0].T)
    w_o = w_out[0].astype(BF16)
    nw = norm_w[0].reshape(1, d)
    fnw = final_norm_w.reshape(1, d)
    conv_w = gdn_conv_w[0]
    alog_row = _gate_row(gdn_a_log[0], SM_DECAY)
    dtb_row = _gate_row(gdn_dt_bias[0], SM_DECAY)
    fb_row = _gate_row(fox_f_bias[0], SM_FORGET)
    onw = gdn_out_norm_w[0].reshape(1, HEAD_DIM)

    xp2 = x_prompt.reshape(bp * lp, d)
    hp, zg, sm, qf, kf, vf, zf, k4, v4 = _proj_call(xp2, nw, w_big, tm=512, sample=False)
    r3 = lambda t: t.reshape(bp, lp, t.shape[-1])
    og_p, ssm_p, tail = _gdn_call(r3(hp), r3(zg), r3(sm), conv_w, alog_row, dtb_row, onw,
                                  c=GDN_CHUNK, l_valid=lp, nb=4, w_big=w_big)
    logf_p, fcol = _fox_gates_call(r3(sm), fb_row)
    of_p = _fox_prompt_call(r3(qf), r3(kf), r3(vf), fcol, r3(zf), tq=512)
    y_p = _out_call(og_p.reshape(bp * lp, GROUP_W), of_p.reshape(bp * lp, GROUP_W), xp2, w_o, fnw, tm=512)

    y_prompt = y_p.reshape(bp, lp, d)
    k_prompt = k4.reshape(1, bp, lp, N_HEADS, HEAD_DIM)
    v_prompt = v4.reshape(1, bp, lp, N_HEADS, HEAD_DIM)
    logf_prompt = logf_p[:, :, SM_FORGET:SM_FORGET + N_HEADS].reshape(1, bp, lp, N_HEADS)
    ssm_prompt = ssm_p.reshape(1, bp, N_HEADS, HEAD_DIM, HEAD_DIM)
    conv_prompt = tail[:, SUBLANES - (CONV_K - 1):, :].reshape(1, bp, CONV_K - 1, CONV_DIM)

    xs2 = x_sample.reshape(bs * ls, d)
    qkv_s, zg_s, sm_s, q4_s, k4_s, v4_s, z4_s = _proj_call(xs2, nw, w_big, tm=256, sample=True)
    r3s = lambda t: t.reshape(bs, ls, t.shape[-1])
    p8 = lambda t: _pad_rows(r3s(t), SUBLANES)
    c0_s = jnp.pad(state_gdn_conv[0], ((0, 0), (SUBLANES - (CONV_K - 1), 0), (0, 0)))
    og_s, ssm_s = _gdn_call(p8(qkv_s), p8(zg_s), p8(sm_s), conv_w, alog_row, dtb_row, onw, c=SUBLANES,
                            l_valid=ls, nb=16, s0=state_gdn_ssm[0], c0=c0_s)
    kcache = cache_fox_k[0].reshape(n_pool, page * N_HEADS, HEAD_DIM)
    vcache = cache_fox_v[0].reshape(n_pool, page * N_HEADS, HEAD_DIM)
    lcache = cache_fox_logf[0].transpose(0, 2, 1)
    of_s, logf_s = _fox_decode_call(page_table, q4_s, k4_s, v4_s, z4_s, p8(sm_s), fb_row,
                                    kcache, vcache, lcache, l_new=ls)
    og_s2 = og_s[:, :ls].reshape(bs * ls, GROUP_W)
    of_s2 = of_s.reshape(bs * ls, GROUP_W)
    y_s = _out_call(og_s2, of_s2, xs2, w_o, fnw, tm=256)

    y_sample = y_s.reshape(bs, ls, d)
    k_sample = k4_s.reshape(1, bs, ls, N_HEADS, HEAD_DIM)
    v_sample = v4_s.reshape(1, bs, ls, N_HEADS, HEAD_DIM)
    logf_sample = logf_s[:, :ls, SM_FORGET:SM_FORGET + N_HEADS].reshape(1, bs, ls, N_HEADS)
    ssm_sample = ssm_s.reshape(1, bs, N_HEADS, HEAD_DIM, HEAD_DIM)
    if ls >= CONV_K - 1:
        conv_sample = r3s(qkv_s)[:, ls - (CONV_K - 1):, :]
    else:
        conv_sample = jnp.concatenate([state_gdn_conv[0], r3s(qkv_s)], axis=1)[:, -(CONV_K - 1):, :]
    conv_sample = conv_sample.reshape(1, bs, CONV_K - 1, CONV_DIM)

    return (y_prompt, y_sample, k_prompt, v_prompt, logf_prompt, ssm_prompt, conv_prompt,
            k_sample, v_sample, logf_sample, ssm_sample, conv_sample)
```

```python
import functools
import math

import jax
import jax.numpy as jnp
from jax import lax
from jax.experimental import pallas as pl
from jax.experimental.pallas import tpu as pltpu

F32 = jnp.float32
BF16 = jnp.bfloat16

NORM_EPS = 1e-6
L2_EPS = 1e-6
HEAD_DIM = 128
N_HEADS = 4
GROUP_W = N_HEADS * HEAD_DIM
CONV_DIM = 3 * GROUP_W
CONV_K = 4
LANES = 128
SUBLANES = 8
GDN_CHUNK = 64
INV_BASE = 32
SM_BETA = 0
SM_DECAY = 4
SM_FORGET = 8
VMEM_LIMIT = 56 * 1024 * 1024


def _sigmoid(x):
    return 1.0 / (1.0 + jnp.exp(-x))


def _softplus(x):
    return jnp.maximum(x, 0.0) + jnp.log(1.0 + jnp.exp(-jnp.abs(x)))


def _bdot(a, b):
    return jnp.dot(a.astype(BF16), b.astype(BF16), preferred_element_type=F32)


def _bdot_nt(a, b):
    return lax.dot_general(a.astype(BF16), b.astype(BF16), (((1,), (1,)), ((), ())),
                           preferred_element_type=F32)


def _bdot_tn(a, b):
    return lax.dot_general(a.astype(BF16), b.astype(BF16), (((0,), (0,)), ((), ())),
                           preferred_element_type=F32)


def _fdot(a, b):
    return jnp.dot(a, b, preferred_element_type=F32, precision=lax.Precision.HIGHEST)


def _iota2(shape, dim):
    return lax.broadcasted_iota(jnp.int32, shape, dim)


W_QKV, W_ZG, W_QF, W_KF, W_VF, W_ZF, W_SM, W_END = 0, 1536, 2048, 2560, 3072, 3584, 4096, 4224
SRC_GATES_G, SRC_FOX, SRC_GATE_F, SRC_END = 2048, 2056, 4104, 4108


def _pack_w_kernel(w_ref, o_ref, qkv_ref):
    qkv_ref[...] = w_ref[W_QKV:W_ZG, :].T.astype(BF16)
    o_ref[W_QKV:W_QF, :] = w_ref[0:SRC_GATES_G, :].astype(BF16)
    o_ref[W_QF:W_SM, :] = w_ref[SRC_FOX:SRC_GATE_F, :].astype(BF16)
    n_gate = (SRC_FOX - SRC_GATES_G) + (SRC_END - SRC_GATE_F)
    gates = jnp.concatenate([w_ref[SRC_GATES_G:SRC_FOX, :], w_ref[SRC_GATE_F:SRC_END, :],
                             jnp.zeros((W_END - W_SM - n_gate, w_ref.shape[1]), F32)], axis=0)
    o_ref[W_SM:W_END, :] = gates.astype(BF16)


def _pack_w_call(w_t):
    return pl.pallas_call(
        _pack_w_kernel,
        out_shape=[jax.ShapeDtypeStruct((W_END, w_t.shape[1]), BF16),
                   jax.ShapeDtypeStruct((w_t.shape[1], CONV_DIM), BF16)],
        compiler_params=pltpu.CompilerParams(vmem_limit_bytes=VMEM_LIMIT),
        name="pack_w",
    )(w_t)


def _store_head_rows(ref, val, tm):
    for h in range(N_HEADS):
        ref[pl.ds(h, tm, stride=N_HEADS), :] = val[:, h * HEAD_DIM:(h + 1) * HEAD_DIM].astype(ref.dtype)


def _conv_silu_qkv(xbuf, cw_ref, g, rows):
    cols = slice(g * GROUP_W, (g + 1) * GROUP_W)
    base = SUBLANES - (CONV_K - 1)
    y = xbuf[pl.ds(base, rows), cols] * cw_ref[0:1, cols]
    for j in range(1, CONV_K):
        y = y + xbuf[pl.ds(base + j, rows), cols] * cw_ref[j:j + 1, cols]
    return y * _sigmoid(y)


def _l2_normalize(t, scale):
    return t * (lax.rsqrt(jnp.sum(t * t, axis=-1, keepdims=True) + L2_EPS) * scale)


def _proj_kernel(x_ref, nw_ref, w_ref, *refs, tm, sample):
    x = x_ref[...]
    var = jnp.mean(x * x, axis=-1, keepdims=True)
    h = (x * lax.rsqrt(var + NORM_EPS) * nw_ref[...]).astype(BF16)
    seg = lambda lo, hi: lax.dot_general(h, w_ref[lo:hi, :], (((1,), (1,)), ((), ())),
                                         preferred_element_type=F32)
    if sample:
        qkv_ref, zg_ref, sm_ref, q4_ref, k4_ref, v4_ref, z4_ref = refs
        qkv_ref[...] = seg(W_QKV, W_ZG)
        zg_ref[...] = seg(W_ZG, W_QF)
        sm_ref[...] = seg(W_SM, W_END)
        _store_head_rows(q4_ref, seg(W_QF, W_KF), tm)
        _store_head_rows(k4_ref, seg(W_KF, W_VF), tm)
        _store_head_rows(v4_ref, seg(W_VF, W_ZF), tm)
        _store_head_rows(z4_ref, seg(W_ZF, W_SM), tm)
        return
    h_ref, zg_ref, sm_ref, qb_ref, kb_ref, vb_ref, zf_ref, k4_ref, v4_ref = refs
    h_ref[...] = h
    zg_ref[...] = seg(W_ZG, W_QF)
    sm_ref[...] = seg(W_SM, W_END)
    qb_ref[...] = (seg(W_QF, W_KF) * (HEAD_DIM ** -0.5)).astype(BF16)
    kf = seg(W_KF, W_VF)
    _store_head_rows(k4_ref, kf, tm)
    kb_ref[...] = kf.astype(BF16)
    vf = seg(W_VF, W_ZF)
    _store_head_rows(v4_ref, vf, tm)
    vb_ref[...] = vf.astype(BF16)
    zf_ref[...] = seg(W_ZF, W_SM)


def _proj_call(x2d, norm_w, w_big, tm, sample):
    t, d = x2d.shape
    n = w_big.shape[0]
    wide = lambda w, dt: (jax.ShapeDtypeStruct((t, w), dt), pl.BlockSpec((tm, w), lambda i: (i, 0)))
    rows4 = (jax.ShapeDtypeStruct((t * N_HEADS, HEAD_DIM), F32),
             pl.BlockSpec((tm * N_HEADS, HEAD_DIM), lambda i: (i, 0)))
    if sample:
        outs = [wide(CONV_DIM, F32), wide(GROUP_W, F32), wide(LANES, F32), rows4, rows4, rows4, rows4]
    else:
        outs = [wide(d, BF16), wide(GROUP_W, F32), wide(LANES, F32), wide(GROUP_W, BF16), wide(GROUP_W, BF16),
                wide(GROUP_W, BF16), wide(GROUP_W, F32), rows4, rows4]
    out_shape = [o[0] for o in outs]
    out_specs = [o[1] for o in outs]
    return pl.pallas_call(
        functools.partial(_proj_kernel, tm=tm, sample=sample),
        grid=(t // tm,),
        in_specs=[pl.BlockSpec((tm, d), lambda i: (i, 0)),
                  pl.BlockSpec((1, d), lambda i: (0, 0)),
                  pl.BlockSpec((n, d), lambda i: (0, 0))],
        out_specs=out_specs,
        out_shape=out_shape,
        compiler_params=pltpu.CompilerParams(dimension_semantics=("arbitrary",),
                                             vmem_limit_bytes=VMEM_LIMIT),
        name="proj",
    )(x2d, norm_w, w_big)


def _gdn_kernel(*refs, c, l_valid, nb, project):
    if project:
        (h_ref, wqkv_ref, zg_ref, sm_ref, cw_ref, alog_ref, dtb_ref, onw_ref,
         og_ref, sout_ref, tail_ref, xbuf, s_scr) = refs
    else:
        (qkv_ref, zg_ref, sm_ref, cw_ref, alog_ref, dtb_ref, onw_ref, s0_ref, c0_ref,
         og_ref, sout_ref, xbuf, s_scr) = refs
    ci = pl.program_id(1)
    n_c = pl.num_programs(1)

    @pl.when(ci == 0)
    def _():
        if project:
            xbuf[:, 0:SUBLANES, :] = jnp.zeros((nb, SUBLANES, CONV_DIM), F32)
            s_scr[...] = jnp.zeros(s_scr.shape, F32)
        else:
            xbuf[:, 0:SUBLANES, :] = c0_ref[...]
            s_scr[...] = s0_ref[...]

    row = _iota2((c, 1), 0) + ci * c
    valid = jnp.broadcast_to((row < l_valid).astype(F32), (c, LANES))
    tri_incl = (_iota2((c, c), 0) >= _iota2((c, c), 1))
    tri_strict = (_iota2((c, c), 0) > _iota2((c, c), 1))
    eye = (_iota2((c, c), 0) == _iota2((c, c), 1)).astype(F32)
    pad_rows = LANES - c
    sl = lambda base, h: slice(base + h * HEAD_DIM, base + (h + 1) * HEAD_DIM)

    if project:
        raw = jnp.dot(h_ref[...].reshape(nb * c, h_ref.shape[-1]), wqkv_ref[...],
                      preferred_element_type=F32)
    q, k, v, beta, gc, gc_row, gc_last = [], [], [], [], [], [], []
    for bb in range(nb):
        xbuf[bb, SUBLANES:SUBLANES + c, :] = raw[bb * c:(bb + 1) * c] if project else qkv_ref[bb]
        yq, yk, yv = (_conv_silu_qkv(xbuf.at[bb], cw_ref, g, c) for g in range(3))
        if project:
            tail_ref[bb] = xbuf[bb, c:c + SUBLANES, :]
        xbuf[bb, 0:SUBLANES, :] = xbuf[bb, c:c + SUBLANES, :]
        sm = sm_ref[bb]
        beta_t = _sigmoid(sm) * valid
        g_t = -jnp.exp(alog_ref[...]) * _softplus(sm + dtb_ref[...]) * valid
        gc_t = _fdot(tri_incl.astype(F32), g_t)
        gc_sq = jnp.concatenate([gc_t, jnp.zeros((pad_rows, LANES), F32)], axis=0) if pad_rows else gc_t
        gc_tr = gc_sq.T
        for h in range(N_HEADS):
            q.append(_l2_normalize(yq[:, sl(0, h)], HEAD_DIM ** -0.5))
            k.append(_l2_normalize(yk[:, sl(0, h)], 1.0) * valid)
            v.append(yv[:, sl(0, h)])
            beta.append(jnp.broadcast_to(beta_t[:, SM_BETA + h:SM_BETA + h + 1], (c, HEAD_DIM)))
            gc.append(jnp.broadcast_to(gc_t[:, SM_DECAY + h:SM_DECAY + h + 1], (c, HEAD_DIM)))
            gc_row.append(gc_tr[SM_DECAY + h:SM_DECAY + h + 1, 0:c])
            gc_last.append(jnp.broadcast_to(gc_t[c - 1:c, SM_DECAY + h:SM_DECAY + h + 1], (1, HEAD_DIM)))

    chains = range(nb * N_HEADS)
    decay = [jnp.where(tri_incl, jnp.exp(jnp.where(tri_incl, gc[i][:, 0:c] - gc_row[i], 0.0)), 0.0)
             for i in chains]
    kb = [k[i] * beta[i] for i in chains]
    kkqk = [_bdot_nt(jnp.concatenate([kb[i], q[i]], axis=0), k[i]) for i in chains]
    qk = [kkqk[i][c:2 * c] * decay[i] for i in chains]
    neg_a = [-jnp.where(tri_strict, kkqk[i][0:c] * decay[i], 0.0) for i in chains]
    base = min(INV_BASE, c)
    blk_r, blk_c = _iota2((c, c), 0), _iota2((c, c), 1)
    same = lambda size: (blk_r // size) == (blk_c // size)
    diag = [jnp.where(same(base), neg_a[i], 0.0) for i in chains] if base < c else neg_a
    t_inv = [eye + diag[i] for i in chains]
    pw = [_bdot(diag[i], diag[i]) for i in chains]
    n_sq = int(math.log2(base))
    for j in range(1, n_sq):
        if j < n_sq - 1:
            both = [_bdot(jnp.concatenate([t_inv[i], pw[i]], axis=0), pw[i]) for i in chains]
            t_inv = [t_inv[i] + both[i][0:c] for i in chains]
            pw = [both[i][c:2 * c] for i in chains]
        else:
            t_inv = [t_inv[i] + _bdot(t_inv[i], pw[i]) for i in chains]
    size = base
    while size < c:
        off = [jnp.where(same(2 * size) & ~same(size), neg_a[i], 0.0) for i in chains]
        right = [_bdot(off[i], t_inv[i]) for i in chains]
        t_inv = [t_inv[i] + _bdot(t_inv[i], right[i]) for i in chains]
        size *= 2
    egc = [jnp.exp(gc[i]) for i in chains]
    sol = [_bdot(t_inv[i], jnp.concatenate([v[i] * beta[i], kb[i] * egc[i]], axis=-1)) for i in chains]
    s = [s_scr[i // N_HEADS, i % N_HEADS] for i in chains]
    ws = [_bdot(jnp.concatenate([sol[i][:, HEAD_DIM:2 * HEAD_DIM], q[i] * egc[i]], axis=0), s[i])
          for i in chains]
    v_new = [sol[i][:, 0:HEAD_DIM] - ws[i][0:c] for i in chains]
    o = [ws[i][c:2 * c] + _bdot(qk[i], v_new[i]) for i in chains]
    k_dec = [k[i] * jnp.exp(gc_last[i] - gc[i]) for i in chains]
    s_new = [s[i] * jnp.exp(gc_last[i]) + _bdot_tn(k_dec[i], v_new[i]) for i in chains]
    for i in chains:
        bb, h = i // N_HEADS, i % N_HEADS
        s_scr[bb, h] = s_new[i]
        oh = o[i] * lax.rsqrt(jnp.mean(o[i] * o[i], axis=-1, keepdims=True) + NORM_EPS) * onw_ref[...]
        z = zg_ref[bb, :, sl(0, h)]
        og_ref[bb, :, sl(0, h)] = (oh * (z * _sigmoid(z))).astype(og_ref.dtype)

    @pl.when(ci == n_c - 1)
    def _():
        sout_ref[...] = s_scr[...]


def _gdn_call(src, zg, sm, conv_w, alog_row, dtb_row, onw, *, c, l_valid, nb, w_qkv=None, s0=None, c0=None):
    project = w_qkv is not None
    b, l, _ = zg.shape
    n_c = l // c
    assert not project or l_valid == l
    kern = functools.partial(_gdn_kernel, c=c, l_valid=l_valid, nb=nb, project=project)
    blk = lambda w: pl.BlockSpec((nb, c, w), lambda bi, ci: (bi, ci, 0))
    full = lambda shape: pl.BlockSpec(shape, lambda bi, ci: (0,) * len(shape))
    state = pl.BlockSpec((nb, N_HEADS, HEAD_DIM, HEAD_DIM), lambda bi, ci: (bi, 0, 0, 0))
    rows8 = pl.BlockSpec((nb, SUBLANES, CONV_DIM), lambda bi, ci: (bi, 0, 0))
    common = [blk(GROUP_W), blk(LANES), full((CONV_K, CONV_DIM)), full((1, LANES)), full((1, LANES)),
              full((1, HEAD_DIM))]
    out_specs = [blk(GROUP_W), state]
    out_shape = [jax.ShapeDtypeStruct((b, l, GROUP_W), BF16),
                 jax.ShapeDtypeStruct((b, N_HEADS, HEAD_DIM, HEAD_DIM), F32)]
    if project:
        d = src.shape[-1]
        operands = (src, w_qkv, zg, sm, conv_w, alog_row, dtb_row, onw)
        in_specs = [blk(d), full((d, CONV_DIM))] + common
        out_specs.append(rows8)
        out_shape.append(jax.ShapeDtypeStruct((b, SUBLANES, CONV_DIM), F32))
    else:
        operands = (src, zg, sm, conv_w, alog_row, dtb_row, onw, s0, c0)
        in_specs = [blk(CONV_DIM)] + common + [state, rows8]
    return pl.pallas_call(
        kern,
        grid=(b // nb, n_c),
        in_specs=in_specs,
        out_specs=out_specs,
        out_shape=out_shape,
        scratch_shapes=[pltpu.VMEM((nb, c + SUBLANES, CONV_DIM), F32),
                        pltpu.VMEM((nb, N_HEADS, HEAD_DIM, HEAD_DIM), F32)],
        compiler_params=pltpu.CompilerParams(dimension_semantics=("arbitrary", "arbitrary"),
                                             vmem_limit_bytes=VMEM_LIMIT),
        name="gdn",
    )(*operands)


def _log_sigmoid(x):
    return -_softplus(-x)


def _fox_gates_kernel(sm_ref, fb_ref, logf_ref, fcol_ref, *, l):
    blk = LANES
    tri = (_iota2((blk, blk), 0) >= _iota2((blk, blk), 1)).astype(F32)
    carry = jnp.zeros((1, LANES), F32)
    for i in range(l // blk):
        lf = _log_sigmoid(sm_ref[i * blk:(i + 1) * blk, :] + fb_ref[...])
        f = _fdot(tri, lf) + carry
        carry = f[blk - 1:blk, :]
        logf_ref[i * blk:(i + 1) * blk, :] = lf
        fcol_ref[i * blk:(i + 1) * blk, :] = f


def _fox_gates_call(sm, fb_row):
    b, l, _ = sm.shape
    kern = functools.partial(_fox_gates_kernel, l=l)
    seq = pl.BlockSpec((None, l, LANES), lambda i: (i, 0, 0))
    return pl.pallas_call(
        kern,
        grid=(b,),
        in_specs=[seq, pl.BlockSpec((1, LANES), lambda i: (0, 0))],
        out_specs=[seq, seq],
        out_shape=[jax.ShapeDtypeStruct((b, l, LANES), F32), jax.ShapeDtypeStruct((b, l, LANES), F32)],
        compiler_params=pltpu.CompilerParams(dimension_semantics=("arbitrary",),
                                             vmem_limit_bytes=VMEM_LIMIT),
        name="fox_gates",
    )(sm, fb_row)


NEG_BIG = -1e30


def _forget_columns(f_tile, h, rows, for_keys):
    f = jnp.broadcast_to(f_tile[:, SM_FORGET + h:SM_FORGET + h + 1], (rows, LANES))
    f1, f2, f3 = (t.astype(F32) for t in _split3(-f if for_keys else f))
    lane = _iota2((rows, LANES), 1)
    base = 3 if for_keys else 0
    ones = ((lane >= 3 - base) & (lane < 6 - base)).astype(F32)
    cols = jnp.where(lane == base, f1, jnp.where(lane == base + 1, f2, jnp.where(lane == base + 2, f3, ones)))
    return cols.astype(BF16)


ROW_GROUP = 32


def _fox_prompt_kernel(q_ref, k_ref, v_ref, fcol_ref, zf_ref, o_ref,
                       kx_ref, qa_ref, s_ref, p_ref, acc_ref, m_ref, a_ref, *, tq, l):
    qi = pl.program_id(1)
    heads = range(N_HEADS)
    sl = lambda h: slice(h * HEAD_DIM, (h + 1) * HEAD_DIM)
    nt = (((1,), (1,)), ((), ()))

    @pl.when(qi == 0)
    def _():
        for r in range(l // tq):
            for h in heads:
                kx_ref[r * tq:(r + 1) * tq, sl(h)] = _forget_columns(fcol_ref[r * tq:(r + 1) * tq, :], h, tq, True)

    f_q = fcol_ref[pl.ds(pl.multiple_of(qi * tq, tq), tq), :]
    for h in heads:
        qa_ref[h, :, 0:HEAD_DIM] = q_ref[:, sl(h)]
        qa_ref[h, :, HEAD_DIM:2 * HEAD_DIM] = _forget_columns(f_q, h, tq, False)
    acc_ref[...] = jnp.zeros(acc_ref.shape, F32)
    m_ref[...] = jnp.full(m_ref.shape, NEG_BIG, F32)
    ones = jnp.ones((tq, HEAD_DIM), BF16)

    def block(ki, masked):
        rows = pl.ds(pl.multiple_of(ki * tq, tq), tq)
        for h in heads:
            ka = jnp.concatenate([k_ref[rows, sl(h)], kx_ref[rows, sl(h)]], axis=1)
            s_ref[h] = lax.dot_general(qa_ref[h], ka, nt, preferred_element_type=F32)
        for h in heads:
            for r in range(0, tq, ROW_GROUP):
                rg = slice(r, r + ROW_GROUP)
                s = s_ref[h, rg, :]
                if masked:
                    keep = _iota2((ROW_GROUP, tq), 1) <= _iota2((ROW_GROUP, tq), 0) + r
                    s = jnp.where(keep, s, NEG_BIG)
                m_old = m_ref[h, rg, :]
                m_new = jnp.maximum(m_old, jnp.max(s, axis=-1, keepdims=True))
                a_ref[h, rg, :] = jnp.exp(m_old - m_new)
                m_ref[h, rg, :] = m_new
                p_ref[h, rg, :] = jnp.exp(s - jnp.concatenate([m_new] * (tq // LANES), axis=1)).astype(BF16)
        for h in heads:
            pv = jnp.dot(p_ref[h], jnp.concatenate([v_ref[rows, sl(h)], ones], axis=1),
                         preferred_element_type=F32)
            alpha = a_ref[h]
            acc_ref[h] = acc_ref[h] * jnp.concatenate([alpha, alpha], axis=1) + pv

    def body(ki, carry):
        block(ki, False)
        return carry

    lax.fori_loop(0, qi, body, 0)
    block(qi, True)
    for h in heads:
        z = zf_ref[:, sl(h)]
        o = acc_ref[h, :, 0:HEAD_DIM] / acc_ref[h, :, HEAD_DIM:2 * HEAD_DIM]
        o_ref[:, sl(h)] = (o * (z * _sigmoid(z))).astype(o_ref.dtype)


def _fox_prompt_call(qf, kf, vf, fcol, zf, tq):
    b, l, _ = qf.shape
    kern = functools.partial(_fox_prompt_kernel, tq=tq, l=l)
    qblk = lambda w: pl.BlockSpec((None, tq, w), lambda bi, qi: (bi, qi, 0))
    seq = lambda w: pl.BlockSpec((None, l, w), lambda bi, qi: (bi, 0, 0))
    return pl.pallas_call(
        kern,
        grid=(b, l // tq),
        in_specs=[qblk(GROUP_W), seq(GROUP_W), seq(GROUP_W), seq(LANES), qblk(GROUP_W)],
        out_specs=qblk(GROUP_W),
        out_shape=jax.ShapeDtypeStruct((b, l, GROUP_W), BF16),
        scratch_shapes=[pltpu.VMEM((l, GROUP_W), BF16),
                        pltpu.VMEM((N_HEADS, tq, 2 * HEAD_DIM), BF16),
                        pltpu.VMEM((N_HEADS, tq, tq), F32),
                        pltpu.VMEM((N_HEADS, tq, tq), BF16),
                        pltpu.VMEM((N_HEADS, tq, 2 * HEAD_DIM), F32),
                        pltpu.VMEM((N_HEADS, tq, LANES), F32),
                        pltpu.VMEM((N_HEADS, tq, LANES), F32)],
        compiler_params=pltpu.CompilerParams(dimension_semantics=("arbitrary", "arbitrary"),
                                             vmem_limit_bytes=VMEM_LIMIT),
        name="fox_prompt",
    )(qf, kf, vf, fcol, zf)


def _out_kernel(og_ref, of_ref, x_ref, w_ref, fnw_ref, y_ref):
    o = jnp.dot(og_ref[...], w_ref[0:GROUP_W, :], preferred_element_type=F32)
    o = o + jnp.dot(of_ref[...], w_ref[GROUP_W:2 * GROUP_W, :], preferred_element_type=F32)
    y = x_ref[...] + o
    var = jnp.mean(y * y, axis=-1, keepdims=True)
    y_ref[...] = y * lax.rsqrt(var + NORM_EPS) * fnw_ref[...]


def _out_call(og, of, x2d, w_out, fnw, tm):
    t, d = x2d.shape
    return pl.pallas_call(
        _out_kernel,
        grid=(t // tm,),
        in_specs=[pl.BlockSpec((tm, GROUP_W), lambda i: (i, 0)),
                  pl.BlockSpec((tm, GROUP_W), lambda i: (i, 0)),
                  pl.BlockSpec((tm, d), lambda i: (i, 0)),
                  pl.BlockSpec((2 * GROUP_W, d), lambda i: (0, 0)),
                  pl.BlockSpec((1, d), lambda i: (0, 0))],
        out_specs=pl.BlockSpec((tm, d), lambda i: (i, 0)),
        out_shape=jax.ShapeDtypeStruct((t, d), F32),
        compiler_params=pltpu.CompilerParams(dimension_semantics=("arbitrary",),
                                             vmem_limit_bytes=VMEM_LIMIT),
        name="out_proj",
    )(og, of, x2d, w_out, fnw)


def _page_copies(pt_ref, kc_ref, vc_ref, lc_ref, kbuf, vbuf, lbuf, sems, bi, slot, n_pages, pg):
    copies = []
    for p in range(n_pages):
        pid = pt_ref[bi, p]
        copies.append((pltpu.make_async_copy(kc_ref.at[pid], kbuf.at[slot, pl.ds(p * pg, pg)],
                                             sems.at[0, slot]), 0))
        copies.append((pltpu.make_async_copy(vc_ref.at[pid], vbuf.at[slot, pl.ds(p * pg, pg)],
                                             sems.at[1, slot]), 1))
        copies.append((pltpu.make_async_copy(lc_ref.at[pid], lbuf.at[slot, :, p, :], sems.at[2, slot]), p % 2))
    return copies


def _split3(x):
    x1 = x.astype(BF16)
    r1 = x - x1.astype(F32)
    x2 = r1.astype(BF16)
    x3 = (r1 - x2.astype(F32)).astype(BF16)
    return x1, x2, x3


def _fox_decode_kernel(pt_ref, q_ref, kn_ref, vn_ref, zf_ref, sm_ref, fbrow_ref, cums_ref,
                       kc_ref, vc_ref, lc_ref, o_ref, logf_ref, kbuf, vbuf, lbuf, sems,
                       *, n_pages, pg, l_new):
    bi = pl.program_id(0)
    nb = pl.num_programs(0)
    slot = bi % 2
    copies = functools.partial(_page_copies, pt_ref, kc_ref, vc_ref, lc_ref, kbuf, vbuf, lbuf, sems,
                               n_pages=n_pages, pg=pg)
    nr = l_new * N_HEADS

    @pl.when(bi == 0)
    def _():
        for cp, prio in copies(bi=bi, slot=slot):
            cp.start(priority=prio)

    @pl.when(bi + 1 < nb)
    def _():
        for cp, prio in copies(bi=bi + 1, slot=1 - slot):
            cp.start(priority=prio)

    for cp, _ in copies(bi=bi, slot=slot):
        cp.wait()

    scale = HEAD_DIM ** -0.5
    res = jnp.zeros((3 * n_pages, 2 * pg), F32)
    for h in range(N_HEADS):
        res = res + jnp.dot(jnp.concatenate(_split3(lbuf[slot, h]), axis=0), cums_ref[h],
                            preferred_element_type=F32)
    res = res[0:n_pages] + res[n_pages:2 * n_pages] + res[2 * n_pages:3 * n_pages]
    within, tot = res[:, 0:pg], res[:, pg:2 * pg]
    earlier = (_iota2((n_pages, n_pages), 0) > _iota2((n_pages, n_pages), 1)).astype(F32)
    carry = _fdot(earlier, tot)
    f_past = within + carry
    f_tot_row = carry[n_pages - 1:n_pages, :] + tot[n_pages - 1:n_pages, :]

    tok_valid = (_iota2((SUBLANES, 1), 0) < l_new).astype(F32)
    lf_col = _log_sigmoid(sm_ref[...] + fbrow_ref[...]) * tok_valid
    logf_ref[...] = lf_col
    r_tok = _iota2((nr, SUBLANES), 0) // N_HEADS
    csum = _fdot((_iota2((nr, SUBLANES), 1) <= r_tok).astype(F32), lf_col)
    own_lane = _iota2((nr, LANES), 1) == SM_FORGET + _iota2((nr, LANES), 0) % N_HEADS
    fq_new = jnp.sum(jnp.where(own_lane, csum, 0.0), axis=-1, keepdims=True)
    eye = _iota2((nr, nr), 0) == _iota2((nr, nr), 1)
    f_tot_col = jnp.sum(jnp.where(eye, jnp.broadcast_to(f_tot_row[:, 0:nr], (nr, nr)), 0.0),
                        axis=-1, keepdims=True)
    fq = fq_new + f_tot_col
    fq_row = jnp.sum(jnp.where(eye, jnp.broadcast_to(fq, (nr, nr)), 0.0), axis=0, keepdims=True)

    q = q_ref[...].astype(BF16)
    s_all = lax.dot_general(q, kbuf[slot].astype(BF16), (((1,), (1,)), ((), ())),
                            preferred_element_type=F32)
    same_head = (_iota2((nr, pg), 1) % N_HEADS) == (_iota2((nr, pg), 0) % N_HEADS)
    sp = [jnp.where(same_head, s_all[:, p * pg:(p + 1) * pg] * scale + (fq - f_past[p:p + 1, :]), NEG_BIG)
          for p in range(n_pages)]
    s_new = lax.dot_general(q, kn_ref[...].astype(BF16), (((1,), (1,)), ((), ())),
                            preferred_element_type=F32)
    rr, cc = _iota2((nr, nr), 0), _iota2((nr, nr), 1)
    new_ok = (rr % N_HEADS == cc % N_HEADS) & (cc // N_HEADS <= rr // N_HEADS)
    s_new = jnp.where(new_ok, s_new * scale + (fq - fq_row), NEG_BIG)
    m_el = sp[0]
    for p in range(1, n_pages):
        m_el = jnp.maximum(m_el, sp[p])
    m = jnp.maximum(jnp.max(m_el, axis=-1, keepdims=True), jnp.max(s_new, axis=-1, keepdims=True))
    pp = [jnp.exp(t - m) for t in sp]
    p_new = jnp.exp(s_new - m)
    l_el = pp[0]
    for p in range(1, n_pages):
        l_el = l_el + pp[p]
    l = jnp.sum(l_el, axis=-1, keepdims=True) + jnp.sum(p_new, axis=-1, keepdims=True)
    p_all = jnp.concatenate([t.astype(BF16) for t in pp], axis=-1)
    acc = jnp.dot(p_all, vbuf[slot].astype(BF16), preferred_element_type=F32)
    acc = acc + jnp.dot(p_new.astype(BF16), vn_ref[...].astype(BF16), preferred_element_type=F32)
    z = zf_ref[...]
    o_ref[...] = ((acc / l) * (z * _sigmoid(z))).astype(o_ref.dtype)


def _head_cumsum_matrix(page):
    t = jnp.arange(page)[None, :, None]
    j = jnp.arange(page * N_HEADS)[None, None, :]
    h = jnp.arange(N_HEADS)[:, None, None]
    own = (j % N_HEADS) == h
    c = own & (t <= j // N_HEADS)
    b = jnp.broadcast_to(own, c.shape)
    return jnp.concatenate([c, b], axis=2).astype(BF16)


def _fox_decode_call(page_table, q4, kn4, vn4, zf4, sm8, fb_row, kcache, vcache, lcache, l_new):
    b, n_pages = page_table.shape
    pg = kcache.shape[1]
    page = lcache.shape[2]
    nr = l_new * N_HEADS
    assert nr % (2 * SUBLANES) == 0 and l_new <= SUBLANES
    kern = functools.partial(_fox_decode_kernel, n_pages=n_pages, pg=pg, l_new=l_new)
    rows = pl.BlockSpec((nr, HEAD_DIM), lambda i, pt: (i, 0))
    tok = pl.BlockSpec((None, SUBLANES, LANES), lambda i, pt: (i, 0, 0))
    const = lambda shape: pl.BlockSpec(shape, lambda i, pt: (0,) * len(shape))
    any_spec = pl.BlockSpec(memory_space=pl.ANY)
    grid_spec = pltpu.PrefetchScalarGridSpec(
        num_scalar_prefetch=1,
        grid=(b,),
        in_specs=[rows, rows, rows, rows, tok, const((1, LANES)), const((N_HEADS, page, 2 * pg)),
                  any_spec, any_spec, any_spec],
        out_specs=[rows, tok],
        scratch_shapes=[pltpu.VMEM((2, n_pages * pg, HEAD_DIM), F32),
                        pltpu.VMEM((2, n_pages * pg, HEAD_DIM), F32),
                        pltpu.VMEM((2, N_HEADS, n_pages, page), F32),
                        pltpu.SemaphoreType.DMA((3, 2))],
    )
    return pl.pallas_call(
        kern,
        grid_spec=grid_spec,
        out_shape=[jax.ShapeDtypeStruct((b * nr, HEAD_DIM), BF16),
                   jax.ShapeDtypeStruct((b, SUBLANES, LANES), F32)],
        compiler_params=pltpu.CompilerParams(dimension_semantics=("arbitrary",),
                                             vmem_limit_bytes=VMEM_LIMIT),
        name="fox_decode",
    )(page_table, q4, kn4, vn4, zf4, sm8, fb_row, _head_cumsum_matrix(page), kcache, vcache, lcache)


def _gate_row(vals, offset):
    return jnp.zeros((1, LANES), F32).at[0, offset:offset + N_HEADS].set(vals.astype(F32))


def _pad_rows(t, rows):
    return jnp.pad(t, ((0, 0), (0, rows - t.shape[1]), (0, 0)))


def kernel(x_prompt, x_sample, cache_fox_k, cache_fox_v, cache_fox_logf, page_table, state_gdn_ssm,
           state_gdn_conv, w_in, gdn_conv_w, gdn_a_log, gdn_dt_bias, gdn_out_norm_w, fox_f_bias, w_out,
           norm_w, final_norm_w):
    bp, lp, d = x_prompt.shape
    bs, ls, _ = x_sample.shape
    depth = w_in.shape[0]
    assert depth == 1, "single-layer trunk"
    n_pool, page = cache_fox_k.shape[1], cache_fox_k.shape[2]

    w_big, w_qkv = _pack_w_call(w_in[0].T)
    w_o = w_out[0].astype(BF16)
    nw = norm_w[0].reshape(1, d)
    fnw = final_norm_w.reshape(1, d)
    conv_w = gdn_conv_w[0]
    alog_row = _gate_row(gdn_a_log[0], SM_DECAY)
    dtb_row = _gate_row(gdn_dt_bias[0], SM_DECAY)
    fb_row = _gate_row(fox_f_bias[0], SM_FORGET)
    onw = gdn_out_norm_w[0].reshape(1, HEAD_DIM)

    xp2 = x_prompt.reshape(bp * lp, d)
    hp, zg, sm, qf, kf, vf, zf, k4, v4 = _proj_call(xp2, nw, w_big, tm=512, sample=False)
    r3 = lambda t: t.reshape(bp, lp, t.shape[-1])
    og_p, ssm_p, tail = _gdn_call(r3(hp), r3(zg), r3(sm), conv_w, alog_row, dtb_row, onw,
                                  c=GDN_CHUNK, l_valid=lp, nb=4, w_qkv=w_qkv)
    logf_p, fcol = _fox_gates_call(r3(sm), fb_row)
    of_p = _fox_prompt_call(r3(qf), r3(kf), r3(vf), fcol, r3(zf), tq=512)
    y_p = _out_call(og_p.reshape(bp * lp, GROUP_W), of_p.reshape(bp * lp, GROUP_W), xp2, w_o, fnw, tm=1024)

    y_prompt = y_p.reshape(bp, lp, d)
    k_prompt = k4.reshape(1, bp, lp, N_HEADS, HEAD_DIM)
    v_prompt = v4.reshape(1, bp, lp, N_HEADS, HEAD_DIM)
    logf_prompt = logf_p[:, :, SM_FORGET:SM_FORGET + N_HEADS].reshape(1, bp, lp, N_HEADS)
    ssm_prompt = ssm_p.reshape(1, bp, N_HEADS, HEAD_DIM, HEAD_DIM)
    conv_prompt = tail[:, SUBLANES - (CONV_K - 1):, :].reshape(1, bp, CONV_K - 1, CONV_DIM)

    xs2 = x_sample.reshape(bs * ls, d)
    qkv_s, zg_s, sm_s, q4_s, k4_s, v4_s, z4_s = _proj_call(xs2, nw, w_big, tm=256, sample=True)
    r3s = lambda t: t.reshape(bs, ls, t.shape[-1])
    p8 = lambda t: _pad_rows(r3s(t), SUBLANES)
    c0_s = jnp.pad(state_gdn_conv[0], ((0, 0), (SUBLANES - (CONV_K - 1), 0), (0, 0)))
    og_s, ssm_s = _gdn_call(p8(qkv_s), p8(zg_s), p8(sm_s), conv_w, alog_row, dtb_row, onw, c=SUBLANES,
                            l_valid=ls, nb=16, s0=state_gdn_ssm[0], c0=c0_s)
    kcache = cache_fox_k[0].reshape(n_pool, page * N_HEADS, HEAD_DIM)
    vcache = cache_fox_v[0].reshape(n_pool, page * N_HEADS, HEAD_DIM)
    lcache = cache_fox_logf[0].transpose(0, 2, 1)
    of_s, logf_s = _fox_decode_call(page_table, q4_s, k4_s, v4_s, z4_s, p8(sm_s), fb_row,
                                    kcache, vcache, lcache, l_new=ls)
    og_s2 = og_s[:, :ls].reshape(bs * ls, GROUP_W)
    of_s2 = of_s.reshape(bs * ls, GROUP_W)
    y_s = _out_call(og_s2, of_s2, xs2, w_o, fnw, tm=256)

    y_sample = y_s.reshape(bs, ls, d)
    k_sample = k4_s.reshape(1, bs, ls, N_HEADS, HEAD_DIM)
    v_sample = v4_s.reshape(1, bs, ls, N_HEADS, HEAD_DIM)
    logf_sample = logf_s[:, :ls, SM_FORGET:SM_FORGET + N_HEADS].reshape(1, bs, ls, N_HEADS)
    ssm_sample = ssm_s.reshape(1, bs, N_HEADS, HEAD_DIM, HEAD_DIM)
    if ls >= CONV_K - 1:
        conv_sample = r3s(qkv_s)[:, ls - (CONV_K - 1):, :]
    else:
        conv_sample = jnp.concatenate([state_gdn_conv[0], r3s(qkv_s)], axis=1)[:, -(CONV_K - 1):, :]
    conv_sample = conv_sample.reshape(1, bs, CONV_K - 1, CONV_DIM)

    return (y_prompt, y_sample, k_prompt, v_prompt, logf_prompt, ssm_prompt, conv_prompt,
            k_sample, v_sample, logf_sample, ssm_sample, conv_sample)
```

```python
import functools
import math

import jax
import jax.numpy as jnp
from jax import lax
from jax.experimental import pallas as pl
from jax.experimental.pallas import tpu as pltpu

F32 = jnp.float32
BF16 = jnp.bfloat16

NORM_EPS = 1e-6
L2_EPS = 1e-6
HEAD_DIM = 128
N_HEADS = 4
GROUP_W = N_HEADS * HEAD_DIM
CONV_DIM = 3 * GROUP_W
CONV_K = 4
LANES = 128
SUBLANES = 8
GDN_CHUNK = 64
INV_BASE = 32
SM_BETA = 0
SM_DECAY = 4
SM_FORGET = 8
VMEM_LIMIT = 56 * 1024 * 1024


def _sigmoid(x):
    return 1.0 / (1.0 + jnp.exp(-x))


def _softplus(x):
    return jnp.maximum(x, 0.0) + jnp.log(1.0 + jnp.exp(-jnp.abs(x)))


def _log_sigmoid(x):
    return -_softplus(-x)


def _bdot(a, b):
    return jnp.dot(a.astype(BF16), b.astype(BF16), preferred_element_type=F32)


def _bdot_nt(a, b):
    return lax.dot_general(a.astype(BF16), b.astype(BF16), (((1,), (1,)), ((), ())),
                           preferred_element_type=F32)


def _bdot_tn(a, b):
    return lax.dot_general(a.astype(BF16), b.astype(BF16), (((0,), (0,)), ((), ())),
                           preferred_element_type=F32)


def _fdot(a, b):
    return jnp.dot(a, b, preferred_element_type=F32, precision=lax.Precision.HIGHEST)


def _iota2(shape, dim):
    return lax.broadcasted_iota(jnp.int32, shape, dim)


W_QKV, W_ZG, W_QF, W_KF, W_VF, W_ZF, W_SM, W_END = 0, 1536, 2048, 2560, 3072, 3584, 4096, 4224
SRC_GATES_G, SRC_FOX, SRC_GATE_F, SRC_END = 2048, 2056, 4104, 4108


def _pack_w_kernel(w_ref, o_ref, qkv_ref):
    qkv_ref[...] = w_ref[W_QKV:W_ZG, :].T.astype(BF16)
    o_ref[W_QKV:W_QF, :] = w_ref[0:SRC_GATES_G, :].astype(BF16)
    o_ref[W_QF:W_SM, :] = w_ref[SRC_FOX:SRC_GATE_F, :].astype(BF16)
    n_gate = (SRC_FOX - SRC_GATES_G) + (SRC_END - SRC_GATE_F)
    gates = jnp.concatenate([w_ref[SRC_GATES_G:SRC_FOX, :], w_ref[SRC_GATE_F:SRC_END, :],
                             jnp.zeros((W_END - W_SM - n_gate, w_ref.shape[1]), F32)], axis=0)
    o_ref[W_SM:W_END, :] = gates.astype(BF16)


def _pack_w_call(w_t):
    return pl.pallas_call(
        _pack_w_kernel,
        out_shape=[jax.ShapeDtypeStruct((W_END, w_t.shape[1]), BF16),
                   jax.ShapeDtypeStruct((w_t.shape[1], CONV_DIM), BF16)],
        compiler_params=pltpu.CompilerParams(vmem_limit_bytes=VMEM_LIMIT),
        name="pack_w",
    )(w_t)


def _store_head_rows(ref, val, tm):
    for h in range(N_HEADS):
        ref[pl.ds(h, tm, stride=N_HEADS), :] = val[:, h * HEAD_DIM:(h + 1) * HEAD_DIM].astype(ref.dtype)


def _conv_silu_qkv(xbuf, cw_ref, g, rows):
    cols = slice(g * GROUP_W, (g + 1) * GROUP_W)
    x = xbuf[0:rows + SUBLANES, cols]
    y = x[SUBLANES:] * cw_ref[CONV_K - 1:CONV_K, cols]
    for j in range(CONV_K - 1):
        shifted = pltpu.roll(x, CONV_K - 1 - j, axis=0)
        y = y + shifted[SUBLANES:] * cw_ref[j:j + 1, cols]
    return y * _sigmoid(y)


def _l2_normalize(t, scale):
    return t * (lax.rsqrt(jnp.sum(t * t, axis=-1, keepdims=True) + L2_EPS) * scale)


def _proj_kernel(x_ref, nw_ref, w_ref, *refs, tm, sample, seq_tiles):
    x = x_ref[...]
    var = jnp.mean(x * x, axis=-1, keepdims=True)
    h = (x * lax.rsqrt(var + NORM_EPS) * nw_ref[...]).astype(BF16)
    seg = lambda lo, hi: lax.dot_general(h, w_ref[lo:hi, :], (((1,), (1,)), ((), ())),
                                         preferred_element_type=F32)
    if sample:
        qkv_ref, zg_ref, sm_ref, q4_ref, k4_ref, v4_ref, z4_ref = refs
        qkv_ref[...] = seg(W_QKV, W_ZG)
        zg_ref[...] = seg(W_ZG, W_QF)
        sm_ref[...] = seg(W_SM, W_END)
        _store_head_rows(q4_ref, seg(W_QF, W_KF), tm)
        _store_head_rows(k4_ref, seg(W_KF, W_VF), tm)
        _store_head_rows(v4_ref, seg(W_VF, W_ZF), tm)
        _store_head_rows(z4_ref, seg(W_ZF, W_SM), tm)
        return
    (fb_ref, h_ref, zg_ref, sm_ref, fcol_ref, logft_ref, qb_ref, kb_ref, vb_ref, zf_ref, k4_ref, v4_ref,
     carry_ref) = refs
    h_ref[...] = h
    zg_ref[...] = seg(W_ZG, W_QF)
    sm = seg(W_SM, W_END)
    sm_ref[...] = sm

    @pl.when(pl.program_id(0) % seq_tiles == 0)
    def _():
        carry_ref[...] = jnp.zeros(carry_ref.shape, F32)

    tri = (_iota2((LANES, LANES), 0) >= _iota2((LANES, LANES), 1)).astype(BF16)
    carry = carry_ref[0:1, :]
    for i in range(tm // LANES):
        blk = slice(i * LANES, (i + 1) * LANES)
        lf = _log_sigmoid(sm[blk] + fb_ref[...])
        parts = jnp.dot(tri, jnp.concatenate(_split3(lf), axis=1), preferred_element_type=F32)
        f = parts[:, 0:LANES] + parts[:, LANES:2 * LANES] + parts[:, 2 * LANES:3 * LANES] + carry
        carry = f[LANES - 1:LANES, :]
        fcol_ref[blk, :] = f
        logft_ref[:, blk] = lf.T[SM_FORGET:SM_FORGET + N_HEADS, :]
    carry_ref[0:1, :] = carry
    qb_ref[...] = (seg(W_QF, W_KF) * (HEAD_DIM ** -0.5)).astype(BF16)
    kf = seg(W_KF, W_VF)
    _store_head_rows(k4_ref, kf, tm)
    kb_ref[...] = kf.astype(BF16)
    vf = seg(W_VF, W_ZF)
    _store_head_rows(v4_ref, vf, tm)
    vb_ref[...] = vf.astype(BF16)
    zf_ref[...] = seg(W_ZF, W_SM)


def _proj_call(x2d, norm_w, w_big, tm, sample, fb_row=None, seq_len=None):
    t, d = x2d.shape
    n = w_big.shape[0]
    wide = lambda w, dt: (jax.ShapeDtypeStruct((t, w), dt), pl.BlockSpec((tm, w), lambda i: (i, 0)))
    rows4 = (jax.ShapeDtypeStruct((t * N_HEADS, HEAD_DIM), F32),
             pl.BlockSpec((tm * N_HEADS, HEAD_DIM), lambda i: (i, 0)))
    operands = [x2d, norm_w, w_big]
    in_specs = [pl.BlockSpec((tm, d), lambda i: (i, 0)),
                pl.BlockSpec((1, d), lambda i: (0, 0)),
                pl.BlockSpec((n, d), lambda i: (0, 0))]
    scratch = []
    seq_tiles = 1
    if sample:
        outs = [wide(CONV_DIM, F32), wide(GROUP_W, F32), wide(LANES, F32), rows4, rows4, rows4, rows4]
    else:
        seq_tiles = seq_len // tm
        logft = (jax.ShapeDtypeStruct((t // seq_len, N_HEADS, seq_len), F32),
                 pl.BlockSpec((None, N_HEADS, tm), lambda i: (i // seq_tiles, 0, i % seq_tiles)))
        outs = [wide(d, BF16), wide(GROUP_W, F32), wide(LANES, F32), wide(LANES, F32), logft,
                wide(GROUP_W, BF16), wide(GROUP_W, BF16), wide(GROUP_W, BF16), wide(GROUP_W, F32), rows4, rows4]
        operands.append(fb_row)
        in_specs.append(pl.BlockSpec((1, LANES), lambda i: (0, 0)))
        scratch = [pltpu.VMEM((SUBLANES, LANES), F32)]
    out_shape = [o[0] for o in outs]
    out_specs = [o[1] for o in outs]
    return pl.pallas_call(
        functools.partial(_proj_kernel, tm=tm, sample=sample, seq_tiles=seq_tiles),
        grid=(t // tm,),
        in_specs=in_specs,
        out_specs=out_specs,
        out_shape=out_shape,
        scratch_shapes=scratch,
        compiler_params=pltpu.CompilerParams(dimension_semantics=("arbitrary",),
                                             vmem_limit_bytes=VMEM_LIMIT),
        name="proj",
    )(*operands)


def _gdn_kernel(*refs, c, l_valid, nb, project):
    if project:
        (h_ref, wqkv_ref, zg_ref, sm_ref, cw_ref, alog_ref, dtb_ref, onw_ref,
         og_ref, sout_ref, tail_ref, xbuf, s_scr) = refs
    else:
        (qkv_ref, zg_ref, sm_ref, cw_ref, alog_ref, dtb_ref, onw_ref, s0_ref, c0_ref,
         og_ref, sout_ref, xbuf, s_scr) = refs
    ci = pl.program_id(1)
    n_c = pl.num_programs(1)

    @pl.when(ci == 0)
    def _():
        if project:
            xbuf[:, 0:SUBLANES, :] = jnp.zeros((nb, SUBLANES, CONV_DIM), F32)
            s_scr[...] = jnp.zeros(s_scr.shape, F32)
        else:
            xbuf[:, SUBLANES - (CONV_K - 1):SUBLANES, :] = c0_ref[...]
            s_scr[...] = s0_ref[...]

    row = _iota2((c, 1), 0) + ci * c
    valid = jnp.broadcast_to((row < l_valid).astype(F32), (c, LANES))
    tri_incl = (_iota2((c, c), 0) >= _iota2((c, c), 1))
    tri_strict = (_iota2((c, c), 0) > _iota2((c, c), 1))
    eye = (_iota2((c, c), 0) == _iota2((c, c), 1)).astype(F32)
    pad_rows = LANES - c
    sl = lambda base, h: slice(base + h * HEAD_DIM, base + (h + 1) * HEAD_DIM)

    if project:
        raw = jnp.dot(h_ref[...].reshape(nb * c, h_ref.shape[-1]), wqkv_ref[...],
                      preferred_element_type=F32)
    q, k, v, beta, gc, gc_row, gc_last = [], [], [], [], [], [], []
    for bb in range(nb):
        xbuf[bb, SUBLANES:SUBLANES + c, :] = raw[bb * c:(bb + 1) * c] if project else qkv_ref[bb]
        yq, yk, yv = (_conv_silu_qkv(xbuf.at[bb], cw_ref, g, c) for g in range(3))
        if project:
            tail_ref[bb] = xbuf[bb, c:c + SUBLANES, :]
        xbuf[bb, 0:SUBLANES, :] = xbuf[bb, c:c + SUBLANES, :]
        sm = sm_ref[bb]
        beta_t = _sigmoid(sm) * valid
        g_t = -jnp.exp(alog_ref[...]) * _softplus(sm + dtb_ref[...]) * valid
        gc_t = _fdot(tri_incl.astype(F32), g_t)
        gc_sq = jnp.concatenate([gc_t, jnp.zeros((pad_rows, LANES), F32)], axis=0) if pad_rows else gc_t
        gc_tr = gc_sq.T
        for h in range(N_HEADS):
            q.append(_l2_normalize(yq[:, sl(0, h)], HEAD_DIM ** -0.5))
            k.append(_l2_normalize(yk[:, sl(0, h)], 1.0) * valid)
            v.append(yv[:, sl(0, h)])
            beta.append(jnp.broadcast_to(beta_t[:, SM_BETA + h:SM_BETA + h + 1], (c, HEAD_DIM)))
            gc.append(jnp.broadcast_to(gc_t[:, SM_DECAY + h:SM_DECAY + h + 1], (c, HEAD_DIM)))
            gc_row.append(gc_tr[SM_DECAY + h:SM_DECAY + h + 1, 0:c])
            gc_last.append(jnp.broadcast_to(gc_t[c - 1:c, SM_DECAY + h:SM_DECAY + h + 1], (1, HEAD_DIM)))

    chains = range(nb * N_HEADS)
    decay = [jnp.where(tri_incl, jnp.exp(jnp.where(tri_incl, gc[i][:, 0:c] - gc_row[i], 0.0)), 0.0)
             for i in chains]
    kb = [k[i] * beta[i] for i in chains]
    kkqk = [_bdot_nt(jnp.concatenate([kb[i], q[i]], axis=0), k[i]) for i in chains]
    qk = [kkqk[i][c:2 * c] * decay[i] for i in chains]
    neg_a = [-jnp.where(tri_strict, kkqk[i][0:c] * decay[i], 0.0) for i in chains]
    base = min(INV_BASE, c)
    blk_r, blk_c = _iota2((c, c), 0), _iota2((c, c), 1)
    same = lambda size: (blk_r // size) == (blk_c // size)
    diag = [jnp.where(same(base), neg_a[i], 0.0) for i in chains] if base < c else neg_a
    t_inv = [eye + diag[i] for i in chains]
    pw = [_bdot(diag[i], diag[i]) for i in chains]
    n_sq = int(math.log2(base))
    for j in range(1, n_sq):
        if j < n_sq - 1:
            both = [_bdot(jnp.concatenate([t_inv[i], pw[i]], axis=0), pw[i]) for i in chains]
            t_inv = [t_inv[i] + both[i][0:c] for i in chains]
            pw = [both[i][c:2 * c] for i in chains]
        else:
            t_inv = [t_inv[i] + _bdot(t_inv[i], pw[i]) for i in chains]
    size = base
    while size < c:
        off = [jnp.where(same(2 * size) & ~same(size), neg_a[i], 0.0) for i in chains]
        right = [_bdot(off[i], t_inv[i]) for i in chains]
        t_inv = [t_inv[i] + _bdot(t_inv[i], right[i]) for i in chains]
        size *= 2
    egc = [jnp.exp(gc[i]) for i in chains]
    sol = [_bdot(t_inv[i], jnp.concatenate([v[i] * beta[i], kb[i] * egc[i]], axis=-1)) for i in chains]
    s = [s_scr[i // N_HEADS, i % N_HEADS] for i in chains]
    ws = [_bdot(jnp.concatenate([sol[i][:, HEAD_DIM:2 * HEAD_DIM], q[i] * egc[i]], axis=0), s[i])
          for i in chains]
    v_new = [sol[i][:, 0:HEAD_DIM] - ws[i][0:c] for i in chains]
    o = [ws[i][c:2 * c] + _bdot(qk[i], v_new[i]) for i in chains]
    k_dec = [k[i] * jnp.exp(gc_last[i] - gc[i]) for i in chains]
    s_new = [s[i] * jnp.exp(gc_last[i]) + _bdot_tn(k_dec[i], v_new[i]) for i in chains]
    for i in chains:
        bb, h = i // N_HEADS, i % N_HEADS
        s_scr[bb, h] = s_new[i]
        oh = o[i] * lax.rsqrt(jnp.mean(o[i] * o[i], axis=-1, keepdims=True) + NORM_EPS) * onw_ref[...]
        z = zg_ref[bb, :, sl(0, h)]
        og_ref[bb, :, sl(0, h)] = (oh * (z * _sigmoid(z))).astype(og_ref.dtype)

    @pl.when(ci == n_c - 1)
    def _():
        sout_ref[...] = s_scr[...]


def _gdn_call(src, zg, sm, conv_w, alog_row, dtb_row, onw, *, c, l_valid, nb, w_qkv=None, s0=None, c0=None):
    project = w_qkv is not None
    b, l, _ = zg.shape
    n_c = l // c
    assert not project or l_valid == l
    kern = functools.partial(_gdn_kernel, c=c, l_valid=l_valid, nb=nb, project=project)
    blk = lambda w: pl.BlockSpec((nb, c, w), lambda bi, ci: (bi, ci, 0))
    full = lambda shape: pl.BlockSpec(shape, lambda bi, ci: (0,) * len(shape))
    state = pl.BlockSpec((nb, N_HEADS, HEAD_DIM, HEAD_DIM), lambda bi, ci: (bi, 0, 0, 0))
    rows8 = pl.BlockSpec((nb, SUBLANES, CONV_DIM), lambda bi, ci: (bi, 0, 0))
    common = [blk(GROUP_W), blk(LANES), full((CONV_K, CONV_DIM)), full((1, LANES)), full((1, LANES)),
              full((1, HEAD_DIM))]
    out_specs = [blk(GROUP_W), state]
    out_shape = [jax.ShapeDtypeStruct((b, l, GROUP_W), BF16),
                 jax.ShapeDtypeStruct((b, N_HEADS, HEAD_DIM, HEAD_DIM), F32)]
    if project:
        d = src.shape[-1]
        operands = (src, w_qkv, zg, sm, conv_w, alog_row, dtb_row, onw)
        in_specs = [blk(d), full((d, CONV_DIM))] + common
        out_specs.append(rows8)
        out_shape.append(jax.ShapeDtypeStruct((b, SUBLANES, CONV_DIM), F32))
    else:
        operands = (src, zg, sm, conv_w, alog_row, dtb_row, onw, s0, c0)
        in_specs = [blk(CONV_DIM)] + common + [
            state, pl.BlockSpec((nb, CONV_K - 1, CONV_DIM), lambda bi, ci: (bi, 0, 0))]
    return pl.pallas_call(
        kern,
        grid=(b // nb, n_c),
        in_specs=in_specs,
        out_specs=out_specs,
        out_shape=out_shape,
        scratch_shapes=[pltpu.VMEM((nb, c + SUBLANES, CONV_DIM), F32),
                        pltpu.VMEM((nb, N_HEADS, HEAD_DIM, HEAD_DIM), F32)],
        compiler_params=pltpu.CompilerParams(dimension_semantics=("arbitrary", "arbitrary"),
                                             vmem_limit_bytes=VMEM_LIMIT),
        name="gdn",
    )(*operands)


NEG_BIG = -1e30


def _forget_columns(f_tile, h, rows, for_keys):
    f = jnp.broadcast_to(f_tile[:, SM_FORGET + h:SM_FORGET + h + 1], (rows, LANES))
    f1, f2, f3 = (t.astype(F32) for t in _split3(-f if for_keys else f))
    lane = _iota2((rows, LANES), 1)
    base = 3 if for_keys else 0
    ones = ((lane >= 3 - base) & (lane < 6 - base)).astype(F32)
    cols = jnp.where(lane == base, f1, jnp.where(lane == base + 1, f2, jnp.where(lane == base + 2, f3, ones)))
    return cols.astype(BF16)


ROW_GROUP = 32


def _fox_prompt_kernel(q_ref, k_ref, v_ref, fcol_ref, zf_ref, o_ref,
                       kx_ref, qa_ref, s_ref, p_ref, acc_ref, m_ref, a_ref, *, tq, l):
    qi = pl.program_id(1)
    heads = range(N_HEADS)
    sl = lambda h: slice(h * HEAD_DIM, (h + 1) * HEAD_DIM)
    nt = (((1,), (1,)), ((), ()))

    @pl.when(qi == 0)
    def _():
        for r in range(l // tq):
            for h in heads:
                kx_ref[r * tq:(r + 1) * tq, sl(h)] = _forget_columns(fcol_ref[r * tq:(r + 1) * tq, :], h, tq, True)

    f_q = fcol_ref[pl.ds(pl.multiple_of(qi * tq, tq), tq), :]
    for h in heads:
        qa_ref[h, :, 0:HEAD_DIM] = q_ref[:, sl(h)]
        qa_ref[h, :, HEAD_DIM:2 * HEAD_DIM] = _forget_columns(f_q, h, tq, False)
    acc_ref[...] = jnp.zeros(acc_ref.shape, F32)
    m_ref[...] = jnp.full(m_ref.shape, NEG_BIG, F32)
    ones = jnp.ones((tq, HEAD_DIM), BF16)

    half = tq // 2

    def block(ki, masked):
        start = pl.multiple_of(ki * tq, tq)
        pieces = [(slice(0, half), half), (slice(half, tq), tq)] if masked else [(slice(0, tq), tq)]
        for h in heads:
            for qr, nk in pieces:
                keys = pl.ds(start, nk)
                ka = jnp.concatenate([k_ref[keys, sl(h)], kx_ref[keys, sl(h)]], axis=1)
                s_ref[h, qr, 0:nk] = lax.dot_general(qa_ref[h, qr, :], ka, nt, preferred_element_type=F32)
        for h in heads:
            for r in range(0, tq, ROW_GROUP):
                rg = slice(r, r + ROW_GROUP)
                nk = half if (masked and r < half) else tq
                s = s_ref[h, rg, 0:nk]
                if masked:
                    keep = _iota2((ROW_GROUP, nk), 1) <= _iota2((ROW_GROUP, nk), 0) + r
                    s = jnp.where(keep, s, NEG_BIG)
                m_old = m_ref[h, rg, :]
                m_new = jnp.maximum(m_old, jnp.max(s, axis=-1, keepdims=True))
                a_ref[h, rg, :] = jnp.exp(m_old - m_new)
                m_ref[h, rg, :] = m_new
                p_ref[h, rg, 0:nk] = jnp.exp(s - jnp.concatenate([m_new] * (nk // LANES), axis=1)).astype(BF16)
        for h in heads:
            alpha = a_ref[h]
            for qr, nk in pieces:
                keys = pl.ds(start, nk)
                pv = jnp.dot(p_ref[h, qr, 0:nk], jnp.concatenate([v_ref[keys, sl(h)], ones[0:nk]], axis=1),
                             preferred_element_type=F32)
                acc_ref[h, qr, :] = acc_ref[h, qr, :] * jnp.concatenate([alpha[qr], alpha[qr]], axis=1) + pv

    def body(ki, carry):
        block(ki, False)
        return carry

    lax.fori_loop(0, qi, body, 0)
    block(qi, True)
    for h in heads:
        z = zf_ref[:, sl(h)]
        o = acc_ref[h, :, 0:HEAD_DIM] / acc_ref[h, :, HEAD_DIM:2 * HEAD_DIM]
        o_ref[:, sl(h)] = (o * (z * _sigmoid(z))).astype(o_ref.dtype)


def _fox_prompt_call(qf, kf, vf, fcol, zf, tq):
    b, l, _ = qf.shape
    kern = functools.partial(_fox_prompt_kernel, tq=tq, l=l)
    qblk = lambda w: pl.BlockSpec((None, tq, w), lambda bi, qi: (bi, qi, 0))
    seq = lambda w: pl.BlockSpec((None, l, w), lambda bi, qi: (bi, 0, 0))
    return pl.pallas_call(
        kern,
        grid=(b, l // tq),
        in_specs=[qblk(GROUP_W), seq(GROUP_W), seq(GROUP_W), seq(LANES), qblk(GROUP_W)],
        out_specs=qblk(GROUP_W),
        out_shape=jax.ShapeDtypeStruct((b, l, GROUP_W), BF16),
        scratch_shapes=[pltpu.VMEM((l, GROUP_W), BF16),
                        pltpu.VMEM((N_HEADS, tq, 2 * HEAD_DIM), BF16),
                        pltpu.VMEM((N_HEADS, tq, tq), F32),
                        pltpu.VMEM((N_HEADS, tq, tq), BF16),
                        pltpu.VMEM((N_HEADS, tq, 2 * HEAD_DIM), F32),
                        pltpu.VMEM((N_HEADS, tq, LANES), F32),
                        pltpu.VMEM((N_HEADS, tq, LANES), F32)],
        compiler_params=pltpu.CompilerParams(dimension_semantics=("arbitrary", "arbitrary"),
                                             vmem_limit_bytes=VMEM_LIMIT),
        name="fox_prompt",
    )(qf, kf, vf, fcol, zf)


def _out_kernel(og_ref, of_ref, x_ref, w_ref, fnw_ref, y_ref):
    o = jnp.dot(og_ref[...], w_ref[0:GROUP_W, :], preferred_element_type=F32)
    o = o + jnp.dot(of_ref[...], w_ref[GROUP_W:2 * GROUP_W, :], preferred_element_type=F32)
    y = x_ref[...] + o
    var = jnp.mean(y * y, axis=-1, keepdims=True)
    y_ref[...] = y * lax.rsqrt(var + NORM_EPS) * fnw_ref[...]


def _out_call(og, of, x2d, w_out, fnw, tm):
    t, d = x2d.shape
    return pl.pallas_call(
        _out_kernel,
        grid=(t // tm,),
        in_specs=[pl.BlockSpec((tm, GROUP_W), lambda i: (i, 0)),
                  pl.BlockSpec((tm, GROUP_W), lambda i: (i, 0)),
                  pl.BlockSpec((tm, d), lambda i: (i, 0)),
                  pl.BlockSpec((2 * GROUP_W, d), lambda i: (0, 0)),
                  pl.BlockSpec((1, d), lambda i: (0, 0))],
        out_specs=pl.BlockSpec((tm, d), lambda i: (i, 0)),
        out_shape=jax.ShapeDtypeStruct((t, d), F32),
        compiler_params=pltpu.CompilerParams(dimension_semantics=("arbitrary",),
                                             vmem_limit_bytes=VMEM_LIMIT),
        name="out_proj",
    )(og, of, x2d, w_out, fnw)


def _page_copies(pt_ref, kc_ref, vc_ref, lc_ref, kbuf, vbuf, lbuf, sems, bi, slot, n_pages, pg):
    copies = []
    for p in range(n_pages):
        pid = pt_ref[bi, p]
        copies.append((pltpu.make_async_copy(kc_ref.at[pid], kbuf.at[slot, pl.ds(p * pg, pg)],
                                             sems.at[0, slot]), 0))
        copies.append((pltpu.make_async_copy(vc_ref.at[pid], vbuf.at[slot, pl.ds(p * pg, pg)],
                                             sems.at[1, slot]), 1))
        copies.append((pltpu.make_async_copy(lc_ref.at[pid], lbuf.at[slot, :, p, :], sems.at[2, slot]), p % 2))
    return copies


def _split3(x):
    x1 = x.astype(BF16)
    r1 = x - x1.astype(F32)
    x2 = r1.astype(BF16)
    x3 = (r1 - x2.astype(F32)).astype(BF16)
    return x1, x2, x3


def _fox_decode_kernel(pt_ref, q_ref, kn_ref, vn_ref, zf_ref, sm_ref, fbrow_ref, cums_ref,
                       kc_ref, vc_ref, lc_ref, o_ref, logf_ref, kbuf, vbuf, lbuf, sems,
                       *, n_pages, pg, l_new):
    bi = pl.program_id(0)
    nb = pl.num_programs(0)
    slot = bi % 2
    copies = functools.partial(_page_copies, pt_ref, kc_ref, vc_ref, lc_ref, kbuf, vbuf, lbuf, sems,
                               n_pages=n_pages, pg=pg)
    nr = l_new * N_HEADS

    @pl.when(bi == 0)
    def _():
        for cp, prio in copies(bi=bi, slot=slot):
            cp.start(priority=prio)

    @pl.when(bi + 1 < nb)
    def _():
        for cp, prio in copies(bi=bi + 1, slot=1 - slot):
            cp.start(priority=prio)

    for cp, _ in copies(bi=bi, slot=slot):
        cp.wait()

    scale = HEAD_DIM ** -0.5
    res = jnp.zeros((3 * n_pages, 2 * pg), F32)
    for h in range(N_HEADS):
        res = res + jnp.dot(jnp.concatenate(_split3(lbuf[slot, h]), axis=0), cums_ref[h],
                            preferred_element_type=F32)
    res = res[0:n_pages] + res[n_pages:2 * n_pages] + res[2 * n_pages:3 * n_pages]
    within, tot = res[:, 0:pg], res[:, pg:2 * pg]
    earlier = (_iota2((n_pages, n_pages), 0) > _iota2((n_pages, n_pages), 1)).astype(F32)
    carry = _fdot(earlier, tot)
    f_past = within + carry
    f_tot_row = carry[n_pages - 1:n_pages, :] + tot[n_pages - 1:n_pages, :]

    tok_valid = (_iota2((SUBLANES, 1), 0) < l_new).astype(F32)
    lf_col = _log_sigmoid(sm_ref[...] + fbrow_ref[...]) * tok_valid
    logf_ref[...] = lf_col
    r_tok = _iota2((nr, SUBLANES), 0) // N_HEADS
    csum = _fdot((_iota2((nr, SUBLANES), 1) <= r_tok).astype(F32), lf_col)
    own_lane = _iota2((nr, LANES), 1) == SM_FORGET + _iota2((nr, LANES), 0) % N_HEADS
    fq_new = jnp.sum(jnp.where(own_lane, csum, 0.0), axis=-1, keepdims=True)
    eye = _iota2((nr, nr), 0) == _iota2((nr, nr), 1)
    f_tot_col = jnp.sum(jnp.where(eye, jnp.broadcast_to(f_tot_row[:, 0:nr], (nr, nr)), 0.0),
                        axis=-1, keepdims=True)
    fq = fq_new + f_tot_col
    fq_row = jnp.sum(jnp.where(eye, jnp.broadcast_to(fq, (nr, nr)), 0.0), axis=0, keepdims=True)

    q = q_ref[...].astype(BF16)
    s_all = lax.dot_general(q, kbuf[slot].astype(BF16), (((1,), (1,)), ((), ())),
                            preferred_element_type=F32)
    same_head = (_iota2((nr, pg), 1) % N_HEADS) == (_iota2((nr, pg), 0) % N_HEADS)
    sp = [jnp.where(same_head, s_all[:, p * pg:(p + 1) * pg] * scale + (fq - f_past[p:p + 1, :]), NEG_BIG)
          for p in range(n_pages)]
    s_new = lax.dot_general(q, kn_ref[...].astype(BF16), (((1,), (1,)), ((), ())),
                            preferred_element_type=F32)
    rr, cc = _iota2((nr, nr), 0), _iota2((nr, nr), 1)
    new_ok = (rr % N_HEADS == cc % N_HEADS) & (cc // N_HEADS <= rr // N_HEADS)
    s_new = jnp.where(new_ok, s_new * scale + (fq - fq_row), NEG_BIG)
    m_el = sp[0]
    for p in range(1, n_pages):
        m_el = jnp.maximum(m_el, sp[p])
    m = jnp.maximum(jnp.max(m_el, axis=-1, keepdims=True), jnp.max(s_new, axis=-1, keepdims=True))
    pp = [jnp.exp(t - m) for t in sp]
    p_new = jnp.exp(s_new - m)
    l_el = pp[0]
    for p in range(1, n_pages):
        l_el = l_el + pp[p]
    l = jnp.sum(l_el, axis=-1, keepdims=True) + jnp.sum(p_new, axis=-1, keepdims=True)
    p_all = jnp.concatenate([t.astype(BF16) for t in pp], axis=-1)
    acc = jnp.dot(p_all, vbuf[slot].astype(BF16), preferred_element_type=F32)
    acc = acc + jnp.dot(p_new.astype(BF16), vn_ref[...].astype(BF16), preferred_element_type=F32)
    z = zf_ref[...]
    o_ref[...] = ((acc / l) * (z * _sigmoid(z))).astype(o_ref.dtype)


def _head_cumsum_matrix(page):
    t = jnp.arange(page)[None, :, None]
    j = jnp.arange(page * N_HEADS)[None, None, :]
    h = jnp.arange(N_HEADS)[:, None, None]
    own = (j % N_HEADS) == h
    c = own & (t <= j // N_HEADS)
    b = jnp.broadcast_to(own, c.shape)
    return jnp.concatenate([c, b], axis=2).astype(BF16)


def _fox_decode_call(page_table, q4, kn4, vn4, zf4, sm8, fb_row, kcache, vcache, lcache, l_new):
    b, n_pages = page_table.shape
    pg = kcache.shape[1]
    page = lcache.shape[2]
    nr = l_new * N_HEADS
    assert nr % (2 * SUBLANES) == 0 and l_new <= SUBLANES
    kern = functools.partial(_fox_decode_kernel, n_pages=n_pages, pg=pg, l_new=l_new)
    rows = pl.BlockSpec((nr, HEAD_DIM), lambda i, pt: (i, 0))
    tok = pl.BlockSpec((None, SUBLANES, LANES), lambda i, pt: (i, 0, 0))
    const = lambda shape: pl.BlockSpec(shape, lambda i, pt: (0,) * len(shape))
    any_spec = pl.BlockSpec(memory_space=pl.ANY)
    grid_spec = pltpu.PrefetchScalarGridSpec(
        num_scalar_prefetch=1,
        grid=(b,),
        in_specs=[rows, rows, rows, rows, tok, const((1, LANES)), const((N_HEADS, page, 2 * pg)),
                  any_spec, any_spec, any_spec],
        out_specs=[rows, tok],
        scratch_shapes=[pltpu.VMEM((2, n_pages * pg, HEAD_DIM), F32),
                        pltpu.VMEM((2, n_pages * pg, HEAD_DIM), F32),
                        pltpu.VMEM((2, N_HEADS, n_pages, page), F32),
                        pltpu.SemaphoreType.DMA((3, 2))],
    )
    return pl.pallas_call(
        kern,
        grid_spec=grid_spec,
        out_shape=[jax.ShapeDtypeStruct((b * nr, HEAD_DIM), BF16),
                   jax.ShapeDtypeStruct((b, SUBLANES, LANES), F32)],
        compiler_params=pltpu.CompilerParams(dimension_semantics=("arbitrary",),
                                             vmem_limit_bytes=VMEM_LIMIT),
        name="fox_decode",
    )(page_table, q4, kn4, vn4, zf4, sm8, fb_row, _head_cumsum_matrix(page), kcache, vcache, lcache)


def _gate_row(vals, offset):
    return jnp.zeros((1, LANES), F32).at[0, offset:offset + N_HEADS].set(vals.astype(F32))


def _pad_rows(t, rows):
    return jnp.pad(t, ((0, 0), (0, rows - t.shape[1]), (0, 0)))


def kernel(x_prompt, x_sample, cache_fox_k, cache_fox_v, cache_fox_logf, page_table, state_gdn_ssm,
           state_gdn_conv, w_in, gdn_conv_w, gdn_a_log, gdn_dt_bias, gdn_out_norm_w, fox_f_bias, w_out,
           norm_w, final_norm_w):
    bp, lp, d = x_prompt.shape
    bs, ls, _ = x_sample.shape
    depth = w_in.shape[0]
    assert depth == 1, "single-layer trunk"
    n_pool, page = cache_fox_k.shape[1], cache_fox_k.shape[2]

    w_big, w_qkv = _pack_w_call(w_in[0].T)
    w_o = w_out[0].astype(BF16)
    nw = norm_w[0].reshape(1, d)
    fnw = final_norm_w.reshape(1, d)
    conv_w = gdn_conv_w[0]
    alog_row = _gate_row(gdn_a_log[0], SM_DECAY)
    dtb_row = _gate_row(gdn_dt_bias[0], SM_DECAY)
    fb_row = _gate_row(fox_f_bias[0], SM_FORGET)
    onw = gdn_out_norm_w[0].reshape(1, HEAD_DIM)

    xp2 = x_prompt.reshape(bp * lp, d)
    hp, zg, sm, fcol, logf_t, qf, kf, vf, zf, k4, v4 = _proj_call(xp2, nw, w_big, tm=512, sample=False,
                                                                  fb_row=fb_row, seq_len=lp)
    r3 = lambda t: t.reshape(bp, lp, t.shape[-1])
    og_p, ssm_p, tail = _gdn_call(r3(hp), r3(zg), r3(sm), conv_w, alog_row, dtb_row, onw,
                                  c=GDN_CHUNK, l_valid=lp, nb=4, w_qkv=w_qkv)
    of_p = _fox_prompt_call(r3(qf), r3(kf), r3(vf), r3(fcol), r3(zf), tq=512)
    y_p = _out_call(og_p.reshape(bp * lp, GROUP_W), of_p.reshape(bp * lp, GROUP_W), xp2, w_o, fnw, tm=1024)

    y_prompt = y_p.reshape(bp, lp, d)
    k_prompt = k4.reshape(1, bp, lp, N_HEADS, HEAD_DIM)
    v_prompt = v4.reshape(1, bp, lp, N_HEADS, HEAD_DIM)
    logf_prompt = logf_t.transpose(0, 2, 1).reshape(1, bp, lp, N_HEADS)
    ssm_prompt = ssm_p.reshape(1, bp, N_HEADS, HEAD_DIM, HEAD_DIM)
    conv_prompt = tail[:, SUBLANES - (CONV_K - 1):, :].reshape(1, bp, CONV_K - 1, CONV_DIM)

    xs2 = x_sample.reshape(bs * ls, d)
    qkv_s, zg_s, sm_s, q4_s, k4_s, v4_s, z4_s = _proj_call(xs2, nw, w_big, tm=256, sample=True)
    r3s = lambda t: t.reshape(bs, ls, t.shape[-1])
    p8 = lambda t: _pad_rows(r3s(t), SUBLANES)
    og_s, ssm_s = _gdn_call(p8(qkv_s), p8(zg_s), p8(sm_s), conv_w, alog_row, dtb_row, onw, c=SUBLANES,
                            l_valid=ls, nb=16, s0=state_gdn_ssm[0], c0=state_gdn_conv[0])
    kcache = cache_fox_k[0].reshape(n_pool, page * N_HEADS, HEAD_DIM)
    vcache = cache_fox_v[0].reshape(n_pool, page * N_HEADS, HEAD_DIM)
    lcache = cache_fox_logf[0].transpose(0, 2, 1)
    of_s, logf_s = _fox_decode_call(page_table, q4_s, k4_s, v4_s, z4_s, p8(sm_s), fb_row,
                                    kcache, vcache, lcache, l_new=ls)
    og_s2 = og_s[:, :ls].reshape(bs * ls, GROUP_W)
    of_s2 = of_s.reshape(bs * ls, GROUP_W)
    y_s = _out_call(og_s2, of_s2, xs2, w_o, fnw, tm=256)

    y_sample = y_s.reshape(bs, ls, d)
    k_sample = k4_s.reshape(1, bs, ls, N_HEADS, HEAD_DIM)
    v_sample = v4_s.reshape(1, bs, ls, N_HEADS, HEAD_DIM)
    logf_sample = logf_s[:, :ls, SM_FORGET:SM_FORGET + N_HEADS].reshape(1, bs, ls, N_HEADS)
    ssm_sample = ssm_s.reshape(1, bs, N_HEADS, HEAD_DIM, HEAD_DIM)
    if ls >= CONV_K - 1:
        conv_sample = r3s(qkv_s)[:, ls - (CONV_K - 1):, :]
    else:
        conv_sample = jnp.concatenate([state_gdn_conv[0], r3s(qkv_s)], axis=1)[:, -(CONV_K - 1):, :]
    conv_sample = conv_sample.reshape(1, bs, CONV_K - 1, CONV_DIM)

    return (y_prompt, y_sample, k_prompt, v_prompt, logf_prompt, ssm_prompt, conv_prompt,
            k_sample, v_sample, logf_sample, ssm_sample, conv_sample)
```

```python
import functools
import math

import jax
import jax.numpy as jnp
from jax import lax
from jax.experimental import pallas as pl
from jax.experimental.pallas import tpu as pltpu

F32 = jnp.float32
BF16 = jnp.bfloat16

NORM_EPS = 1e-6
L2_EPS = 1e-6
HEAD_DIM = 128
N_HEADS = 4
GROUP_W = N_HEADS * HEAD_DIM
CONV_DIM = 3 * GROUP_W
CONV_K = 4
LANES = 128
SUBLANES = 8
GDN_CHUNK = 64
INV_BASE = 32
SM_BETA = 0
SM_DECAY = 4
SM_FORGET = 8
VMEM_LIMIT = 56 * 1024 * 1024


def _sigmoid(x):
    return 1.0 / (1.0 + jnp.exp(-x))


def _softplus(x):
    return jnp.maximum(x, 0.0) + jnp.log(1.0 + jnp.exp(-jnp.abs(x)))


def _log_sigmoid(x):
    return -_softplus(-x)


def _bdot(a, b):
    return jnp.dot(a.astype(BF16), b.astype(BF16), preferred_element_type=F32)


def _bdot_nt(a, b):
    return lax.dot_general(a.astype(BF16), b.astype(BF16), (((1,), (1,)), ((), ())),
                           preferred_element_type=F32)


def _bdot_tn(a, b):
    return lax.dot_general(a.astype(BF16), b.astype(BF16), (((0,), (0,)), ((), ())),
                           preferred_element_type=F32)


def _fdot(a, b):
    return jnp.dot(a, b, preferred_element_type=F32, precision=lax.Precision.HIGHEST)


def _iota2(shape, dim):
    return lax.broadcasted_iota(jnp.int32, shape, dim)


W_QKV, W_ZG, W_QF, W_KF, W_VF, W_ZF, W_SM, W_END = 0, 1536, 2048, 2560, 3072, 3584, 4096, 4224
SRC_GATES_G, SRC_FOX, SRC_GATE_F, SRC_END = 2048, 2056, 4104, 4108


def _pack_w_kernel(w_ref, o_ref, qkv_ref):
    qkv_ref[...] = w_ref[W_QKV:W_ZG, :].T.astype(BF16)
    o_ref[W_QKV:W_QF, :] = w_ref[0:SRC_GATES_G, :].astype(BF16)
    o_ref[W_QF:W_SM, :] = w_ref[SRC_FOX:SRC_GATE_F, :].astype(BF16)
    n_gate = (SRC_FOX - SRC_GATES_G) + (SRC_END - SRC_GATE_F)
    gates = jnp.concatenate([w_ref[SRC_GATES_G:SRC_FOX, :], w_ref[SRC_GATE_F:SRC_END, :],
                             jnp.zeros((W_END - W_SM - n_gate, w_ref.shape[1]), F32)], axis=0)
    o_ref[W_SM:W_END, :] = gates.astype(BF16)


def _pack_w_call(w_t):
    return pl.pallas_call(
        _pack_w_kernel,
        out_shape=[jax.ShapeDtypeStruct((W_END, w_t.shape[1]), BF16),
                   jax.ShapeDtypeStruct((w_t.shape[1], CONV_DIM), BF16)],
        compiler_params=pltpu.CompilerParams(vmem_limit_bytes=VMEM_LIMIT),
        name="pack_w",
    )(w_t)


def _store_head_rows(ref, val, tm):
    for h in range(N_HEADS):
        ref[pl.ds(h, tm, stride=N_HEADS), :] = val[:, h * HEAD_DIM:(h + 1) * HEAD_DIM].astype(ref.dtype)


def _conv_silu_qkv(xbuf, cw_ref, g, rows):
    cols = slice(g * GROUP_W, (g + 1) * GROUP_W)
    x = xbuf[0:rows + SUBLANES, cols]
    y = x[SUBLANES:] * cw_ref[CONV_K - 1:CONV_K, cols]
    for j in range(CONV_K - 1):
        shifted = pltpu.roll(x, CONV_K - 1 - j, axis=0)
        y = y + shifted[SUBLANES:] * cw_ref[j:j + 1, cols]
    return y * _sigmoid(y)


def _l2_normalize(t, scale):
    return t * (lax.rsqrt(jnp.sum(t * t, axis=-1, keepdims=True) + L2_EPS) * scale)


def _proj_kernel(x_ref, nw_ref, w_ref, *refs, tm, sample, seq_tiles):
    x = x_ref[...]
    var = jnp.mean(x * x, axis=-1, keepdims=True)
    h = (x * lax.rsqrt(var + NORM_EPS) * nw_ref[...]).astype(BF16)
    seg = lambda lo, hi: lax.dot_general(h, w_ref[lo:hi, :], (((1,), (1,)), ((), ())),
                                         preferred_element_type=F32)
    if sample:
        qkv_ref, zg_ref, sm_ref, q4_ref, k4_ref, v4_ref, z4_ref = refs
        qkv_ref[...] = seg(W_QKV, W_ZG)
        zg_ref[...] = seg(W_ZG, W_QF)
        sm_ref[...] = seg(W_SM, W_END)
        _store_head_rows(q4_ref, seg(W_QF, W_KF), tm)
        _store_head_rows(k4_ref, seg(W_KF, W_VF), tm)
        _store_head_rows(v4_ref, seg(W_VF, W_ZF), tm)
        _store_head_rows(z4_ref, seg(W_ZF, W_SM), tm)
        return
    (fb_ref, h_ref, zg_ref, sm_ref, fcol_ref, logft_ref, qb_ref, kb_ref, vb_ref, zf_ref, k4_ref, v4_ref,
     carry_ref) = refs
    h_ref[...] = h
    zg_ref[...] = seg(W_ZG, W_QF)
    sm = seg(W_SM, W_END)
    sm_ref[...] = sm

    @pl.when(pl.program_id(0) % seq_tiles == 0)
    def _():
        carry_ref[...] = jnp.zeros(carry_ref.shape, F32)

    tri = (_iota2((LANES, LANES), 0) >= _iota2((LANES, LANES), 1)).astype(BF16)
    carry = carry_ref[0:1, :]
    for i in range(tm // LANES):
        blk = slice(i * LANES, (i + 1) * LANES)
        lf = _log_sigmoid(sm[blk] + fb_ref[...])
        parts = jnp.dot(tri, jnp.concatenate(_split3(lf), axis=1), preferred_element_type=F32)
        f = parts[:, 0:LANES] + parts[:, LANES:2 * LANES] + parts[:, 2 * LANES:3 * LANES] + carry
        carry = f[LANES - 1:LANES, :]
        fcol_ref[blk, :] = f
        logft_ref[:, blk] = lf.T[SM_FORGET:SM_FORGET + N_HEADS, :]
    carry_ref[0:1, :] = carry
    qb_ref[...] = (seg(W_QF, W_KF) * (HEAD_DIM ** -0.5)).astype(BF16)
    kf = seg(W_KF, W_VF)
    _store_head_rows(k4_ref, kf, tm)
    kb_ref[...] = kf.astype(BF16)
    vf = seg(W_VF, W_ZF)
    _store_head_rows(v4_ref, vf, tm)
    vb_ref[...] = vf.astype(BF16)
    zf_ref[...] = seg(W_ZF, W_SM)


def _proj_call(x2d, norm_w, w_big, tm, sample, fb_row=None, seq_len=None):
    t, d = x2d.shape
    n = w_big.shape[0]
    wide = lambda w, dt: (jax.ShapeDtypeStruct((t, w), dt), pl.BlockSpec((tm, w), lambda i: (i, 0)))
    rows4 = (jax.ShapeDtypeStruct((t * N_HEADS, HEAD_DIM), F32),
             pl.BlockSpec((tm * N_HEADS, HEAD_DIM), lambda i: (i, 0)))
    operands = [x2d, norm_w, w_big]
    in_specs = [pl.BlockSpec((tm, d), lambda i: (i, 0)),
                pl.BlockSpec((1, d), lambda i: (0, 0)),
                pl.BlockSpec((n, d), lambda i: (0, 0))]
    scratch = []
    seq_tiles = 1
    if sample:
        outs = [wide(CONV_DIM, F32), wide(GROUP_W, F32), wide(LANES, F32), rows4, rows4, rows4, rows4]
    else:
        seq_tiles = seq_len // tm
        logft = (jax.ShapeDtypeStruct((t // seq_len, N_HEADS, seq_len), F32),
                 pl.BlockSpec((None, N_HEADS, tm), lambda i: (i // seq_tiles, 0, i % seq_tiles)))
        outs = [wide(d, BF16), wide(GROUP_W, F32), wide(LANES, F32), wide(LANES, F32), logft,
                wide(GROUP_W, BF16), wide(GROUP_W, BF16), wide(GROUP_W, BF16), wide(GROUP_W, F32), rows4, rows4]
        operands.append(fb_row)
        in_specs.append(pl.BlockSpec((1, LANES), lambda i: (0, 0)))
        scratch = [pltpu.VMEM((SUBLANES, LANES), F32)]
    out_shape = [o[0] for o in outs]
    out_specs = [o[1] for o in outs]
    return pl.pallas_call(
        functools.partial(_proj_kernel, tm=tm, sample=sample, seq_tiles=seq_tiles),
        grid=(t // tm,),
        in_specs=in_specs,
        out_specs=out_specs,
        out_shape=out_shape,
        scratch_shapes=scratch,
        compiler_params=pltpu.CompilerParams(dimension_semantics=("arbitrary",),
                                             vmem_limit_bytes=VMEM_LIMIT),
        name="proj",
    )(*operands)


def _gdn_kernel(*refs, c, l_valid, nb, project, chunk_of_step=None, hooks=(None, None)):
    if project:
        (h_ref, wqkv_ref, zg_ref, sm_ref, cw_ref, alog_ref, dtb_ref, onw_ref,
         og_ref, sout_ref, tail_ref, xbuf, s_scr) = refs
    else:
        (qkv_ref, zg_ref, sm_ref, cw_ref, alog_ref, dtb_ref, onw_ref, s0_ref, c0_ref,
         og_ref, sout_ref, xbuf, s_scr) = refs
    if chunk_of_step is None:
        ci, n_c = pl.program_id(1), pl.num_programs(1)
    else:
        ci, n_c = chunk_of_step

    @pl.when(ci == 0)
    def _():
        if project:
            xbuf[:, 0:SUBLANES, :] = jnp.zeros((nb, SUBLANES, CONV_DIM), F32)
            s_scr[...] = jnp.zeros(s_scr.shape, F32)
        else:
            xbuf[:, SUBLANES - (CONV_K - 1):SUBLANES, :] = c0_ref[...]
            s_scr[...] = s0_ref[...]

    row = _iota2((c, 1), 0) + ci * c
    valid = jnp.broadcast_to((row < l_valid).astype(F32), (c, LANES))
    tri_incl = (_iota2((c, c), 0) >= _iota2((c, c), 1))
    tri_strict = (_iota2((c, c), 0) > _iota2((c, c), 1))
    eye = (_iota2((c, c), 0) == _iota2((c, c), 1)).astype(F32)
    pad_rows = LANES - c
    sl = lambda base, h: slice(base + h * HEAD_DIM, base + (h + 1) * HEAD_DIM)

    if project:
        raw = jnp.dot(h_ref[...].reshape(nb * c, h_ref.shape[-1]), wqkv_ref[...],
                      preferred_element_type=F32)
    if hooks[0] is not None:
        hooks[0]()
    q, k, v, beta, gc, gc_row, gc_last = [], [], [], [], [], [], []
    for bb in range(nb):
        xbuf[bb, SUBLANES:SUBLANES + c, :] = raw[bb * c:(bb + 1) * c] if project else qkv_ref[bb]
        yq, yk, yv = (_conv_silu_qkv(xbuf.at[bb], cw_ref, g, c) for g in range(3))
        if project:
            tail_ref[bb] = xbuf[bb, c:c + SUBLANES, :]
        xbuf[bb, 0:SUBLANES, :] = xbuf[bb, c:c + SUBLANES, :]
        sm = sm_ref[bb]
        beta_t = _sigmoid(sm) * valid
        g_t = -jnp.exp(alog_ref[...]) * _softplus(sm + dtb_ref[...]) * valid
        gc_t = _fdot(tri_incl.astype(F32), g_t)
        gc_sq = jnp.concatenate([gc_t, jnp.zeros((pad_rows, LANES), F32)], axis=0) if pad_rows else gc_t
        gc_tr = gc_sq.T
        for h in range(N_HEADS):
            q.append(_l2_normalize(yq[:, sl(0, h)], HEAD_DIM ** -0.5))
            k.append(_l2_normalize(yk[:, sl(0, h)], 1.0) * valid)
            v.append(yv[:, sl(0, h)])
            beta.append(jnp.broadcast_to(beta_t[:, SM_BETA + h:SM_BETA + h + 1], (c, HEAD_DIM)))
            gc.append(jnp.broadcast_to(gc_t[:, SM_DECAY + h:SM_DECAY + h + 1], (c, HEAD_DIM)))
            gc_row.append(gc_tr[SM_DECAY + h:SM_DECAY + h + 1, 0:c])
            gc_last.append(jnp.broadcast_to(gc_t[c - 1:c, SM_DECAY + h:SM_DECAY + h + 1], (1, HEAD_DIM)))

    chains = range(nb * N_HEADS)
    decay = [jnp.where(tri_incl, jnp.exp(jnp.where(tri_incl, gc[i][:, 0:c] - gc_row[i], 0.0)), 0.0)
             for i in chains]
    kb = [k[i] * beta[i] for i in chains]
    kkqk = [_bdot_nt(jnp.concatenate([kb[i], q[i]], axis=0), k[i]) for i in chains]
    qk = [kkqk[i][c:2 * c] * decay[i] for i in chains]
    if hooks[1] is not None:
        hooks[1]()
    neg_a = [-jnp.where(tri_strict, kkqk[i][0:c] * decay[i], 0.0) for i in chains]
    base = min(INV_BASE, c)
    blk_r, blk_c = _iota2((c, c), 0), _iota2((c, c), 1)
    same = lambda size: (blk_r // size) == (blk_c // size)
    diag = [jnp.where(same(base), neg_a[i], 0.0) for i in chains] if base < c else neg_a
    t_inv = [eye + diag[i] for i in chains]
    pw = [_bdot(diag[i], diag[i]) for i in chains]
    n_sq = int(math.log2(base))
    for j in range(1, n_sq):
        if j < n_sq - 1:
            both = [_bdot(jnp.concatenate([t_inv[i], pw[i]], axis=0), pw[i]) for i in chains]
            t_inv = [t_inv[i] + both[i][0:c] for i in chains]
            pw = [both[i][c:2 * c] for i in chains]
        else:
            t_inv = [t_inv[i] + _bdot(t_inv[i], pw[i]) for i in chains]
    size = base
    while size < c:
        off = [jnp.where(same(2 * size) & ~same(size), neg_a[i], 0.0) for i in chains]
        right = [_bdot(off[i], t_inv[i]) for i in chains]
        t_inv = [t_inv[i] + _bdot(t_inv[i], right[i]) for i in chains]
        size *= 2
    egc = [jnp.exp(gc[i]) for i in chains]
    sol = [_bdot(t_inv[i], jnp.concatenate([v[i] * beta[i], kb[i] * egc[i]], axis=-1)) for i in chains]
    s = [s_scr[i // N_HEADS, i % N_HEADS] for i in chains]
    ws = [_bdot(jnp.concatenate([sol[i][:, HEAD_DIM:2 * HEAD_DIM], q[i] * egc[i]], axis=0), s[i])
          for i in chains]
    v_new = [sol[i][:, 0:HEAD_DIM] - ws[i][0:c] for i in chains]
    o = [ws[i][c:2 * c] + _bdot(qk[i], v_new[i]) for i in chains]
    k_dec = [k[i] * jnp.exp(gc_last[i] - gc[i]) for i in chains]
    s_new = [s[i] * jnp.exp(gc_last[i]) + _bdot_tn(k_dec[i], v_new[i]) for i in chains]
    for i in chains:
        bb, h = i // N_HEADS, i % N_HEADS
        s_scr[bb, h] = s_new[i]
        oh = o[i] * lax.rsqrt(jnp.mean(o[i] * o[i], axis=-1, keepdims=True) + NORM_EPS) * onw_ref[...]
        z = zg_ref[bb, :, sl(0, h)]
        og_ref[bb, :, sl(0, h)] = (oh * (z * _sigmoid(z))).astype(og_ref.dtype)

    @pl.when(ci == n_c - 1)
    def _():
        sout_ref[...] = s_scr[...]


def _gdn_call(src, zg, sm, conv_w, alog_row, dtb_row, onw, *, c, l_valid, nb, w_qkv=None, s0=None, c0=None):
    project = w_qkv is not None
    b, l, _ = zg.shape
    n_c = l // c
    assert not project or l_valid == l
    kern = functools.partial(_gdn_kernel, c=c, l_valid=l_valid, nb=nb, project=project)
    blk = lambda w: pl.BlockSpec((nb, c, w), lambda bi, ci: (bi, ci, 0))
    full = lambda shape: pl.BlockSpec(shape, lambda bi, ci: (0,) * len(shape))
    state = pl.BlockSpec((nb, N_HEADS, HEAD_DIM, HEAD_DIM), lambda bi, ci: (bi, 0, 0, 0))
    rows8 = pl.BlockSpec((nb, SUBLANES, CONV_DIM), lambda bi, ci: (bi, 0, 0))
    common = [blk(GROUP_W), blk(LANES), full((CONV_K, CONV_DIM)), full((1, LANES)), full((1, LANES)),
              full((1, HEAD_DIM))]
    out_specs = [blk(GROUP_W), state]
    out_shape = [jax.ShapeDtypeStruct((b, l, GROUP_W), BF16),
                 jax.ShapeDtypeStruct((b, N_HEADS, HEAD_DIM, HEAD_DIM), F32)]
    if project:
        d = src.shape[-1]
        operands = (src, w_qkv, zg, sm, conv_w, alog_row, dtb_row, onw)
        in_specs = [blk(d), full((d, CONV_DIM))] + common
        out_specs.append(rows8)
        out_shape.append(jax.ShapeDtypeStruct((b, SUBLANES, CONV_DIM), F32))
    else:
        operands = (src, zg, sm, conv_w, alog_row, dtb_row, onw, s0, c0)
        in_specs = [blk(CONV_DIM)] + common + [
            state, pl.BlockSpec((nb, CONV_K - 1, CONV_DIM), lambda bi, ci: (bi, 0, 0))]
    return pl.pallas_call(
        kern,
        grid=(b // nb, n_c),
        in_specs=in_specs,
        out_specs=out_specs,
        out_shape=out_shape,
        scratch_shapes=[pltpu.VMEM((nb, c + SUBLANES, CONV_DIM), F32),
                        pltpu.VMEM((nb, N_HEADS, HEAD_DIM, HEAD_DIM), F32)],
        compiler_params=pltpu.CompilerParams(dimension_semantics=("arbitrary", "arbitrary"),
                                             vmem_limit_bytes=VMEM_LIMIT),
        name="gdn",
    )(*operands)


NEG_BIG = -1e30


def _forget_columns(f_tile, h, rows, for_keys):
    f = jnp.broadcast_to(f_tile[:, SM_FORGET + h:SM_FORGET + h + 1], (rows, LANES))
    f1, f2, f3 = (t.astype(F32) for t in _split3(-f if for_keys else f))
    lane = _iota2((rows, LANES), 1)
    base = 3 if for_keys else 0
    ones = ((lane >= 3 - base) & (lane < 6 - base)).astype(F32)
    cols = jnp.where(lane == base, f1, jnp.where(lane == base + 1, f2, jnp.where(lane == base + 2, f3, ones)))
    return cols.astype(BF16)


ROW_GROUP = 32


def _fox_prompt_kernel(q_ref, k_ref, v_ref, fcol_ref, zf_ref, o_ref,
                       kx_ref, qa_ref, s_ref, p_ref, acc_ref, m_ref, a_ref, *, tq, l):
    qi = pl.program_id(1)
    heads = range(N_HEADS)
    sl = lambda h: slice(h * HEAD_DIM, (h + 1) * HEAD_DIM)
    nt = (((1,), (1,)), ((), ()))

    @pl.when(qi == 0)
    def _():
        for r in range(l // tq):
            for h in heads:
                kx_ref[r * tq:(r + 1) * tq, sl(h)] = _forget_columns(fcol_ref[r * tq:(r + 1) * tq, :], h, tq, True)

    f_q = fcol_ref[pl.ds(pl.multiple_of(qi * tq, tq), tq), :]
    for h in heads:
        qa_ref[h, :, 0:HEAD_DIM] = q_ref[:, sl(h)]
        qa_ref[h, :, HEAD_DIM:2 * HEAD_DIM] = _forget_columns(f_q, h, tq, False)
    acc_ref[...] = jnp.zeros(acc_ref.shape, F32)
    m_ref[...] = jnp.full(m_ref.shape, NEG_BIG, F32)
    ones = jnp.ones((tq, HEAD_DIM), BF16)

    half = tq // 2

    def block(ki, masked):
        start = pl.multiple_of(ki * tq, tq)
        pieces = [(slice(0, half), half), (slice(half, tq), tq)] if masked else [(slice(0, tq), tq)]
        for h in heads:
            for qr, nk in pieces:
                keys = pl.ds(start, nk)
                ka = jnp.concatenate([k_ref[keys, sl(h)], kx_ref[keys, sl(h)]], axis=1)
                s_ref[h, qr, 0:nk] = lax.dot_general(qa_ref[h, qr, :], ka, nt, preferred_element_type=F32)
        for h in heads:
            for r in range(0, tq, ROW_GROUP):
                rg = slice(r, r + ROW_GROUP)
                nk = half if (masked and r < half) else tq
                s = s_ref[h, rg, 0:nk]
                if masked:
                    keep = _iota2((ROW_GROUP, nk), 1) <= _iota2((ROW_GROUP, nk), 0) + r
                    s = jnp.where(keep, s, NEG_BIG)
                m_old = m_ref[h, rg, :]
                m_new = jnp.maximum(m_old, jnp.max(s, axis=-1, keepdims=True))
                a_ref[h, rg, :] = jnp.exp(m_old - m_new)
                m_ref[h, rg, :] = m_new
                p_ref[h, rg, 0:nk] = jnp.exp(s - jnp.concatenate([m_new] * (nk // LANES), axis=1)).astype(BF16)
        for h in heads:
            alpha = a_ref[h]
            for qr, nk in pieces:
                keys = pl.ds(start, nk)
                pv = jnp.dot(p_ref[h, qr, 0:nk], jnp.concatenate([v_ref[keys, sl(h)], ones[0:nk]], axis=1),
                             preferred_element_type=F32)
                acc_ref[h, qr, :] = acc_ref[h, qr, :] * jnp.concatenate([alpha[qr], alpha[qr]], axis=1) + pv

    def body(ki, carry):
        block(ki, False)
        return carry

    lax.fori_loop(0, qi, body, 0)
    block(qi, True)
    for h in heads:
        z = zf_ref[:, sl(h)]
        o = acc_ref[h, :, 0:HEAD_DIM] / acc_ref[h, :, HEAD_DIM:2 * HEAD_DIM]
        o_ref[:, sl(h)] = (o * (z * _sigmoid(z))).astype(o_ref.dtype)


def _fox_prompt_call(qf, kf, vf, fcol, zf, tq):
    b, l, _ = qf.shape
    kern = functools.partial(_fox_prompt_kernel, tq=tq, l=l)
    qblk = lambda w: pl.BlockSpec((None, tq, w), lambda bi, qi: (bi, qi, 0))
    seq = lambda w: pl.BlockSpec((None, l, w), lambda bi, qi: (bi, 0, 0))
    return pl.pallas_call(
        kern,
        grid=(b, l // tq),
        in_specs=[qblk(GROUP_W), seq(GROUP_W), seq(GROUP_W), seq(LANES), qblk(GROUP_W)],
        out_specs=qblk(GROUP_W),
        out_shape=jax.ShapeDtypeStruct((b, l, GROUP_W), BF16),
        scratch_shapes=[pltpu.VMEM((l, GROUP_W), BF16),
                        pltpu.VMEM((N_HEADS, tq, 2 * HEAD_DIM), BF16),
                        pltpu.VMEM((N_HEADS, tq, tq), F32),
                        pltpu.VMEM((N_HEADS, tq, tq), BF16),
                        pltpu.VMEM((N_HEADS, tq, 2 * HEAD_DIM), F32),
                        pltpu.VMEM((N_HEADS, tq, LANES), F32),
                        pltpu.VMEM((N_HEADS, tq, LANES), F32)],
        compiler_params=pltpu.CompilerParams(dimension_semantics=("arbitrary", "arbitrary"),
                                             vmem_limit_bytes=VMEM_LIMIT),
        name="fox_prompt",
    )(qf, kf, vf, fcol, zf)


def _out_kernel(og_ref, of_ref, x_ref, w_ref, fnw_ref, y_ref):
    o = jnp.dot(og_ref[...], w_ref[0:GROUP_W, :], preferred_element_type=F32)
    o = o + jnp.dot(of_ref[...], w_ref[GROUP_W:2 * GROUP_W, :], preferred_element_type=F32)
    y = x_ref[...] + o
    var = jnp.mean(y * y, axis=-1, keepdims=True)
    y_ref[...] = y * lax.rsqrt(var + NORM_EPS) * fnw_ref[...]


def _out_call(og, of, x2d, w_out, fnw, tm):
    t, d = x2d.shape
    return pl.pallas_call(
        _out_kernel,
        grid=(t // tm,),
        in_specs=[pl.BlockSpec((tm, GROUP_W), lambda i: (i, 0)),
                  pl.BlockSpec((tm, GROUP_W), lambda i: (i, 0)),
                  pl.BlockSpec((tm, d), lambda i: (i, 0)),
                  pl.BlockSpec((2 * GROUP_W, d), lambda i: (0, 0)),
                  pl.BlockSpec((1, d), lambda i: (0, 0))],
        out_specs=pl.BlockSpec((tm, d), lambda i: (i, 0)),
        out_shape=jax.ShapeDtypeStruct((t, d), F32),
        compiler_params=pltpu.CompilerParams(dimension_semantics=("arbitrary",),
                                             vmem_limit_bytes=VMEM_LIMIT),
        name="out_proj",
    )(og, of, x2d, w_out, fnw)


def _page_copies(pt_ref, kc_ref, vc_ref, lc_ref, kbuf, vbuf, lbuf, sems, bi, slot, n_pages, pg):
    copies = []
    for p in range(n_pages):
        pid = pt_ref[bi, p]
        copies.append((pltpu.make_async_copy(kc_ref.at[pid], kbuf.at[slot, pl.ds(p * pg, pg)],
                                             sems.at[0, slot]), 0))
        copies.append((pltpu.make_async_copy(vc_ref.at[pid], vbuf.at[slot, pl.ds(p * pg, pg)],
                                             sems.at[1, slot]), 1))
        copies.append((pltpu.make_async_copy(lc_ref.at[pid], lbuf.at[slot, :, p, :], sems.at[2, slot]), p % 2))
    return copies


def _split3(x):
    x1 = x.astype(BF16)
    r1 = x - x1.astype(F32)
    x2 = r1.astype(BF16)
    x3 = (r1 - x2.astype(F32)).astype(BF16)
    return x1, x2, x3


def _fox_decode_kernel(*refs, **kw):
    for _ in _fox_decode_phases(*refs, **kw):
        pass


def _fox_decode_phases(pt_ref, q_ref, kn_ref, vn_ref, zf_ref, sm_ref, fbrow_ref, cums_ref,
                       kc_ref, vc_ref, lc_ref, o_ref, logf_ref, kbuf, vbuf, lbuf, sems,
                       *, n_pages, pg, l_new):
    bi = pl.program_id(0)
    nb = pl.num_programs(0)
    slot = bi % 2
    copies = functools.partial(_page_copies, pt_ref, kc_ref, vc_ref, lc_ref, kbuf, vbuf, lbuf, sems,
                               n_pages=n_pages, pg=pg)
    nr = l_new * N_HEADS

    @pl.when(bi == 0)
    def _():
        for cp, prio in copies(bi=bi, slot=slot):
            cp.start(priority=prio)

    @pl.when(bi + 1 < nb)
    def _():
        for cp, prio in copies(bi=bi + 1, slot=1 - slot):
            cp.start(priority=prio)

    for cp, _ in copies(bi=bi, slot=slot):
        cp.wait()
    yield

    scale = HEAD_DIM ** -0.5
    res = jnp.zeros((3 * n_pages, 2 * pg), F32)
    for h in range(N_HEADS):
        res = res + jnp.dot(jnp.concatenate(_split3(lbuf[slot, h]), axis=0), cums_ref[h],
                            preferred_element_type=F32)
    res = res[0:n_pages] + res[n_pages:2 * n_pages] + res[2 * n_pages:3 * n_pages]
    within, tot = res[:, 0:pg], res[:, pg:2 * pg]
    earlier = (_iota2((n_pages, n_pages), 0) > _iota2((n_pages, n_pages), 1)).astype(F32)
    carry = _fdot(earlier, tot)
    f_past = within + carry
    f_tot_row = carry[n_pages - 1:n_pages, :] + tot[n_pages - 1:n_pages, :]

    tok_valid = (_iota2((SUBLANES, 1), 0) < l_new).astype(F32)
    lf_col = _log_sigmoid(sm_ref[...] + fbrow_ref[...]) * tok_valid
    logf_ref[...] = lf_col
    r_tok = _iota2((nr, SUBLANES), 0) // N_HEADS
    csum = _fdot((_iota2((nr, SUBLANES), 1) <= r_tok).astype(F32), lf_col)
    q = q_ref[...].astype(BF16)
    s_all = lax.dot_general(q, kbuf[slot].astype(BF16), (((1,), (1,)), ((), ())),
                            preferred_element_type=F32)
    s_new = lax.dot_general(q, kn_ref[...].astype(BF16), (((1,), (1,)), ((), ())),
                            preferred_element_type=F32)
    yield
    own_lane = _iota2((nr, LANES), 1) == SM_FORGET + _iota2((nr, LANES), 0) % N_HEADS
    fq_new = jnp.sum(jnp.where(own_lane, csum, 0.0), axis=-1, keepdims=True)
    eye = _iota2((nr, nr), 0) == _iota2((nr, nr), 1)
    f_tot_col = jnp.sum(jnp.where(eye, jnp.broadcast_to(f_tot_row[:, 0:nr], (nr, nr)), 0.0),
                        axis=-1, keepdims=True)
    fq = fq_new + f_tot_col
    fq_row = jnp.sum(jnp.where(eye, jnp.broadcast_to(fq, (nr, nr)), 0.0), axis=0, keepdims=True)

    same_head = (_iota2((nr, pg), 1) % N_HEADS) == (_iota2((nr, pg), 0) % N_HEADS)
    sp = [jnp.where(same_head, s_all[:, p * pg:(p + 1) * pg] * scale + (fq - f_past[p:p + 1, :]), NEG_BIG)
          for p in range(n_pages)]
    rr, cc = _iota2((nr, nr), 0), _iota2((nr, nr), 1)
    new_ok = (rr % N_HEADS == cc % N_HEADS) & (cc // N_HEADS <= rr // N_HEADS)
    s_new = jnp.where(new_ok, s_new * scale + (fq - fq_row), NEG_BIG)
    m_el = sp[0]
    for p in range(1, n_pages):
        m_el = jnp.maximum(m_el, sp[p])
    m = jnp.maximum(jnp.max(m_el, axis=-1, keepdims=True), jnp.max(s_new, axis=-1, keepdims=True))
    pp = [jnp.exp(t - m) for t in sp]
    p_new = jnp.exp(s_new - m)
    l_el = pp[0]
    for p in range(1, n_pages):
        l_el = l_el + pp[p]
    l = jnp.sum(l_el, axis=-1, keepdims=True) + jnp.sum(p_new, axis=-1, keepdims=True)
    p_all = jnp.concatenate([t.astype(BF16) for t in pp], axis=-1)
    acc = jnp.dot(p_all, vbuf[slot].astype(BF16), preferred_element_type=F32)
    acc = acc + jnp.dot(p_new.astype(BF16), vn_ref[...].astype(BF16), preferred_element_type=F32)
    z = zf_ref[...]
    o_ref[...] = ((acc / l) * (z * _sigmoid(z))).astype(o_ref.dtype)


def _head_cumsum_matrix(page):
    t = jnp.arange(page)[None, :, None]
    j = jnp.arange(page * N_HEADS)[None, None, :]
    h = jnp.arange(N_HEADS)[:, None, None]
    own = (j % N_HEADS) == h
    c = own & (t <= j // N_HEADS)
    b = jnp.broadcast_to(own, c.shape)
    return jnp.concatenate([c, b], axis=2).astype(BF16)


N_DECODE_INPUTS, N_DECODE_OUTPUTS, N_DECODE_SCRATCH = 10, 2, 4
N_GDN_INPUTS, N_GDN_OUTPUTS = 8, 3


def _decode_gdn_kernel(pt_ref, *refs, decode_kw, gdn_kw, n_chunks):
    take = lambda n: (refs[:n], refs[n:])
    dec_in, refs = take(N_DECODE_INPUTS)
    gdn_in, refs = take(N_GDN_INPUTS)
    dec_out, refs = take(N_DECODE_OUTPUTS)
    gdn_out, refs = take(N_GDN_OUTPUTS)
    dec_scr, gdn_scr = take(N_DECODE_SCRATCH)
    decode = _fox_decode_phases(pt_ref, *dec_in, *dec_out, *dec_scr, **decode_kw)
    next(decode)
    _gdn_kernel(*gdn_in, *gdn_out, *gdn_scr, **gdn_kw,
                chunk_of_step=(pl.program_id(0) % n_chunks, n_chunks),
                hooks=(lambda: next(decode),
                       lambda: next(decode, None)))


def _fox_decode_call(page_table, q4, kn4, vn4, zf4, sm8, fb_row, kcache, vcache, lcache, l_new, gdn=None):
    b, n_pages = page_table.shape
    pg = kcache.shape[1]
    page = lcache.shape[2]
    nr = l_new * N_HEADS
    assert nr % (2 * SUBLANES) == 0 and l_new <= SUBLANES
    decode_kw = dict(n_pages=n_pages, pg=pg, l_new=l_new)
    rows = pl.BlockSpec((nr, HEAD_DIM), lambda i, pt: (i, 0))
    tok = pl.BlockSpec((None, SUBLANES, LANES), lambda i, pt: (i, 0, 0))
    const = lambda shape: pl.BlockSpec(shape, lambda i, pt: (0,) * len(shape))
    any_spec = pl.BlockSpec(memory_space=pl.ANY)
    operands = [page_table, q4, kn4, vn4, zf4, sm8, fb_row, _head_cumsum_matrix(page), kcache, vcache, lcache]
    in_specs = [rows, rows, rows, rows, tok, const((1, LANES)), const((N_HEADS, page, 2 * pg)),
                any_spec, any_spec, any_spec]
    out_specs = [rows, tok]
    out_shape = [jax.ShapeDtypeStruct((b * nr, HEAD_DIM), BF16), jax.ShapeDtypeStruct((b, SUBLANES, LANES), F32)]
    scratch = [pltpu.VMEM((2, n_pages * pg, HEAD_DIM), F32),
               pltpu.VMEM((2, n_pages * pg, HEAD_DIM), F32),
               pltpu.VMEM((2, N_HEADS, n_pages, page), F32),
               pltpu.SemaphoreType.DMA((3, 2))]
    kern = functools.partial(_fox_decode_kernel, **decode_kw)
    if gdn is not None:
        h3, w_qkv, zg, sm, conv_w, alog_row, dtb_row, onw, c = gdn
        bp, l, d = h3.shape
        n_c = l // c
        nb = bp * n_c // b
        assert nb >= 1 and (bp // nb) * n_c == b
        blk = lambda w: pl.BlockSpec((nb, c, w), lambda i, pt: (i // n_c, i % n_c, 0))
        state = pl.BlockSpec((nb, N_HEADS, HEAD_DIM, HEAD_DIM), lambda i, pt: (i // n_c, 0, 0, 0))
        rows8 = pl.BlockSpec((nb, SUBLANES, CONV_DIM), lambda i, pt: (i // n_c, 0, 0))
        operands += [h3, w_qkv, zg, sm, conv_w, alog_row, dtb_row, onw]
        in_specs += [blk(d), const((d, CONV_DIM)), blk(GROUP_W), blk(LANES), const((CONV_K, CONV_DIM)),
                     const((1, LANES)), const((1, LANES)), const((1, HEAD_DIM))]
        out_specs += [blk(GROUP_W), state, rows8]
        out_shape += [jax.ShapeDtypeStruct((bp, l, GROUP_W), BF16),
                      jax.ShapeDtypeStruct((bp, N_HEADS, HEAD_DIM, HEAD_DIM), F32),
                      jax.ShapeDtypeStruct((bp, SUBLANES, CONV_DIM), F32)]
        scratch += [pltpu.VMEM((nb, c + SUBLANES, CONV_DIM), F32),
                    pltpu.VMEM((nb, N_HEADS, HEAD_DIM, HEAD_DIM), F32)]
        kern = functools.partial(_decode_gdn_kernel, decode_kw=decode_kw, n_chunks=n_c,
                                 gdn_kw=dict(c=c, l_valid=l, nb=nb, project=True))
    grid_spec = pltpu.PrefetchScalarGridSpec(num_scalar_prefetch=1, grid=(b,), in_specs=in_specs,
                                             out_specs=out_specs, scratch_shapes=scratch)
    return pl.pallas_call(
        kern,
        grid_spec=grid_spec,
        out_shape=out_shape,
        compiler_params=pltpu.CompilerParams(dimension_semantics=("arbitrary",),
                                             vmem_limit_bytes=VMEM_LIMIT),
        name="fox_decode",
    )(*operands)


def _gate_row(vals, offset):
    return jnp.zeros((1, LANES), F32).at[0, offset:offset + N_HEADS].set(vals.astype(F32))


def _pad_rows(t, rows):
    return jnp.pad(t, ((0, 0), (0, rows - t.shape[1]), (0, 0)))


def kernel(x_prompt, x_sample, cache_fox_k, cache_fox_v, cache_fox_logf, page_table, state_gdn_ssm,
           state_gdn_conv, w_in, gdn_conv_w, gdn_a_log, gdn_dt_bias, gdn_out_norm_w, fox_f_bias, w_out,
           norm_w, final_norm_w):
    bp, lp, d = x_prompt.shape
    bs, ls, _ = x_sample.shape
    depth = w_in.shape[0]
    assert depth == 1, "single-layer trunk"
    n_pool, page = cache_fox_k.shape[1], cache_fox_k.shape[2]

    w_big, w_qkv = _pack_w_call(w_in[0].T)
    w_o = w_out[0].astype(BF16)
    nw = norm_w[0].reshape(1, d)
    fnw = final_norm_w.reshape(1, d)
    conv_w = gdn_conv_w[0]
    alog_row = _gate_row(gdn_a_log[0], SM_DECAY)
    dtb_row = _gate_row(gdn_dt_bias[0], SM_DECAY)
    fb_row = _gate_row(fox_f_bias[0], SM_FORGET)
    onw = gdn_out_norm_w[0].reshape(1, HEAD_DIM)

    xp2 = x_prompt.reshape(bp * lp, d)
    hp, zg, sm, fcol, logf_t, qf, kf, vf, zf, k4, v4 = _proj_call(xp2, nw, w_big, tm=512, sample=False,
                                                                  fb_row=fb_row, seq_len=lp)
    r3 = lambda t: t.reshape(bp, lp, t.shape[-1])
    of_p = _fox_prompt_call(r3(qf), r3(kf), r3(vf), r3(fcol), r3(zf), tq=512)

    xs2 = x_sample.reshape(bs * ls, d)
    qkv_s, zg_s, sm_s, q4_s, k4_s, v4_s, z4_s = _proj_call(xs2, nw, w_big, tm=256, sample=True)
    r3s = lambda t: t.reshape(bs, ls, t.shape[-1])
    p8 = lambda t: _pad_rows(r3s(t), SUBLANES)
    og_s, ssm_s = _gdn_call(p8(qkv_s), p8(zg_s), p8(sm_s), conv_w, alog_row, dtb_row, onw, c=SUBLANES,
                            l_valid=ls, nb=16, s0=state_gdn_ssm[0], c0=state_gdn_conv[0])
    kcache = cache_fox_k[0].reshape(n_pool, page * N_HEADS, HEAD_DIM)
    vcache = cache_fox_v[0].reshape(n_pool, page * N_HEADS, HEAD_DIM)
    lcache = cache_fox_logf[0].transpose(0, 2, 1)
    of_s, logf_s, og_p, ssm_p, tail = _fox_decode_call(
        page_table, q4_s, k4_s, v4_s, z4_s, p8(sm_s), fb_row, kcache, vcache, lcache, l_new=ls,
        gdn=(r3(hp), w_qkv, r3(zg), r3(sm), conv_w, alog_row, dtb_row, onw, GDN_CHUNK))
    og_s2 = og_s[:, :ls].reshape(bs * ls, GROUP_W)
    of_s2 = of_s.reshape(bs * ls, GROUP_W)
    y_s = _out_call(og_s2, of_s2, xs2, w_o, fnw, tm=256)
    y_p = _out_call(og_p.reshape(bp * lp, GROUP_W), of_p.reshape(bp * lp, GROUP_W), xp2, w_o, fnw, tm=1024)

    y_prompt = y_p.reshape(bp, lp, d)
    k_prompt = k4.reshape(1, bp, lp, N_HEADS, HEAD_DIM)
    v_prompt = v4.reshape(1, bp, lp, N_HEADS, HEAD_DIM)
    logf_prompt = logf_t.transpose(0, 2, 1).reshape(1, bp, lp, N_HEADS)
    ssm_prompt = ssm_p.reshape(1, bp, N_HEADS, HEAD_DIM, HEAD_DIM)
    conv_prompt = tail[:, SUBLANES - (CONV_K - 1):, :].reshape(1, bp, CONV_K - 1, CONV_DIM)
    y_sample = y_s.reshape(bs, ls, d)
    k_sample = k4_s.reshape(1, bs, ls, N_HEADS, HEAD_DIM)
    v_sample = v4_s.reshape(1, bs, ls, N_HEADS, HEAD_DIM)
    logf_sample = logf_s[:, :ls, SM_FORGET:SM_FORGET + N_HEADS].reshape(1, bs, ls, N_HEADS)
    ssm_sample = ssm_s.reshape(1, bs, N_HEADS, HEAD_DIM, HEAD_DIM)
    if ls >= CONV_K - 1:
        conv_sample = r3s(qkv_s)[:, ls - (CONV_K - 1):, :]
    else:
        conv_sample = jnp.concatenate([state_gdn_conv[0], r3s(qkv_s)], axis=1)[:, -(CONV_K - 1):, :]
    conv_sample = conv_sample.reshape(1, bs, CONV_K - 1, CONV_DIM)

    return (y_prompt, y_sample, k_prompt, v_prompt, logf_prompt, ssm_prompt, conv_prompt,
            k_sample, v_sample, logf_sample, ssm_sample, conv_sample)
```

```python
import functools
import math

import jax
import jax.numpy as jnp
from jax import lax
from jax.experimental import pallas as pl
from jax.experimental.pallas import tpu as pltpu

F32 = jnp.float32
BF16 = jnp.bfloat16

NORM_EPS = 1e-6
L2_EPS = 1e-6
HEAD_DIM = 128
N_HEADS = 4
GROUP_W = N_HEADS * HEAD_DIM
CONV_DIM = 3 * GROUP_W
CONV_K = 4
LANES = 128
SUBLANES = 8
GDN_CHUNK = 32
INV_BASE = 32
SM_BETA = 0
SM_DECAY = 4
SM_FORGET = 8
VMEM_LIMIT = 56 * 1024 * 1024


def _sigmoid(x):
    return 1.0 / (1.0 + jnp.exp(-x))


def _softplus(x):
    return jnp.maximum(x, 0.0) + jnp.log(1.0 + jnp.exp(-jnp.abs(x)))


def _log_sigmoid(x):
    return -_softplus(-x)


def _bdot(a, b):
    return jnp.dot(a.astype(BF16), b.astype(BF16), preferred_element_type=F32)


def _bdot_nt(a, b):
    return lax.dot_general(a.astype(BF16), b.astype(BF16), (((1,), (1,)), ((), ())),
                           preferred_element_type=F32)


def _bdot_tn(a, b):
    return lax.dot_general(a.astype(BF16), b.astype(BF16), (((0,), (0,)), ((), ())),
                           preferred_element_type=F32)


def _fdot(a, b):
    return jnp.dot(a, b, preferred_element_type=F32, precision=lax.Precision.HIGHEST)


def _iota2(shape, dim):
    return lax.broadcasted_iota(jnp.int32, shape, dim)


W_QKV, W_ZG, W_QF, W_KF, W_VF, W_ZF, W_SM, W_END = 0, 1536, 2048, 2560, 3072, 3584, 4096, 4224
SRC_GATES_G, SRC_FOX, SRC_GATE_F, SRC_END = 2048, 2056, 4104, 4108


def _pack_w_kernel(w_ref, o_ref, qkv_ref):
    qkv_ref[...] = w_ref[W_QKV:W_ZG, :].T.astype(BF16)
    o_ref[W_QKV:W_QF, :] = w_ref[0:SRC_GATES_G, :].astype(BF16)
    o_ref[W_QF:W_SM, :] = w_ref[SRC_FOX:SRC_GATE_F, :].astype(BF16)
    n_gate = (SRC_FOX - SRC_GATES_G) + (SRC_END - SRC_GATE_F)
    gates = jnp.concatenate([w_ref[SRC_GATES_G:SRC_FOX, :], w_ref[SRC_GATE_F:SRC_END, :],
                             jnp.zeros((W_END - W_SM - n_gate, w_ref.shape[1]), F32)], axis=0)
    o_ref[W_SM:W_END, :] = gates.astype(BF16)


def _pack_w_call(w_t):
    return pl.pallas_call(
        _pack_w_kernel,
        out_shape=[jax.ShapeDtypeStruct((W_END, w_t.shape[1]), BF16),
                   jax.ShapeDtypeStruct((w_t.shape[1], CONV_DIM), BF16)],
        compiler_params=pltpu.CompilerParams(vmem_limit_bytes=VMEM_LIMIT),
        name="pack_w",
    )(w_t)


def _store_head_rows(ref, val, tm):
    for h in range(N_HEADS):
        ref[pl.ds(h, tm, stride=N_HEADS), :] = val[:, h * HEAD_DIM:(h + 1) * HEAD_DIM].astype(ref.dtype)


def _conv_silu_qkv(xbuf, cw_ref, g, rows):
    cols = slice(g * GROUP_W, (g + 1) * GROUP_W)
    x = xbuf[0:rows + SUBLANES, cols]
    y = x[SUBLANES:] * cw_ref[CONV_K - 1:CONV_K, cols]
    for j in range(CONV_K - 1):
        shifted = pltpu.roll(x, CONV_K - 1 - j, axis=0)
        y = y + shifted[SUBLANES:] * cw_ref[j:j + 1, cols]
    return y * _sigmoid(y)


def _l2_normalize(t, scale):
    return t * (lax.rsqrt(jnp.sum(t * t, axis=-1, keepdims=True) + L2_EPS) * scale)


def _proj_kernel(x_ref, nw_ref, w_ref, *refs, tm, sample, seq_tiles):
    x = x_ref[...]
    var = jnp.mean(x * x, axis=-1, keepdims=True)
    h = (x * lax.rsqrt(var + NORM_EPS) * nw_ref[...]).astype(BF16)
    seg = lambda lo, hi: lax.dot_general(h, w_ref[lo:hi, :], (((1,), (1,)), ((), ())),
                                         preferred_element_type=F32)
    if sample:
        qkv_ref, zg_ref, sm_ref, q4_ref, k4_ref, v4_ref, z4_ref = refs
        qkv_ref[...] = seg(W_QKV, W_ZG)
        zg_ref[...] = seg(W_ZG, W_QF)
        sm_ref[...] = seg(W_SM, W_END)
        _store_head_rows(q4_ref, seg(W_QF, W_KF), tm)
        _store_head_rows(k4_ref, seg(W_KF, W_VF), tm)
        _store_head_rows(v4_ref, seg(W_VF, W_ZF), tm)
        _store_head_rows(z4_ref, seg(W_ZF, W_SM), tm)
        return
    (fb_ref, h_ref, zg_ref, sm_ref, fcol_ref, logft_ref, qb_ref, kb_ref, vb_ref, zf_ref, k4_ref, v4_ref,
     carry_ref) = refs
    h_ref[...] = h

    @pl.when(pl.program_id(0) % seq_tiles == 0)
    def _():
        carry_ref[...] = jnp.zeros(carry_ref.shape, F32)

    sm = seg(W_SM, W_END)
    sm_ref[...] = sm
    zg_ref[...] = seg(W_ZG, W_QF)
    qb_ref[...] = (seg(W_QF, W_KF) * (HEAD_DIM ** -0.5)).astype(BF16)
    kf = seg(W_KF, W_VF)
    _store_head_rows(k4_ref, kf, tm)
    kb_ref[...] = kf.astype(BF16)
    vf = seg(W_VF, W_ZF)
    _store_head_rows(v4_ref, vf, tm)
    vb_ref[...] = vf.astype(BF16)
    zf_ref[...] = seg(W_ZF, W_SM)

    tri = (_iota2((LANES, LANES), 0) >= _iota2((LANES, LANES), 1)).astype(BF16)
    blocks = [slice(i * LANES, (i + 1) * LANES) for i in range(tm // LANES)]
    lf = [_log_sigmoid(sm[blk] + fb_ref[...]) for blk in blocks]
    parts = [jnp.dot(tri, jnp.concatenate(_split3(t), axis=1), preferred_element_type=F32) for t in lf]
    within = [p[:, 0:LANES] + p[:, LANES:2 * LANES] + p[:, 2 * LANES:3 * LANES] for p in parts]
    carry = carry_ref[0:1, :]
    for i, blk in enumerate(blocks):
        fcol_ref[blk, :] = within[i] + carry
        carry = carry + within[i][LANES - 1:LANES, :]
        logft_ref[:, blk] = lf[i].T[SM_FORGET:SM_FORGET + N_HEADS, :]
    carry_ref[0:1, :] = carry


def _proj_call(x2d, norm_w, w_big, tm, sample, fb_row=None, seq_len=None):
    t, d = x2d.shape
    n = w_big.shape[0]
    wide = lambda w, dt: (jax.ShapeDtypeStruct((t, w), dt), pl.BlockSpec((tm, w), lambda i: (i, 0)))
    rows4 = (jax.ShapeDtypeStruct((t * N_HEADS, HEAD_DIM), F32),
             pl.BlockSpec((tm * N_HEADS, HEAD_DIM), lambda i: (i, 0)))
    operands = [x2d, norm_w, w_big]
    in_specs = [pl.BlockSpec((tm, d), lambda i: (i, 0)),
                pl.BlockSpec((1, d), lambda i: (0, 0)),
                pl.BlockSpec((n, d), lambda i: (0, 0))]
    scratch = []
    seq_tiles = 1
    if sample:
        outs = [wide(CONV_DIM, F32), wide(GROUP_W, F32), wide(LANES, F32), rows4, rows4, rows4, rows4]
    else:
        seq_tiles = seq_len // tm
        logft = (jax.ShapeDtypeStruct((t // seq_len, N_HEADS, seq_len), F32),
                 pl.BlockSpec((None, N_HEADS, tm), lambda i: (i // seq_tiles, 0, i % seq_tiles)))
        outs = [wide(d, BF16), wide(GROUP_W, F32), wide(LANES, F32), wide(LANES, F32), logft,
                wide(GROUP_W, BF16), wide(GROUP_W, BF16), wide(GROUP_W, BF16), wide(GROUP_W, F32), rows4, rows4]
        operands.append(fb_row)
        in_specs.append(pl.BlockSpec((1, LANES), lambda i: (0, 0)))
        scratch = [pltpu.VMEM((SUBLANES, LANES), F32)]
    out_shape = [o[0] for o in outs]
    out_specs = [o[1] for o in outs]
    return pl.pallas_call(
        functools.partial(_proj_kernel, tm=tm, sample=sample, seq_tiles=seq_tiles),
        grid=(t // tm,),
        in_specs=in_specs,
        out_specs=out_specs,
        out_shape=out_shape,
        scratch_shapes=scratch,
        compiler_params=pltpu.CompilerParams(dimension_semantics=("arbitrary",),
                                             vmem_limit_bytes=VMEM_LIMIT),
        name="proj",
    )(*operands)


def _gdn_kernel(*refs, c, l_valid, nb, project, chunk_of_step=None, hooks=(None, None)):
    if project:
        (h_ref, wqkv_ref, zg_ref, sm_ref, cw_ref, alog_ref, dtb_ref, onw_ref,
         og_ref, sout_ref, tail_ref, xbuf, s_scr) = refs
    else:
        (qkv_ref, zg_ref, sm_ref, cw_ref, alog_ref, dtb_ref, onw_ref, s0_ref, c0_ref,
         og_ref, sout_ref, xbuf, s_scr) = refs
    if chunk_of_step is None:
        ci, n_c = pl.program_id(1), pl.num_programs(1)
    else:
        ci, n_c = chunk_of_step

    @pl.when(ci == 0)
    def _():
        if project:
            xbuf[:, 0:SUBLANES, :] = jnp.zeros((nb, SUBLANES, CONV_DIM), F32)
            s_scr[...] = jnp.zeros(s_scr.shape, F32)
        else:
            xbuf[:, SUBLANES - (CONV_K - 1):SUBLANES, :] = c0_ref[...]
            s_scr[...] = s0_ref[...]

    row = _iota2((c, 1), 0) + ci * c
    valid = jnp.broadcast_to((row < l_valid).astype(F32), (c, LANES))
    tri_incl = (_iota2((c, c), 0) >= _iota2((c, c), 1))
    tri_strict = (_iota2((c, c), 0) > _iota2((c, c), 1))
    eye = (_iota2((c, c), 0) == _iota2((c, c), 1)).astype(F32)
    pad_rows = LANES - c
    sl = lambda base, h: slice(base + h * HEAD_DIM, base + (h + 1) * HEAD_DIM)

    if project:
        raw = jnp.dot(h_ref[...].reshape(nb * c, h_ref.shape[-1]), wqkv_ref[...],
                      preferred_element_type=F32)
    if hooks[0] is not None:
        hooks[0]()
    q, k, v, beta, gc, gc_row, gc_last = [], [], [], [], [], [], []
    for bb in range(nb):
        xbuf[bb, SUBLANES:SUBLANES + c, :] = raw[bb * c:(bb + 1) * c] if project else qkv_ref[bb]
        yq, yk, yv = (_conv_silu_qkv(xbuf.at[bb], cw_ref, g, c) for g in range(3))
        if project:
            tail_ref[bb] = xbuf[bb, c:c + SUBLANES, :]
        xbuf[bb, 0:SUBLANES, :] = xbuf[bb, c:c + SUBLANES, :]
        sm = sm_ref[bb]
        beta_t = _sigmoid(sm) * valid
        g_t = -jnp.exp(alog_ref[...]) * _softplus(sm + dtb_ref[...]) * valid
        gc_t = _fdot(tri_incl.astype(F32), g_t)
        gc_sq = jnp.concatenate([gc_t, jnp.zeros((pad_rows, LANES), F32)], axis=0) if pad_rows else gc_t
        gc_tr = gc_sq.T
        for h in range(N_HEADS):
            q.append(_l2_normalize(yq[:, sl(0, h)], HEAD_DIM ** -0.5))
            k.append(_l2_normalize(yk[:, sl(0, h)], 1.0) * valid)
            v.append(yv[:, sl(0, h)])
            beta.append(jnp.broadcast_to(beta_t[:, SM_BETA + h:SM_BETA + h + 1], (c, HEAD_DIM)))
            gc.append(jnp.broadcast_to(gc_t[:, SM_DECAY + h:SM_DECAY + h + 1], (c, HEAD_DIM)))
            gc_row.append(gc_tr[SM_DECAY + h:SM_DECAY + h + 1, 0:c])
            gc_last.append(jnp.broadcast_to(gc_t[c - 1:c, SM_DECAY + h:SM_DECAY + h + 1], (1, HEAD_DIM)))

    chains = range(nb * N_HEADS)
    decay = [jnp.where(tri_incl, jnp.exp(jnp.where(tri_incl, gc[i][:, 0:c] - gc_row[i], 0.0)), 0.0)
             for i in chains]
    kb = [k[i] * beta[i] for i in chains]
    kkqk = [_bdot_nt(jnp.concatenate([kb[i], q[i]], axis=0), k[i]) for i in chains]
    qk = [kkqk[i][c:2 * c] * decay[i] for i in chains]
    if hooks[1] is not None:
        hooks[1]()
    neg_a = [-jnp.where(tri_strict, kkqk[i][0:c] * decay[i], 0.0) for i in chains]
    base = min(INV_BASE, c)
    blk_r, blk_c = _iota2((c, c), 0), _iota2((c, c), 1)
    same = lambda size: (blk_r // size) == (blk_c // size)
    diag = [jnp.where(same(base), neg_a[i], 0.0) for i in chains] if base < c else neg_a
    t_inv = [eye + diag[i] for i in chains]
    pw = [_bdot(diag[i], diag[i]) for i in chains]
    n_sq = int(math.log2(base))
    for j in range(1, n_sq):
        if j < n_sq - 1:
            both = [_bdot(jnp.concatenate([t_inv[i], pw[i]], axis=0), pw[i]) for i in chains]
            t_inv = [t_inv[i] + both[i][0:c] for i in chains]
            pw = [both[i][c:2 * c] for i in chains]
        else:
            t_inv = [t_inv[i] + _bdot(t_inv[i], pw[i]) for i in chains]
    size = base
    while size < c:
        off = [jnp.where(same(2 * size) & ~same(size), neg_a[i], 0.0) for i in chains]
        right = [_bdot(off[i], t_inv[i]) for i in chains]
        t_inv = [t_inv[i] + _bdot(t_inv[i], right[i]) for i in chains]
        size *= 2
    egc = [jnp.exp(gc[i]) for i in chains]
    sol = [_bdot(t_inv[i], jnp.concatenate([v[i] * beta[i], kb[i] * egc[i]], axis=-1)) for i in chains]
    s = [s_scr[i // N_HEADS, i % N_HEADS] for i in chains]
    ws = [_bdot(jnp.concatenate([sol[i][:, HEAD_DIM:2 * HEAD_DIM], q[i] * egc[i]], axis=0), s[i])
          for i in chains]
    v_new = [sol[i][:, 0:HEAD_DIM] - ws[i][0:c] for i in chains]
    o = [ws[i][c:2 * c] + _bdot(qk[i], v_new[i]) for i in chains]
    k_dec = [k[i] * jnp.exp(gc_last[i] - gc[i]) for i in chains]
    s_new = [s[i] * jnp.exp(gc_last[i]) + _bdot_tn(k_dec[i], v_new[i]) for i in chains]
    for i in chains:
        bb, h = i // N_HEADS, i % N_HEADS
        s_scr[bb, h] = s_new[i]
        oh = o[i] * lax.rsqrt(jnp.mean(o[i] * o[i], axis=-1, keepdims=True) + NORM_EPS) * onw_ref[...]
        z = zg_ref[bb, :, sl(0, h)]
        og_ref[bb, :, sl(0, h)] = (oh * (z * _sigmoid(z))).astype(og_ref.dtype)

    @pl.when(ci == n_c - 1)
    def _():
        sout_ref[...] = s_scr[...]


def _gdn_call(src, zg, sm, conv_w, alog_row, dtb_row, onw, *, c, l_valid, nb, w_qkv=None, s0=None, c0=None):
    project = w_qkv is not None
    b, l, _ = zg.shape
    n_c = l // c
    assert not project or l_valid == l
    kern = functools.partial(_gdn_kernel, c=c, l_valid=l_valid, nb=nb, project=project)
    blk = lambda w: pl.BlockSpec((nb, c, w), lambda bi, ci: (bi, ci, 0))
    full = lambda shape: pl.BlockSpec(shape, lambda bi, ci: (0,) * len(shape))
    state = pl.BlockSpec((nb, N_HEADS, HEAD_DIM, HEAD_DIM), lambda bi, ci: (bi, 0, 0, 0))
    rows8 = pl.BlockSpec((nb, SUBLANES, CONV_DIM), lambda bi, ci: (bi, 0, 0))
    common = [blk(GROUP_W), blk(LANES), full((CONV_K, CONV_DIM)), full((1, LANES)), full((1, LANES)),
              full((1, HEAD_DIM))]
    out_specs = [blk(GROUP_W), state]
    out_shape = [jax.ShapeDtypeStruct((b, l, GROUP_W), BF16),
                 jax.ShapeDtypeStruct((b, N_HEADS, HEAD_DIM, HEAD_DIM), F32)]
    if project:
        d = src.shape[-1]
        operands = (src, w_qkv, zg, sm, conv_w, alog_row, dtb_row, onw)
        in_specs = [blk(d), full((d, CONV_DIM))] + common
        out_specs.append(rows8)
        out_shape.append(jax.ShapeDtypeStruct((b, SUBLANES, CONV_DIM), F32))
    else:
        operands = (src, zg, sm, conv_w, alog_row, dtb_row, onw, s0, c0)
        in_specs = [blk(CONV_DIM)] + common + [
            state, pl.BlockSpec((nb, CONV_K - 1, CONV_DIM), lambda bi, ci: (bi, 0, 0))]
    return pl.pallas_call(
        kern,
        grid=(b // nb, n_c),
        in_specs=in_specs,
        out_specs=out_specs,
        out_shape=out_shape,
        scratch_shapes=[pltpu.VMEM((nb, c + SUBLANES, CONV_DIM), F32),
                        pltpu.VMEM((nb, N_HEADS, HEAD_DIM, HEAD_DIM), F32)],
        compiler_params=pltpu.CompilerParams(dimension_semantics=("arbitrary", "arbitrary"),
                                             vmem_limit_bytes=VMEM_LIMIT),
        name="gdn",
    )(*operands)


NEG_BIG = -1e30


def _forget_columns(f_tile, h, rows, for_keys):
    f = jnp.broadcast_to(f_tile[:, SM_FORGET + h:SM_FORGET + h + 1], (rows, LANES))
    f1, f2, f3 = (t.astype(F32) for t in _split3(-f if for_keys else f))
    lane = _iota2((rows, LANES), 1)
    base = 3 if for_keys else 0
    ones = ((lane >= 3 - base) & (lane < 6 - base)).astype(F32)
    cols = jnp.where(lane == base, f1, jnp.where(lane == base + 1, f2, jnp.where(lane == base + 2, f3, ones)))
    return cols.astype(BF16)


ROW_GROUP = 32


def _fox_prompt_kernel(q_ref, k_ref, v_ref, fcol_ref, zf_ref, o_ref,
                       kx_ref, qa_ref, s_ref, p_ref, acc_ref, m_ref, a_ref, *, tq, l):
    qi = pl.program_id(1)
    heads = range(N_HEADS)
    sl = lambda h: slice(h * HEAD_DIM, (h + 1) * HEAD_DIM)
    nt = (((1,), (1,)), ((), ()))

    @pl.when(qi == 0)
    def _():
        for r in range(l // tq):
            for h in heads:
                kx_ref[r * tq:(r + 1) * tq, sl(h)] = _forget_columns(fcol_ref[r * tq:(r + 1) * tq, :], h, tq, True)

    f_q = fcol_ref[pl.ds(pl.multiple_of(qi * tq, tq), tq), :]
    for h in heads:
        qa_ref[h, :, 0:HEAD_DIM] = q_ref[:, sl(h)]
        qa_ref[h, :, HEAD_DIM:2 * HEAD_DIM] = _forget_columns(f_q, h, tq, False)
    acc_ref[...] = jnp.zeros(acc_ref.shape, F32)
    m_ref[...] = jnp.full(m_ref.shape, NEG_BIG, F32)
    ones = jnp.ones((tq, HEAD_DIM), BF16)

    half = tq // 2

    def block(ki, masked):
        start = pl.multiple_of(ki * tq, tq)
        pieces = [(slice(0, half), half), (slice(half, tq), tq)] if masked else [(slice(0, tq), tq)]
        for h in heads:
            for qr, nk in pieces:
                keys = pl.ds(start, nk)
                ka = jnp.concatenate([k_ref[keys, sl(h)], kx_ref[keys, sl(h)]], axis=1)
                s_ref[h, qr, 0:nk] = lax.dot_general(qa_ref[h, qr, :], ka, nt, preferred_element_type=F32)
        for h in heads:
            for r in range(0, tq, ROW_GROUP):
                rg = slice(r, r + ROW_GROUP)
                nk = half if (masked and r < half) else tq
                s = s_ref[h, rg, 0:nk]
                if masked:
                    keep = _iota2((ROW_GROUP, nk), 1) <= _iota2((ROW_GROUP, nk), 0) + r
                    s = jnp.where(keep, s, NEG_BIG)
                m_old = m_ref[h, rg, :]
                m_new = jnp.maximum(m_old, jnp.max(s, axis=-1, keepdims=True))
                a_ref[h, rg, :] = jnp.exp(m_old - m_new)
                m_ref[h, rg, :] = m_new
                p_ref[h, rg, 0:nk] = jnp.exp(s - jnp.concatenate([m_new] * (nk // LANES), axis=1)).astype(BF16)
        for h in heads:
            alpha = a_ref[h]
            for qr, nk in pieces:
                keys = pl.ds(start, nk)
                pv = jnp.dot(p_ref[h, qr, 0:nk], jnp.concatenate([v_ref[keys, sl(h)], ones[0:nk]], axis=1),
                             preferred_element_type=F32)
                acc_ref[h, qr, :] = acc_ref[h, qr, :] * jnp.concatenate([alpha[qr], alpha[qr]], axis=1) + pv

    def body(ki, carry):
        block(ki, False)
        return carry

    lax.fori_loop(0, qi, body, 0)
    block(qi, True)
    for h in heads:
        z = zf_ref[:, sl(h)]
        o = acc_ref[h, :, 0:HEAD_DIM] / acc_ref[h, :, HEAD_DIM:2 * HEAD_DIM]
        o_ref[:, sl(h)] = (o * (z * _sigmoid(z))).astype(o_ref.dtype)


def _fox_prompt_call(qf, kf, vf, fcol, zf, tq):
    b, l, _ = qf.shape
    kern = functools.partial(_fox_prompt_kernel, tq=tq, l=l)
    qblk = lambda w: pl.BlockSpec((None, tq, w), lambda bi, qi: (bi, qi, 0))
    seq = lambda w: pl.BlockSpec((None, l, w), lambda bi, qi: (bi, 0, 0))
    return pl.pallas_call(
        kern,
        grid=(b, l // tq),
        in_specs=[qblk(GROUP_W), seq(GROUP_W), seq(GROUP_W), seq(LANES), qblk(GROUP_W)],
        out_specs=qblk(GROUP_W),
        out_shape=jax.ShapeDtypeStruct((b, l, GROUP_W), BF16),
        scratch_shapes=[pltpu.VMEM((l, GROUP_W), BF16),
                        pltpu.VMEM((N_HEADS, tq, 2 * HEAD_DIM), BF16),
                        pltpu.VMEM((N_HEADS, tq, tq), F32),
                        pltpu.VMEM((N_HEADS, tq, tq), BF16),
                        pltpu.VMEM((N_HEADS, tq, 2 * HEAD_DIM), F32),
                        pltpu.VMEM((N_HEADS, tq, LANES), F32),
                        pltpu.VMEM((N_HEADS, tq, LANES), F32)],
        compiler_params=pltpu.CompilerParams(dimension_semantics=("arbitrary", "arbitrary"),
                                             vmem_limit_bytes=VMEM_LIMIT),
        name="fox_prompt",
    )(qf, kf, vf, fcol, zf)


def _out_kernel(og_ref, of_ref, x_ref, w_ref, fnw_ref, y_ref):
    o = jnp.dot(og_ref[...], w_ref[0:GROUP_W, :], preferred_element_type=F32)
    o = o + jnp.dot(of_ref[...], w_ref[GROUP_W:2 * GROUP_W, :], preferred_element_type=F32)
    y = x_ref[...] + o
    var = jnp.mean(y * y, axis=-1, keepdims=True)
    y_ref[...] = y * lax.rsqrt(var + NORM_EPS) * fnw_ref[...]


def _out_call(og, of, x2d, w_out, fnw, tm):
    t, d = x2d.shape
    return pl.pallas_call(
        _out_kernel,
        grid=(t // tm,),
        in_specs=[pl.BlockSpec((tm, GROUP_W), lambda i: (i, 0)),
                  pl.BlockSpec((tm, GROUP_W), lambda i: (i, 0)),
                  pl.BlockSpec((tm, d), lambda i: (i, 0)),
                  pl.BlockSpec((2 * GROUP_W, d), lambda i: (0, 0)),
                  pl.BlockSpec((1, d), lambda i: (0, 0))],
        out_specs=pl.BlockSpec((tm, d), lambda i: (i, 0)),
        out_shape=jax.ShapeDtypeStruct((t, d), F32),
        compiler_params=pltpu.CompilerParams(dimension_semantics=("arbitrary",),
                                             vmem_limit_bytes=VMEM_LIMIT),
        name="out_proj",
    )(og, of, x2d, w_out, fnw)


def _page_copies(pt_ref, kc_ref, vc_ref, lc_ref, kbuf, vbuf, lbuf, sems, bi, slot, n_pages, pg):
    copies = []
    for p in range(n_pages):
        pid = pt_ref[bi, p]
        copies.append((pltpu.make_async_copy(kc_ref.at[pid], kbuf.at[slot, pl.ds(p * pg, pg)],
                                             sems.at[0, slot]), 0))
        copies.append((pltpu.make_async_copy(vc_ref.at[pid], vbuf.at[slot, pl.ds(p * pg, pg)],
                                             sems.at[1, slot]), 1))
        copies.append((pltpu.make_async_copy(lc_ref.at[pid], lbuf.at[slot, :, p, :], sems.at[2, slot]), p % 2))
    return copies


def _split3(x):
    x1 = x.astype(BF16)
    r1 = x - x1.astype(F32)
    x2 = r1.astype(BF16)
    x3 = (r1 - x2.astype(F32)).astype(BF16)
    return x1, x2, x3


def _fox_decode_kernel(*refs, **kw):
    for _ in _fox_decode_phases(*refs, **kw):
        pass


def _fox_decode_phases(pt_ref, q_ref, kn_ref, vn_ref, zf_ref, sm_ref, fbrow_ref, cums_ref,
                       kc_ref, vc_ref, lc_ref, o_ref, logf_ref, kbuf, vbuf, lbuf, sems,
                       *, n_pages, pg, l_new):
    bi = pl.program_id(0)
    nb = pl.num_programs(0)
    slot = bi % 2
    copies = functools.partial(_page_copies, pt_ref, kc_ref, vc_ref, lc_ref, kbuf, vbuf, lbuf, sems,
                               n_pages=n_pages, pg=pg)
    nr = l_new * N_HEADS

    @pl.when(bi == 0)
    def _():
        for cp, prio in copies(bi=bi, slot=slot):
            cp.start(priority=prio)

    @pl.when(bi + 1 < nb)
    def _():
        for cp, prio in copies(bi=bi + 1, slot=1 - slot):
            cp.start(priority=prio)

    for cp, _ in copies(bi=bi, slot=slot):
        cp.wait()
    yield

    scale = HEAD_DIM ** -0.5
    res = jnp.zeros((3 * n_pages, 2 * pg), F32)
    for h in range(N_HEADS):
        res = res + jnp.dot(jnp.concatenate(_split3(lbuf[slot, h]), axis=0), cums_ref[h],
                            preferred_element_type=F32)
    res = res[0:n_pages] + res[n_pages:2 * n_pages] + res[2 * n_pages:3 * n_pages]
    within, tot = res[:, 0:pg], res[:, pg:2 * pg]
    earlier = (_iota2((n_pages, n_pages), 0) > _iota2((n_pages, n_pages), 1)).astype(F32)
    carry = _fdot(earlier, tot)
    f_past = within + carry
    f_tot_row = carry[n_pages - 1:n_pages, :] + tot[n_pages - 1:n_pages, :]

    tok_valid = (_iota2((SUBLANES, 1), 0) < l_new).astype(F32)
    lf_col = _log_sigmoid(sm_ref[...] + fbrow_ref[...]) * tok_valid
    logf_ref[...] = lf_col
    r_tok = _iota2((nr, SUBLANES), 0) // N_HEADS
    csum = _fdot((_iota2((nr, SUBLANES), 1) <= r_tok).astype(F32), lf_col)
    q = q_ref[...].astype(BF16)
    s_all = lax.dot_general(q, kbuf[slot].astype(BF16), (((1,), (1,)), ((), ())),
                            preferred_element_type=F32)
    s_new = lax.dot_general(q, kn_ref[...].astype(BF16), (((1,), (1,)), ((), ())),
                            preferred_element_type=F32)
    yield
    own_lane = _iota2((nr, LANES), 1) == SM_FORGET + _iota2((nr, LANES), 0) % N_HEADS
    fq_new = jnp.sum(jnp.where(own_lane, csum, 0.0), axis=-1, keepdims=True)
    eye = _iota2((nr, nr), 0) == _iota2((nr, nr), 1)
    f_tot_col = jnp.sum(jnp.where(eye, jnp.broadcast_to(f_tot_row[:, 0:nr], (nr, nr)), 0.0),
                        axis=-1, keepdims=True)
    fq = fq_new + f_tot_col
    fq_row = jnp.sum(jnp.where(eye, jnp.broadcast_to(fq, (nr, nr)), 0.0), axis=0, keepdims=True)

    same_head = (_iota2((nr, pg), 1) % N_HEADS) == (_iota2((nr, pg), 0) % N_HEADS)
    sp = [jnp.where(same_head, s_all[:, p * pg:(p + 1) * pg] * scale + (fq - f_past[p:p + 1, :]), NEG_BIG)
          for p in range(n_pages)]
    rr, cc = _iota2((nr, nr), 0), _iota2((nr, nr), 1)
    new_ok = (rr % N_HEADS == cc % N_HEADS) & (cc // N_HEADS <= rr // N_HEADS)
    s_new = jnp.where(new_ok, s_new * scale + (fq - fq_row), NEG_BIG)
    m_el = sp[0]
    for p in range(1, n_pages):
        m_el = jnp.maximum(m_el, sp[p])
    m = jnp.maximum(jnp.max(m_el, axis=-1, keepdims=True), jnp.max(s_new, axis=-1, keepdims=True))
    pp = [jnp.exp(t - m) for t in sp]
    p_new = jnp.exp(s_new - m)
    l_el = pp[0]
    for p in range(1, n_pages):
        l_el = l_el + pp[p]
    l = jnp.sum(l_el, axis=-1, keepdims=True) + jnp.sum(p_new, axis=-1, keepdims=True)
    p_all = jnp.concatenate([t.astype(BF16) for t in pp], axis=-1)
    acc = jnp.dot(p_all, vbuf[slot].astype(BF16), preferred_element_type=F32)
    acc = acc + jnp.dot(p_new.astype(BF16), vn_ref[...].astype(BF16), preferred_element_type=F32)
    z = zf_ref[...]
    o_ref[...] = ((acc / l) * (z * _sigmoid(z))).astype(o_ref.dtype)


def _head_cumsum_matrix(page):
    t = jnp.arange(page)[None, :, None]
    j = jnp.arange(page * N_HEADS)[None, None, :]
    h = jnp.arange(N_HEADS)[:, None, None]
    own = (j % N_HEADS) == h
    c = own & (t <= j // N_HEADS)
    b = jnp.broadcast_to(own, c.shape)
    return jnp.concatenate([c, b], axis=2).astype(BF16)


N_DECODE_INPUTS, N_DECODE_OUTPUTS, N_DECODE_SCRATCH = 10, 2, 4
N_GDN_INPUTS, N_GDN_OUTPUTS = 8, 3


def _decode_gdn_kernel(pt_ref, *refs, decode_kw, gdn_kw, n_chunks):
    take = lambda n: (refs[:n], refs[n:])
    dec_in, refs = take(N_DECODE_INPUTS)
    gdn_in, refs = take(N_GDN_INPUTS)
    dec_out, refs = take(N_DECODE_OUTPUTS)
    gdn_out, refs = take(N_GDN_OUTPUTS)
    dec_scr, gdn_scr = take(N_DECODE_SCRATCH)
    decode = _fox_decode_phases(pt_ref, *dec_in, *dec_out, *dec_scr, **decode_kw)
    next(decode)
    _gdn_kernel(*gdn_in, *gdn_out, *gdn_scr, **gdn_kw,
                chunk_of_step=(pl.program_id(0) % n_chunks, n_chunks),
                hooks=(lambda: next(decode),
                       lambda: next(decode, None)))


def _fox_decode_call(page_table, q4, kn4, vn4, zf4, sm8, fb_row, kcache, vcache, lcache, l_new, gdn=None):
    b, n_pages = page_table.shape
    pg = kcache.shape[1]
    page = lcache.shape[2]
    nr = l_new * N_HEADS
    assert nr % (2 * SUBLANES) == 0 and l_new <= SUBLANES
    decode_kw = dict(n_pages=n_pages, pg=pg, l_new=l_new)
    rows = pl.BlockSpec((nr, HEAD_DIM), lambda i, pt: (i, 0))
    tok = pl.BlockSpec((None, SUBLANES, LANES), lambda i, pt: (i, 0, 0))
    const = lambda shape: pl.BlockSpec(shape, lambda i, pt: (0,) * len(shape))
    any_spec = pl.BlockSpec(memory_space=pl.ANY)
    operands = [page_table, q4, kn4, vn4, zf4, sm8, fb_row, _head_cumsum_matrix(page), kcache, vcache, lcache]
    in_specs = [rows, rows, rows, rows, tok, const((1, LANES)), const((N_HEADS, page, 2 * pg)),
                any_spec, any_spec, any_spec]
    out_specs = [rows, tok]
    out_shape = [jax.ShapeDtypeStruct((b * nr, HEAD_DIM), BF16), jax.ShapeDtypeStruct((b, SUBLANES, LANES), F32)]
    scratch = [pltpu.VMEM((2, n_pages * pg, HEAD_DIM), F32),
               pltpu.VMEM((2, n_pages * pg, HEAD_DIM), F32),
               pltpu.VMEM((2, N_HEADS, n_pages, page), F32),
               pltpu.SemaphoreType.DMA((3, 2))]
    kern = functools.partial(_fox_decode_kernel, **decode_kw)
    if gdn is not None:
        h3, w_qkv, zg, sm, conv_w, alog_row, dtb_row, onw, c = gdn
        bp, l, d = h3.shape
        n_c = l // c
        nb = bp * n_c // b
        assert nb >= 1 and (bp // nb) * n_c == b
        blk = lambda w: pl.BlockSpec((nb, c, w), lambda i, pt: (i // n_c, i % n_c, 0))
        state = pl.BlockSpec((nb, N_HEADS, HEAD_DIM, HEAD_DIM), lambda i, pt: (i // n_c, 0, 0, 0))
        rows8 = pl.BlockSpec((nb, SUBLANES, CONV_DIM), lambda i, pt: (i // n_c, 0, 0))
        operands += [h3, w_qkv, zg, sm, conv_w, alog_row, dtb_row, onw]
        in_specs += [blk(d), const((d, CONV_DIM)), blk(GROUP_W), blk(LANES), const((CONV_K, CONV_DIM)),
                     const((1, LANES)), const((1, LANES)), const((1, HEAD_DIM))]
        out_specs += [blk(GROUP_W), state, rows8]
        out_shape += [jax.ShapeDtypeStruct((bp, l, GROUP_W), BF16),
                      jax.ShapeDtypeStruct((bp, N_HEADS, HEAD_DIM, HEAD_DIM), F32),
                      jax.ShapeDtypeStruct((bp, SUBLANES, CONV_DIM), F32)]
        scratch += [pltpu.VMEM((nb, c + SUBLANES, CONV_DIM), F32),
                    pltpu.VMEM((nb, N_HEADS, HEAD_DIM, HEAD_DIM), F32)]
        kern = functools.partial(_decode_gdn_kernel, decode_kw=decode_kw, n_chunks=n_c,
                                 gdn_kw=dict(c=c, l_valid=l, nb=nb, project=True))
    grid_spec = pltpu.PrefetchScalarGridSpec(num_scalar_prefetch=1, grid=(b,), in_specs=in_specs,
                                             out_specs=out_specs, scratch_shapes=scratch)
    return pl.pallas_call(
        kern,
        grid_spec=grid_spec,
        out_shape=out_shape,
        compiler_params=pltpu.CompilerParams(dimension_semantics=("arbitrary",),
                                             vmem_limit_bytes=VMEM_LIMIT),
        name="fox_decode",
    )(*operands)


def _gate_row(vals, offset):
    return jnp.zeros((1, LANES), F32).at[0, offset:offset + N_HEADS].set(vals.astype(F32))


def _pad_rows(t, rows):
    return jnp.pad(t, ((0, 0), (0, rows - t.shape[1]), (0, 0)))


def kernel(x_prompt, x_sample, cache_fox_k, cache_fox_v, cache_fox_logf, page_table, state_gdn_ssm,
           state_gdn_conv, w_in, gdn_conv_w, gdn_a_log, gdn_dt_bias, gdn_out_norm_w, fox_f_bias, w_out,
           norm_w, final_norm_w):
    bp, lp, d = x_prompt.shape
    bs, ls, _ = x_sample.shape
    depth = w_in.shape[0]
    assert depth == 1, "single-layer trunk"
    n_pool, page = cache_fox_k.shape[1], cache_fox_k.shape[2]

    w_big, w_qkv = _pack_w_call(w_in[0].T)
    w_o = w_out[0].astype(BF16)
    nw = norm_w[0].reshape(1, d)
    fnw = final_norm_w.reshape(1, d)
    conv_w = gdn_conv_w[0]
    alog_row = _gate_row(gdn_a_log[0], SM_DECAY)
    dtb_row = _gate_row(gdn_dt_bias[0], SM_DECAY)
    fb_row = _gate_row(fox_f_bias[0], SM_FORGET)
    onw = gdn_out_norm_w[0].reshape(1, HEAD_DIM)

    xp2 = x_prompt.reshape(bp * lp, d)
    hp, zg, sm, fcol, logf_t, qf, kf, vf, zf, k4, v4 = _proj_call(xp2, nw, w_big, tm=512, sample=False,
                                                                  fb_row=fb_row, seq_len=lp)
    r3 = lambda t: t.reshape(bp, lp, t.shape[-1])
    of_p = _fox_prompt_call(r3(qf), r3(kf), r3(vf), r3(fcol), r3(zf), tq=512)

    xs2 = x_sample.reshape(bs * ls, d)
    qkv_s, zg_s, sm_s, q4_s, k4_s, v4_s, z4_s = _proj_call(xs2, nw, w_big, tm=256, sample=True)
    r3s = lambda t: t.reshape(bs, ls, t.shape[-1])
    p8 = lambda t: _pad_rows(r3s(t), SUBLANES)
    og_s, ssm_s = _gdn_call(p8(qkv_s), p8(zg_s), p8(sm_s), conv_w, alog_row, dtb_row, onw, c=SUBLANES,
                            l_valid=ls, nb=16, s0=state_gdn_ssm[0], c0=state_gdn_conv[0])
    kcache = cache_fox_k[0].reshape(n_pool, page * N_HEADS, HEAD_DIM)
    vcache = cache_fox_v[0].reshape(n_pool, page * N_HEADS, HEAD_DIM)
    lcache = cache_fox_logf[0].transpose(0, 2, 1)
    of_s, logf_s, og_p, ssm_p, tail = _fox_decode_call(
        page_table, q4_s, k4_s, v4_s, z4_s, p8(sm_s), fb_row, kcache, vcache, lcache, l_new=ls,
        gdn=(r3(hp), w_qkv, r3(zg), r3(sm), conv_w, alog_row, dtb_row, onw, GDN_CHUNK))
    og_s2 = og_s[:, :ls].reshape(bs * ls, GROUP_W)
    of_s2 = of_s.reshape(bs * ls, GROUP_W)
    y_s = _out_call(og_s2, of_s2, xs2, w_o, fnw, tm=256)
    y_p = _out_call(og_p.reshape(bp * lp, GROUP_W), of_p.reshape(bp * lp, GROUP_W), xp2, w_o, fnw, tm=1024)

    y_prompt = y_p.reshape(bp, lp, d)
    k_prompt = k4.reshape(1, bp, lp, N_HEADS, HEAD_DIM)
    v_prompt = v4.reshape(1, bp, lp, N_HEADS, HEAD_DIM)
    logf_prompt = logf_t.transpose(0, 2, 1).reshape(1, bp, lp, N_HEADS)
    ssm_prompt = ssm_p.reshape(1, bp, N_HEADS, HEAD_DIM, HEAD_DIM)
    conv_prompt = tail[:, SUBLANES - (CONV_K - 1):, :].reshape(1, bp, CONV_K - 1, CONV_DIM)
    y_sample = y_s.reshape(bs, ls, d)
    k_sample = k4_s.reshape(1, bs, ls, N_HEADS, HEAD_DIM)
    v_sample = v4_s.reshape(1, bs, ls, N_HEADS, HEAD_DIM)
    logf_sample = logf_s[:, :ls, SM_FORGET:SM_FORGET + N_HEADS].reshape(1, bs, ls, N_HEADS)
    ssm_sample = ssm_s.reshape(1, bs, N_HEADS, HEAD_DIM, HEAD_DIM)
    if ls >= CONV_K - 1:
        conv_sample = r3s(qkv_s)[:, ls - (CONV_K - 1):, :]
    else:
        conv_sample = jnp.concatenate([state_gdn_conv[0], r3s(qkv_s)], axis=1)[:, -(CONV_K - 1):, :]
    conv_sample = conv_sample.reshape(1, bs, CONV_K - 1, CONV_DIM)

    return (y_prompt, y_sample, k_prompt, v_prompt, logf_prompt, ssm_prompt, conv_prompt,
            k_sample, v_sample, logf_sample, ssm_sample, conv_sample)
```

```python
import functools
import math

import jax
import jax.numpy as jnp
from jax import lax
from jax.experimental import pallas as pl
from jax.experimental.pallas import tpu as pltpu

F32 = jnp.float32
BF16 = jnp.bfloat16

NORM_EPS = 1e-6
L2_EPS = 1e-6
HEAD_DIM = 128
N_HEADS = 4
GROUP_W = N_HEADS * HEAD_DIM
CONV_DIM = 3 * GROUP_W
CONV_K = 4
LANES = 128
SUBLANES = 8
GDN_CHUNK = 64
INV_BASE = 32
SM_BETA = 0
SM_DECAY = 4
SM_FORGET = 8
VMEM_LIMIT = 56 * 1024 * 1024


def _sigmoid(x):
    return 1.0 / (1.0 + jnp.exp(-x))


def _softplus(x):
    return jnp.maximum(x, 0.0) + jnp.log(1.0 + jnp.exp(-jnp.abs(x)))


def _log_sigmoid(x):
    return -_softplus(-x)


def _bdot(a, b):
    return jnp.dot(a.astype(BF16), b.astype(BF16), preferred_element_type=F32)


def _bdot_nt(a, b):
    return lax.dot_general(a.astype(BF16), b.astype(BF16), (((1,), (1,)), ((), ())),
                           preferred_element_type=F32)


def _bdot_tn(a, b):
    return lax.dot_general(a.astype(BF16), b.astype(BF16), (((0,), (0,)), ((), ())),
                           preferred_element_type=F32)


def _fdot(a, b):
    return jnp.dot(a, b, preferred_element_type=F32, precision=lax.Precision.HIGHEST)


def _iota2(shape, dim):
    return lax.broadcasted_iota(jnp.int32, shape, dim)


W_QKV, W_ZG, W_QF, W_KF, W_VF, W_ZF, W_SM, W_END = 0, 1536, 2048, 2560, 3072, 3584, 4096, 4224
SRC_GATES_G, SRC_FOX, SRC_GATE_F, SRC_END = 2048, 2056, 4104, 4108


def _pack_w_kernel(w_ref, o_ref, qkv_ref):
    qkv_ref[...] = w_ref[W_QKV:W_ZG, :].T.astype(BF16)
    o_ref[W_QKV:W_QF, :] = w_ref[0:SRC_GATES_G, :].astype(BF16)
    o_ref[W_QF:W_SM, :] = w_ref[SRC_FOX:SRC_GATE_F, :].astype(BF16)
    n_gate = (SRC_FOX - SRC_GATES_G) + (SRC_END - SRC_GATE_F)
    gates = jnp.concatenate([w_ref[SRC_GATES_G:SRC_FOX, :], w_ref[SRC_GATE_F:SRC_END, :],
                             jnp.zeros((W_END - W_SM - n_gate, w_ref.shape[1]), F32)], axis=0)
    o_ref[W_SM:W_END, :] = gates.astype(BF16)


def _pack_w_call(w_t):
    return pl.pallas_call(
        _pack_w_kernel,
        out_shape=[jax.ShapeDtypeStruct((W_END, w_t.shape[1]), BF16),
                   jax.ShapeDtypeStruct((w_t.shape[1], CONV_DIM), BF16)],
        compiler_params=pltpu.CompilerParams(vmem_limit_bytes=VMEM_LIMIT),
        name="pack_w",
    )(w_t)


def _store_head_rows(ref, val, tm):
    for h in range(N_HEADS):
        ref[pl.ds(h, tm, stride=N_HEADS), :] = val[:, h * HEAD_DIM:(h + 1) * HEAD_DIM].astype(ref.dtype)


def _conv_silu_qkv(xbuf, cw_ref, g, rows):
    cols = slice(g * GROUP_W, (g + 1) * GROUP_W)
    x = xbuf[0:rows + SUBLANES, cols]
    y = x[SUBLANES:] * cw_ref[CONV_K - 1:CONV_K, cols]
    for j in range(CONV_K - 1):
        shifted = pltpu.roll(x, CONV_K - 1 - j, axis=0)
        y = y + shifted[SUBLANES:] * cw_ref[j:j + 1, cols]
    return y * _sigmoid(y)


def _l2_normalize(t, scale):
    return t * (lax.rsqrt(jnp.sum(t * t, axis=-1, keepdims=True) + L2_EPS) * scale)


def _proj_kernel(x_ref, nw_ref, w_ref, *refs, tm, sample, seq_tiles):
    x = x_ref[...]
    var = jnp.mean(x * x, axis=-1, keepdims=True)
    h = (x * lax.rsqrt(var + NORM_EPS) * nw_ref[...]).astype(BF16)
    seg = lambda lo, hi: lax.dot_general(h, w_ref[lo:hi, :], (((1,), (1,)), ((), ())),
                                         preferred_element_type=F32)
    if sample:
        qkv_ref, zg_ref, sm_ref, q4_ref, k4_ref, v4_ref, z4_ref = refs
        qkv_ref[...] = seg(W_QKV, W_ZG)
        zg_ref[...] = seg(W_ZG, W_QF)
        sm_ref[...] = seg(W_SM, W_END)
        _store_head_rows(q4_ref, seg(W_QF, W_KF), tm)
        _store_head_rows(k4_ref, seg(W_KF, W_VF), tm)
        _store_head_rows(v4_ref, seg(W_VF, W_ZF), tm)
        _store_head_rows(z4_ref, seg(W_ZF, W_SM), tm)
        return
    (fb_ref, h_ref, zg_ref, sm_ref, fcol_ref, logft_ref, qb_ref, kb_ref, vb_ref, zf_ref, k4_ref, v4_ref,
     carry_ref) = refs
    h_ref[...] = h

    @pl.when(pl.program_id(0) % seq_tiles == 0)
    def _():
        carry_ref[...] = jnp.zeros(carry_ref.shape, F32)

    sm = seg(W_SM, W_END)
    sm_ref[...] = sm
    zg_ref[...] = seg(W_ZG, W_QF)
    qb_ref[...] = (seg(W_QF, W_KF) * (HEAD_DIM ** -0.5)).astype(BF16)
    kf = seg(W_KF, W_VF)
    _store_head_rows(k4_ref, kf, tm)
    kb_ref[...] = kf.astype(BF16)
    vf = seg(W_VF, W_ZF)
    _store_head_rows(v4_ref, vf, tm)
    vb_ref[...] = vf.astype(BF16)
    zf_ref[...] = seg(W_ZF, W_SM)

    tri = (_iota2((LANES, LANES), 0) >= _iota2((LANES, LANES), 1)).astype(BF16)
    blocks = [slice(i * LANES, (i + 1) * LANES) for i in range(tm // LANES)]
    lf = [_log_sigmoid(sm[blk] + fb_ref[...]) for blk in blocks]
    parts = [jnp.dot(tri, jnp.concatenate(_split3(t), axis=1), preferred_element_type=F32) for t in lf]
    within = [p[:, 0:LANES] + p[:, LANES:2 * LANES] + p[:, 2 * LANES:3 * LANES] for p in parts]
    carry = carry_ref[0:1, :]
    for i, blk in enumerate(blocks):
        fcol_ref[blk, :] = within[i] + carry
        carry = carry + within[i][LANES - 1:LANES, :]
        logft_ref[:, blk] = lf[i].T[SM_FORGET:SM_FORGET + N_HEADS, :]
    carry_ref[0:1, :] = carry


def _proj_call(x2d, norm_w, w_big, tm, sample, fb_row=None, seq_len=None):
    t, d = x2d.shape
    n = w_big.shape[0]
    wide = lambda w, dt: (jax.ShapeDtypeStruct((t, w), dt), pl.BlockSpec((tm, w), lambda i: (i, 0)))
    rows4 = (jax.ShapeDtypeStruct((t * N_HEADS, HEAD_DIM), F32),
             pl.BlockSpec((tm * N_HEADS, HEAD_DIM), lambda i: (i, 0)))
    operands = [x2d, norm_w, w_big]
    in_specs = [pl.BlockSpec((tm, d), lambda i: (i, 0)),
                pl.BlockSpec((1, d), lambda i: (0, 0)),
                pl.BlockSpec((n, d), lambda i: (0, 0))]
    scratch = []
    seq_tiles = 1
    if sample:
        outs = [wide(CONV_DIM, F32), wide(GROUP_W, F32), wide(LANES, F32), rows4, rows4, rows4, rows4]
    else:
        seq_tiles = seq_len // tm
        logft = (jax.ShapeDtypeStruct((t // seq_len, N_HEADS, seq_len), F32),
                 pl.BlockSpec((None, N_HEADS, tm), lambda i: (i // seq_tiles, 0, i % seq_tiles)))
        outs = [wide(d, BF16), wide(GROUP_W, F32), wide(LANES, F32), wide(LANES, F32), logft,
                wide(GROUP_W, BF16), wide(GROUP_W, BF16), wide(GROUP_W, BF16), wide(GROUP_W, F32), rows4, rows4]
        operands.append(fb_row)
        in_specs.append(pl.BlockSpec((1, LANES), lambda i: (0, 0)))
        scratch = [pltpu.VMEM((SUBLANES, LANES), F32)]
    out_shape = [o[0] for o in outs]
    out_specs = [o[1] for o in outs]
    return pl.pallas_call(
        functools.partial(_proj_kernel, tm=tm, sample=sample, seq_tiles=seq_tiles),
        grid=(t // tm,),
        in_specs=in_specs,
        out_specs=out_specs,
        out_shape=out_shape,
        scratch_shapes=scratch,
        compiler_params=pltpu.CompilerParams(dimension_semantics=("arbitrary",),
                                             vmem_limit_bytes=VMEM_LIMIT),
        name="proj",
    )(*operands)


def _gdn_kernel(*refs, c, l_valid, nb, project, chunk_of_step=None, hooks=(None, None)):
    if project:
        (h_ref, wqkv_ref, zg_ref, sm_ref, cw_ref, alog_ref, dtb_ref, onw_ref,
         og_ref, sout_ref, tail_ref, xbuf, s_scr) = refs
    else:
        (qkv_ref, zg_ref, sm_ref, cw_ref, alog_ref, dtb_ref, onw_ref, s0_ref, c0_ref,
         og_ref, sout_ref, xbuf, s_scr) = refs
    if chunk_of_step is None:
        ci, n_c = pl.program_id(1), pl.num_programs(1)
    else:
        ci, n_c = chunk_of_step

    @pl.when(ci == 0)
    def _():
        if project:
            xbuf[:, 0:SUBLANES, :] = jnp.zeros((nb, SUBLANES, CONV_DIM), F32)
            s_scr[...] = jnp.zeros(s_scr.shape, F32)
        else:
            xbuf[:, SUBLANES - (CONV_K - 1):SUBLANES, :] = c0_ref[...]
            s_scr[...] = s0_ref[...]

    row = _iota2((c, 1), 0) + ci * c
    valid = jnp.broadcast_to((row < l_valid).astype(F32), (c, LANES))
    tri_incl = (_iota2((c, c), 0) >= _iota2((c, c), 1))
    tri_strict = (_iota2((c, c), 0) > _iota2((c, c), 1))
    eye = (_iota2((c, c), 0) == _iota2((c, c), 1)).astype(F32)
    pad_rows = LANES - c
    sl = lambda base, h: slice(base + h * HEAD_DIM, base + (h + 1) * HEAD_DIM)

    if project:
        raw = jnp.dot(h_ref[...].reshape(nb * c, h_ref.shape[-1]), wqkv_ref[...],
                      preferred_element_type=F32)
    if hooks[0] is not None:
        hooks[0]()
    q, k, v, beta, gc, gc_row, gc_last = [], [], [], [], [], [], []
    for bb in range(nb):
        xbuf[bb, SUBLANES:SUBLANES + c, :] = raw[bb * c:(bb + 1) * c] if project else qkv_ref[bb]
        yq, yk, yv = (_conv_silu_qkv(xbuf.at[bb], cw_ref, g, c) for g in range(3))
        if project:
            tail_ref[bb] = xbuf[bb, c:c + SUBLANES, :]
        xbuf[bb, 0:SUBLANES, :] = xbuf[bb, c:c + SUBLANES, :]
        sm = sm_ref[bb]
        beta_t = _sigmoid(sm) * valid
        g_t = -jnp.exp(alog_ref[...]) * _softplus(sm + dtb_ref[...]) * valid
        gc_t = _fdot(tri_incl.astype(F32), g_t)
        gc_sq = jnp.concatenate([gc_t, jnp.zeros((pad_rows, LANES), F32)], axis=0) if pad_rows else gc_t
        gc_tr = gc_sq.T
        for h in range(N_HEADS):
            q.append(_l2_normalize(yq[:, sl(0, h)], HEAD_DIM ** -0.5))
            k.append(_l2_normalize(yk[:, sl(0, h)], 1.0) * valid)
            v.append(yv[:, sl(0, h)])
            beta.append(jnp.broadcast_to(beta_t[:, SM_BETA + h:SM_BETA + h + 1], (c, HEAD_DIM)))
            gc.append(jnp.broadcast_to(gc_t[:, SM_DECAY + h:SM_DECAY + h + 1], (c, HEAD_DIM)))
            gc_row.append(gc_tr[SM_DECAY + h:SM_DECAY + h + 1, 0:c])
            gc_last.append(jnp.broadcast_to(gc_t[c - 1:c, SM_DECAY + h:SM_DECAY + h + 1], (1, HEAD_DIM)))

    chains = range(nb * N_HEADS)
    decay = [jnp.where(tri_incl, jnp.exp(jnp.where(tri_incl, gc[i][:, 0:c] - gc_row[i], 0.0)), 0.0)
             for i in chains]
    kb = [k[i] * beta[i] for i in chains]
    kkqk = [_bdot_nt(jnp.concatenate([kb[i], q[i]], axis=0), k[i]) for i in chains]
    qk = [kkqk[i][c:2 * c] * decay[i] for i in chains]
    if hooks[1] is not None:
        hooks[1]()
    neg_a = [-jnp.where(tri_strict, kkqk[i][0:c] * decay[i], 0.0) for i in chains]
    base = min(INV_BASE, c)
    blk_r, blk_c = _iota2((c, c), 0), _iota2((c, c), 1)
    same = lambda size: (blk_r // size) == (blk_c // size)
    diag = [jnp.where(same(base), neg_a[i], 0.0) for i in chains] if base < c else neg_a
    t_inv = [eye + diag[i] for i in chains]
    pw = [_bdot(diag[i], diag[i]) for i in chains]
    n_sq = int(math.log2(base))
    for j in range(1, n_sq):
        if j < n_sq - 1:
            both = [_bdot(jnp.concatenate([t_inv[i], pw[i]], axis=0), pw[i]) for i in chains]
            t_inv = [t_inv[i] + both[i][0:c] for i in chains]
            pw = [both[i][c:2 * c] for i in chains]
        else:
            t_inv = [t_inv[i] + _bdot(t_inv[i], pw[i]) for i in chains]
    size = base
    while size < c:
        off = [jnp.where(same(2 * size) & ~same(size), neg_a[i], 0.0) for i in chains]
        right = [_bdot(off[i], t_inv[i]) for i in chains]
        t_inv = [t_inv[i] + _bdot(t_inv[i], right[i]) for i in chains]
        size *= 2
    egc = [jnp.exp(gc[i]) for i in chains]
    sol = [_bdot(t_inv[i], jnp.concatenate([v[i] * beta[i], kb[i] * egc[i]], axis=-1)) for i in chains]
    s = [s_scr[i // N_HEADS, i % N_HEADS] for i in chains]
    ws = [_bdot(jnp.concatenate([sol[i][:, HEAD_DIM:2 * HEAD_DIM], q[i] * egc[i]], axis=0), s[i])
          for i in chains]
    v_new = [sol[i][:, 0:HEAD_DIM] - ws[i][0:c] for i in chains]
    o = [ws[i][c:2 * c] + _bdot(qk[i], v_new[i]) for i in chains]
    k_dec = [k[i] * jnp.exp(gc_last[i] - gc[i]) for i in chains]
    s_new = [s[i] * jnp.exp(gc_last[i]) + _bdot_tn(k_dec[i], v_new[i]) for i in chains]
    for i in chains:
        bb, h = i // N_HEADS, i % N_HEADS
        s_scr[bb, h] = s_new[i]
        oh = o[i] * lax.rsqrt(jnp.mean(o[i] * o[i], axis=-1, keepdims=True) + NORM_EPS) * onw_ref[...]
        z = zg_ref[bb, :, sl(0, h)]
        og_ref[bb, :, sl(0, h)] = (oh * (z * _sigmoid(z))).astype(og_ref.dtype)

    @pl.when(ci == n_c - 1)
    def _():
        sout_ref[...] = s_scr[...]


def _gdn_call(src, zg, sm, conv_w, alog_row, dtb_row, onw, *, c, l_valid, nb, w_qkv=None, s0=None, c0=None):
    project = w_qkv is not None
    b, l, _ = zg.shape
    n_c = l // c
    assert not project or l_valid == l
    kern = functools.partial(_gdn_kernel, c=c, l_valid=l_valid, nb=nb, project=project)
    blk = lambda w: pl.BlockSpec((nb, c, w), lambda bi, ci: (bi, ci, 0))
    full = lambda shape: pl.BlockSpec(shape, lambda bi, ci: (0,) * len(shape))
    state = pl.BlockSpec((nb, N_HEADS, HEAD_DIM, HEAD_DIM), lambda bi, ci: (bi, 0, 0, 0))
    rows8 = pl.BlockSpec((nb, SUBLANES, CONV_DIM), lambda bi, ci: (bi, 0, 0))
    common = [blk(GROUP_W), blk(LANES), full((CONV_K, CONV_DIM)), full((1, LANES)), full((1, LANES)),
              full((1, HEAD_DIM))]
    out_specs = [blk(GROUP_W), state]
    out_shape = [jax.ShapeDtypeStruct((b, l, GROUP_W), BF16),
                 jax.ShapeDtypeStruct((b, N_HEADS, HEAD_DIM, HEAD_DIM), F32)]
    if project:
        d = src.shape[-1]
        operands = (src, w_qkv, zg, sm, conv_w, alog_row, dtb_row, onw)
        in_specs = [blk(d), full((d, CONV_DIM))] + common
        out_specs.append(rows8)
        out_shape.append(jax.ShapeDtypeStruct((b, SUBLANES, CONV_DIM), F32))
    else:
        operands = (src, zg, sm, conv_w, alog_row, dtb_row, onw, s0, c0)
        in_specs = [blk(CONV_DIM)] + common + [
            state, pl.BlockSpec((nb, CONV_K - 1, CONV_DIM), lambda bi, ci: (bi, 0, 0))]
    return pl.pallas_call(
        kern,
        grid=(b // nb, n_c),
        in_specs=in_specs,
        out_specs=out_specs,
        out_shape=out_shape,
        scratch_shapes=[pltpu.VMEM((nb, c + SUBLANES, CONV_DIM), F32),
                        pltpu.VMEM((nb, N_HEADS, HEAD_DIM, HEAD_DIM), F32)],
        compiler_params=pltpu.CompilerParams(dimension_semantics=("arbitrary", "arbitrary"),
                                             vmem_limit_bytes=VMEM_LIMIT),
        name="gdn",
    )(*operands)


NEG_BIG = -1e30


def _forget_columns(f_tile, h, rows, for_keys):
    f = jnp.broadcast_to(f_tile[:, SM_FORGET + h:SM_FORGET + h + 1], (rows, LANES))
    f1, f2, f3 = (t.astype(F32) for t in _split3(-f if for_keys else f))
    lane = _iota2((rows, LANES), 1)
    base = 3 if for_keys else 0
    ones = ((lane >= 3 - base) & (lane < 6 - base)).astype(F32)
    cols = jnp.where(lane == base, f1, jnp.where(lane == base + 1, f2, jnp.where(lane == base + 2, f3, ones)))
    return cols.astype(BF16)


ROW_GROUP = 32


def _fox_prompt_kernel(q_ref, k_ref, v_ref, fcol_ref, zf_ref, o_ref,
                       kx_ref, qa_ref, s_ref, p_ref, acc_ref, m_ref, a_ref, *, tq, l):
    qi = pl.program_id(1)
    heads = range(N_HEADS)
    sl = lambda h: slice(h * HEAD_DIM, (h + 1) * HEAD_DIM)
    nt = (((1,), (1,)), ((), ()))

    @pl.when(qi == 0)
    def _():
        for r in range(l // tq):
            for h in heads:
                kx_ref[r * tq:(r + 1) * tq, sl(h)] = _forget_columns(fcol_ref[r * tq:(r + 1) * tq, :], h, tq, True)

    f_q = fcol_ref[pl.ds(pl.multiple_of(qi * tq, tq), tq), :]
    for h in heads:
        qa_ref[h, :, 0:HEAD_DIM] = q_ref[:, sl(h)]
        qa_ref[h, :, HEAD_DIM:2 * HEAD_DIM] = _forget_columns(f_q, h, tq, False)
    acc_ref[...] = jnp.zeros(acc_ref.shape, F32)
    m_ref[...] = jnp.full(m_ref.shape, NEG_BIG, F32)
    ones = jnp.ones((tq, HEAD_DIM), BF16)

    half = tq // 2

    def block(ki, masked):
        start = pl.multiple_of(ki * tq, tq)
        pieces = [(slice(0, half), half), (slice(half, tq), tq)] if masked else [(slice(0, tq), tq)]
        for h in heads:
            for qr, nk in pieces:
                keys = pl.ds(start, nk)
                ka = jnp.concatenate([k_ref[keys, sl(h)], kx_ref[keys, sl(h)]], axis=1)
                s_ref[h, qr, 0:nk] = lax.dot_general(qa_ref[h, qr, :], ka, nt, preferred_element_type=F32)
        for h in heads:
            for r in range(0, tq, ROW_GROUP):
                rg = slice(r, r + ROW_GROUP)
                nk = half if (masked and r < half) else tq
                s = s_ref[h, rg, 0:nk]
                if masked:
                    keep = _iota2((ROW_GROUP, nk), 1) <= _iota2((ROW_GROUP, nk), 0) + r
                    s = jnp.where(keep, s, NEG_BIG)
                m_old = m_ref[h, rg, :]
                m_new = jnp.maximum(m_old, jnp.max(s, axis=-1, keepdims=True))
                a_ref[h, rg, :] = jnp.exp(m_old - m_new)
                m_ref[h, rg, :] = m_new
                p_ref[h, rg, 0:nk] = jnp.exp(s - jnp.concatenate([m_new] * (nk // LANES), axis=1)).astype(BF16)
        for h in heads:
            alpha = a_ref[h]
            for qr, nk in pieces:
                keys = pl.ds(start, nk)
                pv = jnp.dot(p_ref[h, qr, 0:nk], jnp.concatenate([v_ref[keys, sl(h)], ones[0:nk]], axis=1),
                             preferred_element_type=F32)
                acc_ref[h, qr, :] = acc_ref[h, qr, :] * jnp.concatenate([alpha[qr], alpha[qr]], axis=1) + pv

    def body(ki, carry):
        block(ki, False)
        return carry

    lax.fori_loop(0, qi, body, 0)
    block(qi, True)
    for h in heads:
        z = zf_ref[:, sl(h)]
        o = acc_ref[h, :, 0:HEAD_DIM] / acc_ref[h, :, HEAD_DIM:2 * HEAD_DIM]
        o_ref[:, sl(h)] = (o * (z * _sigmoid(z))).astype(o_ref.dtype)


def _fox_prompt_call(qf, kf, vf, fcol, zf, tq):
    b, l, _ = qf.shape
    kern = functools.partial(_fox_prompt_kernel, tq=tq, l=l)
    qblk = lambda w: pl.BlockSpec((None, tq, w), lambda bi, qi: (bi, qi, 0))
    seq = lambda w: pl.BlockSpec((None, l, w), lambda bi, qi: (bi, 0, 0))
    return pl.pallas_call(
        kern,
        grid=(b, l // tq),
        in_specs=[qblk(GROUP_W), seq(GROUP_W), seq(GROUP_W), seq(LANES), qblk(GROUP_W)],
        out_specs=qblk(GROUP_W),
        out_shape=jax.ShapeDtypeStruct((b, l, GROUP_W), BF16),
        scratch_shapes=[pltpu.VMEM((l, GROUP_W), BF16),
                        pltpu.VMEM((N_HEADS, tq, 2 * HEAD_DIM), BF16),
                        pltpu.VMEM((N_HEADS, tq, tq), F32),
                        pltpu.VMEM((N_HEADS, tq, tq), BF16),
                        pltpu.VMEM((N_HEADS, tq, 2 * HEAD_DIM), F32),
                        pltpu.VMEM((N_HEADS, tq, LANES), F32),
                        pltpu.VMEM((N_HEADS, tq, LANES), F32)],
        compiler_params=pltpu.CompilerParams(dimension_semantics=("arbitrary", "arbitrary"),
                                             vmem_limit_bytes=VMEM_LIMIT),
        name="fox_prompt",
    )(qf, kf, vf, fcol, zf)


def _out_kernel(og_ref, of_ref, x_ref, w_ref, fnw_ref, y_ref):
    o = jnp.dot(og_ref[...], w_ref[0:GROUP_W, :], preferred_element_type=F32)
    o = o + jnp.dot(of_ref[...], w_ref[GROUP_W:2 * GROUP_W, :], preferred_element_type=F32)
    y = x_ref[...] + o
    var = jnp.mean(y * y, axis=-1, keepdims=True)
    y_ref[...] = y * lax.rsqrt(var + NORM_EPS) * fnw_ref[...]


def _out_call(og, of, x2d, w_out, fnw, tm):
    t, d = x2d.shape
    return pl.pallas_call(
        _out_kernel,
        grid=(t // tm,),
        in_specs=[pl.BlockSpec((tm, GROUP_W), lambda i: (i, 0)),
                  pl.BlockSpec((tm, GROUP_W), lambda i: (i, 0)),
                  pl.BlockSpec((tm, d), lambda i: (i, 0)),
                  pl.BlockSpec((2 * GROUP_W, d), lambda i: (0, 0)),
                  pl.BlockSpec((1, d), lambda i: (0, 0))],
        out_specs=pl.BlockSpec((tm, d), lambda i: (i, 0)),
        out_shape=jax.ShapeDtypeStruct((t, d), F32),
        compiler_params=pltpu.CompilerParams(dimension_semantics=("arbitrary",),
                                             vmem_limit_bytes=VMEM_LIMIT),
        name="out_proj",
    )(og, of, x2d, w_out, fnw)


def _page_copies(pt_ref, kc_ref, vc_ref, lc_ref, kbuf, vbuf, lbuf, sems, step, slot, n_pages, pg, rows):
    copies = []
    for r in range(rows):
        for p in range(n_pages):
            pid = pt_ref[step * rows + r, p]
            copies.append((pltpu.make_async_copy(kc_ref.at[pid], kbuf.at[slot, r, pl.ds(p * pg, pg)],
                                                 sems.at[0, slot]), 0))
            copies.append((pltpu.make_async_copy(vc_ref.at[pid], vbuf.at[slot, r, pl.ds(p * pg, pg)],
                                                 sems.at[1, slot]), 1))
            copies.append((pltpu.make_async_copy(lc_ref.at[pid], lbuf.at[slot, r, :, p, :], sems.at[2, slot]),
                           p % 2))
    return copies


def _split3(x):
    x1 = x.astype(BF16)
    r1 = x - x1.astype(F32)
    x2 = r1.astype(BF16)
    x3 = (r1 - x2.astype(F32)).astype(BF16)
    return x1, x2, x3


def _fox_decode_kernel(*refs, **kw):
    for _ in _fox_decode_phases(*refs, **kw):
        pass


def _fox_decode_phases(pt_ref, q_ref, kn_ref, vn_ref, zf_ref, sm_ref, fbrow_ref, cums_ref,
                       kc_ref, vc_ref, lc_ref, o_ref, logf_ref, kbuf, vbuf, lbuf, sems,
                       *, n_pages, pg, l_new, rows):
    step = pl.program_id(0)
    n_steps = pl.num_programs(0)
    slot = step % 2
    copies = functools.partial(_page_copies, pt_ref, kc_ref, vc_ref, lc_ref, kbuf, vbuf, lbuf, sems,
                               n_pages=n_pages, pg=pg, rows=rows)
    nr = l_new * N_HEADS

    @pl.when(step == 0)
    def _():
        for cp, prio in copies(step=step, slot=slot):
            cp.start(priority=prio)

    @pl.when(step + 1 < n_steps)
    def _():
        for cp, prio in copies(step=step + 1, slot=1 - slot):
            cp.start(priority=prio)

    for cp, _ in copies(step=step, slot=slot):
        cp.wait()
    yield

    scale = HEAD_DIM ** -0.5
    earlier = (_iota2((n_pages, n_pages), 0) > _iota2((n_pages, n_pages), 1)).astype(F32)
    tok_valid = (_iota2((SUBLANES, 1), 0) < l_new).astype(F32)
    r_tok = _iota2((nr, SUBLANES), 0) // N_HEADS
    f_past, f_tot_row, csum, s_all, s_new = [], [], [], [], []
    for r in range(rows):
        qrows = slice(r * nr, (r + 1) * nr)
        res = jnp.zeros((3 * n_pages, 2 * pg), F32)
        for h in range(N_HEADS):
            res = res + jnp.dot(jnp.concatenate(_split3(lbuf[slot, r, h]), axis=0), cums_ref[h],
                                preferred_element_type=F32)
        res = res[0:n_pages] + res[n_pages:2 * n_pages] + res[2 * n_pages:3 * n_pages]
        within, tot = res[:, 0:pg], res[:, pg:2 * pg]
        carry = _fdot(earlier, tot)
        f_past.append(within + carry)
        f_tot_row.append(carry[n_pages - 1:n_pages, :] + tot[n_pages - 1:n_pages, :])
        lf_col = _log_sigmoid(sm_ref[r] + fbrow_ref[...]) * tok_valid
        logf_ref[r] = lf_col
        csum.append(_fdot((_iota2((nr, SUBLANES), 1) <= r_tok).astype(F32), lf_col))
        q = q_ref[qrows, :].astype(BF16)
        s_all.append(lax.dot_general(q, kbuf[slot, r].astype(BF16), (((1,), (1,)), ((), ())),
                                     preferred_element_type=F32))
        s_new.append(lax.dot_general(q, kn_ref[qrows, :].astype(BF16), (((1,), (1,)), ((), ())),
                                     preferred_element_type=F32))
    yield
    own_lane = _iota2((nr, LANES), 1) == SM_FORGET + _iota2((nr, LANES), 0) % N_HEADS
    eye = _iota2((nr, nr), 0) == _iota2((nr, nr), 1)
    same_head = (_iota2((nr, pg), 1) % N_HEADS) == (_iota2((nr, pg), 0) % N_HEADS)
    rr, cc = _iota2((nr, nr), 0), _iota2((nr, nr), 1)
    new_ok = (rr % N_HEADS == cc % N_HEADS) & (cc // N_HEADS <= rr // N_HEADS)
    for r in range(rows):
        qrows = slice(r * nr, (r + 1) * nr)
        fq_new = jnp.sum(jnp.where(own_lane, csum[r], 0.0), axis=-1, keepdims=True)
        f_tot_col = jnp.sum(jnp.where(eye, jnp.broadcast_to(f_tot_row[r][:, 0:nr], (nr, nr)), 0.0),
                            axis=-1, keepdims=True)
        fq = fq_new + f_tot_col
        fq_row = jnp.sum(jnp.where(eye, jnp.broadcast_to(fq, (nr, nr)), 0.0), axis=0, keepdims=True)
        sp = [jnp.where(same_head, s_all[r][:, p * pg:(p + 1) * pg] * scale + (fq - f_past[r][p:p + 1, :]),
                        NEG_BIG) for p in range(n_pages)]
        sn = jnp.where(new_ok, s_new[r] * scale + (fq - fq_row), NEG_BIG)
        m_el = sp[0]
        for p in range(1, n_pages):
            m_el = jnp.maximum(m_el, sp[p])
        m = jnp.maximum(jnp.max(m_el, axis=-1, keepdims=True), jnp.max(sn, axis=-1, keepdims=True))
        pp = [jnp.exp(t - m) for t in sp]
        p_new = jnp.exp(sn - m)
        l_el = pp[0]
        for p in range(1, n_pages):
            l_el = l_el + pp[p]
        l = jnp.sum(l_el, axis=-1, keepdims=True) + jnp.sum(p_new, axis=-1, keepdims=True)
        p_all = jnp.concatenate([t.astype(BF16) for t in pp], axis=-1)
        acc = jnp.dot(p_all, vbuf[slot, r].astype(BF16), preferred_element_type=F32)
        acc = acc + jnp.dot(p_new.astype(BF16), vn_ref[qrows, :].astype(BF16), preferred_element_type=F32)
        z = zf_ref[qrows, :]
        o_ref[qrows, :] = ((acc / l) * (z * _sigmoid(z))).astype(o_ref.dtype)


def _head_cumsum_matrix(page):
    t = jnp.arange(page)[None, :, None]
    j = jnp.arange(page * N_HEADS)[None, None, :]
    h = jnp.arange(N_HEADS)[:, None, None]
    own = (j % N_HEADS) == h
    c = own & (t <= j // N_HEADS)
    b = jnp.broadcast_to(own, c.shape)
    return jnp.concatenate([c, b], axis=2).astype(BF16)


DECODE_ROWS_PER_STEP = 2
N_DECODE_INPUTS, N_DECODE_OUTPUTS, N_DECODE_SCRATCH = 10, 2, 4
N_GDN_INPUTS, N_GDN_OUTPUTS = 8, 3


def _decode_gdn_kernel(pt_ref, *refs, decode_kw, gdn_kw, n_chunks):
    take = lambda n: (refs[:n], refs[n:])
    dec_in, refs = take(N_DECODE_INPUTS)
    gdn_in, refs = take(N_GDN_INPUTS)
    dec_out, refs = take(N_DECODE_OUTPUTS)
    gdn_out, refs = take(N_GDN_OUTPUTS)
    dec_scr, gdn_scr = take(N_DECODE_SCRATCH)
    decode = _fox_decode_phases(pt_ref, *dec_in, *dec_out, *dec_scr, **decode_kw)
    next(decode)
    _gdn_kernel(*gdn_in, *gdn_out, *gdn_scr, **gdn_kw,
                chunk_of_step=(pl.program_id(0) % n_chunks, n_chunks),
                hooks=(lambda: next(decode),
                       lambda: next(decode, None)))


def _fox_decode_call(page_table, q4, kn4, vn4, zf4, sm8, fb_row, kcache, vcache, lcache, l_new, gdn=None):
    b, n_pages = page_table.shape
    pg = kcache.shape[1]
    page = lcache.shape[2]
    nr = l_new * N_HEADS
    assert nr % (2 * SUBLANES) == 0 and l_new <= SUBLANES
    rps = DECODE_ROWS_PER_STEP
    assert b % rps == 0
    n_steps = b // rps
    decode_kw = dict(n_pages=n_pages, pg=pg, l_new=l_new, rows=rps)
    rows = pl.BlockSpec((rps * nr, HEAD_DIM), lambda i, pt: (i, 0))
    tok = pl.BlockSpec((rps, SUBLANES, LANES), lambda i, pt: (i, 0, 0))
    const = lambda shape: pl.BlockSpec(shape, lambda i, pt: (0,) * len(shape))
    any_spec = pl.BlockSpec(memory_space=pl.ANY)
    operands = [page_table, q4, kn4, vn4, zf4, sm8, fb_row, _head_cumsum_matrix(page), kcache, vcache, lcache]
    in_specs = [rows, rows, rows, rows, tok, const((1, LANES)), const((N_HEADS, page, 2 * pg)),
                any_spec, any_spec, any_spec]
    out_specs = [rows, tok]
    out_shape = [jax.ShapeDtypeStruct((b * nr, HEAD_DIM), BF16), jax.ShapeDtypeStruct((b, SUBLANES, LANES), F32)]
    scratch = [pltpu.VMEM((2, rps, n_pages * pg, HEAD_DIM), F32),
               pltpu.VMEM((2, rps, n_pages * pg, HEAD_DIM), F32),
               pltpu.VMEM((2, rps, N_HEADS, n_pages, page), F32),
               pltpu.SemaphoreType.DMA((3, 2))]
    kern = functools.partial(_fox_decode_kernel, **decode_kw)
    if gdn is not None:
        h3, w_qkv, zg, sm, conv_w, alog_row, dtb_row, onw, c = gdn
        bp, l, d = h3.shape
        n_c = l // c
        nb = bp * n_c // n_steps
        assert nb >= 1 and (bp // nb) * n_c == n_steps
        blk = lambda w: pl.BlockSpec((nb, c, w), lambda i, pt: (i // n_c, i % n_c, 0))
        state = pl.BlockSpec((nb, N_HEADS, HEAD_DIM, HEAD_DIM), lambda i, pt: (i // n_c, 0, 0, 0))
        rows8 = pl.BlockSpec((nb, SUBLANES, CONV_DIM), lambda i, pt: (i // n_c, 0, 0))
        operands += [h3, w_qkv, zg, sm, conv_w, alog_row, dtb_row, onw]
        in_specs += [blk(d), const((d, CONV_DIM)), blk(GROUP_W), blk(LANES), const((CONV_K, CONV_DIM)),
                     const((1, LANES)), const((1, LANES)), const((1, HEAD_DIM))]
        out_specs += [blk(GROUP_W), state, rows8]
        out_shape += [jax.ShapeDtypeStruct((bp, l, GROUP_W), BF16),
                      jax.ShapeDtypeStruct((bp, N_HEADS, HEAD_DIM, HEAD_DIM), F32),
                      jax.ShapeDtypeStruct((bp, SUBLANES, CONV_DIM), F32)]
        scratch += [pltpu.VMEM((nb, c + SUBLANES, CONV_DIM), F32),
                    pltpu.VMEM((nb, N_HEADS, HEAD_DIM, HEAD_DIM), F32)]
        kern = functools.partial(_decode_gdn_kernel, decode_kw=decode_kw, n_chunks=n_c,
                                 gdn_kw=dict(c=c, l_valid=l, nb=nb, project=True))
    grid_spec = pltpu.PrefetchScalarGridSpec(num_scalar_prefetch=1, grid=(n_steps,), in_specs=in_specs,
                                             out_specs=out_specs, scratch_shapes=scratch)
    return pl.pallas_call(
        kern,
        grid_spec=grid_spec,
        out_shape=out_shape,
        compiler_params=pltpu.CompilerParams(dimension_semantics=("arbitrary",),
                                             vmem_limit_bytes=VMEM_LIMIT),
        name="fox_decode",
    )(*operands)


def _gate_row(vals, offset):
    return jnp.zeros((1, LANES), F32).at[0, offset:offset + N_HEADS].set(vals.astype(F32))


def _pad_rows(t, rows):
    return jnp.pad(t, ((0, 0), (0, rows - t.shape[1]), (0, 0)))


def kernel(x_prompt, x_sample, cache_fox_k, cache_fox_v, cache_fox_logf, page_table, state_gdn_ssm,
           state_gdn_conv, w_in, gdn_conv_w, gdn_a_log, gdn_dt_bias, gdn_out_norm_w, fox_f_bias, w_out,
           norm_w, final_norm_w):
    bp, lp, d = x_prompt.shape
    bs, ls, _ = x_sample.shape
    depth = w_in.shape[0]
    assert depth == 1, "single-layer trunk"
    n_pool, page = cache_fox_k.shape[1], cache_fox_k.shape[2]

    w_big, w_qkv = _pack_w_call(w_in[0].T)
    w_o = w_out[0].astype(BF16)
    nw = norm_w[0].reshape(1, d)
    fnw = final_norm_w.reshape(1, d)
    conv_w = gdn_conv_w[0]
    alog_row = _gate_row(gdn_a_log[0], SM_DECAY)
    dtb_row = _gate_row(gdn_dt_bias[0], SM_DECAY)
    fb_row = _gate_row(fox_f_bias[0], SM_FORGET)
    onw = gdn_out_norm_w[0].reshape(1, HEAD_DIM)

    xp2 = x_prompt.reshape(bp * lp, d)
    hp, zg, sm, fcol, logf_t, qf, kf, vf, zf, k4, v4 = _proj_call(xp2, nw, w_big, tm=512, sample=False,
                                                                  fb_row=fb_row, seq_len=lp)
    r3 = lambda t: t.reshape(bp, lp, t.shape[-1])
    of_p = _fox_prompt_call(r3(qf), r3(kf), r3(vf), r3(fcol), r3(zf), tq=512)

    xs2 = x_sample.reshape(bs * ls, d)
    qkv_s, zg_s, sm_s, q4_s, k4_s, v4_s, z4_s = _proj_call(xs2, nw, w_big, tm=256, sample=True)
    r3s = lambda t: t.reshape(bs, ls, t.shape[-1])
    p8 = lambda t: _pad_rows(r3s(t), SUBLANES)
    og_s, ssm_s = _gdn_call(p8(qkv_s), p8(zg_s), p8(sm_s), conv_w, alog_row, dtb_row, onw, c=SUBLANES,
                            l_valid=ls, nb=16, s0=state_gdn_ssm[0], c0=state_gdn_conv[0])
    kcache = cache_fox_k[0].reshape(n_pool, page * N_HEADS, HEAD_DIM)
    vcache = cache_fox_v[0].reshape(n_pool, page * N_HEADS, HEAD_DIM)
    lcache = cache_fox_logf[0].transpose(0, 2, 1)
    of_s, logf_s, og_p, ssm_p, tail = _fox_decode_call(
        page_table, q4_s, k4_s, v4_s, z4_s, p8(sm_s), fb_row, kcache, vcache, lcache, l_new=ls,
        gdn=(r3(hp), w_qkv, r3(zg), r3(sm), conv_w, alog_row, dtb_row, onw, GDN_CHUNK))
    og_s2 = og_s[:, :ls].reshape(bs * ls, GROUP_W)
    of_s2 = of_s.reshape(bs * ls, GROUP_W)
    y_s = _out_call(og_s2, of_s2, xs2, w_o, fnw, tm=256)
    y_p = _out_call(og_p.reshape(bp * lp, GROUP_W), of_p.reshape(bp * lp, GROUP_W), xp2, w_o, fnw, tm=1024)

    y_prompt = y_p.reshape(bp, lp, d)
    k_prompt = k4.reshape(1, bp, lp, N_HEADS, HEAD_DIM)
    v_prompt = v4.reshape(1, bp, lp, N_HEADS, HEAD_DIM)
    logf_prompt = logf_t.transpose(0, 2, 1).reshape(1, bp, lp, N_HEADS)
    ssm_prompt = ssm_p.reshape(1, bp, N_HEADS, HEAD_DIM, HEAD_DIM)
    conv_prompt = tail[:, SUBLANES - (CONV_K - 1):, :].reshape(1, bp, CONV_K - 1, CONV_DIM)
    y_sample = y_s.reshape(bs, ls, d)
    k_sample = k4_s.reshape(1, bs, ls, N_HEADS, HEAD_DIM)
    v_sample = v4_s.reshape(1, bs, ls, N_HEADS, HEAD_DIM)
    logf_sample = logf_s[:, :ls, SM_FORGET:SM_FORGET + N_HEADS].reshape(1, bs, ls, N_HEADS)
    ssm_sample = ssm_s.reshape(1, bs, N_HEADS, HEAD_DIM, HEAD_DIM)
    if ls >= CONV_K - 1:
        conv_sample = r3s(qkv_s)[:, ls - (CONV_K - 1):, :]
    else:
        conv_sample = jnp.concatenate([state_gdn_conv[0], r3s(qkv_s)], axis=1)[:, -(CONV_K - 1):, :]
    conv_sample = conv_sample.reshape(1, bs, CONV_K - 1, CONV_DIM)

    return (y_prompt, y_sample, k_prompt, v_prompt, logf_prompt, ssm_prompt, conv_prompt,
            k_sample, v_sample, logf_sample, ssm_sample, conv_sample)
```

```python
import functools
import math

import jax
import jax.numpy as jnp
from jax import lax
from jax.experimental import pallas as pl
from jax.experimental.pallas import tpu as pltpu

F32 = jnp.float32
BF16 = jnp.bfloat16

NORM_EPS = 1e-6
L2_EPS = 1e-6
HEAD_DIM = 128
N_HEADS = 4
GROUP_W = N_HEADS * HEAD_DIM
CONV_DIM = 3 * GROUP_W
CONV_K = 4
LANES = 128
SUBLANES = 8
GDN_CHUNK = 64
INV_BASE = 32
SM_BETA = 0
SM_DECAY = 4
SM_FORGET = 8
VMEM_LIMIT = 56 * 1024 * 1024


def _sigmoid(x):
    return 1.0 / (1.0 + jnp.exp(-x))


def _softplus(x):
    return jnp.maximum(x, 0.0) + jnp.log(1.0 + jnp.exp(-jnp.abs(x)))


def _log_sigmoid(x):
    return -_softplus(-x)


def _bdot(a, b):
    return jnp.dot(a.astype(BF16), b.astype(BF16), preferred_element_type=F32)


def _bdot_nt(a, b):
    return lax.dot_general(a.astype(BF16), b.astype(BF16), (((1,), (1,)), ((), ())),
                           preferred_element_type=F32)


def _bdot_tn(a, b):
    return lax.dot_general(a.astype(BF16), b.astype(BF16), (((0,), (0,)), ((), ())),
                           preferred_element_type=F32)


def _fdot(a, b):
    return jnp.dot(a, b, preferred_element_type=F32, precision=lax.Precision.HIGHEST)


def _iota2(shape, dim):
    return lax.broadcasted_iota(jnp.int32, shape, dim)


W_QKV, W_ZG, W_QF, W_KF, W_VF, W_ZF, W_SM, W_END = 0, 1536, 2048, 2560, 3072, 3584, 4096, 4224
SRC_GATES_G, SRC_FOX, SRC_GATE_F, SRC_END = 2048, 2056, 4104, 4108


def _pack_w_kernel(w_ref, o_ref, qkv_ref):
    qkv_ref[...] = w_ref[W_QKV:W_ZG, :].T.astype(BF16)
    o_ref[W_QKV:W_QF, :] = w_ref[0:SRC_GATES_G, :].astype(BF16)
    o_ref[W_QF:W_SM, :] = w_ref[SRC_FOX:SRC_GATE_F, :].astype(BF16)
    n_gate = (SRC_FOX - SRC_GATES_G) + (SRC_END - SRC_GATE_F)
    gates = jnp.concatenate([w_ref[SRC_GATES_G:SRC_FOX, :], w_ref[SRC_GATE_F:SRC_END, :],
                             jnp.zeros((W_END - W_SM - n_gate, w_ref.shape[1]), F32)], axis=0)
    o_ref[W_SM:W_END, :] = gates.astype(BF16)


def _pack_w_call(w_t):
    return pl.pallas_call(
        _pack_w_kernel,
        out_shape=[jax.ShapeDtypeStruct((W_END, w_t.shape[1]), BF16),
                   jax.ShapeDtypeStruct((w_t.shape[1], CONV_DIM), BF16)],
        compiler_params=pltpu.CompilerParams(vmem_limit_bytes=VMEM_LIMIT),
        name="pack_w",
    )(w_t)


def _store_head_rows(ref, val, tm):
    for h in range(N_HEADS):
        ref[pl.ds(h, tm, stride=N_HEADS), :] = val[:, h * HEAD_DIM:(h + 1) * HEAD_DIM].astype(ref.dtype)


def _conv_silu_qkv(xbuf, cw_ref, g, rows):
    cols = slice(g * GROUP_W, (g + 1) * GROUP_W)
    x = xbuf[0:rows + SUBLANES, cols]
    y = x[SUBLANES:] * cw_ref[CONV_K - 1:CONV_K, cols]
    for j in range(CONV_K - 1):
        shifted = pltpu.roll(x, CONV_K - 1 - j, axis=0)
        y = y + shifted[SUBLANES:] * cw_ref[j:j + 1, cols]
    return y * _sigmoid(y)


def _l2_normalize(t, scale):
    return t * (lax.rsqrt(jnp.sum(t * t, axis=-1, keepdims=True) + L2_EPS) * scale)


def _proj_kernel(x_ref, nw_ref, w_ref, *refs, tm, sample, seq_tiles):
    x = x_ref[...]
    var = jnp.mean(x * x, axis=-1, keepdims=True)
    h = (x * lax.rsqrt(var + NORM_EPS) * nw_ref[...]).astype(BF16)
    seg = lambda lo, hi: lax.dot_general(h, w_ref[lo:hi, :], (((1,), (1,)), ((), ())),
                                         preferred_element_type=F32)
    if sample:
        qkv_ref, zg_ref, sm_ref, q4_ref, k4_ref, v4_ref, z4_ref = refs
        qkv_ref[...] = seg(W_QKV, W_ZG)
        zg_ref[...] = seg(W_ZG, W_QF)
        sm_ref[...] = seg(W_SM, W_END)
        _store_head_rows(q4_ref, seg(W_QF, W_KF), tm)
        _store_head_rows(k4_ref, seg(W_KF, W_VF), tm)
        _store_head_rows(v4_ref, seg(W_VF, W_ZF), tm)
        _store_head_rows(z4_ref, seg(W_ZF, W_SM), tm)
        return
    (fb_ref, h_ref, zg_ref, sm_ref, fcol_ref, logft_ref, qb_ref, kb_ref, vb_ref, zf_ref, k4_ref, v4_ref,
     carry_ref) = refs
    h_ref[...] = h

    @pl.when(pl.program_id(0) % seq_tiles == 0)
    def _():
        carry_ref[...] = jnp.zeros(carry_ref.shape, F32)

    sm = seg(W_SM, W_END)
    sm_ref[...] = sm
    zg_ref[...] = seg(W_ZG, W_QF)
    qb_ref[...] = (seg(W_QF, W_KF) * (HEAD_DIM ** -0.5)).astype(BF16)
    kf = seg(W_KF, W_VF)
    _store_head_rows(k4_ref, kf, tm)
    kb_ref[...] = kf.astype(BF16)
    vf = seg(W_VF, W_ZF)
    _store_head_rows(v4_ref, vf, tm)
    vb_ref[...] = vf.astype(BF16)
    zf_ref[...] = seg(W_ZF, W_SM)

    tri = (_iota2((LANES, LANES), 0) >= _iota2((LANES, LANES), 1)).astype(BF16)
    blocks = [slice(i * LANES, (i + 1) * LANES) for i in range(tm // LANES)]
    lf = [_log_sigmoid(sm[blk] + fb_ref[...]) for blk in blocks]
    parts = [jnp.dot(tri, jnp.concatenate(_split3(t), axis=1), preferred_element_type=F32) for t in lf]
    within = [p[:, 0:LANES] + p[:, LANES:2 * LANES] + p[:, 2 * LANES:3 * LANES] for p in parts]
    carry = carry_ref[0:1, :]
    for i, blk in enumerate(blocks):
        fcol_ref[blk, :] = within[i] + carry
        carry = carry + within[i][LANES - 1:LANES, :]
        logft_ref[:, blk] = lf[i].T[SM_FORGET:SM_FORGET + N_HEADS, :]
    carry_ref[0:1, :] = carry


def _proj_call(x2d, norm_w, w_big, tm, sample, fb_row=None, seq_len=None):
    t, d = x2d.shape
    n = w_big.shape[0]
    wide = lambda w, dt: (jax.ShapeDtypeStruct((t, w), dt), pl.BlockSpec((tm, w), lambda i: (i, 0)))
    rows4 = (jax.ShapeDtypeStruct((t * N_HEADS, HEAD_DIM), F32),
             pl.BlockSpec((tm * N_HEADS, HEAD_DIM), lambda i: (i, 0)))
    operands = [x2d, norm_w, w_big]
    in_specs = [pl.BlockSpec((tm, d), lambda i: (i, 0)),
                pl.BlockSpec((1, d), lambda i: (0, 0)),
                pl.BlockSpec((n, d), lambda i: (0, 0))]
    scratch = []
    seq_tiles = 1
    if sample:
        outs = [wide(CONV_DIM, F32), wide(GROUP_W, F32), wide(LANES, F32), rows4, rows4, rows4, rows4]
    else:
        seq_tiles = seq_len // tm
        logft = (jax.ShapeDtypeStruct((t // seq_len, N_HEADS, seq_len), F32),
                 pl.BlockSpec((None, N_HEADS, tm), lambda i: (i // seq_tiles, 0, i % seq_tiles)))
        outs = [wide(d, BF16), wide(GROUP_W, F32), wide(LANES, F32), wide(LANES, F32), logft,
                wide(GROUP_W, BF16), wide(GROUP_W, BF16), wide(GROUP_W, BF16), wide(GROUP_W, F32), rows4, rows4]
        operands.append(fb_row)
        in_specs.append(pl.BlockSpec((1, LANES), lambda i: (0, 0)))
        scratch = [pltpu.VMEM((SUBLANES, LANES), F32)]
    out_shape = [o[0] for o in outs]
    out_specs = [o[1] for o in outs]
    return pl.pallas_call(
        functools.partial(_proj_kernel, tm=tm, sample=sample, seq_tiles=seq_tiles),
        grid=(t // tm,),
        in_specs=in_specs,
        out_specs=out_specs,
        out_shape=out_shape,
        scratch_shapes=scratch,
        compiler_params=pltpu.CompilerParams(dimension_semantics=("arbitrary",),
                                             vmem_limit_bytes=VMEM_LIMIT),
        name="proj",
    )(*operands)


def _gdn_kernel(*refs, c, l_valid, nb, project, chunk_of_step=None, hooks=(None, None)):
    if project:
        (h_ref, wqkv_ref, zg_ref, sm_ref, cw_ref, alog_ref, dtb_ref, onw_ref,
         og_ref, sout_ref, tail_ref, xbuf, s_scr) = refs
    else:
        (qkv_ref, zg_ref, sm_ref, cw_ref, alog_ref, dtb_ref, onw_ref, s0_ref, c0_ref,
         og_ref, sout_ref, xbuf, s_scr) = refs
    if chunk_of_step is None:
        ci, n_c = pl.program_id(1), pl.num_programs(1)
    else:
        ci, n_c = chunk_of_step

    @pl.when(ci == 0)
    def _():
        if project:
            xbuf[:, 0:SUBLANES, :] = jnp.zeros((nb, SUBLANES, CONV_DIM), F32)
            s_scr[...] = jnp.zeros(s_scr.shape, F32)
        else:
            xbuf[:, SUBLANES - (CONV_K - 1):SUBLANES, :] = c0_ref[...]
            s_scr[...] = s0_ref[...]

    row = _iota2((c, 1), 0) + ci * c
    valid = jnp.broadcast_to((row < l_valid).astype(F32), (c, LANES))
    tri_incl = (_iota2((c, c), 0) >= _iota2((c, c), 1))
    tri_strict = (_iota2((c, c), 0) > _iota2((c, c), 1))
    eye = (_iota2((c, c), 0) == _iota2((c, c), 1)).astype(F32)
    pad_rows = LANES - c
    sl = lambda base, h: slice(base + h * HEAD_DIM, base + (h + 1) * HEAD_DIM)

    if project:
        raw = jnp.dot(h_ref[...].reshape(nb * c, h_ref.shape[-1]), wqkv_ref[...],
                      preferred_element_type=F32)
    if hooks[0] is not None:
        hooks[0]()
    q, k, v, beta, gc, gc_row, gc_last = [], [], [], [], [], [], []
    for bb in range(nb):
        xbuf[bb, SUBLANES:SUBLANES + c, :] = raw[bb * c:(bb + 1) * c] if project else qkv_ref[bb]
        yq, yk, yv = (_conv_silu_qkv(xbuf.at[bb], cw_ref, g, c) for g in range(3))
        if project:
            tail_ref[bb] = xbuf[bb, c:c + SUBLANES, :]
        xbuf[bb, 0:SUBLANES, :] = xbuf[bb, c:c + SUBLANES, :]
        sm = sm_ref[bb]
        beta_t = _sigmoid(sm) * valid
        g_t = -jnp.exp(alog_ref[...]) * _softplus(sm + dtb_ref[...]) * valid
        gc_t = _fdot(tri_incl.astype(F32), g_t)
        gc_sq = jnp.concatenate([gc_t, jnp.zeros((pad_rows, LANES), F32)], axis=0) if pad_rows else gc_t
        gc_tr = gc_sq.T
        for h in range(N_HEADS):
            q.append(_l2_normalize(yq[:, sl(0, h)], HEAD_DIM ** -0.5))
            k.append(_l2_normalize(yk[:, sl(0, h)], 1.0) * valid)
            v.append(yv[:, sl(0, h)])
            beta.append(jnp.broadcast_to(beta_t[:, SM_BETA + h:SM_BETA + h + 1], (c, HEAD_DIM)))
            gc.append(jnp.broadcast_to(gc_t[:, SM_DECAY + h:SM_DECAY + h + 1], (c, HEAD_DIM)))
            gc_row.append(gc_tr[SM_DECAY + h:SM_DECAY + h + 1, 0:c])
            gc_last.append(jnp.broadcast_to(gc_t[c - 1:c, SM_DECAY + h:SM_DECAY + h + 1], (1, HEAD_DIM)))

    chains = range(nb * N_HEADS)
    decay = [jnp.where(tri_incl, jnp.exp(jnp.where(tri_incl, gc[i][:, 0:c] - gc_row[i], 0.0)), 0.0)
             for i in chains]
    kb = [k[i] * beta[i] for i in chains]
    kkqk = [_bdot_nt(jnp.concatenate([kb[i], q[i]], axis=0), k[i]) for i in chains]
    qk = [kkqk[i][c:2 * c] * decay[i] for i in chains]
    if hooks[1] is not None:
        hooks[1]()
    neg_a = [-jnp.where(tri_strict, kkqk[i][0:c] * decay[i], 0.0) for i in chains]
    base = min(INV_BASE, c)
    blk_r, blk_c = _iota2((c, c), 0), _iota2((c, c), 1)
    same = lambda size: (blk_r // size) == (blk_c // size)
    diag = [jnp.where(same(base), neg_a[i], 0.0) for i in chains] if base < c else neg_a
    t_inv = [eye + diag[i] for i in chains]
    pw = [_bdot(diag[i], diag[i]) for i in chains]
    n_sq = int(math.log2(base))
    for j in range(1, n_sq):
        if j < n_sq - 1:
            both = [_bdot(jnp.concatenate([t_inv[i], pw[i]], axis=0), pw[i]) for i in chains]
            t_inv = [t_inv[i] + both[i][0:c] for i in chains]
            pw = [both[i][c:2 * c] for i in chains]
        else:
            t_inv = [t_inv[i] + _bdot(t_inv[i], pw[i]) for i in chains]
    size = base
    while size < c:
        off = [jnp.where(same(2 * size) & ~same(size), neg_a[i], 0.0) for i in chains]
        right = [_bdot(off[i], t_inv[i]) for i in chains]
        t_inv = [t_inv[i] + _bdot(t_inv[i], right[i]) for i in chains]
        size *= 2
    egc = [jnp.exp(gc[i]) for i in chains]
    sol = [_bdot(t_inv[i], jnp.concatenate([v[i] * beta[i], kb[i] * egc[i]], axis=-1)) for i in chains]
    s = [s_scr[i // N_HEADS, i % N_HEADS] for i in chains]
    ws = [_bdot(jnp.concatenate([sol[i][:, HEAD_DIM:2 * HEAD_DIM], q[i] * egc[i]], axis=0), s[i])
          for i in chains]
    v_new = [sol[i][:, 0:HEAD_DIM] - ws[i][0:c] for i in chains]
    o = [ws[i][c:2 * c] + _bdot(qk[i], v_new[i]) for i in chains]
    k_dec = [k[i] * jnp.exp(gc_last[i] - gc[i]) for i in chains]
    s_new = [s[i] * jnp.exp(gc_last[i]) + _bdot_tn(k_dec[i], v_new[i]) for i in chains]
    for i in chains:
        bb, h = i // N_HEADS, i % N_HEADS
        s_scr[bb, h] = s_new[i]
        oh = o[i] * lax.rsqrt(jnp.mean(o[i] * o[i], axis=-1, keepdims=True) + NORM_EPS) * onw_ref[...]
        z = zg_ref[bb, :, sl(0, h)]
        og_ref[bb, :, sl(0, h)] = (oh * (z * _sigmoid(z))).astype(og_ref.dtype)

    @pl.when(ci == n_c - 1)
    def _():
        sout_ref[...] = s_scr[...]


def _gdn_call(src, zg, sm, conv_w, alog_row, dtb_row, onw, *, c, l_valid, nb, w_qkv=None, s0=None, c0=None):
    project = w_qkv is not None
    b, l, _ = zg.shape
    n_c = l // c
    assert not project or l_valid == l
    kern = functools.partial(_gdn_kernel, c=c, l_valid=l_valid, nb=nb, project=project)
    blk = lambda w: pl.BlockSpec((nb, c, w), lambda bi, ci: (bi, ci, 0))
    full = lambda shape: pl.BlockSpec(shape, lambda bi, ci: (0,) * len(shape))
    state = pl.BlockSpec((nb, N_HEADS, HEAD_DIM, HEAD_DIM), lambda bi, ci: (bi, 0, 0, 0))
    rows8 = pl.BlockSpec((nb, SUBLANES, CONV_DIM), lambda bi, ci: (bi, 0, 0))
    common = [blk(GROUP_W), blk(LANES), full((CONV_K, CONV_DIM)), full((1, LANES)), full((1, LANES)),
              full((1, HEAD_DIM))]
    out_specs = [blk(GROUP_W), state]
    out_shape = [jax.ShapeDtypeStruct((b, l, GROUP_W), BF16),
                 jax.ShapeDtypeStruct((b, N_HEADS, HEAD_DIM, HEAD_DIM), F32)]
    if project:
        d = src.shape[-1]
        operands = (src, w_qkv, zg, sm, conv_w, alog_row, dtb_row, onw)
        in_specs = [blk(d), full((d, CONV_DIM))] + common
        out_specs.append(rows8)
        out_shape.append(jax.ShapeDtypeStruct((b, SUBLANES, CONV_DIM), F32))
    else:
        operands = (src, zg, sm, conv_w, alog_row, dtb_row, onw, s0, c0)
        in_specs = [blk(CONV_DIM)] + common + [
            state, pl.BlockSpec((nb, CONV_K - 1, CONV_DIM), lambda bi, ci: (bi, 0, 0))]
    return pl.pallas_call(
        kern,
        grid=(b // nb, n_c),
        in_specs=in_specs,
        out_specs=out_specs,
        out_shape=out_shape,
        scratch_shapes=[pltpu.VMEM((nb, c + SUBLANES, CONV_DIM), F32),
                        pltpu.VMEM((nb, N_HEADS, HEAD_DIM, HEAD_DIM), F32)],
        compiler_params=pltpu.CompilerParams(dimension_semantics=("arbitrary", "arbitrary"),
                                             vmem_limit_bytes=VMEM_LIMIT),
        name="gdn",
    )(*operands)


NEG_BIG = -1e30


def _forget_columns(f_tile, h, rows, for_keys):
    f = jnp.broadcast_to(f_tile[:, SM_FORGET + h:SM_FORGET + h + 1], (rows, LANES))
    f1, f2, f3 = (t.astype(F32) for t in _split3(-f if for_keys else f))
    lane = _iota2((rows, LANES), 1)
    base = 3 if for_keys else 0
    ones = ((lane >= 3 - base) & (lane < 6 - base)).astype(F32)
    cols = jnp.where(lane == base, f1, jnp.where(lane == base + 1, f2, jnp.where(lane == base + 2, f3, ones)))
    return cols.astype(BF16)


ROW_GROUP = 32


def _fox_prompt_kernel(q_ref, k_ref, v_ref, fcol_ref, zf_ref, og_ref, x_ref, wo_ref, fnw_ref, y_ref,
                       kx_ref, qa_ref, s_ref, p_ref, acc_ref, m_ref, a_ref, *, tq, l):
    qi = pl.program_id(1)
    heads = range(N_HEADS)
    sl = lambda h: slice(h * HEAD_DIM, (h + 1) * HEAD_DIM)
    nt = (((1,), (1,)), ((), ()))

    @pl.when(qi == 0)
    def _():
        for r in range(l // tq):
            for h in heads:
                kx_ref[r * tq:(r + 1) * tq, sl(h)] = _forget_columns(fcol_ref[r * tq:(r + 1) * tq, :], h, tq, True)

    f_q = fcol_ref[pl.ds(pl.multiple_of(qi * tq, tq), tq), :]
    for h in heads:
        qa_ref[h, :, 0:HEAD_DIM] = q_ref[:, sl(h)]
        qa_ref[h, :, HEAD_DIM:2 * HEAD_DIM] = _forget_columns(f_q, h, tq, False)
    acc_ref[...] = jnp.zeros(acc_ref.shape, F32)
    m_ref[...] = jnp.full(m_ref.shape, NEG_BIG, F32)
    ones = jnp.ones((tq, HEAD_DIM), BF16)

    half = tq // 2

    def block(ki, masked):
        start = pl.multiple_of(ki * tq, tq)
        pieces = [(slice(0, half), half), (slice(half, tq), tq)] if masked else [(slice(0, tq), tq)]
        for h in heads:
            for qr, nk in pieces:
                keys = pl.ds(start, nk)
                ka = jnp.concatenate([k_ref[keys, sl(h)], kx_ref[keys, sl(h)]], axis=1)
                s_ref[h, qr, 0:nk] = lax.dot_general(qa_ref[h, qr, :], ka, nt, preferred_element_type=F32)
        for h in heads:
            for r in range(0, tq, ROW_GROUP):
                rg = slice(r, r + ROW_GROUP)
                nk = half if (masked and r < half) else tq
                s = s_ref[h, rg, 0:nk]
                if masked:
                    keep = _iota2((ROW_GROUP, nk), 1) <= _iota2((ROW_GROUP, nk), 0) + r
                    s = jnp.where(keep, s, NEG_BIG)
                m_old = m_ref[h, rg, :]
                m_new = jnp.maximum(m_old, jnp.max(s, axis=-1, keepdims=True))
                a_ref[h, rg, :] = jnp.exp(m_old - m_new)
                m_ref[h, rg, :] = m_new
                p_ref[h, rg, 0:nk] = jnp.exp(s - jnp.concatenate([m_new] * (nk // LANES), axis=1)).astype(BF16)
        for h in heads:
            alpha = a_ref[h]
            for qr, nk in pieces:
                keys = pl.ds(start, nk)
                pv = jnp.dot(p_ref[h, qr, 0:nk], jnp.concatenate([v_ref[keys, sl(h)], ones[0:nk]], axis=1),
                             preferred_element_type=F32)
                acc_ref[h, qr, :] = acc_ref[h, qr, :] * jnp.concatenate([alpha[qr], alpha[qr]], axis=1) + pv

    def body(ki, carry):
        block(ki, False)
        return carry

    lax.fori_loop(0, qi, body, 0)
    block(qi, True)
    gated = []
    for h in heads:
        z = zf_ref[:, sl(h)]
        o = acc_ref[h, :, 0:HEAD_DIM] / acc_ref[h, :, HEAD_DIM:2 * HEAD_DIM]
        gated.append((o * (z * _sigmoid(z))).astype(BF16))
    mixed = jnp.dot(og_ref[...], wo_ref[0:GROUP_W, :], preferred_element_type=F32)
    mixed = mixed + jnp.dot(jnp.concatenate(gated, axis=1), wo_ref[GROUP_W:2 * GROUP_W, :],
                            preferred_element_type=F32)
    y = x_ref[...] + mixed
    var = jnp.mean(y * y, axis=-1, keepdims=True)
    y_ref[...] = y * lax.rsqrt(var + NORM_EPS) * fnw_ref[...]


def _fox_prompt_call(qf, kf, vf, fcol, zf, og, x, w_out, fnw, tq):
    b, l, _ = qf.shape
    d = x.shape[-1]
    kern = functools.partial(_fox_prompt_kernel, tq=tq, l=l)
    qblk = lambda w: pl.BlockSpec((None, tq, w), lambda bi, qi: (bi, qi, 0))
    seq = lambda w: pl.BlockSpec((None, l, w), lambda bi, qi: (bi, 0, 0))
    const = lambda shape: pl.BlockSpec(shape, lambda bi, qi: (0,) * len(shape))
    return pl.pallas_call(
        kern,
        grid=(b, l // tq),
        in_specs=[qblk(GROUP_W), seq(GROUP_W), seq(GROUP_W), seq(LANES), qblk(GROUP_W),
                  qblk(GROUP_W), qblk(d), const((2 * GROUP_W, d)), const((1, d))],
        out_specs=qblk(d),
        out_shape=jax.ShapeDtypeStruct((b, l, d), F32),
        scratch_shapes=[pltpu.VMEM((l, GROUP_W), BF16),
                        pltpu.VMEM((N_HEADS, tq, 2 * HEAD_DIM), BF16),
                        pltpu.VMEM((N_HEADS, tq, tq), F32),
                        pltpu.VMEM((N_HEADS, tq, tq), BF16),
                        pltpu.VMEM((N_HEADS, tq, 2 * HEAD_DIM), F32),
                        pltpu.VMEM((N_HEADS, tq, LANES), F32),
                        pltpu.VMEM((N_HEADS, tq, LANES), F32)],
        compiler_params=pltpu.CompilerParams(dimension_semantics=("arbitrary", "arbitrary"),
                                             vmem_limit_bytes=VMEM_LIMIT),
        name="fox_prompt",
    )(qf, kf, vf, fcol, zf, og, x, w_out, fnw)


def _out_kernel(og_ref, of_ref, x_ref, w_ref, fnw_ref, y_ref):
    o = jnp.dot(og_ref[...], w_ref[0:GROUP_W, :], preferred_element_type=F32)
    o = o + jnp.dot(of_ref[...], w_ref[GROUP_W:2 * GROUP_W, :], preferred_element_type=F32)
    y = x_ref[...] + o
    var = jnp.mean(y * y, axis=-1, keepdims=True)
    y_ref[...] = y * lax.rsqrt(var + NORM_EPS) * fnw_ref[...]


def _out_call(og, of, x2d, w_out, fnw, tm):
    t, d = x2d.shape
    return pl.pallas_call(
        _out_kernel,
        grid=(t // tm,),
        in_specs=[pl.BlockSpec((tm, GROUP_W), lambda i: (i, 0)),
                  pl.BlockSpec((tm, GROUP_W), lambda i: (i, 0)),
                  pl.BlockSpec((tm, d), lambda i: (i, 0)),
                  pl.BlockSpec((2 * GROUP_W, d), lambda i: (0, 0)),
                  pl.BlockSpec((1, d), lambda i: (0, 0))],
        out_specs=pl.BlockSpec((tm, d), lambda i: (i, 0)),
        out_shape=jax.ShapeDtypeStruct((t, d), F32),
        compiler_params=pltpu.CompilerParams(dimension_semantics=("arbitrary",),
                                             vmem_limit_bytes=VMEM_LIMIT),
        name="out_proj",
    )(og, of, x2d, w_out, fnw)


def _page_copies(pt_ref, kc_ref, vc_ref, lc_ref, kbuf, vbuf, lbuf, sems, step, slot, n_pages, pg, rows):
    copies = []
    for r in range(rows):
        for p in range(n_pages):
            pid = pt_ref[step * rows + r, p]
            copies.append((pltpu.make_async_copy(kc_ref.at[pid], kbuf.at[slot, r, pl.ds(p * pg, pg)],
                                                 sems.at[0, slot]), 0))
            copies.append((pltpu.make_async_copy(vc_ref.at[pid], vbuf.at[slot, r, pl.ds(p * pg, pg)],
                                                 sems.at[1, slot]), 1))
            copies.append((pltpu.make_async_copy(lc_ref.at[pid], lbuf.at[slot, r, :, p, :], sems.at[2, slot]),
                           p % 2))
    return copies


def _split3(x):
    x1 = x.astype(BF16)
    r1 = x - x1.astype(F32)
    x2 = r1.astype(BF16)
    x3 = (r1 - x2.astype(F32)).astype(BF16)
    return x1, x2, x3


def _fox_decode_kernel(*refs, **kw):
    for _ in _fox_decode_phases(*refs, **kw):
        pass


def _fox_decode_phases(pt_ref, q_ref, kn_ref, vn_ref, zf_ref, sm_ref, fbrow_ref, cums_ref,
                       kc_ref, vc_ref, lc_ref, o_ref, logf_ref, kbuf, vbuf, lbuf, sems,
                       *, n_pages, pg, l_new, rows):
    step = pl.program_id(0)
    n_steps = pl.num_programs(0)
    slot = step % 2
    copies = functools.partial(_page_copies, pt_ref, kc_ref, vc_ref, lc_ref, kbuf, vbuf, lbuf, sems,
                               n_pages=n_pages, pg=pg, rows=rows)
    nr = l_new * N_HEADS

    @pl.when(step == 0)
    def _():
        for cp, prio in copies(step=step, slot=slot):
            cp.start(priority=prio)

    @pl.when(step + 1 < n_steps)
    def _():
        for cp, prio in copies(step=step + 1, slot=1 - slot):
            cp.start(priority=prio)

    for cp, _ in copies(step=step, slot=slot):
        cp.wait()
    yield

    scale = HEAD_DIM ** -0.5
    earlier = (_iota2((n_pages, n_pages), 0) > _iota2((n_pages, n_pages), 1)).astype(F32)
    tok_valid = (_iota2((SUBLANES, 1), 0) < l_new).astype(F32)
    r_tok = _iota2((nr, SUBLANES), 0) // N_HEADS
    f_past, f_tot_row, csum, s_all, s_new = [], [], [], [], []
    for r in range(rows):
        qrows = slice(r * nr, (r + 1) * nr)
        res = jnp.zeros((3 * n_pages, 2 * pg), F32)
        for h in range(N_HEADS):
            res = res + jnp.dot(jnp.concatenate(_split3(lbuf[slot, r, h]), axis=0), cums_ref[h],
                                preferred_element_type=F32)
        res = res[0:n_pages] + res[n_pages:2 * n_pages] + res[2 * n_pages:3 * n_pages]
        within, tot = res[:, 0:pg], res[:, pg:2 * pg]
        carry = _fdot(earlier, tot)
        f_past.append(within + carry)
        f_tot_row.append(carry[n_pages - 1:n_pages, :] + tot[n_pages - 1:n_pages, :])
        lf_col = _log_sigmoid(sm_ref[r] + fbrow_ref[...]) * tok_valid
        logf_ref[r] = lf_col
        csum.append(_fdot((_iota2((nr, SUBLANES), 1) <= r_tok).astype(F32), lf_col))
        q = q_ref[qrows, :].astype(BF16)
        s_all.append(lax.dot_general(q, kbuf[slot, r].astype(BF16), (((1,), (1,)), ((), ())),
                                     preferred_element_type=F32))
        s_new.append(lax.dot_general(q, kn_ref[qrows, :].astype(BF16), (((1,), (1,)), ((), ())),
                                     preferred_element_type=F32))
    yield
    own_lane = _iota2((nr, LANES), 1) == SM_FORGET + _iota2((nr, LANES), 0) % N_HEADS
    eye = _iota2((nr, nr), 0) == _iota2((nr, nr), 1)
    same_head = (_iota2((nr, pg), 1) % N_HEADS) == (_iota2((nr, pg), 0) % N_HEADS)
    rr, cc = _iota2((nr, nr), 0), _iota2((nr, nr), 1)
    new_ok = (rr % N_HEADS == cc % N_HEADS) & (cc // N_HEADS <= rr // N_HEADS)
    for r in range(rows):
        qrows = slice(r * nr, (r + 1) * nr)
        fq_new = jnp.sum(jnp.where(own_lane, csum[r], 0.0), axis=-1, keepdims=True)
        f_tot_col = jnp.sum(jnp.where(eye, jnp.broadcast_to(f_tot_row[r][:, 0:nr], (nr, nr)), 0.0),
                            axis=-1, keepdims=True)
        fq = fq_new + f_tot_col
        fq_row = jnp.sum(jnp.where(eye, jnp.broadcast_to(fq, (nr, nr)), 0.0), axis=0, keepdims=True)
        sp = [jnp.where(same_head, s_all[r][:, p * pg:(p + 1) * pg] * scale + (fq - f_past[r][p:p + 1, :]),
                        NEG_BIG) for p in range(n_pages)]
        sn = jnp.where(new_ok, s_new[r] * scale + (fq - fq_row), NEG_BIG)
        m_el = sp[0]
        for p in range(1, n_pages):
            m_el = jnp.maximum(m_el, sp[p])
        m = jnp.maximum(jnp.max(m_el, axis=-1, keepdims=True), jnp.max(sn, axis=-1, keepdims=True))
        pp = [jnp.exp(t - m) for t in sp]
        p_new = jnp.exp(sn - m)
        l_el = pp[0]
        for p in range(1, n_pages):
            l_el = l_el + pp[p]
        l = jnp.sum(l_el, axis=-1, keepdims=True) + jnp.sum(p_new, axis=-1, keepdims=True)
        p_all = jnp.concatenate([t.astype(BF16) for t in pp], axis=-1)
        acc = jnp.dot(p_all, vbuf[slot, r].astype(BF16), preferred_element_type=F32)
        acc = acc + jnp.dot(p_new.astype(BF16), vn_ref[qrows, :].astype(BF16), preferred_element_type=F32)
        z = zf_ref[qrows, :]
        o_ref[qrows, :] = ((acc / l) * (z * _sigmoid(z))).astype(o_ref.dtype)


def _head_cumsum_matrix(page):
    t = jnp.arange(page)[None, :, None]
    j = jnp.arange(page * N_HEADS)[None, None, :]
    h = jnp.arange(N_HEADS)[:, None, None]
    own = (j % N_HEADS) == h
    c = own & (t <= j // N_HEADS)
    b = jnp.broadcast_to(own, c.shape)
    return jnp.concatenate([c, b], axis=2).astype(BF16)


DECODE_ROWS_PER_STEP = 2
N_DECODE_INPUTS, N_DECODE_OUTPUTS, N_DECODE_SCRATCH = 10, 2, 4
N_GDN_INPUTS, N_GDN_OUTPUTS = 8, 3


def _decode_gdn_kernel(pt_ref, *refs, decode_kw, gdn_kw, n_chunks):
    take = lambda n: (refs[:n], refs[n:])
    dec_in, refs = take(N_DECODE_INPUTS)
    gdn_in, refs = take(N_GDN_INPUTS)
    dec_out, refs = take(N_DECODE_OUTPUTS)
    gdn_out, refs = take(N_GDN_OUTPUTS)
    dec_scr, gdn_scr = take(N_DECODE_SCRATCH)
    decode = _fox_decode_phases(pt_ref, *dec_in, *dec_out, *dec_scr, **decode_kw)
    next(decode)
    _gdn_kernel(*gdn_in, *gdn_out, *gdn_scr, **gdn_kw,
                chunk_of_step=(pl.program_id(0) % n_chunks, n_chunks),
                hooks=(lambda: next(decode),
                       lambda: next(decode, None)))


def _fox_decode_call(page_table, q4, kn4, vn4, zf4, sm8, fb_row, kcache, vcache, lcache, l_new, gdn=None):
    b, n_pages = page_table.shape
    pg = kcache.shape[1]
    page = lcache.shape[2]
    nr = l_new * N_HEADS
    assert nr % (2 * SUBLANES) == 0 and l_new <= SUBLANES
    rps = DECODE_ROWS_PER_STEP
    assert b % rps == 0
    n_steps = b // rps
    decode_kw = dict(n_pages=n_pages, pg=pg, l_new=l_new, rows=rps)
    rows = pl.BlockSpec((rps * nr, HEAD_DIM), lambda i, pt: (i, 0))
    tok = pl.BlockSpec((rps, SUBLANES, LANES), lambda i, pt: (i, 0, 0))
    const = lambda shape: pl.BlockSpec(shape, lambda i, pt: (0,) * len(shape))
    any_spec = pl.BlockSpec(memory_space=pl.ANY)
    operands = [page_table, q4, kn4, vn4, zf4, sm8, fb_row, _head_cumsum_matrix(page), kcache, vcache, lcache]
    in_specs = [rows, rows, rows, rows, tok, const((1, LANES)), const((N_HEADS, page, 2 * pg)),
                any_spec, any_spec, any_spec]
    out_specs = [rows, tok]
    out_shape = [jax.ShapeDtypeStruct((b * nr, HEAD_DIM), BF16), jax.ShapeDtypeStruct((b, SUBLANES, LANES), F32)]
    scratch = [pltpu.VMEM((2, rps, n_pages * pg, HEAD_DIM), F32),
               pltpu.VMEM((2, rps, n_pages * pg, HEAD_DIM), F32),
               pltpu.VMEM((2, rps, N_HEADS, n_pages, page), F32),
               pltpu.SemaphoreType.DMA((3, 2))]
    kern = functools.partial(_fox_decode_kernel, **decode_kw)
    if gdn is not None:
        h3, w_qkv, zg, sm, conv_w, alog_row, dtb_row, onw, c = gdn
        bp, l, d = h3.shape
        n_c = l // c
        nb = bp * n_c // n_steps
        assert nb >= 1 and (bp // nb) * n_c == n_steps
        blk = lambda w: pl.BlockSpec((nb, c, w), lambda i, pt: (i // n_c, i % n_c, 0))
        state = pl.BlockSpec((nb, N_HEADS, HEAD_DIM, HEAD_DIM), lambda i, pt: (i // n_c, 0, 0, 0))
        rows8 = pl.BlockSpec((nb, SUBLANES, CONV_DIM), lambda i, pt: (i // n_c, 0, 0))
        operands += [h3, w_qkv, zg, sm, conv_w, alog_row, dtb_row, onw]
        in_specs += [blk(d), const((d, CONV_DIM)), blk(GROUP_W), blk(LANES), const((CONV_K, CONV_DIM)),
                     const((1, LANES)), const((1, LANES)), const((1, HEAD_DIM))]
        out_specs += [blk(GROUP_W), state, rows8]
        out_shape += [jax.ShapeDtypeStruct((bp, l, GROUP_W), BF16),
                      jax.ShapeDtypeStruct((bp, N_HEADS, HEAD_DIM, HEAD_DIM), F32),
                      jax.ShapeDtypeStruct((bp, SUBLANES, CONV_DIM), F32)]
        scratch += [pltpu.VMEM((nb, c + SUBLANES, CONV_DIM), F32),
                    pltpu.VMEM((nb, N_HEADS, HEAD_DIM, HEAD_DIM), F32)]
        kern = functools.partial(_decode_gdn_kernel, decode_kw=decode_kw, n_chunks=n_c,
                                 gdn_kw=dict(c=c, l_valid=l, nb=nb, project=True))
    grid_spec = pltpu.PrefetchScalarGridSpec(num_scalar_prefetch=1, grid=(n_steps,), in_specs=in_specs,
                                             out_specs=out_specs, scratch_shapes=scratch)
    return pl.pallas_call(
        kern,
        grid_spec=grid_spec,
        out_shape=out_shape,
        compiler_params=pltpu.CompilerParams(dimension_semantics=("arbitrary",),
                                             vmem_limit_bytes=VMEM_LIMIT),
        name="fox_decode",
    )(*operands)


def _gate_row(vals, offset):
    return jnp.zeros((1, LANES), F32).at[0, offset:offset + N_HEADS].set(vals.astype(F32))


def _pad_rows(t, rows):
    return jnp.pad(t, ((0, 0), (0, rows - t.shape[1]), (0, 0)))


def kernel(x_prompt, x_sample, cache_fox_k, cache_fox_v, cache_fox_logf, page_table, state_gdn_ssm,
           state_gdn_conv, w_in, gdn_conv_w, gdn_a_log, gdn_dt_bias, gdn_out_norm_w, fox_f_bias, w_out,
           norm_w, final_norm_w):
    bp, lp, d = x_prompt.shape
    bs, ls, _ = x_sample.shape
    depth = w_in.shape[0]
    assert depth == 1, "single-layer trunk"
    n_pool, page = cache_fox_k.shape[1], cache_fox_k.shape[2]

    w_big, w_qkv = _pack_w_call(w_in[0].T)
    w_o = w_out[0].astype(BF16)
    nw = norm_w[0].reshape(1, d)
    fnw = final_norm_w.reshape(1, d)
    conv_w = gdn_conv_w[0]
    alog_row = _gate_row(gdn_a_log[0], SM_DECAY)
    dtb_row = _gate_row(gdn_dt_bias[0], SM_DECAY)
    fb_row = _gate_row(fox_f_bias[0], SM_FORGET)
    onw = gdn_out_norm_w[0].reshape(1, HEAD_DIM)

    xp2 = x_prompt.reshape(bp * lp, d)
    hp, zg, sm, fcol, logf_t, qf, kf, vf, zf, k4, v4 = _proj_call(xp2, nw, w_big, tm=512, sample=False,
                                                                  fb_row=fb_row, seq_len=lp)
    r3 = lambda t: t.reshape(bp, lp, t.shape[-1])

    xs2 = x_sample.reshape(bs * ls, d)
    qkv_s, zg_s, sm_s, q4_s, k4_s, v4_s, z4_s = _proj_call(xs2, nw, w_big, tm=256, sample=True)
    r3s = lambda t: t.reshape(bs, ls, t.shape[-1])
    p8 = lambda t: _pad_rows(r3s(t), SUBLANES)
    og_s, ssm_s = _gdn_call(p8(qkv_s), p8(zg_s), p8(sm_s), conv_w, alog_row, dtb_row, onw, c=SUBLANES,
                            l_valid=ls, nb=16, s0=state_gdn_ssm[0], c0=state_gdn_conv[0])
    kcache = cache_fox_k[0].reshape(n_pool, page * N_HEADS, HEAD_DIM)
    vcache = cache_fox_v[0].reshape(n_pool, page * N_HEADS, HEAD_DIM)
    lcache = cache_fox_logf[0].transpose(0, 2, 1)
    of_s, logf_s, og_p, ssm_p, tail = _fox_decode_call(
        page_table, q4_s, k4_s, v4_s, z4_s, p8(sm_s), fb_row, kcache, vcache, lcache, l_new=ls,
        gdn=(r3(hp), w_qkv, r3(zg), r3(sm), conv_w, alog_row, dtb_row, onw, GDN_CHUNK))
    og_s2 = og_s[:, :ls].reshape(bs * ls, GROUP_W)
    of_s2 = of_s.reshape(bs * ls, GROUP_W)
    y_s = _out_call(og_s2, of_s2, xs2, w_o, fnw, tm=256)
    y_prompt = _fox_prompt_call(r3(qf), r3(kf), r3(vf), r3(fcol), r3(zf), og_p, x_prompt, w_o, fnw, tq=512)

    k_prompt = k4.reshape(1, bp, lp, N_HEADS, HEAD_DIM)
    v_prompt = v4.reshape(1, bp, lp, N_HEADS, HEAD_DIM)
    logf_prompt = logf_t.transpose(0, 2, 1).reshape(1, bp, lp, N_HEADS)
    ssm_prompt = ssm_p.reshape(1, bp, N_HEADS, HEAD_DIM, HEAD_DIM)
    conv_prompt = tail[:, SUBLANES - (CONV_K - 1):, :].reshape(1, bp, CONV_K - 1, CONV_DIM)
    y_sample = y_s.reshape(bs, ls, d)
    k_sample = k4_s.reshape(1, bs, ls, N_HEADS, HEAD_DIM)
    v_sample = v4_s.reshape(1, bs, ls, N_HEADS, HEAD_DIM)
    logf_sample = logf_s[:, :ls, SM_FORGET:SM_FORGET + N_HEADS].reshape(1, bs, ls, N_HEADS)
    ssm_sample = ssm_s.reshape(1, bs, N_HEADS, HEAD_DIM, HEAD_DIM)
    if ls >= CONV_K - 1:
        conv_sample = r3s(qkv_s)[:, ls - (CONV_K - 1):, :]
    else:
        conv_sample = jnp.concatenate([state_gdn_conv[0], r3s(qkv_s)], axis=1)[:, -(CONV_K - 1):, :]
    conv_sample = conv_sample.reshape(1, bs, CONV_K - 1, CONV_DIM)

    return (y_prompt, y_sample, k_prompt, v_prompt, logf_prompt, ssm_prompt, conv_prompt,
            k_sample, v_sample, logf_sample, ssm_sample, conv_sample)
```

```python
import functools
import math

import jax
import jax.numpy as jnp
from jax import lax
from jax.experimental import pallas as pl
from jax.experimental.pallas import tpu as pltpu

F32 = jnp.float32
BF16 = jnp.bfloat16

NORM_EPS = 1e-6
L2_EPS = 1e-6
HEAD_DIM = 128
N_HEADS = 4
GROUP_W = N_HEADS * HEAD_DIM
CONV_DIM = 3 * GROUP_W
CONV_K = 4
LANES = 128
SUBLANES = 8
GDN_CHUNK = 64
INV_BASE = 32
SM_BETA = 0
SM_DECAY = 4
SM_FORGET = 8
VMEM_LIMIT = 56 * 1024 * 1024


def _sigmoid(x):
    return 1.0 / (1.0 + jnp.exp(-x))


def _softplus(x):
    return jnp.maximum(x, 0.0) + jnp.log(1.0 + jnp.exp(-jnp.abs(x)))


def _log_sigmoid(x):
    return -_softplus(-x)


def _bdot(a, b):
    return jnp.dot(a.astype(BF16), b.astype(BF16), preferred_element_type=F32)


def _bdot_nt(a, b):
    return lax.dot_general(a.astype(BF16), b.astype(BF16), (((1,), (1,)), ((), ())),
                           preferred_element_type=F32)


def _bdot_tn(a, b):
    return lax.dot_general(a.astype(BF16), b.astype(BF16), (((0,), (0,)), ((), ())),
                           preferred_element_type=F32)


def _fdot(a, b):
    return jnp.dot(a, b, preferred_element_type=F32, precision=lax.Precision.HIGHEST)


def _iota2(shape, dim):
    return lax.broadcasted_iota(jnp.int32, shape, dim)


W_QKV, W_ZG, W_QF, W_KF, W_VF, W_ZF, W_SM, W_END = 0, 1536, 2048, 2560, 3072, 3584, 4096, 4224
SRC_GATES_G, SRC_FOX, SRC_GATE_F, SRC_END = 2048, 2056, 4104, 4108


def _pack_w_kernel(w_ref, o_ref, qkv_ref):
    qkv_ref[...] = w_ref[W_QKV:W_ZG, :].T.astype(BF16)
    o_ref[W_QKV:W_QF, :] = w_ref[0:SRC_GATES_G, :].astype(BF16)
    o_ref[W_QF:W_SM, :] = w_ref[SRC_FOX:SRC_GATE_F, :].astype(BF16)
    n_gate = (SRC_FOX - SRC_GATES_G) + (SRC_END - SRC_GATE_F)
    gates = jnp.concatenate([w_ref[SRC_GATES_G:SRC_FOX, :], w_ref[SRC_GATE_F:SRC_END, :],
                             jnp.zeros((W_END - W_SM - n_gate, w_ref.shape[1]), F32)], axis=0)
    o_ref[W_SM:W_END, :] = gates.astype(BF16)


def _pack_w_call(w_t):
    return pl.pallas_call(
        _pack_w_kernel,
        out_shape=[jax.ShapeDtypeStruct((W_END, w_t.shape[1]), BF16),
                   jax.ShapeDtypeStruct((w_t.shape[1], CONV_DIM), BF16)],
        compiler_params=pltpu.CompilerParams(vmem_limit_bytes=VMEM_LIMIT),
        name="pack_w",
    )(w_t)


def _store_head_rows(ref, val, tm):
    for h in range(N_HEADS):
        ref[pl.ds(h, tm, stride=N_HEADS), :] = val[:, h * HEAD_DIM:(h + 1) * HEAD_DIM].astype(ref.dtype)


def _conv_silu_qkv(xbuf, cw_ref, g, rows):
    cols = slice(g * GROUP_W, (g + 1) * GROUP_W)
    x = xbuf[0:rows + SUBLANES, cols]
    y = x[SUBLANES:] * cw_ref[CONV_K - 1:CONV_K, cols]
    for j in range(CONV_K - 1):
        shifted = pltpu.roll(x, CONV_K - 1 - j, axis=0)
        y = y + shifted[SUBLANES:] * cw_ref[j:j + 1, cols]
    return y * _sigmoid(y)


def _l2_normalize(t, scale):
    return t * (lax.rsqrt(jnp.sum(t * t, axis=-1, keepdims=True) + L2_EPS) * scale)


def _proj_kernel(x_ref, nw_ref, w_ref, *refs, tm, sample, seq_tiles):
    x = x_ref[...]
    var = jnp.mean(x * x, axis=-1, keepdims=True)
    h = (x * lax.rsqrt(var + NORM_EPS) * nw_ref[...]).astype(BF16)
    seg = lambda lo, hi: lax.dot_general(h, w_ref[lo:hi, :], (((1,), (1,)), ((), ())),
                                         preferred_element_type=F32)
    if sample:
        qkv_ref, zg_ref, sm_ref, q4_ref, k4_ref, v4_ref, z4_ref = refs
        qkv_ref[...] = seg(W_QKV, W_ZG)
        zg_ref[...] = seg(W_ZG, W_QF)
        sm_ref[...] = seg(W_SM, W_END)
        _store_head_rows(q4_ref, seg(W_QF, W_KF), tm)
        _store_head_rows(k4_ref, seg(W_KF, W_VF), tm)
        _store_head_rows(v4_ref, seg(W_VF, W_ZF), tm)
        _store_head_rows(z4_ref, seg(W_ZF, W_SM), tm)
        return
    (fb_ref, h_ref, zg_ref, sm_ref, fcol_ref, logft_ref, qb_ref, kb_ref, vb_ref, zf_ref, k4_ref, v4_ref,
     carry_ref) = refs
    h_ref[...] = h

    @pl.when(pl.program_id(0) % seq_tiles == 0)
    def _():
        carry_ref[...] = jnp.zeros(carry_ref.shape, F32)

    sm = seg(W_SM, W_END)
    sm_ref[...] = sm
    zg_ref[...] = seg(W_ZG, W_QF)
    qb_ref[...] = (seg(W_QF, W_KF) * (HEAD_DIM ** -0.5)).astype(BF16)
    kf = seg(W_KF, W_VF)
    _store_head_rows(k4_ref, kf, tm)
    kb_ref[...] = kf.astype(BF16)
    vf = seg(W_VF, W_ZF)
    _store_head_rows(v4_ref, vf, tm)
    vb_ref[...] = vf.astype(BF16)
    zf_ref[...] = seg(W_ZF, W_SM)

    tri = (_iota2((LANES, LANES), 0) >= _iota2((LANES, LANES), 1)).astype(BF16)
    blocks = [slice(i * LANES, (i + 1) * LANES) for i in range(tm // LANES)]
    lf = [_log_sigmoid(sm[blk] + fb_ref[...]) for blk in blocks]
    parts = [jnp.dot(tri, jnp.concatenate(_split3(t), axis=1), preferred_element_type=F32) for t in lf]
    within = [p[:, 0:LANES] + p[:, LANES:2 * LANES] + p[:, 2 * LANES:3 * LANES] for p in parts]
    carry = carry_ref[0:1, :]
    for i, blk in enumerate(blocks):
        fcol_ref[blk, :] = within[i] + carry
        carry = carry + within[i][LANES - 1:LANES, :]
        logft_ref[:, blk] = lf[i].T[SM_FORGET:SM_FORGET + N_HEADS, :]
    carry_ref[0:1, :] = carry


def _proj_call(x2d, norm_w, w_big, tm, sample, fb_row=None, seq_len=None):
    t, d = x2d.shape
    n = w_big.shape[0]
    wide = lambda w, dt: (jax.ShapeDtypeStruct((t, w), dt), pl.BlockSpec((tm, w), lambda i: (i, 0)))
    rows4 = (jax.ShapeDtypeStruct((t * N_HEADS, HEAD_DIM), F32),
             pl.BlockSpec((tm * N_HEADS, HEAD_DIM), lambda i: (i, 0)))
    operands = [x2d, norm_w, w_big]
    in_specs = [pl.BlockSpec((tm, d), lambda i: (i, 0)),
                pl.BlockSpec((1, d), lambda i: (0, 0)),
                pl.BlockSpec((n, d), lambda i: (0, 0))]
    scratch = []
    seq_tiles = 1
    if sample:
        outs = [wide(CONV_DIM, F32), wide(GROUP_W, F32), wide(LANES, F32), rows4, rows4, rows4, rows4]
    else:
        seq_tiles = seq_len // tm
        logft = (jax.ShapeDtypeStruct((t // seq_len, N_HEADS, seq_len), F32),
                 pl.BlockSpec((None, N_HEADS, tm), lambda i: (i // seq_tiles, 0, i % seq_tiles)))
        outs = [wide(d, BF16), wide(GROUP_W, F32), wide(LANES, F32), wide(LANES, F32), logft,
                wide(GROUP_W, BF16), wide(GROUP_W, BF16), wide(GROUP_W, BF16), wide(GROUP_W, F32), rows4, rows4]
        operands.append(fb_row)
        in_specs.append(pl.BlockSpec((1, LANES), lambda i: (0, 0)))
        scratch = [pltpu.VMEM((SUBLANES, LANES), F32)]
    out_shape = [o[0] for o in outs]
    out_specs = [o[1] for o in outs]
    return pl.pallas_call(
        functools.partial(_proj_kernel, tm=tm, sample=sample, seq_tiles=seq_tiles),
        grid=(t // tm,),
        in_specs=in_specs,
        out_specs=out_specs,
        out_shape=out_shape,
        scratch_shapes=scratch,
        compiler_params=pltpu.CompilerParams(dimension_semantics=("arbitrary",),
                                             vmem_limit_bytes=VMEM_LIMIT),
        name="proj",
    )(*operands)


def _gdn_kernel(*refs, **kw):
    for _ in _gdn_stages(*refs, **kw):
        pass


def _gdn_stages(*refs, c, l_valid, nb, project, chunk_of_step=None):
    if project:
        (h_ref, wqkv_ref, zg_ref, sm_ref, cw_ref, alog_ref, dtb_ref, onw_ref,
         og_ref, sout_ref, tail_ref, xbuf, s_scr) = refs
    else:
        (qkv_ref, zg_ref, sm_ref, cw_ref, alog_ref, dtb_ref, onw_ref, s0_ref, c0_ref,
         og_ref, sout_ref, xbuf, s_scr) = refs
    if chunk_of_step is None:
        ci, n_c = pl.program_id(1), pl.num_programs(1)
    else:
        ci, n_c = chunk_of_step

    @pl.when(ci == 0)
    def _():
        if project:
            xbuf[:, 0:SUBLANES, :] = jnp.zeros((nb, SUBLANES, CONV_DIM), F32)
            s_scr[...] = jnp.zeros(s_scr.shape, F32)
        else:
            xbuf[:, SUBLANES - (CONV_K - 1):SUBLANES, :] = c0_ref[...]
            s_scr[...] = s0_ref[...]

    row = _iota2((c, 1), 0) + ci * c
    valid = jnp.broadcast_to((row < l_valid).astype(F32), (c, LANES))
    tri_incl = (_iota2((c, c), 0) >= _iota2((c, c), 1))
    tri_strict = (_iota2((c, c), 0) > _iota2((c, c), 1))
    eye = (_iota2((c, c), 0) == _iota2((c, c), 1)).astype(F32)
    pad_rows = LANES - c
    sl = lambda base, h: slice(base + h * HEAD_DIM, base + (h + 1) * HEAD_DIM)

    if project:
        raw = jnp.dot(h_ref[...].reshape(nb * c, h_ref.shape[-1]), wqkv_ref[...],
                      preferred_element_type=F32)
    yield
    q, k, v, beta, gc, gc_row, gc_last = [], [], [], [], [], [], []
    for bb in range(nb):
        xbuf[bb, SUBLANES:SUBLANES + c, :] = raw[bb * c:(bb + 1) * c] if project else qkv_ref[bb]
        yq, yk, yv = (_conv_silu_qkv(xbuf.at[bb], cw_ref, g, c) for g in range(3))
        if project:
            tail_ref[bb] = xbuf[bb, c:c + SUBLANES, :]
        xbuf[bb, 0:SUBLANES, :] = xbuf[bb, c:c + SUBLANES, :]
        sm = sm_ref[bb]
        beta_t = _sigmoid(sm) * valid
        g_t = -jnp.exp(alog_ref[...]) * _softplus(sm + dtb_ref[...]) * valid
        gc_t = _fdot(tri_incl.astype(F32), g_t)
        gc_sq = jnp.concatenate([gc_t, jnp.zeros((pad_rows, LANES), F32)], axis=0) if pad_rows else gc_t
        gc_tr = gc_sq.T
        for h in range(N_HEADS):
            q.append(_l2_normalize(yq[:, sl(0, h)], HEAD_DIM ** -0.5))
            k.append(_l2_normalize(yk[:, sl(0, h)], 1.0) * valid)
            v.append(yv[:, sl(0, h)])
            beta.append(jnp.broadcast_to(beta_t[:, SM_BETA + h:SM_BETA + h + 1], (c, HEAD_DIM)))
            gc.append(jnp.broadcast_to(gc_t[:, SM_DECAY + h:SM_DECAY + h + 1], (c, HEAD_DIM)))
            gc_row.append(gc_tr[SM_DECAY + h:SM_DECAY + h + 1, 0:c])
            gc_last.append(jnp.broadcast_to(gc_t[c - 1:c, SM_DECAY + h:SM_DECAY + h + 1], (1, HEAD_DIM)))

    chains = range(nb * N_HEADS)
    decay = [jnp.where(tri_incl, jnp.exp(jnp.where(tri_incl, gc[i][:, 0:c] - gc_row[i], 0.0)), 0.0)
             for i in chains]
    kb = [k[i] * beta[i] for i in chains]
    kkqk = [_bdot_nt(jnp.concatenate([kb[i], q[i]], axis=0), k[i]) for i in chains]
    qk = [kkqk[i][c:2 * c] * decay[i] for i in chains]
    yield
    neg_a = [-jnp.where(tri_strict, kkqk[i][0:c] * decay[i], 0.0) for i in chains]
    base = min(INV_BASE, c)
    blk_r, blk_c = _iota2((c, c), 0), _iota2((c, c), 1)
    same = lambda size: (blk_r // size) == (blk_c // size)
    diag = [jnp.where(same(base), neg_a[i], 0.0) for i in chains] if base < c else neg_a
    t_inv = [eye + diag[i] for i in chains]
    pw = [_bdot(diag[i], diag[i]) for i in chains]
    yield
    n_sq = int(math.log2(base))
    for j in range(1, n_sq):
        if j < n_sq - 1:
            both = [_bdot(jnp.concatenate([t_inv[i], pw[i]], axis=0), pw[i]) for i in chains]
            t_inv = [t_inv[i] + both[i][0:c] for i in chains]
            pw = [both[i][c:2 * c] for i in chains]
        else:
            t_inv = [t_inv[i] + _bdot(t_inv[i], pw[i]) for i in chains]
        yield
    size = base
    while size < c:
        off = [jnp.where(same(2 * size) & ~same(size), neg_a[i], 0.0) for i in chains]
        right = [_bdot(off[i], t_inv[i]) for i in chains]
        yield
        t_inv = [t_inv[i] + _bdot(t_inv[i], right[i]) for i in chains]
        yield
        size *= 2
    egc = [jnp.exp(gc[i]) for i in chains]
    sol = [_bdot(t_inv[i], jnp.concatenate([v[i] * beta[i], kb[i] * egc[i]], axis=-1)) for i in chains]
    yield
    s = [s_scr[i // N_HEADS, i % N_HEADS] for i in chains]
    ws = [_bdot(jnp.concatenate([sol[i][:, HEAD_DIM:2 * HEAD_DIM], q[i] * egc[i]], axis=0), s[i])
          for i in chains]
    yield
    v_new = [sol[i][:, 0:HEAD_DIM] - ws[i][0:c] for i in chains]
    o = [ws[i][c:2 * c] + _bdot(qk[i], v_new[i]) for i in chains]
    k_dec = [k[i] * jnp.exp(gc_last[i] - gc[i]) for i in chains]
    s_new = [s[i] * jnp.exp(gc_last[i]) + _bdot_tn(k_dec[i], v_new[i]) for i in chains]
    yield
    for i in chains:
        bb, h = i // N_HEADS, i % N_HEADS
        s_scr[bb, h] = s_new[i]
        oh = o[i] * lax.rsqrt(jnp.mean(o[i] * o[i], axis=-1, keepdims=True) + NORM_EPS) * onw_ref[...]
        z = zg_ref[bb, :, sl(0, h)]
        og_ref[bb, :, sl(0, h)] = (oh * (z * _sigmoid(z))).astype(og_ref.dtype)

    @pl.when(ci == n_c - 1)
    def _():
        sout_ref[...] = s_scr[...]


def _gdn_call(src, zg, sm, conv_w, alog_row, dtb_row, onw, *, c, l_valid, nb, w_qkv=None, s0=None, c0=None):
    project = w_qkv is not None
    b, l, _ = zg.shape
    n_c = l // c
    assert not project or l_valid == l
    kern = functools.partial(_gdn_kernel, c=c, l_valid=l_valid, nb=nb, project=project)
    blk = lambda w: pl.BlockSpec((nb, c, w), lambda bi, ci: (bi, ci, 0))
    full = lambda shape: pl.BlockSpec(shape, lambda bi, ci: (0,) * len(shape))
    state = pl.BlockSpec((nb, N_HEADS, HEAD_DIM, HEAD_DIM), lambda bi, ci: (bi, 0, 0, 0))
    rows8 = pl.BlockSpec((nb, SUBLANES, CONV_DIM), lambda bi, ci: (bi, 0, 0))
    common = [blk(GROUP_W), blk(LANES), full((CONV_K, CONV_DIM)), full((1, LANES)), full((1, LANES)),
              full((1, HEAD_DIM))]
    out_specs = [blk(GROUP_W), state]
    out_shape = [jax.ShapeDtypeStruct((b, l, GROUP_W), BF16),
                 jax.ShapeDtypeStruct((b, N_HEADS, HEAD_DIM, HEAD_DIM), F32)]
    if project:
        d = src.shape[-1]
        operands = (src, w_qkv, zg, sm, conv_w, alog_row, dtb_row, onw)
        in_specs = [blk(d), full((d, CONV_DIM))] + common
        out_specs.append(rows8)
        out_shape.append(jax.ShapeDtypeStruct((b, SUBLANES, CONV_DIM), F32))
    else:
        operands = (src, zg, sm, conv_w, alog_row, dtb_row, onw, s0, c0)
        in_specs = [blk(CONV_DIM)] + common + [
            state, pl.BlockSpec((nb, CONV_K - 1, CONV_DIM), lambda bi, ci: (bi, 0, 0))]
    return pl.pallas_call(
        kern,
        grid=(b // nb, n_c),
        in_specs=in_specs,
        out_specs=out_specs,
        out_shape=out_shape,
        scratch_shapes=[pltpu.VMEM((nb, c + SUBLANES, CONV_DIM), F32),
                        pltpu.VMEM((nb, N_HEADS, HEAD_DIM, HEAD_DIM), F32)],
        compiler_params=pltpu.CompilerParams(dimension_semantics=("arbitrary", "arbitrary"),
                                             vmem_limit_bytes=VMEM_LIMIT),
        name="gdn",
    )(*operands)


NEG_BIG = -1e30


def _forget_columns(f_tile, h, rows, for_keys):
    f = jnp.broadcast_to(f_tile[:, SM_FORGET + h:SM_FORGET + h + 1], (rows, LANES))
    f1, f2, f3 = (t.astype(F32) for t in _split3(-f if for_keys else f))
    lane = _iota2((rows, LANES), 1)
    base = 3 if for_keys else 0
    ones = ((lane >= 3 - base) & (lane < 6 - base)).astype(F32)
    cols = jnp.where(lane == base, f1, jnp.where(lane == base + 1, f2, jnp.where(lane == base + 2, f3, ones)))
    return cols.astype(BF16)


ROW_GROUP = 32


def _fox_prompt_kernel(q_ref, k_ref, v_ref, fcol_ref, zf_ref, og_ref, x_ref, wo_ref, fnw_ref, y_ref,
                       kx_ref, qa_ref, s_ref, p_ref, acc_ref, m_ref, a_ref, *, tq, l):
    qi = pl.program_id(1)
    heads = range(N_HEADS)
    sl = lambda h: slice(h * HEAD_DIM, (h + 1) * HEAD_DIM)
    nt = (((1,), (1,)), ((), ()))

    @pl.when(qi == 0)
    def _():
        for r in range(l // tq):
            for h in heads:
                kx_ref[r * tq:(r + 1) * tq, sl(h)] = _forget_columns(fcol_ref[r * tq:(r + 1) * tq, :], h, tq, True)

    f_q = fcol_ref[pl.ds(pl.multiple_of(qi * tq, tq), tq), :]
    for h in heads:
        qa_ref[h, :, 0:HEAD_DIM] = q_ref[:, sl(h)]
        qa_ref[h, :, HEAD_DIM:2 * HEAD_DIM] = _forget_columns(f_q, h, tq, False)
    acc_ref[...] = jnp.zeros(acc_ref.shape, F32)
    m_ref[...] = jnp.full(m_ref.shape, NEG_BIG, F32)
    ones = jnp.ones((tq, HEAD_DIM), BF16)

    half = tq // 2

    def block(ki, masked):
        start = pl.multiple_of(ki * tq, tq)
        pieces = [(slice(0, half), half), (slice(half, tq), tq)] if masked else [(slice(0, tq), tq)]
        for h in heads:
            for qr, nk in pieces:
                keys = pl.ds(start, nk)
                ka = jnp.concatenate([k_ref[keys, sl(h)], kx_ref[keys, sl(h)]], axis=1)
                s_ref[h, qr, 0:nk] = lax.dot_general(qa_ref[h, qr, :], ka, nt, preferred_element_type=F32)
        for h in heads:
            for r in range(0, tq, ROW_GROUP):
                rg = slice(r, r + ROW_GROUP)
                nk = half if (masked and r < half) else tq
                s = s_ref[h, rg, 0:nk]
                if masked:
                    keep = _iota2((ROW_GROUP, nk), 1) <= _iota2((ROW_GROUP, nk), 0) + r
                    s = jnp.where(keep, s, NEG_BIG)
                m_old = m_ref[h, rg, :]
                m_new = jnp.maximum(m_old, jnp.max(s, axis=-1, keepdims=True))
                a_ref[h, rg, :] = jnp.exp(m_old - m_new)
                m_ref[h, rg, :] = m_new
                p_ref[h, rg, 0:nk] = jnp.exp(s - jnp.concatenate([m_new] * (nk // LANES), axis=1)).astype(BF16)
        for h in heads:
            alpha = a_ref[h]
            for qr, nk in pieces:
                keys = pl.ds(start, nk)
                pv = jnp.dot(p_ref[h, qr, 0:nk], jnp.concatenate([v_ref[keys, sl(h)], ones[0:nk]], axis=1),
                             preferred_element_type=F32)
                acc_ref[h, qr, :] = acc_ref[h, qr, :] * jnp.concatenate([alpha[qr], alpha[qr]], axis=1) + pv

    def body(ki, carry):
        block(ki, False)
        return carry

    lax.fori_loop(0, qi, body, 0)
    block(qi, True)
    gated = []
    for h in heads:
        z = zf_ref[:, sl(h)]
        o = acc_ref[h, :, 0:HEAD_DIM] / acc_ref[h, :, HEAD_DIM:2 * HEAD_DIM]
        gated.append((o * (z * _sigmoid(z))).astype(BF16))
    mixed = jnp.dot(og_ref[...], wo_ref[0:GROUP_W, :], preferred_element_type=F32)
    mixed = mixed + jnp.dot(jnp.concatenate(gated, axis=1), wo_ref[GROUP_W:2 * GROUP_W, :],
                            preferred_element_type=F32)
    y = x_ref[...] + mixed
    var = jnp.mean(y * y, axis=-1, keepdims=True)
    y_ref[...] = y * lax.rsqrt(var + NORM_EPS) * fnw_ref[...]


def _fox_prompt_call(qf, kf, vf, fcol, zf, og, x, w_out, fnw, tq):
    b, l, _ = qf.shape
    d = x.shape[-1]
    kern = functools.partial(_fox_prompt_kernel, tq=tq, l=l)
    qblk = lambda w: pl.BlockSpec((None, tq, w), lambda bi, qi: (bi, qi, 0))
    seq = lambda w: pl.BlockSpec((None, l, w), lambda bi, qi: (bi, 0, 0))
    const = lambda shape: pl.BlockSpec(shape, lambda bi, qi: (0,) * len(shape))
    return pl.pallas_call(
        kern,
        grid=(b, l // tq),
        in_specs=[qblk(GROUP_W), seq(GROUP_W), seq(GROUP_W), seq(LANES), qblk(GROUP_W),
                  qblk(GROUP_W), qblk(d), const((2 * GROUP_W, d)), const((1, d))],
        out_specs=qblk(d),
        out_shape=jax.ShapeDtypeStruct((b, l, d), F32),
        scratch_shapes=[pltpu.VMEM((l, GROUP_W), BF16),
                        pltpu.VMEM((N_HEADS, tq, 2 * HEAD_DIM), BF16),
                        pltpu.VMEM((N_HEADS, tq, tq), F32),
                        pltpu.VMEM((N_HEADS, tq, tq), BF16),
                        pltpu.VMEM((N_HEADS, tq, 2 * HEAD_DIM), F32),
                        pltpu.VMEM((N_HEADS, tq, LANES), F32),
                        pltpu.VMEM((N_HEADS, tq, LANES), F32)],
        compiler_params=pltpu.CompilerParams(dimension_semantics=("arbitrary", "arbitrary"),
                                             vmem_limit_bytes=VMEM_LIMIT),
        name="fox_prompt",
    )(qf, kf, vf, fcol, zf, og, x, w_out, fnw)


def _out_kernel(og_ref, of_ref, x_ref, w_ref, fnw_ref, y_ref):
    o = jnp.dot(og_ref[...], w_ref[0:GROUP_W, :], preferred_element_type=F32)
    o = o + jnp.dot(of_ref[...], w_ref[GROUP_W:2 * GROUP_W, :], preferred_element_type=F32)
    y = x_ref[...] + o
    var = jnp.mean(y * y, axis=-1, keepdims=True)
    y_ref[...] = y * lax.rsqrt(var + NORM_EPS) * fnw_ref[...]


def _out_call(og, of, x2d, w_out, fnw, tm):
    t, d = x2d.shape
    return pl.pallas_call(
        _out_kernel,
        grid=(t // tm,),
        in_specs=[pl.BlockSpec((tm, GROUP_W), lambda i: (i, 0)),
                  pl.BlockSpec((tm, GROUP_W), lambda i: (i, 0)),
                  pl.BlockSpec((tm, d), lambda i: (i, 0)),
                  pl.BlockSpec((2 * GROUP_W, d), lambda i: (0, 0)),
                  pl.BlockSpec((1, d), lambda i: (0, 0))],
        out_specs=pl.BlockSpec((tm, d), lambda i: (i, 0)),
        out_shape=jax.ShapeDtypeStruct((t, d), F32),
        compiler_params=pltpu.CompilerParams(dimension_semantics=("arbitrary",),
                                             vmem_limit_bytes=VMEM_LIMIT),
        name="out_proj",
    )(og, of, x2d, w_out, fnw)


def _page_copies(pt_ref, kc_ref, vc_ref, lc_ref, kbuf, vbuf, lbuf, sems, step, slot, n_pages, pg, rows):
    copies = []
    for r in range(rows):
        for p in range(n_pages):
            pid = pt_ref[step * rows + r, p]
            copies.append((pltpu.make_async_copy(kc_ref.at[pid], kbuf.at[slot, r, pl.ds(p * pg, pg)],
                                                 sems.at[0, slot]), 0))
            copies.append((pltpu.make_async_copy(vc_ref.at[pid], vbuf.at[slot, r, pl.ds(p * pg, pg)],
                                                 sems.at[1, slot]), 1))
            copies.append((pltpu.make_async_copy(lc_ref.at[pid], lbuf.at[slot, r, :, p, :], sems.at[2, slot]),
                           p % 2))
    return copies


def _split3(x):
    x1 = x.astype(BF16)
    r1 = x - x1.astype(F32)
    x2 = r1.astype(BF16)
    x3 = (r1 - x2.astype(F32)).astype(BF16)
    return x1, x2, x3


def _fox_decode_kernel(*refs, **kw):
    for _ in _fox_decode_phases(*refs, **kw):
        pass


def _fox_decode_phases(pt_ref, q_ref, kn_ref, vn_ref, zf_ref, sm_ref, fbrow_ref, cums_ref,
                       kc_ref, vc_ref, lc_ref, o_ref, logf_ref, kbuf, vbuf, lbuf, sems,
                       *, n_pages, pg, l_new, rows):
    step = pl.program_id(0)
    n_steps = pl.num_programs(0)
    slot = step % 2
    copies = functools.partial(_page_copies, pt_ref, kc_ref, vc_ref, lc_ref, kbuf, vbuf, lbuf, sems,
                               n_pages=n_pages, pg=pg, rows=rows)
    nr = l_new * N_HEADS

    @pl.when(step == 0)
    def _():
        for cp, prio in copies(step=step, slot=slot):
            cp.start(priority=prio)

    @pl.when(step + 1 < n_steps)
    def _():
        for cp, prio in copies(step=step + 1, slot=1 - slot):
            cp.start(priority=prio)

    for cp, _ in copies(step=step, slot=slot):
        cp.wait()
    yield

    scale = HEAD_DIM ** -0.5
    earlier = (_iota2((n_pages, n_pages), 0) > _iota2((n_pages, n_pages), 1)).astype(F32)
    tok_valid = (_iota2((SUBLANES, 1), 0) < l_new).astype(F32)
    r_tok = _iota2((nr, SUBLANES), 0) // N_HEADS
    f_past, f_tot_row, csum, s_all, s_new = [], [], [], [], []
    for r in range(rows):
        qrows = slice(r * nr, (r + 1) * nr)
        res = jnp.zeros((3 * n_pages, 2 * pg), F32)
        for h in range(N_HEADS):
            res = res + jnp.dot(jnp.concatenate(_split3(lbuf[slot, r, h]), axis=0), cums_ref[h],
                                preferred_element_type=F32)
        res = res[0:n_pages] + res[n_pages:2 * n_pages] + res[2 * n_pages:3 * n_pages]
        within, tot = res[:, 0:pg], res[:, pg:2 * pg]
        carry = _fdot(earlier, tot)
        f_past.append(within + carry)
        f_tot_row.append(carry[n_pages - 1:n_pages, :] + tot[n_pages - 1:n_pages, :])
        lf_col = _log_sigmoid(sm_ref[r] + fbrow_ref[...]) * tok_valid
        logf_ref[r] = lf_col
        csum.append(_fdot((_iota2((nr, SUBLANES), 1) <= r_tok).astype(F32), lf_col))
        q = q_ref[qrows, :].astype(BF16)
        s_all.append(lax.dot_general(q, kbuf[slot, r].astype(BF16), (((1,), (1,)), ((), ())),
                                     preferred_element_type=F32))
        s_new.append(lax.dot_general(q, kn_ref[qrows, :].astype(BF16), (((1,), (1,)), ((), ())),
                                     preferred_element_type=F32))
    yield
    own_lane = _iota2((nr, LANES), 1) == SM_FORGET + _iota2((nr, LANES), 0) % N_HEADS
    eye = _iota2((nr, nr), 0) == _iota2((nr, nr), 1)
    same_head = (_iota2((nr, pg), 1) % N_HEADS) == (_iota2((nr, pg), 0) % N_HEADS)
    rr, cc = _iota2((nr, nr), 0), _iota2((nr, nr), 1)
    new_ok = (rr % N_HEADS == cc % N_HEADS) & (cc // N_HEADS <= rr // N_HEADS)
    for r in range(rows):
        qrows = slice(r * nr, (r + 1) * nr)
        fq_new = jnp.sum(jnp.where(own_lane, csum[r], 0.0), axis=-1, keepdims=True)
        f_tot_col = jnp.sum(jnp.where(eye, jnp.broadcast_to(f_tot_row[r][:, 0:nr], (nr, nr)), 0.0),
                            axis=-1, keepdims=True)
        fq = fq_new + f_tot_col
        fq_row = jnp.sum(jnp.where(eye, jnp.broadcast_to(fq, (nr, nr)), 0.0), axis=0, keepdims=True)
        sp = [jnp.where(same_head, s_all[r][:, p * pg:(p + 1) * pg] * scale + (fq - f_past[r][p:p + 1, :]),
                        NEG_BIG) for p in range(n_pages)]
        sn = jnp.where(new_ok, s_new[r] * scale + (fq - fq_row), NEG_BIG)
        m_el = sp[0]
        for p in range(1, n_pages):
            m_el = jnp.maximum(m_el, sp[p])
        m = jnp.maximum(jnp.max(m_el, axis=-1, keepdims=True), jnp.max(sn, axis=-1, keepdims=True))
        pp = [jnp.exp(t - m) for t in sp]
        p_new = jnp.exp(sn - m)
        l_el = pp[0]
        for p in range(1, n_pages):
            l_el = l_el + pp[p]
        l = jnp.sum(l_el, axis=-1, keepdims=True) + jnp.sum(p_new, axis=-1, keepdims=True)
        p_all = jnp.concatenate([t.astype(BF16) for t in pp], axis=-1)
        acc = jnp.dot(p_all, vbuf[slot, r].astype(BF16), preferred_element_type=F32)
        acc = acc + jnp.dot(p_new.astype(BF16), vn_ref[qrows, :].astype(BF16), preferred_element_type=F32)
        z = zf_ref[qrows, :]
        o_ref[qrows, :] = ((acc / l) * (z * _sigmoid(z))).astype(o_ref.dtype)


def _head_cumsum_matrix(page):
    t = jnp.arange(page)[None, :, None]
    j = jnp.arange(page * N_HEADS)[None, None, :]
    h = jnp.arange(N_HEADS)[:, None, None]
    own = (j % N_HEADS) == h
    c = own & (t <= j // N_HEADS)
    b = jnp.broadcast_to(own, c.shape)
    return jnp.concatenate([c, b], axis=2).astype(BF16)


DECODE_ROWS_PER_STEP = 2
N_DECODE_INPUTS, N_DECODE_OUTPUTS, N_DECODE_SCRATCH = 10, 2, 4
N_GDN_INPUTS, N_GDN_OUTPUTS, N_GDN_SCRATCH = 8, 3, 2
N_SAMPLE_GDN_INPUTS, N_SAMPLE_GDN_OUTPUTS = 4, 2


def _decode_gdn_kernel(pt_ref, *refs, decode_kw, gdn_kw, sample_kw, n_chunks):
    take = lambda n: (refs[:n], refs[n:])
    dec_in, refs = take(N_DECODE_INPUTS)
    gdn_in, refs = take(N_GDN_INPUTS)
    smp_in, refs = take(N_SAMPLE_GDN_INPUTS)
    dec_out, refs = take(N_DECODE_OUTPUTS)
    gdn_out, refs = take(N_GDN_OUTPUTS)
    smp_out, refs = take(N_SAMPLE_GDN_OUTPUTS)
    dec_scr, refs = take(N_DECODE_SCRATCH)
    gdn_scr, smp_scr = take(N_GDN_SCRATCH)
    decode = _fox_decode_phases(pt_ref, *dec_in, *dec_out, *dec_scr, **decode_kw)
    next(decode)
    prompt = _gdn_stages(*gdn_in, *gdn_out, *gdn_scr, **gdn_kw,
                         chunk_of_step=(pl.program_id(0) % n_chunks, n_chunks))
    qkv_s, zg_s, s0_s, c0_s = smp_in
    sm_s = dec_in[4]
    sample = _gdn_stages(qkv_s, zg_s, sm_s, *gdn_in[4:8], s0_s, c0_s, *smp_out, *smp_scr, **sample_kw,
                         chunk_of_step=(jnp.int32(0), 1))
    programs = [prompt, sample, decode]
    while programs:
        for prog in list(programs):
            if next(prog, StopIteration) is StopIteration:
                programs.remove(prog)


def _fox_decode_call(page_table, q4, kn4, vn4, zf4, sm8, fb_row, kcache, vcache, lcache, l_new, gdn=None):
    b, n_pages = page_table.shape
    pg = kcache.shape[1]
    page = lcache.shape[2]
    nr = l_new * N_HEADS
    assert nr % (2 * SUBLANES) == 0 and l_new <= SUBLANES
    rps = DECODE_ROWS_PER_STEP
    assert b % rps == 0
    n_steps = b // rps
    decode_kw = dict(n_pages=n_pages, pg=pg, l_new=l_new, rows=rps)
    rows = pl.BlockSpec((rps * nr, HEAD_DIM), lambda i, pt: (i, 0))
    tok = pl.BlockSpec((rps, SUBLANES, LANES), lambda i, pt: (i, 0, 0))
    const = lambda shape: pl.BlockSpec(shape, lambda i, pt: (0,) * len(shape))
    any_spec = pl.BlockSpec(memory_space=pl.ANY)
    operands = [page_table, q4, kn4, vn4, zf4, sm8, fb_row, _head_cumsum_matrix(page), kcache, vcache, lcache]
    in_specs = [rows, rows, rows, rows, tok, const((1, LANES)), const((N_HEADS, page, 2 * pg)),
                any_spec, any_spec, any_spec]
    out_specs = [rows, tok]
    out_shape = [jax.ShapeDtypeStruct((b * nr, HEAD_DIM), BF16), jax.ShapeDtypeStruct((b, SUBLANES, LANES), F32)]
    scratch = [pltpu.VMEM((2, rps, n_pages * pg, HEAD_DIM), F32),
               pltpu.VMEM((2, rps, n_pages * pg, HEAD_DIM), F32),
               pltpu.VMEM((2, rps, N_HEADS, n_pages, page), F32),
               pltpu.SemaphoreType.DMA((3, 2))]
    kern = functools.partial(_fox_decode_kernel, **decode_kw)
    if gdn is not None:
        h3, w_qkv, zg, sm, conv_w, alog_row, dtb_row, onw, c, (qkv_s8, zg_s8, s0_s, c0_s) = gdn
        bp, l, d = h3.shape
        n_c = l // c
        nb = bp * n_c // n_steps
        assert nb >= 1 and (bp // nb) * n_c == n_steps
        blk = lambda w: pl.BlockSpec((nb, c, w), lambda i, pt: (i // n_c, i % n_c, 0))
        state = pl.BlockSpec((nb, N_HEADS, HEAD_DIM, HEAD_DIM), lambda i, pt: (i // n_c, 0, 0, 0))
        rows8 = pl.BlockSpec((nb, SUBLANES, CONV_DIM), lambda i, pt: (i // n_c, 0, 0))
        srow = lambda w: pl.BlockSpec((rps, SUBLANES, w), lambda i, pt: (i, 0, 0))
        sstate = pl.BlockSpec((rps, N_HEADS, HEAD_DIM, HEAD_DIM), lambda i, pt: (i, 0, 0, 0))
        operands += [h3, w_qkv, zg, sm, conv_w, alog_row, dtb_row, onw, qkv_s8, zg_s8, s0_s, c0_s]
        in_specs += [blk(d), const((d, CONV_DIM)), blk(GROUP_W), blk(LANES), const((CONV_K, CONV_DIM)),
                     const((1, LANES)), const((1, LANES)), const((1, HEAD_DIM)),
                     srow(CONV_DIM), srow(GROUP_W), sstate,
                     pl.BlockSpec((rps, CONV_K - 1, CONV_DIM), lambda i, pt: (i, 0, 0))]
        out_specs += [blk(GROUP_W), state, rows8, srow(GROUP_W), sstate]
        out_shape += [jax.ShapeDtypeStruct((bp, l, GROUP_W), BF16),
                      jax.ShapeDtypeStruct((bp, N_HEADS, HEAD_DIM, HEAD_DIM), F32),
                      jax.ShapeDtypeStruct((bp, SUBLANES, CONV_DIM), F32),
                      jax.ShapeDtypeStruct((b, SUBLANES, GROUP_W), BF16),
                      jax.ShapeDtypeStruct((b, N_HEADS, HEAD_DIM, HEAD_DIM), F32)]
        scratch += [pltpu.VMEM((nb, c + SUBLANES, CONV_DIM), F32),
                    pltpu.VMEM((nb, N_HEADS, HEAD_DIM, HEAD_DIM), F32),
                    pltpu.VMEM((rps, 2 * SUBLANES, CONV_DIM), F32),
                    pltpu.VMEM((rps, N_HEADS, HEAD_DIM, HEAD_DIM), F32)]
        kern = functools.partial(_decode_gdn_kernel, decode_kw=decode_kw, n_chunks=n_c,
                                 gdn_kw=dict(c=c, l_valid=l, nb=nb, project=True),
                                 sample_kw=dict(c=SUBLANES, l_valid=l_new, nb=rps, project=False))
    grid_spec = pltpu.PrefetchScalarGridSpec(num_scalar_prefetch=1, grid=(n_steps,), in_specs=in_specs,
                                             out_specs=out_specs, scratch_shapes=scratch)
    return pl.pallas_call(
        kern,
        grid_spec=grid_spec,
        out_shape=out_shape,
        compiler_params=pltpu.CompilerParams(dimension_semantics=("arbitrary",),
                                             vmem_limit_bytes=VMEM_LIMIT),
        name="fox_decode",
    )(*operands)


def _gate_row(vals, offset):
    return jnp.zeros((1, LANES), F32).at[0, offset:offset + N_HEADS].set(vals.astype(F32))


def _pad_rows(t, rows):
    return jnp.pad(t, ((0, 0), (0, rows - t.shape[1]), (0, 0)))


def kernel(x_prompt, x_sample, cache_fox_k, cache_fox_v, cache_fox_logf, page_table, state_gdn_ssm,
           state_gdn_conv, w_in, gdn_conv_w, gdn_a_log, gdn_dt_bias, gdn_out_norm_w, fox_f_bias, w_out,
           norm_w, final_norm_w):
    bp, lp, d = x_prompt.shape
    bs, ls, _ = x_sample.shape
    depth = w_in.shape[0]
    assert depth == 1, "single-layer trunk"
    n_pool, page = cache_fox_k.shape[1], cache_fox_k.shape[2]

    w_big, w_qkv = _pack_w_call(w_in[0].T)
    w_o = w_out[0].astype(BF16)
    nw = norm_w[0].reshape(1, d)
    fnw = final_norm_w.reshape(1, d)
    conv_w = gdn_conv_w[0]
    alog_row = _gate_row(gdn_a_log[0], SM_DECAY)
    dtb_row = _gate_row(gdn_dt_bias[0], SM_DECAY)
    fb_row = _gate_row(fox_f_bias[0], SM_FORGET)
    onw = gdn_out_norm_w[0].reshape(1, HEAD_DIM)

    xp2 = x_prompt.reshape(bp * lp, d)
    hp, zg, sm, fcol, logf_t, qf, kf, vf, zf, k4, v4 = _proj_call(xp2, nw, w_big, tm=512, sample=False,
                                                                  fb_row=fb_row, seq_len=lp)
    r3 = lambda t: t.reshape(bp, lp, t.shape[-1])

    xs2 = x_sample.reshape(bs * ls, d)
    qkv_s, zg_s, sm_s, q4_s, k4_s, v4_s, z4_s = _proj_call(xs2, nw, w_big, tm=256, sample=True)
    r3s = lambda t: t.reshape(bs, ls, t.shape[-1])
    p8 = lambda t: _pad_rows(r3s(t), SUBLANES)
    kcache = cache_fox_k[0].reshape(n_pool, page * N_HEADS, HEAD_DIM)
    vcache = cache_fox_v[0].reshape(n_pool, page * N_HEADS, HEAD_DIM)
    lcache = cache_fox_logf[0].transpose(0, 2, 1)
    of_s, logf_s, og_p, ssm_p, tail, og_s, ssm_s = _fox_decode_call(
        page_table, q4_s, k4_s, v4_s, z4_s, p8(sm_s), fb_row, kcache, vcache, lcache, l_new=ls,
        gdn=(r3(hp), w_qkv, r3(zg), r3(sm), conv_w, alog_row, dtb_row, onw, GDN_CHUNK,
             (p8(qkv_s), p8(zg_s), state_gdn_ssm[0], state_gdn_conv[0])))
    og_s2 = og_s[:, :ls].reshape(bs * ls, GROUP_W)
    of_s2 = of_s.reshape(bs * ls, GROUP_W)
    y_s = _out_call(og_s2, of_s2, xs2, w_o, fnw, tm=256)
    y_prompt = _fox_prompt_call(r3(qf), r3(kf), r3(vf), r3(fcol), r3(zf), og_p, x_prompt, w_o, fnw, tq=512)

    k_prompt = k4.reshape(1, bp, lp, N_HEADS, HEAD_DIM)
    v_prompt = v4.reshape(1, bp, lp, N_HEADS, HEAD_DIM)
    logf_prompt = logf_t.transpose(0, 2, 1).reshape(1, bp, lp, N_HEADS)
    ssm_prompt = ssm_p.reshape(1, bp, N_HEADS, HEAD_DIM, HEAD_DIM)
    conv_prompt = tail[:, SUBLANES - (CONV_K - 1):, :].reshape(1, bp, CONV_K - 1, CONV_DIM)
    y_sample = y_s.reshape(bs, ls, d)
    k_sample = k4_s.reshape(1, bs, ls, N_HEADS, HEAD_DIM)
    v_sample = v4_s.reshape(1, bs, ls, N_HEADS, HEAD_DIM)
    logf_sample = logf_s[:, :ls, SM_FORGET:SM_FORGET + N_HEADS].reshape(1, bs, ls, N_HEADS)
    ssm_sample = ssm_s.reshape(1, bs, N_HEADS, HEAD_DIM, HEAD_DIM)
    if ls >= CONV_K - 1:
        conv_sample = r3s(qkv_s)[:, ls - (CONV_K - 1):, :]
    else:
        conv_sample = jnp.concatenate([state_gdn_conv[0], r3s(qkv_s)], axis=1)[:, -(CONV_K - 1):, :]
    conv_sample = conv_sample.reshape(1, bs, CONV_K - 1, CONV_DIM)

    return (y_prompt, y_sample, k_prompt, v_prompt, logf_prompt, ssm_prompt, conv_prompt,
            k_sample, v_sample, logf_sample, ssm_sample, conv_sample)
```

```python
import functools
import math

import jax
import jax.numpy as jnp
from jax import lax
from jax.experimental import pallas as pl
from jax.experimental.pallas import tpu as pltpu

F32 = jnp.float32
BF16 = jnp.bfloat16

NORM_EPS = 1e-6
L2_EPS = 1e-6
HEAD_DIM = 128
N_HEADS = 4
GROUP_W = N_HEADS * HEAD_DIM
CONV_DIM = 3 * GROUP_W
CONV_K = 4
LANES = 128
SUBLANES = 8
GDN_CHUNK = 64
INV_BASE = 32
ROWS_PER_STEP_PROMPT = 512
ROWS_PER_STEP_SAMPLE = 256
SM_BETA = 0
SM_DECAY = 4
SM_FORGET = 8
VMEM_LIMIT = 56 * 1024 * 1024


def _sigmoid(x):
    return 1.0 / (1.0 + jnp.exp(-x))


def _softplus(x):
    return jnp.maximum(x, 0.0) + jnp.log(1.0 + jnp.exp(-jnp.abs(x)))


def _log_sigmoid(x):
    return -_softplus(-x)


def _bdot(a, b):
    return jnp.dot(a.astype(BF16), b.astype(BF16), preferred_element_type=F32)


def _bdot_nt(a, b):
    return lax.dot_general(a.astype(BF16), b.astype(BF16), (((1,), (1,)), ((), ())),
                           preferred_element_type=F32)


def _bdot_tn(a, b):
    return lax.dot_general(a.astype(BF16), b.astype(BF16), (((0,), (0,)), ((), ())),
                           preferred_element_type=F32)


def _fdot(a, b):
    return jnp.dot(a, b, preferred_element_type=F32, precision=lax.Precision.HIGHEST)


def _iota2(shape, dim):
    return lax.broadcasted_iota(jnp.int32, shape, dim)


W_QKV, W_ZG, W_QF, W_KF, W_VF, W_ZF, W_SM, W_END = 0, 1536, 2048, 2560, 3072, 3584, 4096, 4224
SRC_GATES_G, SRC_FOX, SRC_GATE_F, SRC_END = 2048, 2056, 4104, 4108


def _pack_w_kernel(w_ref, o_ref, qkv_ref):
    qkv_ref[...] = w_ref[W_QKV:W_ZG, :].T.astype(BF16)
    o_ref[W_QKV:W_QF, :] = w_ref[0:SRC_GATES_G, :].astype(BF16)
    o_ref[W_QF:W_SM, :] = w_ref[SRC_FOX:SRC_GATE_F, :].astype(BF16)
    n_gate = (SRC_FOX - SRC_GATES_G) + (SRC_END - SRC_GATE_F)
    gates = jnp.concatenate([w_ref[SRC_GATES_G:SRC_FOX, :], w_ref[SRC_GATE_F:SRC_END, :],
                             jnp.zeros((W_END - W_SM - n_gate, w_ref.shape[1]), F32)], axis=0)
    o_ref[W_SM:W_END, :] = gates.astype(BF16)


def _pack_w_call(w_t):
    return pl.pallas_call(
        _pack_w_kernel,
        out_shape=[jax.ShapeDtypeStruct((W_END, w_t.shape[1]), BF16),
                   jax.ShapeDtypeStruct((w_t.shape[1], CONV_DIM), BF16)],
        compiler_params=pltpu.CompilerParams(vmem_limit_bytes=VMEM_LIMIT),
        name="pack_w",
    )(w_t)


def _store_head_rows(ref, val, tm):
    for h in range(N_HEADS):
        ref[pl.ds(h, tm, stride=N_HEADS), :] = val[:, h * HEAD_DIM:(h + 1) * HEAD_DIM].astype(ref.dtype)


def _conv_silu_qkv(xbuf, cw_ref, g, rows):
    cols = slice(g * GROUP_W, (g + 1) * GROUP_W)
    x = xbuf[0:rows + SUBLANES, cols]
    y = x[SUBLANES:] * cw_ref[CONV_K - 1:CONV_K, cols]
    for j in range(CONV_K - 1):
        shifted = pltpu.roll(x, CONV_K - 1 - j, axis=0)
        y = y + shifted[SUBLANES:] * cw_ref[j:j + 1, cols]
    return y * _sigmoid(y)


def _l2_normalize(t, scale):
    return t * (lax.rsqrt(jnp.sum(t * t, axis=-1, keepdims=True) + L2_EPS) * scale)


def _proj_kernel(x_ref, nw_ref, w_ref, *refs, tm, sample, seq_tiles):
    x = x_ref[...]
    var = jnp.mean(x * x, axis=-1, keepdims=True)
    h = (x * lax.rsqrt(var + NORM_EPS) * nw_ref[...]).astype(BF16)
    seg = lambda lo, hi: lax.dot_general(h, w_ref[lo:hi, :], (((1,), (1,)), ((), ())),
                                         preferred_element_type=F32)
    if sample:
        qkv_ref, zg_ref, sm_ref, q4_ref, k4_ref, v4_ref, z4_ref = refs
        qkv_ref[...] = seg(W_QKV, W_ZG)
        zg_ref[...] = seg(W_ZG, W_QF)
        sm_ref[...] = seg(W_SM, W_END)
        _store_head_rows(q4_ref, seg(W_QF, W_KF), tm)
        _store_head_rows(k4_ref, seg(W_KF, W_VF), tm)
        _store_head_rows(v4_ref, seg(W_VF, W_ZF), tm)
        _store_head_rows(z4_ref, seg(W_ZF, W_SM), tm)
        return
    (fb_ref, h_ref, zg_ref, sm_ref, fcol_ref, logft_ref, qb_ref, kb_ref, vb_ref, zf_ref, k4_ref, v4_ref,
     carry_ref) = refs
    h_ref[...] = h

    @pl.when(pl.program_id(0) % seq_tiles == 0)
    def _():
        carry_ref[...] = jnp.zeros(carry_ref.shape, F32)

    sm = seg(W_SM, W_END)
    sm_ref[...] = sm
    zg_ref[...] = seg(W_ZG, W_QF)
    qb_ref[...] = (seg(W_QF, W_KF) * (HEAD_DIM ** -0.5)).astype(BF16)
    kf = seg(W_KF, W_VF)
    _store_head_rows(k4_ref, kf, tm)
    kb_ref[...] = kf.astype(BF16)
    vf = seg(W_VF, W_ZF)
    _store_head_rows(v4_ref, vf, tm)
    vb_ref[...] = vf.astype(BF16)
    zf_ref[...] = seg(W_ZF, W_SM)

    tri = (_iota2((LANES, LANES), 0) >= _iota2((LANES, LANES), 1)).astype(BF16)
    blocks = [slice(i * LANES, (i + 1) * LANES) for i in range(tm // LANES)]
    lf = [_log_sigmoid(sm[blk] + fb_ref[...]) for blk in blocks]
    parts = [jnp.dot(tri, jnp.concatenate(_split3(t), axis=1), preferred_element_type=F32) for t in lf]
    within = [p[:, 0:LANES] + p[:, LANES:2 * LANES] + p[:, 2 * LANES:3 * LANES] for p in parts]
    carry = carry_ref[0:1, :]
    for i, blk in enumerate(blocks):
        fcol_ref[blk, :] = within[i] + carry
        carry = carry + within[i][LANES - 1:LANES, :]
        logft_ref[:, blk] = lf[i].T[SM_FORGET:SM_FORGET + N_HEADS, :]
    carry_ref[0:1, :] = carry


def _proj_call(x2d, norm_w, w_big, tm, sample, fb_row=None, seq_len=None):
    t, d = x2d.shape
    n = w_big.shape[0]
    wide = lambda w, dt: (jax.ShapeDtypeStruct((t, w), dt), pl.BlockSpec((tm, w), lambda i: (i, 0)))
    rows4 = (jax.ShapeDtypeStruct((t * N_HEADS, HEAD_DIM), F32),
             pl.BlockSpec((tm * N_HEADS, HEAD_DIM), lambda i: (i, 0)))
    operands = [x2d, norm_w, w_big]
    in_specs = [pl.BlockSpec((tm, d), lambda i: (i, 0)),
                pl.BlockSpec((1, d), lambda i: (0, 0)),
                pl.BlockSpec((n, d), lambda i: (0, 0))]
    scratch = []
    seq_tiles = 1
    if sample:
        outs = [wide(CONV_DIM, F32), wide(GROUP_W, F32), wide(LANES, F32), rows4, rows4, rows4, rows4]
    else:
        seq_tiles = seq_len // tm
        logft = (jax.ShapeDtypeStruct((t // seq_len, N_HEADS, seq_len), F32),
                 pl.BlockSpec((None, N_HEADS, tm), lambda i: (i // seq_tiles, 0, i % seq_tiles)))
        outs = [wide(d, BF16), wide(GROUP_W, F32), wide(LANES, F32), wide(LANES, F32), logft,
                wide(GROUP_W, BF16), wide(GROUP_W, BF16), wide(GROUP_W, BF16), wide(GROUP_W, F32), rows4, rows4]
        operands.append(fb_row)
        in_specs.append(pl.BlockSpec((1, LANES), lambda i: (0, 0)))
        scratch = [pltpu.VMEM((SUBLANES, LANES), F32)]
    out_shape = [o[0] for o in outs]
    out_specs = [o[1] for o in outs]
    return pl.pallas_call(
        functools.partial(_proj_kernel, tm=tm, sample=sample, seq_tiles=seq_tiles),
        grid=(t // tm,),
        in_specs=in_specs,
        out_specs=out_specs,
        out_shape=out_shape,
        scratch_shapes=scratch,
        compiler_params=pltpu.CompilerParams(dimension_semantics=("arbitrary",),
                                             vmem_limit_bytes=VMEM_LIMIT),
        name="proj",
    )(*operands)


def _gdn_stages(*refs, c, l_valid, nb, project, chunk_of_step):
    if project:
        (h_ref, wqkv_ref, zg_ref, sm_ref, cw_ref, alog_ref, dtb_ref, onw_ref,
         og_ref, sout_ref, tail_ref, xbuf, s_scr) = refs
    else:
        (qkv_ref, zg_ref, sm_ref, cw_ref, alog_ref, dtb_ref, onw_ref, s0_ref, c0_ref,
         og_ref, sout_ref, xbuf, s_scr) = refs
    ci, n_c = chunk_of_step

    @pl.when(ci == 0)
    def _():
        if project:
            xbuf[:, 0:SUBLANES, :] = jnp.zeros((nb, SUBLANES, CONV_DIM), F32)
            s_scr[...] = jnp.zeros(s_scr.shape, F32)
        else:
            xbuf[:, SUBLANES - (CONV_K - 1):SUBLANES, :] = c0_ref[...]
            s_scr[...] = s0_ref[...]

    row = _iota2((c, 1), 0) + ci * c
    valid = jnp.broadcast_to((row < l_valid).astype(F32), (c, LANES))
    tri_incl = (_iota2((c, c), 0) >= _iota2((c, c), 1))
    tri_strict = (_iota2((c, c), 0) > _iota2((c, c), 1))
    eye = (_iota2((c, c), 0) == _iota2((c, c), 1)).astype(F32)
    pad_rows = LANES - c
    sl = lambda base, h: slice(base + h * HEAD_DIM, base + (h + 1) * HEAD_DIM)

    if project:
        raw = jnp.dot(h_ref[...].reshape(nb * c, h_ref.shape[-1]), wqkv_ref[...],
                      preferred_element_type=F32)
    yield
    q, k, v, beta, gc, gc_row, gc_last = [], [], [], [], [], [], []
    for bb in range(nb):
        xbuf[bb, SUBLANES:SUBLANES + c, :] = raw[bb * c:(bb + 1) * c] if project else qkv_ref[bb]
        yq, yk, yv = (_conv_silu_qkv(xbuf.at[bb], cw_ref, g, c) for g in range(3))
        if project:
            tail_ref[bb] = xbuf[bb, c:c + SUBLANES, :]
        xbuf[bb, 0:SUBLANES, :] = xbuf[bb, c:c + SUBLANES, :]
        sm = sm_ref[bb]
        beta_t = _sigmoid(sm) * valid
        g_t = -jnp.exp(alog_ref[...]) * _softplus(sm + dtb_ref[...]) * valid
        gc_t = _fdot(tri_incl.astype(F32), g_t)
        gc_sq = jnp.concatenate([gc_t, jnp.zeros((pad_rows, LANES), F32)], axis=0) if pad_rows else gc_t
        gc_tr = gc_sq.T
        for h in range(N_HEADS):
            q.append(_l2_normalize(yq[:, sl(0, h)], HEAD_DIM ** -0.5))
            k.append(_l2_normalize(yk[:, sl(0, h)], 1.0) * valid)
            v.append(yv[:, sl(0, h)])
            beta.append(jnp.broadcast_to(beta_t[:, SM_BETA + h:SM_BETA + h + 1], (c, HEAD_DIM)))
            gc.append(jnp.broadcast_to(gc_t[:, SM_DECAY + h:SM_DECAY + h + 1], (c, HEAD_DIM)))
            gc_row.append(gc_tr[SM_DECAY + h:SM_DECAY + h + 1, 0:c])
            gc_last.append(jnp.broadcast_to(gc_t[c - 1:c, SM_DECAY + h:SM_DECAY + h + 1], (1, HEAD_DIM)))

    chains = range(nb * N_HEADS)
    decay = [jnp.where(tri_incl, jnp.exp(jnp.where(tri_incl, gc[i][:, 0:c] - gc_row[i], 0.0)), 0.0)
             for i in chains]
    kb = [k[i] * beta[i] for i in chains]
    kkqk = [_bdot_nt(jnp.concatenate([kb[i], q[i]], axis=0), k[i]) for i in chains]
    qk = [kkqk[i][c:2 * c] * decay[i] for i in chains]
    yield
    neg_a = [-jnp.where(tri_strict, kkqk[i][0:c] * decay[i], 0.0) for i in chains]
    base = min(INV_BASE, c)
    blk_r, blk_c = _iota2((c, c), 0), _iota2((c, c), 1)
    same = lambda size: (blk_r // size) == (blk_c // size)
    diag = [jnp.where(same(base), neg_a[i], 0.0) for i in chains] if base < c else neg_a
    t_inv = [eye + diag[i] for i in chains]
    pw = [_bdot(diag[i], diag[i]) for i in chains]
    yield
    n_sq = int(math.log2(base))
    for j in range(1, n_sq):
        if j < n_sq - 1:
            both = [_bdot(jnp.concatenate([t_inv[i], pw[i]], axis=0), pw[i]) for i in chains]
            t_inv = [t_inv[i] + both[i][0:c] for i in chains]
            pw = [both[i][c:2 * c] for i in chains]
        else:
            t_inv = [t_inv[i] + _bdot(t_inv[i], pw[i]) for i in chains]
        yield
    size = base
    while size < c:
        off = [jnp.where(same(2 * size) & ~same(size), neg_a[i], 0.0) for i in chains]
        right = [_bdot(off[i], t_inv[i]) for i in chains]
        yield
        t_inv = [t_inv[i] + _bdot(t_inv[i], right[i]) for i in chains]
        yield
        size *= 2
    egc = [jnp.exp(gc[i]) for i in chains]
    sol = [_bdot(t_inv[i], jnp.concatenate([v[i] * beta[i], kb[i] * egc[i]], axis=-1)) for i in chains]
    yield
    s = [s_scr[i // N_HEADS, i % N_HEADS] for i in chains]
    ws = [_bdot(jnp.concatenate([sol[i][:, HEAD_DIM:2 * HEAD_DIM], q[i] * egc[i]], axis=0), s[i])
          for i in chains]
    yield
    v_new = [sol[i][:, 0:HEAD_DIM] - ws[i][0:c] for i in chains]
    o = [ws[i][c:2 * c] + _bdot(qk[i], v_new[i]) for i in chains]
    k_dec = [k[i] * jnp.exp(gc_last[i] - gc[i]) for i in chains]
    s_new = [s[i] * jnp.exp(gc_last[i]) + _bdot_tn(k_dec[i], v_new[i]) for i in chains]
    yield
    for i in chains:
        bb, h = i // N_HEADS, i % N_HEADS
        s_scr[bb, h] = s_new[i]
        oh = o[i] * lax.rsqrt(jnp.mean(o[i] * o[i], axis=-1, keepdims=True) + NORM_EPS) * onw_ref[...]
        z = zg_ref[bb, :, sl(0, h)]
        og_ref[bb, :, sl(0, h)] = (oh * (z * _sigmoid(z))).astype(og_ref.dtype)

    @pl.when(ci == n_c - 1)
    def _():
        sout_ref[...] = s_scr[...]


NEG_BIG = -1e30


def _forget_columns(f_tile, h, rows, for_keys):
    f = jnp.broadcast_to(f_tile[:, SM_FORGET + h:SM_FORGET + h + 1], (rows, LANES))
    f1, f2, f3 = (t.astype(F32) for t in _split3(-f if for_keys else f))
    lane = _iota2((rows, LANES), 1)
    base = 3 if for_keys else 0
    ones = ((lane >= 3 - base) & (lane < 6 - base)).astype(F32)
    cols = jnp.where(lane == base, f1, jnp.where(lane == base + 1, f2, jnp.where(lane == base + 2, f3, ones)))
    return cols.astype(BF16)


ROW_GROUP = 32


def _fox_prompt_kernel(q_ref, k_ref, v_ref, fcol_ref, zf_ref, og_ref, x_ref, wo_ref, fnw_ref, y_ref,
                       kx_ref, qa_ref, s_ref, p_ref, acc_ref, m_ref, a_ref, *, tq, l):
    qi = pl.program_id(1)
    heads = range(N_HEADS)
    sl = lambda h: slice(h * HEAD_DIM, (h + 1) * HEAD_DIM)
    nt = (((1,), (1,)), ((), ()))

    @pl.when(qi == 0)
    def _():
        for r in range(l // tq):
            for h in heads:
                kx_ref[r * tq:(r + 1) * tq, sl(h)] = _forget_columns(fcol_ref[r * tq:(r + 1) * tq, :], h, tq, True)

    f_q = fcol_ref[pl.ds(pl.multiple_of(qi * tq, tq), tq), :]
    for h in heads:
        qa_ref[h, :, 0:HEAD_DIM] = q_ref[:, sl(h)]
        qa_ref[h, :, HEAD_DIM:2 * HEAD_DIM] = _forget_columns(f_q, h, tq, False)
    acc_ref[...] = jnp.zeros(acc_ref.shape, F32)
    m_ref[...] = jnp.full(m_ref.shape, NEG_BIG, F32)
    ones = jnp.ones((tq, HEAD_DIM), BF16)

    half = tq // 2

    def block(ki, masked):
        start = pl.multiple_of(ki * tq, tq)
        pieces = [(slice(0, half), half), (slice(half, tq), tq)] if masked else [(slice(0, tq), tq)]
        for h in heads:
            for qr, nk in pieces:
                keys = pl.ds(start, nk)
                ka = jnp.concatenate([k_ref[keys, sl(h)], kx_ref[keys, sl(h)]], axis=1)
                s_ref[h, qr, 0:nk] = lax.dot_general(qa_ref[h, qr, :], ka, nt, preferred_element_type=F32)
        for h in heads:
            for r in range(0, tq, ROW_GROUP):
                rg = slice(r, r + ROW_GROUP)
                nk = half if (masked and r < half) else tq
                s = s_ref[h, rg, 0:nk]
                if masked:
                    keep = _iota2((ROW_GROUP, nk), 1) <= _iota2((ROW_GROUP, nk), 0) + r
                    s = jnp.where(keep, s, NEG_BIG)
                m_old = m_ref[h, rg, :]
                m_new = jnp.maximum(m_old, jnp.max(s, axis=-1, keepdims=True))
                a_ref[h, rg, :] = jnp.exp(m_old - m_new)
                m_ref[h, rg, :] = m_new
                p_ref[h, rg, 0:nk] = jnp.exp(s - jnp.concatenate([m_new] * (nk // LANES), axis=1)).astype(BF16)
        for h in heads:
            alpha = a_ref[h]
            for qr, nk in pieces:
                keys = pl.ds(start, nk)
                pv = jnp.dot(p_ref[h, qr, 0:nk], jnp.concatenate([v_ref[keys, sl(h)], ones[0:nk]], axis=1),
                             preferred_element_type=F32)
                acc_ref[h, qr, :] = acc_ref[h, qr, :] * jnp.concatenate([alpha[qr], alpha[qr]], axis=1) + pv

    def body(ki, carry):
        block(ki, False)
        return carry

    lax.fori_loop(0, qi, body, 0)
    block(qi, True)
    gated = []
    for h in heads:
        z = zf_ref[:, sl(h)]
        o = acc_ref[h, :, 0:HEAD_DIM] / acc_ref[h, :, HEAD_DIM:2 * HEAD_DIM]
        gated.append((o * (z * _sigmoid(z))).astype(BF16))
    mixed = jnp.dot(og_ref[...], wo_ref[0:GROUP_W, :], preferred_element_type=F32)
    mixed = mixed + jnp.dot(jnp.concatenate(gated, axis=1), wo_ref[GROUP_W:2 * GROUP_W, :],
                            preferred_element_type=F32)
    y = x_ref[...] + mixed
    var = jnp.mean(y * y, axis=-1, keepdims=True)
    y_ref[...] = y * lax.rsqrt(var + NORM_EPS) * fnw_ref[...]


def _fox_prompt_call(qf, kf, vf, fcol, zf, og, x, w_out, fnw, tq):
    b, l, _ = qf.shape
    d = x.shape[-1]
    kern = functools.partial(_fox_prompt_kernel, tq=tq, l=l)
    qblk = lambda w: pl.BlockSpec((None, tq, w), lambda bi, qi: (bi, qi, 0))
    seq = lambda w: pl.BlockSpec((None, l, w), lambda bi, qi: (bi, 0, 0))
    const = lambda shape: pl.BlockSpec(shape, lambda bi, qi: (0,) * len(shape))
    return pl.pallas_call(
        kern,
        grid=(b, l // tq),
        in_specs=[qblk(GROUP_W), seq(GROUP_W), seq(GROUP_W), seq(LANES), qblk(GROUP_W),
                  qblk(GROUP_W), qblk(d), const((2 * GROUP_W, d)), const((1, d))],
        out_specs=qblk(d),
        out_shape=jax.ShapeDtypeStruct((b, l, d), F32),
        scratch_shapes=[pltpu.VMEM((l, GROUP_W), BF16),
                        pltpu.VMEM((N_HEADS, tq, 2 * HEAD_DIM), BF16),
                        pltpu.VMEM((N_HEADS, tq, tq), F32),
                        pltpu.VMEM((N_HEADS, tq, tq), BF16),
                        pltpu.VMEM((N_HEADS, tq, 2 * HEAD_DIM), F32),
                        pltpu.VMEM((N_HEADS, tq, LANES), F32),
                        pltpu.VMEM((N_HEADS, tq, LANES), F32)],
        compiler_params=pltpu.CompilerParams(dimension_semantics=("arbitrary", "arbitrary"),
                                             vmem_limit_bytes=VMEM_LIMIT),
        name="fox_prompt",
    )(qf, kf, vf, fcol, zf, og, x, w_out, fnw)


def _out_kernel(og_ref, of_ref, x_ref, w_ref, fnw_ref, y_ref):
    o = jnp.dot(og_ref[...], w_ref[0:GROUP_W, :], preferred_element_type=F32)
    o = o + jnp.dot(of_ref[...], w_ref[GROUP_W:2 * GROUP_W, :], preferred_element_type=F32)
    y = x_ref[...] + o
    var = jnp.mean(y * y, axis=-1, keepdims=True)
    y_ref[...] = y * lax.rsqrt(var + NORM_EPS) * fnw_ref[...]


def _out_call(og, of, x2d, w_out, fnw, tm):
    t, d = x2d.shape
    return pl.pallas_call(
        _out_kernel,
        grid=(t // tm,),
        in_specs=[pl.BlockSpec((tm, GROUP_W), lambda i: (i, 0)),
                  pl.BlockSpec((tm, GROUP_W), lambda i: (i, 0)),
                  pl.BlockSpec((tm, d), lambda i: (i, 0)),
                  pl.BlockSpec((2 * GROUP_W, d), lambda i: (0, 0)),
                  pl.BlockSpec((1, d), lambda i: (0, 0))],
        out_specs=pl.BlockSpec((tm, d), lambda i: (i, 0)),
        out_shape=jax.ShapeDtypeStruct((t, d), F32),
        compiler_params=pltpu.CompilerParams(dimension_semantics=("arbitrary",),
                                             vmem_limit_bytes=VMEM_LIMIT),
        name="out_proj",
    )(og, of, x2d, w_out, fnw)


def _page_copies(pt_ref, kc_ref, vc_ref, lc_ref, kbuf, vbuf, lbuf, sems, step, slot, n_pages, pg, rows):
    copies = []
    for r in range(rows):
        for p in range(n_pages):
            pid = pt_ref[step * rows + r, p]
            copies.append((pltpu.make_async_copy(kc_ref.at[pid], kbuf.at[slot, r, pl.ds(p * pg, pg)],
                                                 sems.at[0, slot]), 0))
            copies.append((pltpu.make_async_copy(vc_ref.at[pid], vbuf.at[slot, r, pl.ds(p * pg, pg)],
                                                 sems.at[1, slot]), 1))
            copies.append((pltpu.make_async_copy(lc_ref.at[pid], lbuf.at[slot, r, :, p, :], sems.at[2, slot]),
                           p % 2))
    return copies


def _split3(x):
    x1 = x.astype(BF16)
    r1 = x - x1.astype(F32)
    x2 = r1.astype(BF16)
    x3 = (r1 - x2.astype(F32)).astype(BF16)
    return x1, x2, x3


def _fox_decode_phases(pt_ref, q_ref, kn_ref, vn_ref, zf_ref, sm_ref, fbrow_ref, cums_ref,
                       kc_ref, vc_ref, lc_ref, o_ref, logf_ref, kbuf, vbuf, lbuf, sems,
                       *, n_pages, pg, l_new, rows):
    step = pl.program_id(0)
    n_steps = pl.num_programs(0)
    slot = step % 2
    copies = functools.partial(_page_copies, pt_ref, kc_ref, vc_ref, lc_ref, kbuf, vbuf, lbuf, sems,
                               n_pages=n_pages, pg=pg, rows=rows)
    nr = l_new * N_HEADS

    @pl.when(step == 0)
    def _():
        for cp, prio in copies(step=step, slot=slot):
            cp.start(priority=prio)

    @pl.when(step + 1 < n_steps)
    def _():
        for cp, prio in copies(step=step + 1, slot=1 - slot):
            cp.start(priority=prio)

    for cp, _ in copies(step=step, slot=slot):
        cp.wait()
    yield

    scale = HEAD_DIM ** -0.5
    earlier = (_iota2((n_pages, n_pages), 0) > _iota2((n_pages, n_pages), 1)).astype(F32)
    tok_valid = (_iota2((SUBLANES, 1), 0) < l_new).astype(F32)
    r_tok = _iota2((nr, SUBLANES), 0) // N_HEADS
    f_past, f_tot_row, csum, s_all, s_new = [], [], [], [], []
    for r in range(rows):
        qrows = slice(r * nr, (r + 1) * nr)
        res = jnp.zeros((3 * n_pages, 2 * pg), F32)
        for h in range(N_HEADS):
            res = res + jnp.dot(jnp.concatenate(_split3(lbuf[slot, r, h]), axis=0), cums_ref[h],
                                preferred_element_type=F32)
        res = res[0:n_pages] + res[n_pages:2 * n_pages] + res[2 * n_pages:3 * n_pages]
        within, tot = res[:, 0:pg], res[:, pg:2 * pg]
        carry = _fdot(earlier, tot)
        f_past.append(within + carry)
        f_tot_row.append(carry[n_pages - 1:n_pages, :] + tot[n_pages - 1:n_pages, :])
        lf_col = _log_sigmoid(sm_ref[r] + fbrow_ref[...]) * tok_valid
        logf_ref[r] = lf_col
        csum.append(_fdot((_iota2((nr, SUBLANES), 1) <= r_tok).astype(F32), lf_col))
        q = q_ref[qrows, :].astype(BF16)
        s_all.append(lax.dot_general(q, kbuf[slot, r].astype(BF16), (((1,), (1,)), ((), ())),
                                     preferred_element_type=F32))
        s_new.append(lax.dot_general(q, kn_ref[qrows, :].astype(BF16), (((1,), (1,)), ((), ())),
                                     preferred_element_type=F32))
    yield
    own_lane = _iota2((nr, LANES), 1) == SM_FORGET + _iota2((nr, LANES), 0) % N_HEADS
    eye = _iota2((nr, nr), 0) == _iota2((nr, nr), 1)
    same_head = (_iota2((nr, pg), 1) % N_HEADS) == (_iota2((nr, pg), 0) % N_HEADS)
    rr, cc = _iota2((nr, nr), 0), _iota2((nr, nr), 1)
    new_ok = (rr % N_HEADS == cc % N_HEADS) & (cc // N_HEADS <= rr // N_HEADS)
    for r in range(rows):
        qrows = slice(r * nr, (r + 1) * nr)
        fq_new = jnp.sum(jnp.where(own_lane, csum[r], 0.0), axis=-1, keepdims=True)
        f_tot_col = jnp.sum(jnp.where(eye, jnp.broadcast_to(f_tot_row[r][:, 0:nr], (nr, nr)), 0.0),
                            axis=-1, keepdims=True)
        fq = fq_new + f_tot_col
        fq_row = jnp.sum(jnp.where(eye, jnp.broadcast_to(fq, (nr, nr)), 0.0), axis=0, keepdims=True)
        sp = [jnp.where(same_head, s_all[r][:, p * pg:(p + 1) * pg] * scale + (fq - f_past[r][p:p + 1, :]),
                        NEG_BIG) for p in range(n_pages)]
        sn = jnp.where(new_ok, s_new[r] * scale + (fq - fq_row), NEG_BIG)
        m_el = sp[0]
        for p in range(1, n_pages):
            m_el = jnp.maximum(m_el, sp[p])
        m = jnp.maximum(jnp.max(m_el, axis=-1, keepdims=True), jnp.max(sn, axis=-1, keepdims=True))
        pp = [jnp.exp(t - m) for t in sp]
        p_new = jnp.exp(sn - m)
        l_el = pp[0]
        for p in range(1, n_pages):
            l_el = l_el + pp[p]
        l = jnp.sum(l_el, axis=-1, keepdims=True) + jnp.sum(p_new, axis=-1, keepdims=True)
        p_all = jnp.concatenate([t.astype(BF16) for t in pp], axis=-1)
        acc = jnp.dot(p_all, vbuf[slot, r].astype(BF16), preferred_element_type=F32)
        acc = acc + jnp.dot(p_new.astype(BF16), vn_ref[qrows, :].astype(BF16), preferred_element_type=F32)
        z = zf_ref[qrows, :]
        o_ref[qrows, :] = ((acc / l) * (z * _sigmoid(z))).astype(o_ref.dtype)


def _head_cumsum_matrix(page):
    t = jnp.arange(page)[None, :, None]
    j = jnp.arange(page * N_HEADS)[None, None, :]
    h = jnp.arange(N_HEADS)[:, None, None]
    own = (j % N_HEADS) == h
    c = own & (t <= j // N_HEADS)
    b = jnp.broadcast_to(own, c.shape)
    return jnp.concatenate([c, b], axis=2).astype(BF16)


DECODE_ROWS_PER_STEP = 2
N_DECODE_INPUTS, N_DECODE_OUTPUTS, N_DECODE_SCRATCH = 10, 2, 4
N_GDN_INPUTS, N_GDN_OUTPUTS, N_GDN_SCRATCH = 8, 3, 2
N_SAMPLE_GDN_INPUTS, N_SAMPLE_GDN_OUTPUTS = 4, 2


def _decode_gdn_kernel(pt_ref, *refs, decode_kw, gdn_kw, sample_kw, n_chunks):
    take = lambda n: (refs[:n], refs[n:])
    dec_in, refs = take(N_DECODE_INPUTS)
    gdn_in, refs = take(N_GDN_INPUTS)
    smp_in, refs = take(N_SAMPLE_GDN_INPUTS)
    dec_out, refs = take(N_DECODE_OUTPUTS)
    gdn_out, refs = take(N_GDN_OUTPUTS)
    smp_out, refs = take(N_SAMPLE_GDN_OUTPUTS)
    dec_scr, refs = take(N_DECODE_SCRATCH)
    gdn_scr, smp_scr = take(N_GDN_SCRATCH)
    decode = _fox_decode_phases(pt_ref, *dec_in, *dec_out, *dec_scr, **decode_kw)
    next(decode)
    prompt = _gdn_stages(*gdn_in, *gdn_out, *gdn_scr, **gdn_kw,
                         chunk_of_step=(pl.program_id(0) % n_chunks, n_chunks))
    qkv_s, zg_s, s0_s, c0_s = smp_in
    sm_s = dec_in[4]
    sample = _gdn_stages(qkv_s, zg_s, sm_s, *gdn_in[4:8], s0_s, c0_s, *smp_out, *smp_scr, **sample_kw,
                         chunk_of_step=(jnp.int32(0), 1))
    programs = [prompt, sample, decode]
    while programs:
        for prog in list(programs):
            if next(prog, StopIteration) is StopIteration:
                programs.remove(prog)


def _decode_gdn_call(page_table, q4, kn4, vn4, zf4, sm8, fb_row, kcache, vcache, lcache, l_new, gdn):
    b, n_pages = page_table.shape
    pg = kcache.shape[1]
    page = lcache.shape[2]
    nr = l_new * N_HEADS
    assert nr % (2 * SUBLANES) == 0 and l_new <= SUBLANES
    rps = DECODE_ROWS_PER_STEP
    assert b % rps == 0
    n_steps = b // rps
    decode_kw = dict(n_pages=n_pages, pg=pg, l_new=l_new, rows=rps)
    rows = pl.BlockSpec((rps * nr, HEAD_DIM), lambda i, pt: (i, 0))
    tok = pl.BlockSpec((rps, SUBLANES, LANES), lambda i, pt: (i, 0, 0))
    const = lambda shape: pl.BlockSpec(shape, lambda i, pt: (0,) * len(shape))
    any_spec = pl.BlockSpec(memory_space=pl.ANY)
    operands = [page_table, q4, kn4, vn4, zf4, sm8, fb_row, _head_cumsum_matrix(page), kcache, vcache, lcache]
    in_specs = [rows, rows, rows, rows, tok, const((1, LANES)), const((N_HEADS, page, 2 * pg)),
                any_spec, any_spec, any_spec]
    out_specs = [rows, tok]
    out_shape = [jax.ShapeDtypeStruct((b * nr, HEAD_DIM), BF16), jax.ShapeDtypeStruct((b, SUBLANES, LANES), F32)]
    scratch = [pltpu.VMEM((2, rps, n_pages * pg, HEAD_DIM), F32),
               pltpu.VMEM((2, rps, n_pages * pg, HEAD_DIM), F32),
               pltpu.VMEM((2, rps, N_HEADS, n_pages, page), F32),
               pltpu.SemaphoreType.DMA((3, 2))]
    h3, w_qkv, zg, sm, conv_w, alog_row, dtb_row, onw, c, (qkv_s8, zg_s8, s0_s, c0_s) = gdn
    bp, l, d = h3.shape
    n_c = l // c
    nb = bp * n_c // n_steps
    assert nb >= 1 and (bp // nb) * n_c == n_steps
    blk = lambda w: pl.BlockSpec((nb, c, w), lambda i, pt: (i // n_c, i % n_c, 0))
    state = pl.BlockSpec((nb, N_HEADS, HEAD_DIM, HEAD_DIM), lambda i, pt: (i // n_c, 0, 0, 0))
    rows8 = pl.BlockSpec((nb, SUBLANES, CONV_DIM), lambda i, pt: (i // n_c, 0, 0))
    srow = lambda w: pl.BlockSpec((rps, SUBLANES, w), lambda i, pt: (i, 0, 0))
    sstate = pl.BlockSpec((rps, N_HEADS, HEAD_DIM, HEAD_DIM), lambda i, pt: (i, 0, 0, 0))
    operands += [h3, w_qkv, zg, sm, conv_w, alog_row, dtb_row, onw, qkv_s8, zg_s8, s0_s, c0_s]
    in_specs += [blk(d), const((d, CONV_DIM)), blk(GROUP_W), blk(LANES), const((CONV_K, CONV_DIM)),
                 const((1, LANES)), const((1, LANES)), const((1, HEAD_DIM)),
                 srow(CONV_DIM), srow(GROUP_W), sstate,
                 pl.BlockSpec((rps, CONV_K - 1, CONV_DIM), lambda i, pt: (i, 0, 0))]
    out_specs += [blk(GROUP_W), state, rows8, srow(GROUP_W), sstate]
    out_shape += [jax.ShapeDtypeStruct((bp, l, GROUP_W), BF16),
                  jax.ShapeDtypeStruct((bp, N_HEADS, HEAD_DIM, HEAD_DIM), F32),
                  jax.ShapeDtypeStruct((bp, SUBLANES, CONV_DIM), F32),
                  jax.ShapeDtypeStruct((b, SUBLANES, GROUP_W), BF16),
                  jax.ShapeDtypeStruct((b, N_HEADS, HEAD_DIM, HEAD_DIM), F32)]
    scratch += [pltpu.VMEM((nb, c + SUBLANES, CONV_DIM), F32),
                pltpu.VMEM((nb, N_HEADS, HEAD_DIM, HEAD_DIM), F32),
                pltpu.VMEM((rps, 2 * SUBLANES, CONV_DIM), F32),
                pltpu.VMEM((rps, N_HEADS, HEAD_DIM, HEAD_DIM), F32)]
    kern = functools.partial(_decode_gdn_kernel, decode_kw=decode_kw, n_chunks=n_c,
                             gdn_kw=dict(c=c, l_valid=l, nb=nb, project=True),
                             sample_kw=dict(c=SUBLANES, l_valid=l_new, nb=rps, project=False))
    grid_spec = pltpu.PrefetchScalarGridSpec(num_scalar_prefetch=1, grid=(n_steps,), in_specs=in_specs,
                                             out_specs=out_specs, scratch_shapes=scratch)
    return pl.pallas_call(
        kern,
        grid_spec=grid_spec,
        out_shape=out_shape,
        compiler_params=pltpu.CompilerParams(dimension_semantics=("arbitrary",),
                                             vmem_limit_bytes=VMEM_LIMIT),
        name="decode_gdn",
    )(*operands)


def _gate_row(vals, offset):
    return jnp.zeros((1, LANES), F32).at[0, offset:offset + N_HEADS].set(vals.astype(F32))


def _pad_rows(t, rows):
    return jnp.pad(t, ((0, 0), (0, rows - t.shape[1]), (0, 0)))


def kernel(x_prompt, x_sample, cache_fox_k, cache_fox_v, cache_fox_logf, page_table, state_gdn_ssm,
           state_gdn_conv, w_in, gdn_conv_w, gdn_a_log, gdn_dt_bias, gdn_out_norm_w, fox_f_bias, w_out,
           norm_w, final_norm_w):
    bp, lp, d = x_prompt.shape
    bs, ls, _ = x_sample.shape
    depth = w_in.shape[0]
    assert depth == 1, "single-layer trunk"
    n_pool, page = cache_fox_k.shape[1], cache_fox_k.shape[2]

    w_big, w_qkv = _pack_w_call(w_in[0].T)
    w_o = w_out[0].astype(BF16)
    nw = norm_w[0].reshape(1, d)
    fnw = final_norm_w.reshape(1, d)
    conv_w = gdn_conv_w[0]
    alog_row = _gate_row(gdn_a_log[0], SM_DECAY)
    dtb_row = _gate_row(gdn_dt_bias[0], SM_DECAY)
    fb_row = _gate_row(fox_f_bias[0], SM_FORGET)
    onw = gdn_out_norm_w[0].reshape(1, HEAD_DIM)

    xp2 = x_prompt.reshape(bp * lp, d)
    hp, zg, sm, fcol, logf_t, qf, kf, vf, zf, k4, v4 = _proj_call(xp2, nw, w_big, tm=ROWS_PER_STEP_PROMPT,
                                                                  sample=False, fb_row=fb_row, seq_len=lp)
    r3 = lambda t: t.reshape(bp, lp, t.shape[-1])

    xs2 = x_sample.reshape(bs * ls, d)
    qkv_s, zg_s, sm_s, q4_s, k4_s, v4_s, z4_s = _proj_call(xs2, nw, w_big, tm=ROWS_PER_STEP_SAMPLE, sample=True)
    r3s = lambda t: t.reshape(bs, ls, t.shape[-1])
    p8 = lambda t: _pad_rows(r3s(t), SUBLANES)
    kcache = cache_fox_k[0].reshape(n_pool, page * N_HEADS, HEAD_DIM)
    vcache = cache_fox_v[0].reshape(n_pool, page * N_HEADS, HEAD_DIM)
    lcache = cache_fox_logf[0].transpose(0, 2, 1)
    of_s, logf_s, og_p, ssm_p, tail, og_s, ssm_s = _decode_gdn_call(
        page_table, q4_s, k4_s, v4_s, z4_s, p8(sm_s), fb_row, kcache, vcache, lcache, l_new=ls,
        gdn=(r3(hp), w_qkv, r3(zg), r3(sm), conv_w, alog_row, dtb_row, onw, GDN_CHUNK,
             (p8(qkv_s), p8(zg_s), state_gdn_ssm[0], state_gdn_conv[0])))
    og_s2 = og_s[:, :ls].reshape(bs * ls, GROUP_W)
    of_s2 = of_s.reshape(bs * ls, GROUP_W)
    y_s = _out_call(og_s2, of_s2, xs2, w_o, fnw, tm=ROWS_PER_STEP_SAMPLE)
    y_prompt = _fox_prompt_call(r3(qf), r3(kf), r3(vf), r3(fcol), r3(zf), og_p, x_prompt, w_o, fnw,
                                tq=ROWS_PER_STEP_PROMPT)

    k_prompt = k4.reshape(1, bp, lp, N_HEADS, HEAD_DIM)
    v_prompt = v4.reshape(1, bp, lp, N_HEADS, HEAD_DIM)
    logf_prompt = logf_t.transpose(0, 2, 1).reshape(1, bp, lp, N_HEADS)
    ssm_prompt = ssm_p.reshape(1, bp, N_HEADS, HEAD_DIM, HEAD_DIM)
    conv_prompt = tail[:, SUBLANES - (CONV_K - 1):, :].reshape(1, bp, CONV_K - 1, CONV_DIM)
    y_sample = y_s.reshape(bs, ls, d)
    k_sample = k4_s.reshape(1, bs, ls, N_HEADS, HEAD_DIM)
    v_sample = v4_s.reshape(1, bs, ls, N_HEADS, HEAD_DIM)
    logf_sample = logf_s[:, :ls, SM_FORGET:SM_FORGET + N_HEADS].reshape(1, bs, ls, N_HEADS)
    ssm_sample = ssm_s.reshape(1, bs, N_HEADS, HEAD_DIM, HEAD_DIM)
    if ls >= CONV_K - 1:
        conv_sample = r3s(qkv_s)[:, ls - (CONV_K - 1):, :]
    else:
        conv_sample = jnp.concatenate([state_gdn_conv[0], r3s(qkv_s)], axis=1)[:, -(CONV_K - 1):, :]
    conv_sample = conv_sample.reshape(1, bs, CONV_K - 1, CONV_DIM)

    return (y_prompt, y_sample, k_prompt, v_prompt, logf_prompt, ssm_prompt, conv_prompt,
            k_sample, v_sample, logf_sample, ssm_sample, conv_sample)
```

```python
import functools
import math

import jax
import jax.numpy as jnp
from jax import lax
from jax.experimental import pallas as pl
from jax.experimental.pallas import tpu as pltpu

F32 = jnp.float32
BF16 = jnp.bfloat16

NORM_EPS = 1e-6
L2_EPS = 1e-6
HEAD_DIM = 128
N_HEADS = 4
GROUP_W = N_HEADS * HEAD_DIM
CONV_DIM = 3 * GROUP_W
CONV_K = 4
LANES = 128
SUBLANES = 8
GDN_CHUNK = 64
INV_BASE = 32
ROWS_PER_STEP_PROMPT = 512
ROWS_PER_STEP_SAMPLE = 256
SM_BETA = 0
SM_DECAY = 4
SM_FORGET = 8
VMEM_LIMIT = 56 * 1024 * 1024


def _sigmoid(x):
    return 1.0 / (1.0 + jnp.exp(-x))


def _softplus(x):
    return jnp.maximum(x, 0.0) + jnp.log(1.0 + jnp.exp(-jnp.abs(x)))


def _log_sigmoid(x):
    return -_softplus(-x)


def _bdot(a, b):
    return jnp.dot(a.astype(BF16), b.astype(BF16), preferred_element_type=F32)


def _bdot_nt(a, b):
    return lax.dot_general(a.astype(BF16), b.astype(BF16), (((1,), (1,)), ((), ())),
                           preferred_element_type=F32)


def _bdot_tn(a, b):
    return lax.dot_general(a.astype(BF16), b.astype(BF16), (((0,), (0,)), ((), ())),
                           preferred_element_type=F32)


def _fdot(a, b):
    return jnp.dot(a, b, preferred_element_type=F32, precision=lax.Precision.HIGHEST)


def _iota2(shape, dim):
    return lax.broadcasted_iota(jnp.int32, shape, dim)


W_QKV, W_ZG, W_QF, W_KF, W_VF, W_ZF, W_SM, W_END = 0, 1536, 2048, 2560, 3072, 3584, 4096, 4224
SRC_GATES_G, SRC_FOX, SRC_GATE_F, SRC_END = 2048, 2056, 4104, 4108


def _pack_w_kernel(w_ref, o_ref, qkv_ref):
    qkv_ref[...] = w_ref[W_QKV:W_ZG, :].T.astype(BF16)
    o_ref[W_QKV:W_QF, :] = w_ref[0:SRC_GATES_G, :].astype(BF16)
    o_ref[W_QF:W_SM, :] = w_ref[SRC_FOX:SRC_GATE_F, :].astype(BF16)
    n_gate = (SRC_FOX - SRC_GATES_G) + (SRC_END - SRC_GATE_F)
    gates = jnp.concatenate([w_ref[SRC_GATES_G:SRC_FOX, :], w_ref[SRC_GATE_F:SRC_END, :],
                             jnp.zeros((W_END - W_SM - n_gate, w_ref.shape[1]), F32)], axis=0)
    o_ref[W_SM:W_END, :] = gates.astype(BF16)


def _pack_w_call(w_t):
    return pl.pallas_call(
        _pack_w_kernel,
        out_shape=[jax.ShapeDtypeStruct((W_END, w_t.shape[1]), BF16),
                   jax.ShapeDtypeStruct((w_t.shape[1], CONV_DIM), BF16)],
        compiler_params=pltpu.CompilerParams(vmem_limit_bytes=VMEM_LIMIT),
        name="pack_w",
    )(w_t)


def _store_head_rows(ref, val, tm):
    for h in range(N_HEADS):
        ref[pl.ds(h, tm, stride=N_HEADS), :] = val[:, h * HEAD_DIM:(h + 1) * HEAD_DIM].astype(ref.dtype)


def _conv_silu_qkv(xbuf, cw_ref, g, rows):
    cols = slice(g * GROUP_W, (g + 1) * GROUP_W)
    x = xbuf[0:rows + SUBLANES, cols]
    y = x[SUBLANES:] * cw_ref[CONV_K - 1:CONV_K, cols]
    for j in range(CONV_K - 1):
        shifted = pltpu.roll(x, CONV_K - 1 - j, axis=0)
        y = y + shifted[SUBLANES:] * cw_ref[j:j + 1, cols]
    return y * _sigmoid(y)


def _l2_normalize(t, scale):
    return t * (lax.rsqrt(jnp.sum(t * t, axis=-1, keepdims=True) + L2_EPS) * scale)


def _proj_kernel(x_ref, nw_ref, w_ref, *refs, tm, sample, seq_tiles):
    x = x_ref[...]
    var = jnp.mean(x * x, axis=-1, keepdims=True)
    h = (x * lax.rsqrt(var + NORM_EPS) * nw_ref[...]).astype(BF16)
    seg = lambda lo, hi: lax.dot_general(h, w_ref[lo:hi, :], (((1,), (1,)), ((), ())),
                                         preferred_element_type=F32)
    if sample:
        qkv_ref, zg_ref, sm_ref, q4_ref, k4_ref, v4_ref, z4_ref = refs
        qkv_ref[...] = seg(W_QKV, W_ZG)
        zg_ref[...] = seg(W_ZG, W_QF)
        sm_ref[...] = seg(W_SM, W_END)
        _store_head_rows(q4_ref, seg(W_QF, W_KF), tm)
        _store_head_rows(k4_ref, seg(W_KF, W_VF), tm)
        _store_head_rows(v4_ref, seg(W_VF, W_ZF), tm)
        _store_head_rows(z4_ref, seg(W_ZF, W_SM), tm)
        return
    (fb_ref, h_ref, zg_ref, sm_ref, fcol_ref, logft_ref, qb_ref, kb_ref, vb_ref, zf_ref, k4_ref, v4_ref,
     carry_ref) = refs
    h_ref[...] = h

    @pl.when(pl.program_id(0) % seq_tiles == 0)
    def _():
        carry_ref[...] = jnp.zeros(carry_ref.shape, F32)

    sm = seg(W_SM, W_END)
    sm_ref[...] = sm
    zg_ref[...] = seg(W_ZG, W_QF)
    qb_ref[...] = (seg(W_QF, W_KF) * (HEAD_DIM ** -0.5)).astype(BF16)
    kf = seg(W_KF, W_VF)
    _store_head_rows(k4_ref, kf, tm)
    kb_ref[...] = kf.astype(BF16)
    vf = seg(W_VF, W_ZF)
    _store_head_rows(v4_ref, vf, tm)
    vb_ref[...] = vf.astype(BF16)
    zf_ref[...] = seg(W_ZF, W_SM)

    tri = (_iota2((LANES, LANES), 0) >= _iota2((LANES, LANES), 1)).astype(BF16)
    blocks = [slice(i * LANES, (i + 1) * LANES) for i in range(tm // LANES)]
    lf = [_log_sigmoid(sm[blk] + fb_ref[...]) for blk in blocks]
    parts = [jnp.dot(tri, jnp.concatenate(_split3(t), axis=1), preferred_element_type=F32) for t in lf]
    within = [p[:, 0:LANES] + p[:, LANES:2 * LANES] + p[:, 2 * LANES:3 * LANES] for p in parts]
    carry = carry_ref[0:1, :]
    for i, blk in enumerate(blocks):
        fcol_ref[blk, :] = within[i] + carry
        carry = carry + within[i][LANES - 1:LANES, :]
        logft_ref[:, blk] = lf[i].T[SM_FORGET:SM_FORGET + N_HEADS, :]
    carry_ref[0:1, :] = carry


def _proj_call(x2d, norm_w, w_big, tm, sample, fb_row=None, seq_len=None):
    t, d = x2d.shape
    n = w_big.shape[0]
    wide = lambda w, dt: (jax.ShapeDtypeStruct((t, w), dt), pl.BlockSpec((tm, w), lambda i: (i, 0)))
    rows4 = (jax.ShapeDtypeStruct((t * N_HEADS, HEAD_DIM), F32),
             pl.BlockSpec((tm * N_HEADS, HEAD_DIM), lambda i: (i, 0)))
    operands = [x2d, norm_w, w_big]
    in_specs = [pl.BlockSpec((tm, d), lambda i: (i, 0)),
                pl.BlockSpec((1, d), lambda i: (0, 0)),
                pl.BlockSpec((n, d), lambda i: (0, 0))]
    scratch = []
    seq_tiles = 1
    if sample:
        outs = [wide(CONV_DIM, F32), wide(GROUP_W, F32), wide(LANES, F32), rows4, rows4, rows4, rows4]
    else:
        seq_tiles = seq_len // tm
        logft = (jax.ShapeDtypeStruct((t // seq_len, N_HEADS, seq_len), F32),
                 pl.BlockSpec((None, N_HEADS, tm), lambda i: (i // seq_tiles, 0, i % seq_tiles)))
        outs = [wide(d, BF16), wide(GROUP_W, F32), wide(LANES, F32), wide(LANES, F32), logft,
                wide(GROUP_W, BF16), wide(GROUP_W, BF16), wide(GROUP_W, BF16), wide(GROUP_W, F32), rows4, rows4]
        operands.append(fb_row)
        in_specs.append(pl.BlockSpec((1, LANES), lambda i: (0, 0)))
        scratch = [pltpu.VMEM((SUBLANES, LANES), F32)]
    out_shape = [o[0] for o in outs]
    out_specs = [o[1] for o in outs]
    return pl.pallas_call(
        functools.partial(_proj_kernel, tm=tm, sample=sample, seq_tiles=seq_tiles),
        grid=(t // tm,),
        in_specs=in_specs,
        out_specs=out_specs,
        out_shape=out_shape,
        scratch_shapes=scratch,
        compiler_params=pltpu.CompilerParams(dimension_semantics=("arbitrary",),
                                             vmem_limit_bytes=VMEM_LIMIT),
        name="proj",
    )(*operands)


def _gdn_stages(*refs, c, l_valid, nb, project, chunk_of_step):
    if project:
        (h_ref, wqkv_ref, zg_ref, sm_ref, cw_ref, alog_ref, dtb_ref, onw_ref,
         og_ref, sout_ref, tail_ref, xbuf, s_scr) = refs
    else:
        (qkv_ref, zg_ref, sm_ref, cw_ref, alog_ref, dtb_ref, onw_ref, s0_ref, c0_ref,
         og_ref, sout_ref, xbuf, s_scr) = refs
    ci, n_c = chunk_of_step

    @pl.when(ci == 0)
    def _():
        if project:
            xbuf[:, 0:SUBLANES, :] = jnp.zeros((nb, SUBLANES, CONV_DIM), F32)
            s_scr[...] = jnp.zeros(s_scr.shape, F32)
        else:
            xbuf[:, SUBLANES - (CONV_K - 1):SUBLANES, :] = c0_ref[...]
            s_scr[...] = s0_ref[...]

    row = _iota2((c, 1), 0) + ci * c
    valid = jnp.broadcast_to((row < l_valid).astype(F32), (c, LANES))
    tri_incl = (_iota2((c, c), 0) >= _iota2((c, c), 1))
    tri_strict = (_iota2((c, c), 0) > _iota2((c, c), 1))
    eye = (_iota2((c, c), 0) == _iota2((c, c), 1)).astype(F32)
    pad_rows = LANES - c
    sl = lambda base, h: slice(base + h * HEAD_DIM, base + (h + 1) * HEAD_DIM)

    if project:
        raw = jnp.dot(h_ref[...].reshape(nb * c, h_ref.shape[-1]), wqkv_ref[...],
                      preferred_element_type=F32)
    yield
    q, k, v, beta, gc, gc_row, gc_last = [], [], [], [], [], [], []
    own_rows = lambda t, bb: pltpu.roll(t, (c - bb * l_valid) % c, axis=0) if bb else t
    for bb in range(nb):
        if project:
            xbuf[bb, SUBLANES:SUBLANES + c, :] = raw[bb * c:(bb + 1) * c]
        else:
            xbuf[bb, SUBLANES:SUBLANES + c, :] = jnp.where(row < l_valid, own_rows(qkv_ref[...], bb), 0.0)
        yq, yk, yv = (_conv_silu_qkv(xbuf.at[bb], cw_ref, g, c) for g in range(3))
        if project:
            tail_ref[bb] = xbuf[bb, c:c + SUBLANES, :]
        xbuf[bb, 0:SUBLANES, :] = xbuf[bb, c:c + SUBLANES, :]
        sm = sm_ref[bb]
        beta_t = _sigmoid(sm) * valid
        g_t = -jnp.exp(alog_ref[...]) * _softplus(sm + dtb_ref[...]) * valid
        gc_t = _fdot(tri_incl.astype(F32), g_t)
        gc_sq = jnp.concatenate([gc_t, jnp.zeros((pad_rows, LANES), F32)], axis=0) if pad_rows else gc_t
        gc_tr = gc_sq.T
        for h in range(N_HEADS):
            q.append(_l2_normalize(yq[:, sl(0, h)], HEAD_DIM ** -0.5))
            k.append(_l2_normalize(yk[:, sl(0, h)], 1.0) * valid)
            v.append(yv[:, sl(0, h)])
            beta.append(jnp.broadcast_to(beta_t[:, SM_BETA + h:SM_BETA + h + 1], (c, HEAD_DIM)))
            gc.append(jnp.broadcast_to(gc_t[:, SM_DECAY + h:SM_DECAY + h + 1], (c, HEAD_DIM)))
            gc_row.append(gc_tr[SM_DECAY + h:SM_DECAY + h + 1, 0:c])
            gc_last.append(jnp.broadcast_to(gc_t[c - 1:c, SM_DECAY + h:SM_DECAY + h + 1], (1, HEAD_DIM)))

    chains = range(nb * N_HEADS)
    decay = [jnp.where(tri_incl, jnp.exp(jnp.where(tri_incl, gc[i][:, 0:c] - gc_row[i], 0.0)), 0.0)
             for i in chains]
    kb = [k[i] * beta[i] for i in chains]
    kkqk = [_bdot_nt(jnp.concatenate([kb[i], q[i]], axis=0), k[i]) for i in chains]
    qk = [kkqk[i][c:2 * c] * decay[i] for i in chains]
    yield
    neg_a = [-jnp.where(tri_strict, kkqk[i][0:c] * decay[i], 0.0) for i in chains]
    base = min(INV_BASE, c)
    blk_r, blk_c = _iota2((c, c), 0), _iota2((c, c), 1)
    same = lambda size: (blk_r // size) == (blk_c // size)
    diag = [jnp.where(same(base), neg_a[i], 0.0) for i in chains] if base < c else neg_a
    t_inv = [eye + diag[i] for i in chains]
    pw = [_bdot(diag[i], diag[i]) for i in chains]
    yield
    n_sq = int(math.log2(base))
    for j in range(1, n_sq):
        if j < n_sq - 1:
            both = [_bdot(jnp.concatenate([t_inv[i], pw[i]], axis=0), pw[i]) for i in chains]
            t_inv = [t_inv[i] + both[i][0:c] for i in chains]
            pw = [both[i][c:2 * c] for i in chains]
        else:
            t_inv = [t_inv[i] + _bdot(t_inv[i], pw[i]) for i in chains]
        yield
    size = base
    while size < c:
        off = [jnp.where(same(2 * size) & ~same(size), neg_a[i], 0.0) for i in chains]
        right = [_bdot(off[i], t_inv[i]) for i in chains]
        yield
        t_inv = [t_inv[i] + _bdot(t_inv[i], right[i]) for i in chains]
        yield
        size *= 2
    egc = [jnp.exp(gc[i]) for i in chains]
    sol = [_bdot(t_inv[i], jnp.concatenate([v[i] * beta[i], kb[i] * egc[i]], axis=-1)) for i in chains]
    yield
    s = [s_scr[i // N_HEADS, i % N_HEADS] for i in chains]
    ws = [_bdot(jnp.concatenate([sol[i][:, HEAD_DIM:2 * HEAD_DIM], q[i] * egc[i]], axis=0), s[i])
          for i in chains]
    yield
    v_new = [sol[i][:, 0:HEAD_DIM] - ws[i][0:c] for i in chains]
    o = [ws[i][c:2 * c] + _bdot(qk[i], v_new[i]) for i in chains]
    k_dec = [k[i] * jnp.exp(gc_last[i] - gc[i]) for i in chains]
    s_new = [s[i] * jnp.exp(gc_last[i]) + _bdot_tn(k_dec[i], v_new[i]) for i in chains]
    yield
    for i in chains:
        bb, h = i // N_HEADS, i % N_HEADS
        s_scr[bb, h] = s_new[i]
        oh = o[i] * lax.rsqrt(jnp.mean(o[i] * o[i], axis=-1, keepdims=True) + NORM_EPS) * onw_ref[...]
        z = zg_ref[bb, :, sl(0, h)] if project else own_rows(zg_ref[:, sl(0, h)], bb)
        og_ref[bb, :, sl(0, h)] = (oh * (z * _sigmoid(z))).astype(og_ref.dtype)

    @pl.when(ci == n_c - 1)
    def _():
        sout_ref[...] = s_scr[...]


NEG_BIG = -1e30


def _forget_columns(f_tile, h, rows, for_keys):
    f = jnp.broadcast_to(f_tile[:, SM_FORGET + h:SM_FORGET + h + 1], (rows, LANES))
    f1, f2, f3 = (t.astype(F32) for t in _split3(-f if for_keys else f))
    lane = _iota2((rows, LANES), 1)
    base = 3 if for_keys else 0
    ones = ((lane >= 3 - base) & (lane < 6 - base)).astype(F32)
    cols = jnp.where(lane == base, f1, jnp.where(lane == base + 1, f2, jnp.where(lane == base + 2, f3, ones)))
    return cols.astype(BF16)


ROW_GROUP = 32


def _fox_prompt_kernel(q_ref, k_ref, v_ref, fcol_ref, zf_ref, og_ref, x_ref, wo_ref, fnw_ref, y_ref,
                       kx_ref, qa_ref, s_ref, p_ref, acc_ref, m_ref, a_ref, *, tq, l):
    qi = pl.program_id(1)
    heads = range(N_HEADS)
    sl = lambda h: slice(h * HEAD_DIM, (h + 1) * HEAD_DIM)
    nt = (((1,), (1,)), ((), ()))

    @pl.when(qi == 0)
    def _():
        for r in range(l // tq):
            for h in heads:
                kx_ref[r * tq:(r + 1) * tq, sl(h)] = _forget_columns(fcol_ref[r * tq:(r + 1) * tq, :], h, tq, True)

    f_q = fcol_ref[pl.ds(pl.multiple_of(qi * tq, tq), tq), :]
    for h in heads:
        qa_ref[h, :, 0:HEAD_DIM] = q_ref[:, sl(h)]
        qa_ref[h, :, HEAD_DIM:2 * HEAD_DIM] = _forget_columns(f_q, h, tq, False)
    acc_ref[...] = jnp.zeros(acc_ref.shape, F32)
    m_ref[...] = jnp.full(m_ref.shape, NEG_BIG, F32)
    ones = jnp.ones((tq, HEAD_DIM), BF16)

    half = tq // 2

    def block(ki, masked):
        start = pl.multiple_of(ki * tq, tq)
        pieces = [(slice(0, half), half), (slice(half, tq), tq)] if masked else [(slice(0, tq), tq)]
        for h in heads:
            for qr, nk in pieces:
                keys = pl.ds(start, nk)
                ka = jnp.concatenate([k_ref[keys, sl(h)], kx_ref[keys, sl(h)]], axis=1)
                s_ref[h, qr, 0:nk] = lax.dot_general(qa_ref[h, qr, :], ka, nt, preferred_element_type=F32)
        for h in heads:
            for r in range(0, tq, ROW_GROUP):
                rg = slice(r, r + ROW_GROUP)
                nk = half if (masked and r < half) else tq
                s = s_ref[h, rg, 0:nk]
                if masked:
                    keep = _iota2((ROW_GROUP, nk), 1) <= _iota2((ROW_GROUP, nk), 0) + r
                    s = jnp.where(keep, s, NEG_BIG)
                m_old = m_ref[h, rg, :]
                m_new = jnp.maximum(m_old, jnp.max(s, axis=-1, keepdims=True))
                a_ref[h, rg, :] = jnp.exp(m_old - m_new)
                m_ref[h, rg, :] = m_new
                p_ref[h, rg, 0:nk] = jnp.exp(s - jnp.concatenate([m_new] * (nk // LANES), axis=1)).astype(BF16)
        for h in heads:
            alpha = a_ref[h]
            for qr, nk in pieces:
                keys = pl.ds(start, nk)
                pv = jnp.dot(p_ref[h, qr, 0:nk], jnp.concatenate([v_ref[keys, sl(h)], ones[0:nk]], axis=1),
                             preferred_element_type=F32)
                acc_ref[h, qr, :] = acc_ref[h, qr, :] * jnp.concatenate([alpha[qr], alpha[qr]], axis=1) + pv

    def body(ki, carry):
        block(ki, False)
        return carry

    lax.fori_loop(0, qi, body, 0)
    block(qi, True)
    gated = []
    for h in heads:
        z = zf_ref[:, sl(h)]
        o = acc_ref[h, :, 0:HEAD_DIM] / acc_ref[h, :, HEAD_DIM:2 * HEAD_DIM]
        gated.append((o * (z * _sigmoid(z))).astype(BF16))
    mixed = jnp.dot(og_ref[...], wo_ref[0:GROUP_W, :], preferred_element_type=F32)
    mixed = mixed + jnp.dot(jnp.concatenate(gated, axis=1), wo_ref[GROUP_W:2 * GROUP_W, :],
                            preferred_element_type=F32)
    y = x_ref[...] + mixed
    var = jnp.mean(y * y, axis=-1, keepdims=True)
    y_ref[...] = y * lax.rsqrt(var + NORM_EPS) * fnw_ref[...]


def _fox_prompt_call(qf, kf, vf, fcol, zf, og, x, w_out, fnw, tq):
    b, l, _ = qf.shape
    d = x.shape[-1]
    kern = functools.partial(_fox_prompt_kernel, tq=tq, l=l)
    qblk = lambda w: pl.BlockSpec((None, tq, w), lambda bi, qi: (bi, qi, 0))
    seq = lambda w: pl.BlockSpec((None, l, w), lambda bi, qi: (bi, 0, 0))
    const = lambda shape: pl.BlockSpec(shape, lambda bi, qi: (0,) * len(shape))
    return pl.pallas_call(
        kern,
        grid=(b, l // tq),
        in_specs=[qblk(GROUP_W), seq(GROUP_W), seq(GROUP_W), seq(LANES), qblk(GROUP_W),
                  qblk(GROUP_W), qblk(d), const((2 * GROUP_W, d)), const((1, d))],
        out_specs=qblk(d),
        out_shape=jax.ShapeDtypeStruct((b, l, d), F32),
        scratch_shapes=[pltpu.VMEM((l, GROUP_W), BF16),
                        pltpu.VMEM((N_HEADS, tq, 2 * HEAD_DIM), BF16),
                        pltpu.VMEM((N_HEADS, tq, tq), F32),
                        pltpu.VMEM((N_HEADS, tq, tq), BF16),
                        pltpu.VMEM((N_HEADS, tq, 2 * HEAD_DIM), F32),
                        pltpu.VMEM((N_HEADS, tq, LANES), F32),
                        pltpu.VMEM((N_HEADS, tq, LANES), F32)],
        compiler_params=pltpu.CompilerParams(dimension_semantics=("arbitrary", "arbitrary"),
                                             vmem_limit_bytes=VMEM_LIMIT),
        name="fox_prompt",
    )(qf, kf, vf, fcol, zf, og, x, w_out, fnw)


def _out_kernel(og_ref, of_ref, x_ref, w_ref, fnw_ref, y_ref):
    o = jnp.dot(og_ref[...], w_ref[0:GROUP_W, :], preferred_element_type=F32)
    o = o + jnp.dot(of_ref[...], w_ref[GROUP_W:2 * GROUP_W, :], preferred_element_type=F32)
    y = x_ref[...] + o
    var = jnp.mean(y * y, axis=-1, keepdims=True)
    y_ref[...] = y * lax.rsqrt(var + NORM_EPS) * fnw_ref[...]


def _out_call(og, of, x2d, w_out, fnw, tm):
    t, d = x2d.shape
    return pl.pallas_call(
        _out_kernel,
        grid=(t // tm,),
        in_specs=[pl.BlockSpec((tm, GROUP_W), lambda i: (i, 0)),
                  pl.BlockSpec((tm, GROUP_W), lambda i: (i, 0)),
                  pl.BlockSpec((tm, d), lambda i: (i, 0)),
                  pl.BlockSpec((2 * GROUP_W, d), lambda i: (0, 0)),
                  pl.BlockSpec((1, d), lambda i: (0, 0))],
        out_specs=pl.BlockSpec((tm, d), lambda i: (i, 0)),
        out_shape=jax.ShapeDtypeStruct((t, d), F32),
        compiler_params=pltpu.CompilerParams(dimension_semantics=("arbitrary",),
                                             vmem_limit_bytes=VMEM_LIMIT),
        name="out_proj",
    )(og, of, x2d, w_out, fnw)


def _page_copies(pt_ref, kc_ref, vc_ref, lc_ref, kbuf, vbuf, lbuf, sems, step, slot, n_pages, pg, rows):
    copies = []
    for r in range(rows):
        for p in range(n_pages):
            pid = pt_ref[step * rows + r, p]
            copies.append((pltpu.make_async_copy(kc_ref.at[pid], kbuf.at[slot, r, pl.ds(p * pg, pg)],
                                                 sems.at[0, slot]), 0))
            copies.append((pltpu.make_async_copy(vc_ref.at[pid], vbuf.at[slot, r, pl.ds(p * pg, pg)],
                                                 sems.at[1, slot]), 1))
            copies.append((pltpu.make_async_copy(lc_ref.at[pid], lbuf.at[slot, r, :, p, :], sems.at[2, slot]),
                           p % 2))
    return copies


def _split3(x):
    x1 = x.astype(BF16)
    r1 = x - x1.astype(F32)
    x2 = r1.astype(BF16)
    x3 = (r1 - x2.astype(F32)).astype(BF16)
    return x1, x2, x3


def _fox_decode_phases(pt_ref, q_ref, kn_ref, vn_ref, zf_ref, sm_ref, fbrow_ref, cums_ref,
                       kc_ref, vc_ref, lc_ref, o_ref, logf_ref, kbuf, vbuf, lbuf, sems,
                       *, n_pages, pg, l_new, rows):
    step = pl.program_id(0)
    n_steps = pl.num_programs(0)
    slot = step % 2
    copies = functools.partial(_page_copies, pt_ref, kc_ref, vc_ref, lc_ref, kbuf, vbuf, lbuf, sems,
                               n_pages=n_pages, pg=pg, rows=rows)
    nr = l_new * N_HEADS

    @pl.when(step == 0)
    def _():
        for cp, prio in copies(step=step, slot=slot):
            cp.start(priority=prio)

    @pl.when(step + 1 < n_steps)
    def _():
        for cp, prio in copies(step=step + 1, slot=1 - slot):
            cp.start(priority=prio)

    for cp, _ in copies(step=step, slot=slot):
        cp.wait()
    yield

    scale = HEAD_DIM ** -0.5
    earlier = (_iota2((n_pages, n_pages), 0) > _iota2((n_pages, n_pages), 1)).astype(F32)
    tok_valid = (_iota2((SUBLANES, 1), 0) < l_new).astype(F32)
    r_tok = _iota2((nr, SUBLANES), 0) // N_HEADS
    f_past, f_tot_row, csum, s_all, s_new = [], [], [], [], []
    for r in range(rows):
        qrows = slice(r * nr, (r + 1) * nr)
        res = jnp.zeros((3 * n_pages, 2 * pg), F32)
        for h in range(N_HEADS):
            res = res + jnp.dot(jnp.concatenate(_split3(lbuf[slot, r, h]), axis=0), cums_ref[h],
                                preferred_element_type=F32)
        res = res[0:n_pages] + res[n_pages:2 * n_pages] + res[2 * n_pages:3 * n_pages]
        within, tot = res[:, 0:pg], res[:, pg:2 * pg]
        carry = _fdot(earlier, tot)
        f_past.append(within + carry)
        f_tot_row.append(carry[n_pages - 1:n_pages, :] + tot[n_pages - 1:n_pages, :])
        lf_col = _log_sigmoid(sm_ref[r] + fbrow_ref[...]) * tok_valid
        logf_ref[r] = lf_col
        csum.append(_fdot((_iota2((nr, SUBLANES), 1) <= r_tok).astype(F32), lf_col))
        q = q_ref[qrows, :].astype(BF16)
        s_all.append(lax.dot_general(q, kbuf[slot, r].astype(BF16), (((1,), (1,)), ((), ())),
                                     preferred_element_type=F32))
        s_new.append(lax.dot_general(q, kn_ref[qrows, :].astype(BF16), (((1,), (1,)), ((), ())),
                                     preferred_element_type=F32))
    yield
    own_lane = _iota2((nr, LANES), 1) == SM_FORGET + _iota2((nr, LANES), 0) % N_HEADS
    eye = _iota2((nr, nr), 0) == _iota2((nr, nr), 1)
    same_head = (_iota2((nr, pg), 1) % N_HEADS) == (_iota2((nr, pg), 0) % N_HEADS)
    rr, cc = _iota2((nr, nr), 0), _iota2((nr, nr), 1)
    new_ok = (rr % N_HEADS == cc % N_HEADS) & (cc // N_HEADS <= rr // N_HEADS)
    for r in range(rows):
        qrows = slice(r * nr, (r + 1) * nr)
        fq_new = jnp.sum(jnp.where(own_lane, csum[r], 0.0), axis=-1, keepdims=True)
        f_tot_col = jnp.sum(jnp.where(eye, jnp.broadcast_to(f_tot_row[r][:, 0:nr], (nr, nr)), 0.0),
                            axis=-1, keepdims=True)
        fq = fq_new + f_tot_col
        fq_row = jnp.sum(jnp.where(eye, jnp.broadcast_to(fq, (nr, nr)), 0.0), axis=0, keepdims=True)
        sp = [jnp.where(same_head, s_all[r][:, p * pg:(p + 1) * pg] * scale + (fq - f_past[r][p:p + 1, :]),
                        NEG_BIG) for p in range(n_pages)]
        sn = jnp.where(new_ok, s_new[r] * scale + (fq - fq_row), NEG_BIG)
        m_el = sp[0]
        for p in range(1, n_pages):
            m_el = jnp.maximum(m_el, sp[p])
        m = jnp.maximum(jnp.max(m_el, axis=-1, keepdims=True), jnp.max(sn, axis=-1, keepdims=True))
        pp = [jnp.exp(t - m) for t in sp]
        p_new = jnp.exp(sn - m)
        l_el = pp[0]
        for p in range(1, n_pages):
            l_el = l_el + pp[p]
        l = jnp.sum(l_el, axis=-1, keepdims=True) + jnp.sum(p_new, axis=-1, keepdims=True)
        p_all = jnp.concatenate([t.astype(BF16) for t in pp], axis=-1)
        acc = jnp.dot(p_all, vbuf[slot, r].astype(BF16), preferred_element_type=F32)
        acc = acc + jnp.dot(p_new.astype(BF16), vn_ref[qrows, :].astype(BF16), preferred_element_type=F32)
        z = zf_ref[qrows, :]
        o_ref[qrows, :] = ((acc / l) * (z * _sigmoid(z))).astype(o_ref.dtype)


def _head_cumsum_matrix(page):
    t = jnp.arange(page)[None, :, None]
    j = jnp.arange(page * N_HEADS)[None, None, :]
    h = jnp.arange(N_HEADS)[:, None, None]
    own = (j % N_HEADS) == h
    c = own & (t <= j // N_HEADS)
    b = jnp.broadcast_to(own, c.shape)
    return jnp.concatenate([c, b], axis=2).astype(BF16)


DECODE_ROWS_PER_STEP = 2
N_DECODE_INPUTS, N_DECODE_OUTPUTS, N_DECODE_SCRATCH = 10, 2, 4
N_GDN_INPUTS, N_GDN_OUTPUTS, N_GDN_SCRATCH = 8, 3, 2
N_SAMPLE_GDN_INPUTS, N_SAMPLE_GDN_OUTPUTS = 4, 2


def _decode_gdn_kernel(pt_ref, *refs, decode_kw, gdn_kw, sample_kw, n_chunks):
    take = lambda n: (refs[:n], refs[n:])
    dec_in, refs = take(N_DECODE_INPUTS)
    gdn_in, refs = take(N_GDN_INPUTS)
    smp_in, refs = take(N_SAMPLE_GDN_INPUTS)
    dec_out, refs = take(N_DECODE_OUTPUTS)
    gdn_out, refs = take(N_GDN_OUTPUTS)
    smp_out, refs = take(N_SAMPLE_GDN_OUTPUTS)
    dec_scr, refs = take(N_DECODE_SCRATCH)
    gdn_scr, smp_scr = take(N_GDN_SCRATCH)
    decode = _fox_decode_phases(pt_ref, *dec_in, *dec_out, *dec_scr, **decode_kw)
    next(decode)
    prompt = _gdn_stages(*gdn_in, *gdn_out, *gdn_scr, **gdn_kw,
                         chunk_of_step=(pl.program_id(0) % n_chunks, n_chunks))
    qkv_s, zg_s, s0_s, c0_s = smp_in
    sm_s = dec_in[4]
    sample = _gdn_stages(qkv_s, zg_s, sm_s, *gdn_in[4:8], s0_s, c0_s, *smp_out, *smp_scr, **sample_kw,
                         chunk_of_step=(jnp.int32(0), 1))
    programs = [prompt, sample, decode]
    while programs:
        for prog in list(programs):
            if next(prog, StopIteration) is StopIteration:
                programs.remove(prog)


def _decode_gdn_call(page_table, q4, kn4, vn4, zf4, sm8, fb_row, kcache, vcache, lcache, l_new, gdn):
    b, n_pages = page_table.shape
    pg = kcache.shape[1]
    page = lcache.shape[2]
    nr = l_new * N_HEADS
    assert nr % (2 * SUBLANES) == 0 and l_new <= SUBLANES
    rps = DECODE_ROWS_PER_STEP
    assert b % rps == 0
    n_steps = b // rps
    decode_kw = dict(n_pages=n_pages, pg=pg, l_new=l_new, rows=rps)
    rows = pl.BlockSpec((rps * nr, HEAD_DIM), lambda i, pt: (i, 0))
    tok = pl.BlockSpec((rps, SUBLANES, LANES), lambda i, pt: (i, 0, 0))
    const = lambda shape: pl.BlockSpec(shape, lambda i, pt: (0,) * len(shape))
    any_spec = pl.BlockSpec(memory_space=pl.ANY)
    operands = [page_table, q4, kn4, vn4, zf4, sm8, fb_row, _head_cumsum_matrix(page), kcache, vcache, lcache]
    in_specs = [rows, rows, rows, rows, tok, const((1, LANES)), const((N_HEADS, page, 2 * pg)),
                any_spec, any_spec, any_spec]
    out_specs = [rows, tok]
    out_shape = [jax.ShapeDtypeStruct((b * nr, HEAD_DIM), BF16), jax.ShapeDtypeStruct((b, SUBLANES, LANES), F32)]
    scratch = [pltpu.VMEM((2, rps, n_pages * pg, HEAD_DIM), F32),
               pltpu.VMEM((2, rps, n_pages * pg, HEAD_DIM), F32),
               pltpu.VMEM((2, rps, N_HEADS, n_pages, page), F32),
               pltpu.SemaphoreType.DMA((3, 2))]
    h3, w_qkv, zg, sm, conv_w, alog_row, dtb_row, onw, c, (qkv_s, zg_s, s0_s, c0_s) = gdn
    assert rps * l_new == SUBLANES
    bp, l, d = h3.shape
    n_c = l // c
    nb = bp * n_c // n_steps
    assert nb >= 1 and (bp // nb) * n_c == n_steps
    blk = lambda w: pl.BlockSpec((nb, c, w), lambda i, pt: (i // n_c, i % n_c, 0))
    state = pl.BlockSpec((nb, N_HEADS, HEAD_DIM, HEAD_DIM), lambda i, pt: (i // n_c, 0, 0, 0))
    rows8 = pl.BlockSpec((nb, SUBLANES, CONV_DIM), lambda i, pt: (i // n_c, 0, 0))
    srow = lambda w: pl.BlockSpec((rps, SUBLANES, w), lambda i, pt: (i, 0, 0))
    sstate = pl.BlockSpec((rps, N_HEADS, HEAD_DIM, HEAD_DIM), lambda i, pt: (i, 0, 0, 0))
    stok = lambda w: pl.BlockSpec((rps * l_new, w), lambda i, pt: (i, 0))
    operands += [h3, w_qkv, zg, sm, conv_w, alog_row, dtb_row, onw, qkv_s, zg_s, s0_s, c0_s]
    in_specs += [blk(d), const((d, CONV_DIM)), blk(GROUP_W), blk(LANES), const((CONV_K, CONV_DIM)),
                 const((1, LANES)), const((1, LANES)), const((1, HEAD_DIM)),
                 stok(CONV_DIM), stok(GROUP_W), sstate,
                 pl.BlockSpec((rps, CONV_K - 1, CONV_DIM), lambda i, pt: (i, 0, 0))]
    out_specs += [blk(GROUP_W), state, rows8, srow(GROUP_W), sstate]
    out_shape += [jax.ShapeDtypeStruct((bp, l, GROUP_W), BF16),
                  jax.ShapeDtypeStruct((bp, N_HEADS, HEAD_DIM, HEAD_DIM), F32),
                  jax.ShapeDtypeStruct((bp, SUBLANES, CONV_DIM), F32),
                  jax.ShapeDtypeStruct((b, SUBLANES, GROUP_W), BF16),
                  jax.ShapeDtypeStruct((b, N_HEADS, HEAD_DIM, HEAD_DIM), F32)]
    scratch += [pltpu.VMEM((nb, c + SUBLANES, CONV_DIM), F32),
                pltpu.VMEM((nb, N_HEADS, HEAD_DIM, HEAD_DIM), F32),
                pltpu.VMEM((rps, 2 * SUBLANES, CONV_DIM), F32),
                pltpu.VMEM((rps, N_HEADS, HEAD_DIM, HEAD_DIM), F32)]
    kern = functools.partial(_decode_gdn_kernel, decode_kw=decode_kw, n_chunks=n_c,
                             gdn_kw=dict(c=c, l_valid=l, nb=nb, project=True),
                             sample_kw=dict(c=SUBLANES, l_valid=l_new, nb=rps, project=False))
    grid_spec = pltpu.PrefetchScalarGridSpec(num_scalar_prefetch=1, grid=(n_steps,), in_specs=in_specs,
                                             out_specs=out_specs, scratch_shapes=scratch)
    return pl.pallas_call(
        kern,
        grid_spec=grid_spec,
        out_shape=out_shape,
        compiler_params=pltpu.CompilerParams(dimension_semantics=("arbitrary",),
                                             vmem_limit_bytes=VMEM_LIMIT),
        name="decode_gdn",
    )(*operands)


def _gate_row(vals, offset):
    return jnp.zeros((1, LANES), F32).at[0, offset:offset + N_HEADS].set(vals.astype(F32))


def _pad_rows(t, rows):
    return jnp.pad(t, ((0, 0), (0, rows - t.shape[1]), (0, 0)))


def kernel(x_prompt, x_sample, cache_fox_k, cache_fox_v, cache_fox_logf, page_table, state_gdn_ssm,
           state_gdn_conv, w_in, gdn_conv_w, gdn_a_log, gdn_dt_bias, gdn_out_norm_w, fox_f_bias, w_out,
           norm_w, final_norm_w):
    bp, lp, d = x_prompt.shape
    bs, ls, _ = x_sample.shape
    depth = w_in.shape[0]
    assert depth == 1, "single-layer trunk"
    n_pool, page = cache_fox_k.shape[1], cache_fox_k.shape[2]

    w_big, w_qkv = _pack_w_call(w_in[0].T)
    w_o = w_out[0].astype(BF16)
    nw = norm_w[0].reshape(1, d)
    fnw = final_norm_w.reshape(1, d)
    conv_w = gdn_conv_w[0]
    alog_row = _gate_row(gdn_a_log[0], SM_DECAY)
    dtb_row = _gate_row(gdn_dt_bias[0], SM_DECAY)
    fb_row = _gate_row(fox_f_bias[0], SM_FORGET)
    onw = gdn_out_norm_w[0].reshape(1, HEAD_DIM)

    xp2 = x_prompt.reshape(bp * lp, d)
    hp, zg, sm, fcol, logf_t, qf, kf, vf, zf, k4, v4 = _proj_call(xp2, nw, w_big, tm=ROWS_PER_STEP_PROMPT,
                                                                  sample=False, fb_row=fb_row, seq_len=lp)
    r3 = lambda t: t.reshape(bp, lp, t.shape[-1])

    xs2 = x_sample.reshape(bs * ls, d)
    qkv_s, zg_s, sm_s, q4_s, k4_s, v4_s, z4_s = _proj_call(xs2, nw, w_big, tm=ROWS_PER_STEP_SAMPLE, sample=True)
    r3s = lambda t: t.reshape(bs, ls, t.shape[-1])
    p8 = lambda t: _pad_rows(r3s(t), SUBLANES)
    kcache = cache_fox_k[0].reshape(n_pool, page * N_HEADS, HEAD_DIM)
    vcache = cache_fox_v[0].reshape(n_pool, page * N_HEADS, HEAD_DIM)
    lcache = cache_fox_logf[0].transpose(0, 2, 1)
    of_s, logf_s, og_p, ssm_p, tail, og_s, ssm_s = _decode_gdn_call(
        page_table, q4_s, k4_s, v4_s, z4_s, p8(sm_s), fb_row, kcache, vcache, lcache, l_new=ls,
        gdn=(r3(hp), w_qkv, r3(zg), r3(sm), conv_w, alog_row, dtb_row, onw, GDN_CHUNK,
             (qkv_s, zg_s, state_gdn_ssm[0], state_gdn_conv[0])))
    og_s2 = og_s[:, :ls].reshape(bs * ls, GROUP_W)
    of_s2 = of_s.reshape(bs * ls, GROUP_W)
    y_s = _out_call(og_s2, of_s2, xs2, w_o, fnw, tm=ROWS_PER_STEP_SAMPLE)
    y_prompt = _fox_prompt_call(r3(qf), r3(kf), r3(vf), r3(fcol), r3(zf), og_p, x_prompt, w_o, fnw,
                                tq=ROWS_PER_STEP_PROMPT)

    k_prompt = k4.reshape(1, bp, lp, N_HEADS, HEAD_DIM)
    v_prompt = v4.reshape(1, bp, lp, N_HEADS, HEAD_DIM)
    logf_prompt = logf_t.transpose(0, 2, 1).reshape(1, bp, lp, N_HEADS)
    ssm_prompt = ssm_p.reshape(1, bp, N_HEADS, HEAD_DIM, HEAD_DIM)
    conv_prompt = tail[:, SUBLANES - (CONV_K - 1):, :].reshape(1, bp, CONV_K - 1, CONV_DIM)
    y_sample = y_s.reshape(bs, ls, d)
    k_sample = k4_s.reshape(1, bs, ls, N_HEADS, HEAD_DIM)
    v_sample = v4_s.reshape(1, bs, ls, N_HEADS, HEAD_DIM)
    logf_sample = logf_s[:, :ls, SM_FORGET:SM_FORGET + N_HEADS].reshape(1, bs, ls, N_HEADS)
    ssm_sample = ssm_s.reshape(1, bs, N_HEADS, HEAD_DIM, HEAD_DIM)
    if ls >= CONV_K - 1:
        conv_sample = r3s(qkv_s)[:, ls - (CONV_K - 1):, :]
    else:
        conv_sample = jnp.concatenate([state_gdn_conv[0], r3s(qkv_s)], axis=1)[:, -(CONV_K - 1):, :]
    conv_sample = conv_sample.reshape(1, bs, CONV_K - 1, CONV_DIM)

    return (y_prompt, y_sample, k_prompt, v_prompt, logf_prompt, ssm_prompt, conv_prompt,
            k_sample, v_sample, logf_sample, ssm_sample, conv_sample)
```

```python
import functools
import math

import jax
import jax.numpy as jnp
from jax import lax
from jax.experimental import pallas as pl
from jax.experimental.pallas import tpu as pltpu

F32 = jnp.float32
BF16 = jnp.bfloat16

NORM_EPS = 1e-6
L2_EPS = 1e-6
HEAD_DIM = 128
N_HEADS = 4
GROUP_W = N_HEADS * HEAD_DIM
CONV_DIM = 3 * GROUP_W
CONV_K = 4
LANES = 128
SUBLANES = 8
GDN_CHUNK = 64
INV_BASE = 32
ROWS_PER_STEP_PROMPT = 512
ROWS_PER_STEP_SAMPLE = 256
SM_BETA = 0
SM_DECAY = 4
SM_FORGET = 8
VMEM_LIMIT = 56 * 1024 * 1024


def _sigmoid(x):
    return 1.0 / (1.0 + jnp.exp(-x))


def _softplus(x):
    return jnp.maximum(x, 0.0) + jnp.log(1.0 + jnp.exp(-jnp.abs(x)))


def _log_sigmoid(x):
    return -_softplus(-x)


def _bdot(a, b):
    return jnp.dot(a.astype(BF16), b.astype(BF16), preferred_element_type=F32)


def _bdot_nt(a, b):
    return lax.dot_general(a.astype(BF16), b.astype(BF16), (((1,), (1,)), ((), ())),
                           preferred_element_type=F32)


def _bdot_tn(a, b):
    return lax.dot_general(a.astype(BF16), b.astype(BF16), (((0,), (0,)), ((), ())),
                           preferred_element_type=F32)


def _fdot(a, b):
    return jnp.dot(a, b, preferred_element_type=F32, precision=lax.Precision.HIGHEST)


def _iota2(shape, dim):
    return lax.broadcasted_iota(jnp.int32, shape, dim)


W_QKV, W_ZG, W_QF, W_KF, W_VF, W_ZF, W_SM, W_END = 0, 1536, 2048, 2560, 3072, 3584, 4096, 4224
SRC_GATES_G, SRC_FOX, SRC_GATE_F, SRC_END = 2048, 2056, 4104, 4108


def _pack_w_kernel(w_ref, o_ref, qkv_ref):
    qkv_ref[...] = w_ref[W_QKV:W_ZG, :].T.astype(BF16)
    o_ref[W_QKV:W_QF, :] = w_ref[0:SRC_GATES_G, :].astype(BF16)
    o_ref[W_QF:W_SM, :] = w_ref[SRC_FOX:SRC_GATE_F, :].astype(BF16)
    n_gate = (SRC_FOX - SRC_GATES_G) + (SRC_END - SRC_GATE_F)
    gates = jnp.concatenate([w_ref[SRC_GATES_G:SRC_FOX, :], w_ref[SRC_GATE_F:SRC_END, :],
                             jnp.zeros((W_END - W_SM - n_gate, w_ref.shape[1]), F32)], axis=0)
    o_ref[W_SM:W_END, :] = gates.astype(BF16)


def _pack_w_call(w_t):
    return pl.pallas_call(
        _pack_w_kernel,
        out_shape=[jax.ShapeDtypeStruct((W_END, w_t.shape[1]), BF16),
                   jax.ShapeDtypeStruct((w_t.shape[1], CONV_DIM), BF16)],
        compiler_params=pltpu.CompilerParams(vmem_limit_bytes=VMEM_LIMIT),
        name="pack_w",
    )(w_t)


def _store_head_rows(ref, val, tm):
    for h in range(N_HEADS):
        ref[pl.ds(h, tm, stride=N_HEADS), :] = val[:, h * HEAD_DIM:(h + 1) * HEAD_DIM].astype(ref.dtype)


def _conv_silu_qkv(xbuf, cw_ref, g, rows):
    cols = slice(g * GROUP_W, (g + 1) * GROUP_W)
    x = xbuf[0:rows + SUBLANES, cols]
    y = x[SUBLANES:] * cw_ref[CONV_K - 1:CONV_K, cols]
    for j in range(CONV_K - 1):
        shifted = pltpu.roll(x, CONV_K - 1 - j, axis=0)
        y = y + shifted[SUBLANES:] * cw_ref[j:j + 1, cols]
    return y * _sigmoid(y)


def _l2_normalize(t, scale):
    return t * (lax.rsqrt(jnp.sum(t * t, axis=-1, keepdims=True) + L2_EPS) * scale)


def _proj_kernel(x_ref, nw_ref, w_ref, *refs, tm, sample, seq_tiles):
    x = x_ref[...]
    var = jnp.mean(x * x, axis=-1, keepdims=True)
    h = (x * lax.rsqrt(var + NORM_EPS) * nw_ref[...]).astype(BF16)
    seg = lambda lo, hi: lax.dot_general(h, w_ref[lo:hi, :], (((1,), (1,)), ((), ())),
                                         preferred_element_type=F32)
    if sample:
        qkv_ref, zg_ref, sm_ref, q4_ref, k4_ref, v4_ref, z4_ref = refs
        qkv_ref[...] = seg(W_QKV, W_ZG)
        zg_ref[...] = seg(W_ZG, W_QF)
        sm_ref[...] = seg(W_SM, W_END)
        _store_head_rows(q4_ref, seg(W_QF, W_KF), tm)
        _store_head_rows(k4_ref, seg(W_KF, W_VF), tm)
        _store_head_rows(v4_ref, seg(W_VF, W_ZF), tm)
        _store_head_rows(z4_ref, seg(W_ZF, W_SM), tm)
        return
    (fb_ref, h_ref, zg_ref, sm_ref, fcol_ref, logft_ref, qb_ref, kb_ref, vb_ref, zf_ref, k4_ref, v4_ref,
     carry_ref) = refs
    h_ref[...] = h

    @pl.when(pl.program_id(0) % seq_tiles == 0)
    def _():
        carry_ref[...] = jnp.zeros(carry_ref.shape, F32)

    sm = seg(W_SM, W_END)
    sm_ref[...] = sm
    zg_ref[...] = seg(W_ZG, W_QF)
    qb_ref[...] = (seg(W_QF, W_KF) * (HEAD_DIM ** -0.5)).astype(BF16)
    kf = seg(W_KF, W_VF)
    _store_head_rows(k4_ref, kf, tm)
    kb_ref[...] = kf.astype(BF16)
    vf = seg(W_VF, W_ZF)
    _store_head_rows(v4_ref, vf, tm)
    vb_ref[...] = vf.astype(BF16)
    zf_ref[...] = seg(W_ZF, W_SM)

    tri = (_iota2((LANES, LANES), 0) >= _iota2((LANES, LANES), 1)).astype(BF16)
    blocks = [slice(i * LANES, (i + 1) * LANES) for i in range(tm // LANES)]
    lf = [_log_sigmoid(sm[blk] + fb_ref[...]) for blk in blocks]
    parts = [jnp.dot(tri, jnp.concatenate(_split3(t), axis=1), preferred_element_type=F32) for t in lf]
    within = [p[:, 0:LANES] + p[:, LANES:2 * LANES] + p[:, 2 * LANES:3 * LANES] for p in parts]
    carry = carry_ref[0:1, :]
    for i, blk in enumerate(blocks):
        fcol_ref[blk, :] = within[i] + carry
        carry = carry + within[i][LANES - 1:LANES, :]
        logft_ref[:, blk] = lf[i].T[SM_FORGET:SM_FORGET + N_HEADS, :]
    carry_ref[0:1, :] = carry


def _proj_call(x2d, norm_w, w_big, tm, sample, fb_row=None, seq_len=None):
    t, d = x2d.shape
    n = w_big.shape[0]
    wide = lambda w, dt: (jax.ShapeDtypeStruct((t, w), dt), pl.BlockSpec((tm, w), lambda i: (i, 0)))
    rows4 = (jax.ShapeDtypeStruct((t * N_HEADS, HEAD_DIM), F32),
             pl.BlockSpec((tm * N_HEADS, HEAD_DIM), lambda i: (i, 0)))
    operands = [x2d, norm_w, w_big]
    in_specs = [pl.BlockSpec((tm, d), lambda i: (i, 0)),
                pl.BlockSpec((1, d), lambda i: (0, 0)),
                pl.BlockSpec((n, d), lambda i: (0, 0))]
    scratch = []
    seq_tiles = 1
    if sample:
        outs = [wide(CONV_DIM, F32), wide(GROUP_W, F32), wide(LANES, F32), rows4, rows4, rows4, rows4]
    else:
        seq_tiles = seq_len // tm
        logft = (jax.ShapeDtypeStruct((t // seq_len, N_HEADS, seq_len), F32),
                 pl.BlockSpec((None, N_HEADS, tm), lambda i: (i // seq_tiles, 0, i % seq_tiles)))
        outs = [wide(d, BF16), wide(GROUP_W, F32), wide(LANES, F32), wide(LANES, F32), logft,
                wide(GROUP_W, BF16), wide(GROUP_W, BF16), wide(GROUP_W, BF16), wide(GROUP_W, F32), rows4, rows4]
        operands.append(fb_row)
        in_specs.append(pl.BlockSpec((1, LANES), lambda i: (0, 0)))
        scratch = [pltpu.VMEM((SUBLANES, LANES), F32)]
    out_shape = [o[0] for o in outs]
    out_specs = [o[1] for o in outs]
    return pl.pallas_call(
        functools.partial(_proj_kernel, tm=tm, sample=sample, seq_tiles=seq_tiles),
        grid=(t // tm,),
        in_specs=in_specs,
        out_specs=out_specs,
        out_shape=out_shape,
        scratch_shapes=scratch,
        compiler_params=pltpu.CompilerParams(dimension_semantics=("arbitrary",),
                                             vmem_limit_bytes=VMEM_LIMIT),
        name="proj",
    )(*operands)


def _gdn_stages(*refs, c, l_valid, nb, project, chunk_of_step):
    if project:
        (h_ref, wqkv_ref, zg_ref, sm_ref, cw_ref, alog_ref, dtb_ref, onw_ref,
         og_ref, sout_ref, tail_ref, xbuf, s_scr) = refs
    else:
        (qkv_ref, zg_ref, sm_ref, cw_ref, alog_ref, dtb_ref, onw_ref, s0_ref, c0_ref,
         og_ref, sout_ref, ctail_ref, xbuf, s_scr) = refs
    ci, n_c = chunk_of_step

    @pl.when(ci == 0)
    def _():
        if project:
            xbuf[:, 0:SUBLANES, :] = jnp.zeros((nb, SUBLANES, CONV_DIM), F32)
            s_scr[...] = jnp.zeros(s_scr.shape, F32)
        else:
            xbuf[:, SUBLANES - (CONV_K - 1):SUBLANES, :] = c0_ref[...]
            s_scr[...] = s0_ref[...]

    row = _iota2((c, 1), 0) + ci * c
    valid = jnp.broadcast_to((row < l_valid).astype(F32), (c, LANES))
    tri_incl = (_iota2((c, c), 0) >= _iota2((c, c), 1))
    tri_strict = (_iota2((c, c), 0) > _iota2((c, c), 1))
    eye = (_iota2((c, c), 0) == _iota2((c, c), 1)).astype(F32)
    pad_rows = LANES - c
    sl = lambda base, h: slice(base + h * HEAD_DIM, base + (h + 1) * HEAD_DIM)

    if project:
        raw = jnp.dot(h_ref[...].reshape(nb * c, h_ref.shape[-1]), wqkv_ref[...],
                      preferred_element_type=F32)
    yield
    q, k, v, beta, gc, gc_row, gc_last = [], [], [], [], [], [], []
    own_rows = lambda t, bb: pltpu.roll(t, (c - bb * l_valid) % c, axis=0) if bb else t
    for bb in range(nb):
        if project:
            xbuf[bb, SUBLANES:SUBLANES + c, :] = raw[bb * c:(bb + 1) * c]
        else:
            xbuf[bb, SUBLANES:SUBLANES + c, :] = jnp.where(row < l_valid, own_rows(qkv_ref[...], bb), 0.0)
        yq, yk, yv = (_conv_silu_qkv(xbuf.at[bb], cw_ref, g, c) for g in range(3))
        if project:
            tail_ref[bb] = xbuf[bb, c:c + SUBLANES, :]
        else:
            last = SUBLANES + l_valid
            ctail_ref[bb] = xbuf[bb, last - (CONV_K - 1):last, :]
        xbuf[bb, 0:SUBLANES, :] = xbuf[bb, c:c + SUBLANES, :]
        sm = sm_ref[bb]
        beta_t = _sigmoid(sm) * valid
        g_t = -jnp.exp(alog_ref[...]) * _softplus(sm + dtb_ref[...]) * valid
        gc_t = _fdot(tri_incl.astype(F32), g_t)
        gc_sq = jnp.concatenate([gc_t, jnp.zeros((pad_rows, LANES), F32)], axis=0) if pad_rows else gc_t
        gc_tr = gc_sq.T
        for h in range(N_HEADS):
            q.append(_l2_normalize(yq[:, sl(0, h)], HEAD_DIM ** -0.5))
            k.append(_l2_normalize(yk[:, sl(0, h)], 1.0) * valid)
            v.append(yv[:, sl(0, h)])
            beta.append(jnp.broadcast_to(beta_t[:, SM_BETA + h:SM_BETA + h + 1], (c, HEAD_DIM)))
            gc.append(jnp.broadcast_to(gc_t[:, SM_DECAY + h:SM_DECAY + h + 1], (c, HEAD_DIM)))
            gc_row.append(gc_tr[SM_DECAY + h:SM_DECAY + h + 1, 0:c])
            gc_last.append(jnp.broadcast_to(gc_t[c - 1:c, SM_DECAY + h:SM_DECAY + h + 1], (1, HEAD_DIM)))

    chains = range(nb * N_HEADS)
    decay = [jnp.where(tri_incl, jnp.exp(jnp.where(tri_incl, gc[i][:, 0:c] - gc_row[i], 0.0)), 0.0)
             for i in chains]
    kb = [k[i] * beta[i] for i in chains]
    kkqk = [_bdot_nt(jnp.concatenate([kb[i], q[i]], axis=0), k[i]) for i in chains]
    qk = [kkqk[i][c:2 * c] * decay[i] for i in chains]
    yield
    neg_a = [-jnp.where(tri_strict, kkqk[i][0:c] * decay[i], 0.0) for i in chains]
    base = min(INV_BASE, c)
    blk_r, blk_c = _iota2((c, c), 0), _iota2((c, c), 1)
    same = lambda size: (blk_r // size) == (blk_c // size)
    diag = [jnp.where(same(base), neg_a[i], 0.0) for i in chains] if base < c else neg_a
    t_inv = [eye + diag[i] for i in chains]
    pw = [_bdot(diag[i], diag[i]) for i in chains]
    yield
    n_sq = int(math.log2(base))
    for j in range(1, n_sq):
        if j < n_sq - 1:
            both = [_bdot(jnp.concatenate([t_inv[i], pw[i]], axis=0), pw[i]) for i in chains]
            t_inv = [t_inv[i] + both[i][0:c] for i in chains]
            pw = [both[i][c:2 * c] for i in chains]
        else:
            t_inv = [t_inv[i] + _bdot(t_inv[i], pw[i]) for i in chains]
        yield
    size = base
    while size < c:
        off = [jnp.where(same(2 * size) & ~same(size), neg_a[i], 0.0) for i in chains]
        right = [_bdot(off[i], t_inv[i]) for i in chains]
        yield
        t_inv = [t_inv[i] + _bdot(t_inv[i], right[i]) for i in chains]
        yield
        size *= 2
    egc = [jnp.exp(gc[i]) for i in chains]
    sol = [_bdot(t_inv[i], jnp.concatenate([v[i] * beta[i], kb[i] * egc[i]], axis=-1)) for i in chains]
    yield
    s = [s_scr[i // N_HEADS, i % N_HEADS] for i in chains]
    ws = [_bdot(jnp.concatenate([sol[i][:, HEAD_DIM:2 * HEAD_DIM], q[i] * egc[i]], axis=0), s[i])
          for i in chains]
    yield
    v_new = [sol[i][:, 0:HEAD_DIM] - ws[i][0:c] for i in chains]
    o = [ws[i][c:2 * c] + _bdot(qk[i], v_new[i]) for i in chains]
    k_dec = [k[i] * jnp.exp(gc_last[i] - gc[i]) for i in chains]
    s_new = [s[i] * jnp.exp(gc_last[i]) + _bdot_tn(k_dec[i], v_new[i]) for i in chains]
    yield
    for i in chains:
        bb, h = i // N_HEADS, i % N_HEADS
        s_scr[bb, h] = s_new[i]
        oh = o[i] * lax.rsqrt(jnp.mean(o[i] * o[i], axis=-1, keepdims=True) + NORM_EPS) * onw_ref[...]
        z = zg_ref[bb, :, sl(0, h)] if project else own_rows(zg_ref[:, sl(0, h)], bb)
        og_ref[bb, :, sl(0, h)] = (oh * (z * _sigmoid(z))).astype(og_ref.dtype)

    @pl.when(ci == n_c - 1)
    def _():
        sout_ref[...] = s_scr[...]


NEG_BIG = -1e30


def _forget_columns(f_tile, h, rows, for_keys):
    f = jnp.broadcast_to(f_tile[:, SM_FORGET + h:SM_FORGET + h + 1], (rows, LANES))
    f1, f2, f3 = (t.astype(F32) for t in _split3(-f if for_keys else f))
    lane = _iota2((rows, LANES), 1)
    base = 3 if for_keys else 0
    ones = ((lane >= 3 - base) & (lane < 6 - base)).astype(F32)
    cols = jnp.where(lane == base, f1, jnp.where(lane == base + 1, f2, jnp.where(lane == base + 2, f3, ones)))
    return cols.astype(BF16)


ROW_GROUP = 32


def _fox_prompt_kernel(q_ref, k_ref, v_ref, fcol_ref, zf_ref, og_ref, x_ref, wo_ref, fnw_ref, y_ref,
                       kx_ref, qa_ref, s_ref, p_ref, acc_ref, m_ref, a_ref, *, tq, l):
    qi = pl.program_id(1)
    heads = range(N_HEADS)
    sl = lambda h: slice(h * HEAD_DIM, (h + 1) * HEAD_DIM)
    nt = (((1,), (1,)), ((), ()))

    @pl.when(qi == 0)
    def _():
        for r in range(l // tq):
            for h in heads:
                kx_ref[r * tq:(r + 1) * tq, sl(h)] = _forget_columns(fcol_ref[r * tq:(r + 1) * tq, :], h, tq, True)

    f_q = fcol_ref[pl.ds(pl.multiple_of(qi * tq, tq), tq), :]
    for h in heads:
        qa_ref[h, :, 0:HEAD_DIM] = q_ref[:, sl(h)]
        qa_ref[h, :, HEAD_DIM:2 * HEAD_DIM] = _forget_columns(f_q, h, tq, False)
    acc_ref[...] = jnp.zeros(acc_ref.shape, F32)
    m_ref[...] = jnp.full(m_ref.shape, NEG_BIG, F32)
    ones = jnp.ones((tq, HEAD_DIM), BF16)

    half = tq // 2

    def block(ki, masked):
        start = pl.multiple_of(ki * tq, tq)
        pieces = [(slice(0, half), half), (slice(half, tq), tq)] if masked else [(slice(0, tq), tq)]
        for h in heads:
            for qr, nk in pieces:
                keys = pl.ds(start, nk)
                ka = jnp.concatenate([k_ref[keys, sl(h)], kx_ref[keys, sl(h)]], axis=1)
                s_ref[h, qr, 0:nk] = lax.dot_general(qa_ref[h, qr, :], ka, nt, preferred_element_type=F32)
        for h in heads:
            for r in range(0, tq, ROW_GROUP):
                rg = slice(r, r + ROW_GROUP)
                nk = half if (masked and r < half) else tq
                s = s_ref[h, rg, 0:nk]
                if masked:
                    keep = _iota2((ROW_GROUP, nk), 1) <= _iota2((ROW_GROUP, nk), 0) + r
                    s = jnp.where(keep, s, NEG_BIG)
                m_old = m_ref[h, rg, :]
                m_new = jnp.maximum(m_old, jnp.max(s, axis=-1, keepdims=True))
                a_ref[h, rg, :] = jnp.exp(m_old - m_new)
                m_ref[h, rg, :] = m_new
                p_ref[h, rg, 0:nk] = jnp.exp(s - jnp.concatenate([m_new] * (nk // LANES), axis=1)).astype(BF16)
        for h in heads:
            alpha = a_ref[h]
            for qr, nk in pieces:
                keys = pl.ds(start, nk)
                pv = jnp.dot(p_ref[h, qr, 0:nk], jnp.concatenate([v_ref[keys, sl(h)], ones[0:nk]], axis=1),
                             preferred_element_type=F32)
                acc_ref[h, qr, :] = acc_ref[h, qr, :] * jnp.concatenate([alpha[qr], alpha[qr]], axis=1) + pv

    def body(ki, carry):
        block(ki, False)
        return carry

    lax.fori_loop(0, qi, body, 0)
    block(qi, True)
    gated = []
    for h in heads:
        z = zf_ref[:, sl(h)]
        o = acc_ref[h, :, 0:HEAD_DIM] / acc_ref[h, :, HEAD_DIM:2 * HEAD_DIM]
        gated.append((o * (z * _sigmoid(z))).astype(BF16))
    mixed = jnp.dot(og_ref[...], wo_ref[0:GROUP_W, :], preferred_element_type=F32)
    mixed = mixed + jnp.dot(jnp.concatenate(gated, axis=1), wo_ref[GROUP_W:2 * GROUP_W, :],
                            preferred_element_type=F32)
    y = x_ref[...] + mixed
    var = jnp.mean(y * y, axis=-1, keepdims=True)
    y_ref[...] = y * lax.rsqrt(var + NORM_EPS) * fnw_ref[...]


def _fox_prompt_call(qf, kf, vf, fcol, zf, og, x, w_out, fnw, tq):
    b, l, _ = qf.shape
    d = x.shape[-1]
    kern = functools.partial(_fox_prompt_kernel, tq=tq, l=l)
    qblk = lambda w: pl.BlockSpec((None, tq, w), lambda bi, qi: (bi, qi, 0))
    seq = lambda w: pl.BlockSpec((None, l, w), lambda bi, qi: (bi, 0, 0))
    const = lambda shape: pl.BlockSpec(shape, lambda bi, qi: (0,) * len(shape))
    return pl.pallas_call(
        kern,
        grid=(b, l // tq),
        in_specs=[qblk(GROUP_W), seq(GROUP_W), seq(GROUP_W), seq(LANES), qblk(GROUP_W),
                  qblk(GROUP_W), qblk(d), const((2 * GROUP_W, d)), const((1, d))],
        out_specs=qblk(d),
        out_shape=jax.ShapeDtypeStruct((b, l, d), F32),
        scratch_shapes=[pltpu.VMEM((l, GROUP_W), BF16),
                        pltpu.VMEM((N_HEADS, tq, 2 * HEAD_DIM), BF16),
                        pltpu.VMEM((N_HEADS, tq, tq), F32),
                        pltpu.VMEM((N_HEADS, tq, tq), BF16),
                        pltpu.VMEM((N_HEADS, tq, 2 * HEAD_DIM), F32),
                        pltpu.VMEM((N_HEADS, tq, LANES), F32),
                        pltpu.VMEM((N_HEADS, tq, LANES), F32)],
        compiler_params=pltpu.CompilerParams(dimension_semantics=("arbitrary", "arbitrary"),
                                             vmem_limit_bytes=VMEM_LIMIT),
        name="fox_prompt",
    )(qf, kf, vf, fcol, zf, og, x, w_out, fnw)


def _out_kernel(og_ref, of_ref, x_ref, w_ref, fnw_ref, y_ref):
    o = jnp.dot(og_ref[...], w_ref[0:GROUP_W, :], preferred_element_type=F32)
    o = o + jnp.dot(of_ref[...], w_ref[GROUP_W:2 * GROUP_W, :], preferred_element_type=F32)
    y = x_ref[...] + o
    var = jnp.mean(y * y, axis=-1, keepdims=True)
    y_ref[...] = y * lax.rsqrt(var + NORM_EPS) * fnw_ref[...]


def _out_call(og, of, x2d, w_out, fnw, tm):
    t, d = x2d.shape
    return pl.pallas_call(
        _out_kernel,
        grid=(t // tm,),
        in_specs=[pl.BlockSpec((tm, GROUP_W), lambda i: (i, 0)),
                  pl.BlockSpec((tm, GROUP_W), lambda i: (i, 0)),
                  pl.BlockSpec((tm, d), lambda i: (i, 0)),
                  pl.BlockSpec((2 * GROUP_W, d), lambda i: (0, 0)),
                  pl.BlockSpec((1, d), lambda i: (0, 0))],
        out_specs=pl.BlockSpec((tm, d), lambda i: (i, 0)),
        out_shape=jax.ShapeDtypeStruct((t, d), F32),
        compiler_params=pltpu.CompilerParams(dimension_semantics=("arbitrary",),
                                             vmem_limit_bytes=VMEM_LIMIT),
        name="out_proj",
    )(og, of, x2d, w_out, fnw)


def _page_copies(pt_ref, kc_ref, vc_ref, lc_ref, kbuf, vbuf, lbuf, sems, step, slot, n_pages, pg, rows):
    copies = []
    for r in range(rows):
        for p in range(n_pages):
            pid = pt_ref[step * rows + r, p]
            copies.append((pltpu.make_async_copy(kc_ref.at[pid], kbuf.at[slot, r, pl.ds(p * pg, pg)],
                                                 sems.at[0, slot]), 0))
            copies.append((pltpu.make_async_copy(vc_ref.at[pid], vbuf.at[slot, r, pl.ds(p * pg, pg)],
                                                 sems.at[1, slot]), 1))
            copies.append((pltpu.make_async_copy(lc_ref.at[pid], lbuf.at[slot, r, :, p, :], sems.at[2, slot]),
                           p % 2))
    return copies


def _split3(x):
    x1 = x.astype(BF16)
    r1 = x - x1.astype(F32)
    x2 = r1.astype(BF16)
    x3 = (r1 - x2.astype(F32)).astype(BF16)
    return x1, x2, x3


def _fox_decode_phases(pt_ref, q_ref, kn_ref, vn_ref, zf_ref, sm_ref, fbrow_ref, cums_ref,
                       kc_ref, vc_ref, lc_ref, o_ref, logf_ref, kbuf, vbuf, lbuf, sems,
                       *, n_pages, pg, l_new, rows):
    step = pl.program_id(0)
    n_steps = pl.num_programs(0)
    slot = step % 2
    copies = functools.partial(_page_copies, pt_ref, kc_ref, vc_ref, lc_ref, kbuf, vbuf, lbuf, sems,
                               n_pages=n_pages, pg=pg, rows=rows)
    nr = l_new * N_HEADS

    @pl.when(step == 0)
    def _():
        for cp, prio in copies(step=step, slot=slot):
            cp.start(priority=prio)

    @pl.when(step + 1 < n_steps)
    def _():
        for cp, prio in copies(step=step + 1, slot=1 - slot):
            cp.start(priority=prio)

    for cp, _ in copies(step=step, slot=slot):
        cp.wait()
    yield

    scale = HEAD_DIM ** -0.5
    earlier = (_iota2((n_pages, n_pages), 0) > _iota2((n_pages, n_pages), 1)).astype(F32)
    tok_valid = (_iota2((SUBLANES, 1), 0) < l_new).astype(F32)
    r_tok = _iota2((nr, SUBLANES), 0) // N_HEADS
    f_past, f_tot_row, csum, s_all, s_new = [], [], [], [], []
    for r in range(rows):
        qrows = slice(r * nr, (r + 1) * nr)
        res = jnp.zeros((3 * n_pages, 2 * pg), F32)
        for h in range(N_HEADS):
            res = res + jnp.dot(jnp.concatenate(_split3(lbuf[slot, r, h]), axis=0), cums_ref[h],
                                preferred_element_type=F32)
        res = res[0:n_pages] + res[n_pages:2 * n_pages] + res[2 * n_pages:3 * n_pages]
        within, tot = res[:, 0:pg], res[:, pg:2 * pg]
        carry = _fdot(earlier, tot)
        f_past.append(within + carry)
        f_tot_row.append(carry[n_pages - 1:n_pages, :] + tot[n_pages - 1:n_pages, :])
        lf_col = _log_sigmoid(sm_ref[r] + fbrow_ref[...]) * tok_valid
        logf_ref[r] = lf_col
        csum.append(_fdot((_iota2((nr, SUBLANES), 1) <= r_tok).astype(F32), lf_col))
        q = q_ref[qrows, :].astype(BF16)
        s_all.append(lax.dot_general(q, kbuf[slot, r].astype(BF16), (((1,), (1,)), ((), ())),
                                     preferred_element_type=F32))
        s_new.append(lax.dot_general(q, kn_ref[qrows, :].astype(BF16), (((1,), (1,)), ((), ())),
                                     preferred_element_type=F32))
    yield
    own_lane = _iota2((nr, LANES), 1) == SM_FORGET + _iota2((nr, LANES), 0) % N_HEADS
    eye = _iota2((nr, nr), 0) == _iota2((nr, nr), 1)
    same_head = (_iota2((nr, pg), 1) % N_HEADS) == (_iota2((nr, pg), 0) % N_HEADS)
    rr, cc = _iota2((nr, nr), 0), _iota2((nr, nr), 1)
    new_ok = (rr % N_HEADS == cc % N_HEADS) & (cc // N_HEADS <= rr // N_HEADS)
    for r in range(rows):
        qrows = slice(r * nr, (r + 1) * nr)
        fq_new = jnp.sum(jnp.where(own_lane, csum[r], 0.0), axis=-1, keepdims=True)
        f_tot_col = jnp.sum(jnp.where(eye, jnp.broadcast_to(f_tot_row[r][:, 0:nr], (nr, nr)), 0.0),
                            axis=-1, keepdims=True)
        fq = fq_new + f_tot_col
        fq_row = jnp.sum(jnp.where(eye, jnp.broadcast_to(fq, (nr, nr)), 0.0), axis=0, keepdims=True)
        sp = [jnp.where(same_head, s_all[r][:, p * pg:(p + 1) * pg] * scale + (fq - f_past[r][p:p + 1, :]),
                        NEG_BIG) for p in range(n_pages)]
        sn = jnp.where(new_ok, s_new[r] * scale + (fq - fq_row), NEG_BIG)
        m_el = sp[0]
        for p in range(1, n_pages):
            m_el = jnp.maximum(m_el, sp[p])
        m = jnp.maximum(jnp.max(m_el, axis=-1, keepdims=True), jnp.max(sn, axis=-1, keepdims=True))
        pp = [jnp.exp(t - m) for t in sp]
        p_new = jnp.exp(sn - m)
        l_el = pp[0]
        for p in range(1, n_pages):
            l_el = l_el + pp[p]
        l = jnp.sum(l_el, axis=-1, keepdims=True) + jnp.sum(p_new, axis=-1, keepdims=True)
        p_all = jnp.concatenate([t.astype(BF16) for t in pp], axis=-1)
        acc = jnp.dot(p_all, vbuf[slot, r].astype(BF16), preferred_element_type=F32)
        acc = acc + jnp.dot(p_new.astype(BF16), vn_ref[qrows, :].astype(BF16), preferred_element_type=F32)
        z = zf_ref[qrows, :]
        o_ref[qrows, :] = ((acc / l) * (z * _sigmoid(z))).astype(o_ref.dtype)


def _head_cumsum_matrix(page):
    t = jnp.arange(page)[None, :, None]
    j = jnp.arange(page * N_HEADS)[None, None, :]
    h = jnp.arange(N_HEADS)[:, None, None]
    own = (j % N_HEADS) == h
    c = own & (t <= j // N_HEADS)
    b = jnp.broadcast_to(own, c.shape)
    return jnp.concatenate([c, b], axis=2).astype(BF16)


DECODE_ROWS_PER_STEP = 2
N_DECODE_INPUTS, N_DECODE_OUTPUTS, N_DECODE_SCRATCH = 10, 2, 4
N_GDN_INPUTS, N_GDN_OUTPUTS, N_GDN_SCRATCH = 8, 3, 2
N_SAMPLE_GDN_INPUTS, N_SAMPLE_GDN_OUTPUTS = 4, 3


def _decode_gdn_kernel(pt_ref, *refs, decode_kw, gdn_kw, sample_kw, n_chunks):
    take = lambda n: (refs[:n], refs[n:])
    dec_in, refs = take(N_DECODE_INPUTS)
    gdn_in, refs = take(N_GDN_INPUTS)
    smp_in, refs = take(N_SAMPLE_GDN_INPUTS)
    dec_out, refs = take(N_DECODE_OUTPUTS)
    gdn_out, refs = take(N_GDN_OUTPUTS)
    smp_out, refs = take(N_SAMPLE_GDN_OUTPUTS)
    dec_scr, refs = take(N_DECODE_SCRATCH)
    gdn_scr, smp_scr = take(N_GDN_SCRATCH)
    decode = _fox_decode_phases(pt_ref, *dec_in, *dec_out, *dec_scr, **decode_kw)
    next(decode)
    prompt = _gdn_stages(*gdn_in, *gdn_out, *gdn_scr, **gdn_kw,
                         chunk_of_step=(pl.program_id(0) % n_chunks, n_chunks))
    qkv_s, zg_s, s0_s, c0_s = smp_in
    sm_s = dec_in[4]
    sample = _gdn_stages(qkv_s, zg_s, sm_s, *gdn_in[4:8], s0_s, c0_s, *smp_out, *smp_scr, **sample_kw,
                         chunk_of_step=(jnp.int32(0), 1))
    programs = [prompt, sample, decode]
    while programs:
        for prog in list(programs):
            if next(prog, StopIteration) is StopIteration:
                programs.remove(prog)


def _decode_gdn_call(page_table, q4, kn4, vn4, zf4, sm8, fb_row, kcache, vcache, lcache, l_new, gdn):
    b, n_pages = page_table.shape
    pg = kcache.shape[1]
    page = lcache.shape[2]
    nr = l_new * N_HEADS
    assert nr % (2 * SUBLANES) == 0 and l_new <= SUBLANES
    rps = DECODE_ROWS_PER_STEP
    assert b % rps == 0
    n_steps = b // rps
    decode_kw = dict(n_pages=n_pages, pg=pg, l_new=l_new, rows=rps)
    rows = pl.BlockSpec((rps * nr, HEAD_DIM), lambda i, pt: (i, 0))
    tok = pl.BlockSpec((rps, SUBLANES, LANES), lambda i, pt: (i, 0, 0))
    const = lambda shape: pl.BlockSpec(shape, lambda i, pt: (0,) * len(shape))
    any_spec = pl.BlockSpec(memory_space=pl.ANY)
    operands = [page_table, q4, kn4, vn4, zf4, sm8, fb_row, _head_cumsum_matrix(page), kcache, vcache, lcache]
    in_specs = [rows, rows, rows, rows, tok, const((1, LANES)), const((N_HEADS, page, 2 * pg)),
                any_spec, any_spec, any_spec]
    out_specs = [rows, tok]
    out_shape = [jax.ShapeDtypeStruct((b * nr, HEAD_DIM), BF16), jax.ShapeDtypeStruct((b, SUBLANES, LANES), F32)]
    scratch = [pltpu.VMEM((2, rps, n_pages * pg, HEAD_DIM), F32),
               pltpu.VMEM((2, rps, n_pages * pg, HEAD_DIM), F32),
               pltpu.VMEM((2, rps, N_HEADS, n_pages, page), F32),
               pltpu.SemaphoreType.DMA((3, 2))]
    h3, w_qkv, zg, sm, conv_w, alog_row, dtb_row, onw, c, (qkv_s, zg_s, s0_s, c0_s) = gdn
    assert rps * l_new == SUBLANES
    bp, l, d = h3.shape
    n_c = l // c
    nb = bp * n_c // n_steps
    assert nb >= 1 and (bp // nb) * n_c == n_steps
    blk = lambda w: pl.BlockSpec((nb, c, w), lambda i, pt: (i // n_c, i % n_c, 0))
    state = pl.BlockSpec((nb, N_HEADS, HEAD_DIM, HEAD_DIM), lambda i, pt: (i // n_c, 0, 0, 0))
    rows8 = pl.BlockSpec((nb, SUBLANES, CONV_DIM), lambda i, pt: (i // n_c, 0, 0))
    srow = lambda w: pl.BlockSpec((rps, SUBLANES, w), lambda i, pt: (i, 0, 0))
    sstate = pl.BlockSpec((rps, N_HEADS, HEAD_DIM, HEAD_DIM), lambda i, pt: (i, 0, 0, 0))
    stok = lambda w: pl.BlockSpec((rps * l_new, w), lambda i, pt: (i, 0))
    sconv = pl.BlockSpec((rps, CONV_K - 1, CONV_DIM), lambda i, pt: (i, 0, 0))
    operands += [h3, w_qkv, zg, sm, conv_w, alog_row, dtb_row, onw, qkv_s, zg_s, s0_s, c0_s]
    in_specs += [blk(d), const((d, CONV_DIM)), blk(GROUP_W), blk(LANES), const((CONV_K, CONV_DIM)),
                 const((1, LANES)), const((1, LANES)), const((1, HEAD_DIM)),
                 stok(CONV_DIM), stok(GROUP_W), sstate, sconv]
    out_specs += [blk(GROUP_W), state, rows8, srow(GROUP_W), sstate, sconv]
    out_shape += [jax.ShapeDtypeStruct((bp, l, GROUP_W), BF16),
                  jax.ShapeDtypeStruct((bp, N_HEADS, HEAD_DIM, HEAD_DIM), F32),
                  jax.ShapeDtypeStruct((bp, SUBLANES, CONV_DIM), F32),
                  jax.ShapeDtypeStruct((b, SUBLANES, GROUP_W), BF16),
                  jax.ShapeDtypeStruct((b, N_HEADS, HEAD_DIM, HEAD_DIM), F32),
                  jax.ShapeDtypeStruct((b, CONV_K - 1, CONV_DIM), F32)]
    scratch += [pltpu.VMEM((nb, c + SUBLANES, CONV_DIM), F32),
                pltpu.VMEM((nb, N_HEADS, HEAD_DIM, HEAD_DIM), F32),
                pltpu.VMEM((rps, 2 * SUBLANES, CONV_DIM), F32),
                pltpu.VMEM((rps, N_HEADS, HEAD_DIM, HEAD_DIM), F32)]
    kern = functools.partial(_decode_gdn_kernel, decode_kw=decode_kw, n_chunks=n_c,
                             gdn_kw=dict(c=c, l_valid=l, nb=nb, project=True),
                             sample_kw=dict(c=SUBLANES, l_valid=l_new, nb=rps, project=False))
    grid_spec = pltpu.PrefetchScalarGridSpec(num_scalar_prefetch=1, grid=(n_steps,), in_specs=in_specs,
                                             out_specs=out_specs, scratch_shapes=scratch)
    return pl.pallas_call(
        kern,
        grid_spec=grid_spec,
        out_shape=out_shape,
        compiler_params=pltpu.CompilerParams(dimension_semantics=("arbitrary",),
                                             vmem_limit_bytes=VMEM_LIMIT),
        name="decode_gdn",
    )(*operands)


def _gate_row(vals, offset):
    return jnp.zeros((1, LANES), F32).at[0, offset:offset + N_HEADS].set(vals.astype(F32))


def _pad_rows(t, rows):
    return jnp.pad(t, ((0, 0), (0, rows - t.shape[1]), (0, 0)))


def kernel(x_prompt, x_sample, cache_fox_k, cache_fox_v, cache_fox_logf, page_table, state_gdn_ssm,
           state_gdn_conv, w_in, gdn_conv_w, gdn_a_log, gdn_dt_bias, gdn_out_norm_w, fox_f_bias, w_out,
           norm_w, final_norm_w):
    bp, lp, d = x_prompt.shape
    bs, ls, _ = x_sample.shape
    depth = w_in.shape[0]
    assert depth == 1, "single-layer trunk"
    n_pool, page = cache_fox_k.shape[1], cache_fox_k.shape[2]

    w_big, w_qkv = _pack_w_call(w_in[0].T)
    w_o = w_out[0].astype(BF16)
    nw = norm_w[0].reshape(1, d)
    fnw = final_norm_w.reshape(1, d)
    conv_w = gdn_conv_w[0]
    alog_row = _gate_row(gdn_a_log[0], SM_DECAY)
    dtb_row = _gate_row(gdn_dt_bias[0], SM_DECAY)
    fb_row = _gate_row(fox_f_bias[0], SM_FORGET)
    onw = gdn_out_norm_w[0].reshape(1, HEAD_DIM)

    xp2 = x_prompt.reshape(bp * lp, d)
    hp, zg, sm, fcol, logf_t, qf, kf, vf, zf, k4, v4 = _proj_call(xp2, nw, w_big, tm=ROWS_PER_STEP_PROMPT,
                                                                  sample=False, fb_row=fb_row, seq_len=lp)
    r3 = lambda t: t.reshape(bp, lp, t.shape[-1])

    xs2 = x_sample.reshape(bs * ls, d)
    qkv_s, zg_s, sm_s, q4_s, k4_s, v4_s, z4_s = _proj_call(xs2, nw, w_big, tm=ROWS_PER_STEP_SAMPLE, sample=True)
    r3s = lambda t: t.reshape(bs, ls, t.shape[-1])
    p8 = lambda t: _pad_rows(r3s(t), SUBLANES)
    kcache = cache_fox_k[0].reshape(n_pool, page * N_HEADS, HEAD_DIM)
    vcache = cache_fox_v[0].reshape(n_pool, page * N_HEADS, HEAD_DIM)
    lcache = cache_fox_logf[0].transpose(0, 2, 1)
    of_s, logf_s, og_p, ssm_p, tail, og_s, ssm_s, ctail_s = _decode_gdn_call(
        page_table, q4_s, k4_s, v4_s, z4_s, p8(sm_s), fb_row, kcache, vcache, lcache, l_new=ls,
        gdn=(r3(hp), w_qkv, r3(zg), r3(sm), conv_w, alog_row, dtb_row, onw, GDN_CHUNK,
             (qkv_s, zg_s, state_gdn_ssm[0], state_gdn_conv[0])))
    og_s2 = og_s[:, :ls].reshape(bs * ls, GROUP_W)
    of_s2 = of_s.reshape(bs * ls, GROUP_W)
    y_s = _out_call(og_s2, of_s2, xs2, w_o, fnw, tm=ROWS_PER_STEP_SAMPLE)
    y_prompt = _fox_prompt_call(r3(qf), r3(kf), r3(vf), r3(fcol), r3(zf), og_p, x_prompt, w_o, fnw,
                                tq=ROWS_PER_STEP_PROMPT)

    k_prompt = k4.reshape(1, bp, lp, N_HEADS, HEAD_DIM)
    v_prompt = v4.reshape(1, bp, lp, N_HEADS, HEAD_DIM)
    logf_prompt = logf_t.transpose(0, 2, 1).reshape(1, bp, lp, N_HEADS)
    ssm_prompt = ssm_p.reshape(1, bp, N_HEADS, HEAD_DIM, HEAD_DIM)
    conv_prompt = tail[:, SUBLANES - (CONV_K - 1):, :].reshape(1, bp, CONV_K - 1, CONV_DIM)
    y_sample = y_s.reshape(bs, ls, d)
    k_sample = k4_s.reshape(1, bs, ls, N_HEADS, HEAD_DIM)
    v_sample = v4_s.reshape(1, bs, ls, N_HEADS, HEAD_DIM)
    logf_sample = logf_s[:, :ls, SM_FORGET:SM_FORGET + N_HEADS].reshape(1, bs, ls, N_HEADS)
    ssm_sample = ssm_s.reshape(1, bs, N_HEADS, HEAD_DIM, HEAD_DIM)
    conv_sample = ctail_s.reshape(1, bs, CONV_K - 1, CONV_DIM)

    return (y_prompt, y_sample, k_prompt, v_prompt, logf_prompt, ssm_prompt, conv_prompt,
            k_sample, v_sample, logf_sample, ssm_sample, conv_sample)
```

```python
import functools
import math

import jax
import jax.numpy as jnp
from jax import lax
from jax.experimental import pallas as pl
from jax.experimental.pallas import tpu as pltpu

F32 = jnp.float32
BF16 = jnp.bfloat16

NORM_EPS = 1e-6
L2_EPS = 1e-6
HEAD_DIM = 128
N_HEADS = 4
GROUP_W = N_HEADS * HEAD_DIM
CONV_DIM = 3 * GROUP_W
CONV_K = 4
LANES = 128
SUBLANES = 8
GDN_CHUNK = 64
INV_BASE = 32
ROWS_PER_STEP_PROMPT = 512
ROWS_PER_STEP_SAMPLE = 256
SM_BETA = 0
SM_DECAY = 4
SM_FORGET = 8
VMEM_LIMIT = 56 * 1024 * 1024


def _sigmoid(x):
    return 1.0 / (1.0 + jnp.exp(-x))


def _softplus(x):
    return jnp.maximum(x, 0.0) + jnp.log(1.0 + jnp.exp(-jnp.abs(x)))


def _log_sigmoid(x):
    return -_softplus(-x)


def _bdot(a, b):
    return jnp.dot(a.astype(BF16), b.astype(BF16), preferred_element_type=F32)


def _bdot_nt(a, b):
    return lax.dot_general(a.astype(BF16), b.astype(BF16), (((1,), (1,)), ((), ())),
                           preferred_element_type=F32)


def _bdot_tn(a, b):
    return lax.dot_general(a.astype(BF16), b.astype(BF16), (((0,), (0,)), ((), ())),
                           preferred_element_type=F32)


def _fdot(a, b):
    return jnp.dot(a, b, preferred_element_type=F32, precision=lax.Precision.HIGHEST)


def _iota2(shape, dim):
    return lax.broadcasted_iota(jnp.int32, shape, dim)


W_QKV, W_ZG, W_QF, W_KF, W_VF, W_ZF, W_SM, W_END = 0, 1536, 2048, 2560, 3072, 3584, 4096, 4224
SRC_GATES_G, SRC_FOX, SRC_GATE_F, SRC_END = 2048, 2056, 4104, 4108


def _pack_w_kernel(w_ref, o_ref, qkv_ref):
    qkv_ref[...] = w_ref[W_QKV:W_ZG, :].T.astype(BF16)
    o_ref[W_QKV:W_QF, :] = w_ref[0:SRC_GATES_G, :].astype(BF16)
    o_ref[W_QF:W_SM, :] = w_ref[SRC_FOX:SRC_GATE_F, :].astype(BF16)
    n_gate = (SRC_FOX - SRC_GATES_G) + (SRC_END - SRC_GATE_F)
    gates = jnp.concatenate([w_ref[SRC_GATES_G:SRC_FOX, :], w_ref[SRC_GATE_F:SRC_END, :],
                             jnp.zeros((W_END - W_SM - n_gate, w_ref.shape[1]), F32)], axis=0)
    o_ref[W_SM:W_END, :] = gates.astype(BF16)


def _pack_w_call(w_t):
    return pl.pallas_call(
        _pack_w_kernel,
        out_shape=[jax.ShapeDtypeStruct((W_END, w_t.shape[1]), BF16),
                   jax.ShapeDtypeStruct((w_t.shape[1], CONV_DIM), BF16)],
        compiler_params=pltpu.CompilerParams(vmem_limit_bytes=VMEM_LIMIT),
        name="pack_w",
    )(w_t)


def _store_head_rows(ref, val, tm):
    for h in range(N_HEADS):
        ref[pl.ds(h, tm, stride=N_HEADS), :] = val[:, h * HEAD_DIM:(h + 1) * HEAD_DIM].astype(ref.dtype)


def _conv_silu_qkv(xbuf, cw_ref, g, rows):
    cols = slice(g * GROUP_W, (g + 1) * GROUP_W)
    x = xbuf[0:rows + SUBLANES, cols]
    y = x[SUBLANES:] * cw_ref[CONV_K - 1:CONV_K, cols]
    for j in range(CONV_K - 1):
        shifted = pltpu.roll(x, CONV_K - 1 - j, axis=0)
        y = y + shifted[SUBLANES:] * cw_ref[j:j + 1, cols]
    return y * _sigmoid(y)


def _l2_normalize(t, scale):
    return t * (lax.rsqrt(jnp.sum(t * t, axis=-1, keepdims=True) + L2_EPS) * scale)


def _proj_kernel(x_ref, nw_ref, w_ref, *refs, tm, sample, seq_tiles):
    x = x_ref[...]
    var = jnp.mean(x * x, axis=-1, keepdims=True)
    h = (x * lax.rsqrt(var + NORM_EPS) * nw_ref[...]).astype(BF16)
    seg = lambda lo, hi: lax.dot_general(h, w_ref[lo:hi, :], (((1,), (1,)), ((), ())),
                                         preferred_element_type=F32)
    if sample:
        qkv_ref, zg_ref, sm_ref, q4_ref, k4_ref, v4_ref, z4_ref = refs
        qkv_ref[...] = seg(W_QKV, W_ZG)
        zg_ref[...] = seg(W_ZG, W_QF)
        sm_ref[...] = seg(W_SM, W_END)
        _store_head_rows(q4_ref, seg(W_QF, W_KF), tm)
        _store_head_rows(k4_ref, seg(W_KF, W_VF), tm)
        _store_head_rows(v4_ref, seg(W_VF, W_ZF), tm)
        _store_head_rows(z4_ref, seg(W_ZF, W_SM), tm)
        return
    (fb_ref, h_ref, zg_ref, sm_ref, fcol_ref, logft_ref, qb_ref, kb_ref, vb_ref, zf_ref, k4_ref, v4_ref,
     carry_ref) = refs
    h_ref[...] = h

    @pl.when(pl.program_id(0) % seq_tiles == 0)
    def _():
        carry_ref[...] = jnp.zeros(carry_ref.shape, F32)

    sm = seg(W_SM, W_END)
    sm_ref[...] = sm
    zg_ref[...] = seg(W_ZG, W_QF)
    qb_ref[...] = (seg(W_QF, W_KF) * (HEAD_DIM ** -0.5)).astype(BF16)
    kf = seg(W_KF, W_VF)
    _store_head_rows(k4_ref, kf, tm)
    kb_ref[...] = kf.astype(BF16)
    vf = seg(W_VF, W_ZF)
    _store_head_rows(v4_ref, vf, tm)
    vb_ref[...] = vf.astype(BF16)
    zf_ref[...] = seg(W_ZF, W_SM)

    tri = (_iota2((LANES, LANES), 0) >= _iota2((LANES, LANES), 1)).astype(BF16)
    blocks = [slice(i * LANES, (i + 1) * LANES) for i in range(tm // LANES)]
    lf = [_log_sigmoid(sm[blk] + fb_ref[...]) for blk in blocks]
    parts = [jnp.dot(tri, jnp.concatenate(_split3(t), axis=1), preferred_element_type=F32) for t in lf]
    within = [p[:, 0:LANES] + p[:, LANES:2 * LANES] + p[:, 2 * LANES:3 * LANES] for p in parts]
    carry = carry_ref[0:1, :]
    for i, blk in enumerate(blocks):
        fcol_ref[blk, :] = within[i] + carry
        carry = carry + within[i][LANES - 1:LANES, :]
        logft_ref[:, blk] = lf[i].T[SM_FORGET:SM_FORGET + N_HEADS, :]
    carry_ref[0:1, :] = carry


def _proj_call(x2d, norm_w, w_big, tm, sample, fb_row=None, seq_len=None):
    t, d = x2d.shape
    n = w_big.shape[0]
    wide = lambda w, dt: (jax.ShapeDtypeStruct((t, w), dt), pl.BlockSpec((tm, w), lambda i: (i, 0)))
    rows4 = (jax.ShapeDtypeStruct((t * N_HEADS, HEAD_DIM), F32),
             pl.BlockSpec((tm * N_HEADS, HEAD_DIM), lambda i: (i, 0)))
    operands = [x2d, norm_w, w_big]
    in_specs = [pl.BlockSpec((tm, d), lambda i: (i, 0)),
                pl.BlockSpec((1, d), lambda i: (0, 0)),
                pl.BlockSpec((n, d), lambda i: (0, 0))]
    scratch = []
    seq_tiles = 1
    if sample:
        outs = [wide(CONV_DIM, F32), wide(GROUP_W, F32), wide(LANES, F32), rows4, rows4, rows4, rows4]
    else:
        seq_tiles = seq_len // tm
        logft = (jax.ShapeDtypeStruct((t // seq_len, N_HEADS, seq_len), F32),
                 pl.BlockSpec((None, N_HEADS, tm), lambda i: (i // seq_tiles, 0, i % seq_tiles)))
        outs = [wide(d, BF16), wide(GROUP_W, F32), wide(LANES, F32), wide(LANES, F32), logft,
                wide(GROUP_W, BF16), wide(GROUP_W, BF16), wide(GROUP_W, BF16), wide(GROUP_W, F32), rows4, rows4]
        operands.append(fb_row)
        in_specs.append(pl.BlockSpec((1, LANES), lambda i: (0, 0)))
        scratch = [pltpu.VMEM((SUBLANES, LANES), F32)]
    out_shape = [o[0] for o in outs]
    out_specs = [o[1] for o in outs]
    return pl.pallas_call(
        functools.partial(_proj_kernel, tm=tm, sample=sample, seq_tiles=seq_tiles),
        grid=(t // tm,),
        in_specs=in_specs,
        out_specs=out_specs,
        out_shape=out_shape,
        scratch_shapes=scratch,
        compiler_params=pltpu.CompilerParams(dimension_semantics=("arbitrary",),
                                             vmem_limit_bytes=VMEM_LIMIT),
        name="proj",
    )(*operands)


def _gdn_stages(*refs, c, l_valid, nb, project, chunk_of_step):
    if project:
        (h_ref, wqkv_ref, zg_ref, sm_ref, cw_ref, alog_ref, dtb_ref, onw_ref,
         og_ref, sout_ref, tail_ref, xbuf, s_scr) = refs
    else:
        (qkv_ref, zg_ref, sm_ref, cw_ref, alog_ref, dtb_ref, onw_ref, s0_ref, c0_ref,
         og_ref, sout_ref, ctail_ref, xbuf, s_scr) = refs
    ci, n_c = chunk_of_step

    @pl.when(ci == 0)
    def _():
        if project:
            xbuf[:, 0:SUBLANES, :] = jnp.zeros((nb, SUBLANES, CONV_DIM), F32)
            s_scr[...] = jnp.zeros(s_scr.shape, F32)
        else:
            xbuf[:, SUBLANES - (CONV_K - 1):SUBLANES, :] = c0_ref[...]
            s_scr[...] = s0_ref[...]

    row = _iota2((c, 1), 0) + ci * c
    valid = jnp.broadcast_to((row < l_valid).astype(F32), (c, LANES))
    tri_incl = (_iota2((c, c), 0) >= _iota2((c, c), 1))
    tri_strict = (_iota2((c, c), 0) > _iota2((c, c), 1))
    eye = (_iota2((c, c), 0) == _iota2((c, c), 1)).astype(F32)
    pad_rows = LANES - c
    sl = lambda base, h: slice(base + h * HEAD_DIM, base + (h + 1) * HEAD_DIM)

    if project:
        raw = jnp.dot(h_ref[...].reshape(nb * c, h_ref.shape[-1]), wqkv_ref[...],
                      preferred_element_type=F32)
    yield
    q, k, v, beta, gc, gc_row, gc_last = [], [], [], [], [], [], []
    own_rows = lambda t, bb: pltpu.roll(t, (c - bb * l_valid) % c, axis=0) if bb else t
    for bb in range(nb):
        if project:
            xbuf[bb, SUBLANES:SUBLANES + c, :] = raw[bb * c:(bb + 1) * c]
        else:
            xbuf[bb, SUBLANES:SUBLANES + c, :] = jnp.where(row < l_valid, own_rows(qkv_ref[...], bb), 0.0)
        yq, yk, yv = (_conv_silu_qkv(xbuf.at[bb], cw_ref, g, c) for g in range(3))
        if project:
            tail_ref[bb] = xbuf[bb, c:c + SUBLANES, :]
        else:
            last = SUBLANES + l_valid
            ctail_ref[bb] = xbuf[bb, last - (CONV_K - 1):last, :]
        xbuf[bb, 0:SUBLANES, :] = xbuf[bb, c:c + SUBLANES, :]
        sm = sm_ref[bb] if project else own_rows(sm_ref[...], bb)
        beta_t = _sigmoid(sm) * valid
        g_t = -jnp.exp(alog_ref[...]) * _softplus(sm + dtb_ref[...]) * valid
        gc_t = _fdot(tri_incl.astype(F32), g_t)
        gc_sq = jnp.concatenate([gc_t, jnp.zeros((pad_rows, LANES), F32)], axis=0) if pad_rows else gc_t
        gc_tr = gc_sq.T
        for h in range(N_HEADS):
            q.append(_l2_normalize(yq[:, sl(0, h)], HEAD_DIM ** -0.5))
            k.append(_l2_normalize(yk[:, sl(0, h)], 1.0) * valid)
            v.append(yv[:, sl(0, h)])
            beta.append(jnp.broadcast_to(beta_t[:, SM_BETA + h:SM_BETA + h + 1], (c, HEAD_DIM)))
            gc.append(jnp.broadcast_to(gc_t[:, SM_DECAY + h:SM_DECAY + h + 1], (c, HEAD_DIM)))
            gc_row.append(gc_tr[SM_DECAY + h:SM_DECAY + h + 1, 0:c])
            gc_last.append(jnp.broadcast_to(gc_t[c - 1:c, SM_DECAY + h:SM_DECAY + h + 1], (1, HEAD_DIM)))

    chains = range(nb * N_HEADS)
    decay = [jnp.where(tri_incl, jnp.exp(jnp.where(tri_incl, gc[i][:, 0:c] - gc_row[i], 0.0)), 0.0)
             for i in chains]
    kb = [k[i] * beta[i] for i in chains]
    kkqk = [_bdot_nt(jnp.concatenate([kb[i], q[i]], axis=0), k[i]) for i in chains]
    qk = [kkqk[i][c:2 * c] * decay[i] for i in chains]
    yield
    neg_a = [-jnp.where(tri_strict, kkqk[i][0:c] * decay[i], 0.0) for i in chains]
    base = min(INV_BASE, c)
    blk_r, blk_c = _iota2((c, c), 0), _iota2((c, c), 1)
    same = lambda size: (blk_r // size) == (blk_c // size)
    diag = [jnp.where(same(base), neg_a[i], 0.0) for i in chains] if base < c else neg_a
    t_inv = [eye + diag[i] for i in chains]
    pw = [_bdot(diag[i], diag[i]) for i in chains]
    yield
    n_sq = int(math.log2(base))
    for j in range(1, n_sq):
        if j < n_sq - 1:
            both = [_bdot(jnp.concatenate([t_inv[i], pw[i]], axis=0), pw[i]) for i in chains]
            t_inv = [t_inv[i] + both[i][0:c] for i in chains]
            pw = [both[i][c:2 * c] for i in chains]
        else:
            t_inv = [t_inv[i] + _bdot(t_inv[i], pw[i]) for i in chains]
        yield
    size = base
    while size < c:
        off = [jnp.where(same(2 * size) & ~same(size), neg_a[i], 0.0) for i in chains]
        right = [_bdot(off[i], t_inv[i]) for i in chains]
        yield
        t_inv = [t_inv[i] + _bdot(t_inv[i], right[i]) for i in chains]
        yield
        size *= 2
    egc = [jnp.exp(gc[i]) for i in chains]
    sol = [_bdot(t_inv[i], jnp.concatenate([v[i] * beta[i], kb[i] * egc[i]], axis=-1)) for i in chains]
    yield
    s = [s_scr[i // N_HEADS, i % N_HEADS] for i in chains]
    ws = [_bdot(jnp.concatenate([sol[i][:, HEAD_DIM:2 * HEAD_DIM], q[i] * egc[i]], axis=0), s[i])
          for i in chains]
    yield
    v_new = [sol[i][:, 0:HEAD_DIM] - ws[i][0:c] for i in chains]
    o = [ws[i][c:2 * c] + _bdot(qk[i], v_new[i]) for i in chains]
    k_dec = [k[i] * jnp.exp(gc_last[i] - gc[i]) for i in chains]
    s_new = [s[i] * jnp.exp(gc_last[i]) + _bdot_tn(k_dec[i], v_new[i]) for i in chains]
    yield
    o_tok = [None] * N_HEADS
    for i in chains:
        bb, h = i // N_HEADS, i % N_HEADS
        s_scr[bb, h] = s_new[i]
        oh = o[i] * lax.rsqrt(jnp.mean(o[i] * o[i], axis=-1, keepdims=True) + NORM_EPS) * onw_ref[...]
        if project:
            z = zg_ref[bb, :, sl(0, h)]
            og_ref[bb, :, sl(0, h)] = (oh * (z * _sigmoid(z))).astype(og_ref.dtype)
        else:
            mine = (row >= bb * l_valid) & (row < (bb + 1) * l_valid)
            placed = pltpu.roll(oh, bb * l_valid, axis=0) if bb else oh
            o_tok[h] = jnp.where(mine, placed, o_tok[h]) if bb else placed
    if not project:
        for h in range(N_HEADS):
            z = zg_ref[:, sl(0, h)]
            og_ref[:, sl(0, h)] = (o_tok[h] * (z * _sigmoid(z))).astype(og_ref.dtype)

    @pl.when(ci == n_c - 1)
    def _():
        sout_ref[...] = s_scr[...]


NEG_BIG = -1e30


def _forget_columns(f_tile, h, rows, for_keys):
    f = jnp.broadcast_to(f_tile[:, SM_FORGET + h:SM_FORGET + h + 1], (rows, LANES))
    f1, f2, f3 = (t.astype(F32) for t in _split3(-f if for_keys else f))
    lane = _iota2((rows, LANES), 1)
    base = 3 if for_keys else 0
    ones = ((lane >= 3 - base) & (lane < 6 - base)).astype(F32)
    cols = jnp.where(lane == base, f1, jnp.where(lane == base + 1, f2, jnp.where(lane == base + 2, f3, ones)))
    return cols.astype(BF16)


ROW_GROUP = 32


def _fox_prompt_kernel(q_ref, k_ref, v_ref, fcol_ref, zf_ref, og_ref, x_ref, wo_ref, fnw_ref, y_ref,
                       kx_ref, qa_ref, s_ref, p_ref, acc_ref, m_ref, a_ref, *, tq, l):
    qi = pl.program_id(1)
    heads = range(N_HEADS)
    sl = lambda h: slice(h * HEAD_DIM, (h + 1) * HEAD_DIM)
    nt = (((1,), (1,)), ((), ()))

    @pl.when(qi == 0)
    def _():
        for r in range(l // tq):
            for h in heads:
                kx_ref[r * tq:(r + 1) * tq, sl(h)] = _forget_columns(fcol_ref[r * tq:(r + 1) * tq, :], h, tq, True)

    f_q = fcol_ref[pl.ds(pl.multiple_of(qi * tq, tq), tq), :]
    for h in heads:
        qa_ref[h, :, 0:HEAD_DIM] = q_ref[:, sl(h)]
        qa_ref[h, :, HEAD_DIM:2 * HEAD_DIM] = _forget_columns(f_q, h, tq, False)
    acc_ref[...] = jnp.zeros(acc_ref.shape, F32)
    m_ref[...] = jnp.full(m_ref.shape, NEG_BIG, F32)
    ones = jnp.ones((tq, HEAD_DIM), BF16)

    half = tq // 2

    def block(ki, masked):
        start = pl.multiple_of(ki * tq, tq)
        pieces = [(slice(0, half), half), (slice(half, tq), tq)] if masked else [(slice(0, tq), tq)]
        for h in heads:
            for qr, nk in pieces:
                keys = pl.ds(start, nk)
                ka = jnp.concatenate([k_ref[keys, sl(h)], kx_ref[keys, sl(h)]], axis=1)
                s_ref[h, qr, 0:nk] = lax.dot_general(qa_ref[h, qr, :], ka, nt, preferred_element_type=F32)
        for h in heads:
            for r in range(0, tq, ROW_GROUP):
                rg = slice(r, r + ROW_GROUP)
                nk = half if (masked and r < half) else tq
                s = s_ref[h, rg, 0:nk]
                if masked:
                    keep = _iota2((ROW_GROUP, nk), 1) <= _iota2((ROW_GROUP, nk), 0) + r
                    s = jnp.where(keep, s, NEG_BIG)
                m_old = m_ref[h, rg, :]
                m_new = jnp.maximum(m_old, jnp.max(s, axis=-1, keepdims=True))
                a_ref[h, rg, :] = jnp.exp(m_old - m_new)
                m_ref[h, rg, :] = m_new
                p_ref[h, rg, 0:nk] = jnp.exp(s - jnp.concatenate([m_new] * (nk // LANES), axis=1)).astype(BF16)
        for h in heads:
            alpha = a_ref[h]
            for qr, nk in pieces:
                keys = pl.ds(start, nk)
                pv = jnp.dot(p_ref[h, qr, 0:nk], jnp.concatenate([v_ref[keys, sl(h)], ones[0:nk]], axis=1),
                             preferred_element_type=F32)
                acc_ref[h, qr, :] = acc_ref[h, qr, :] * jnp.concatenate([alpha[qr], alpha[qr]], axis=1) + pv

    def body(ki, carry):
        block(ki, False)
        return carry

    lax.fori_loop(0, qi, body, 0)
    block(qi, True)
    gated = []
    for h in heads:
        z = zf_ref[:, sl(h)]
        o = acc_ref[h, :, 0:HEAD_DIM] / acc_ref[h, :, HEAD_DIM:2 * HEAD_DIM]
        gated.append((o * (z * _sigmoid(z))).astype(BF16))
    mixed = jnp.dot(og_ref[...], wo_ref[0:GROUP_W, :], preferred_element_type=F32)
    mixed = mixed + jnp.dot(jnp.concatenate(gated, axis=1), wo_ref[GROUP_W:2 * GROUP_W, :],
                            preferred_element_type=F32)
    y = x_ref[...] + mixed
    var = jnp.mean(y * y, axis=-1, keepdims=True)
    y_ref[...] = y * lax.rsqrt(var + NORM_EPS) * fnw_ref[...]


def _fox_prompt_call(qf, kf, vf, fcol, zf, og, x, w_out, fnw, tq):
    b, l, _ = qf.shape
    d = x.shape[-1]
    kern = functools.partial(_fox_prompt_kernel, tq=tq, l=l)
    qblk = lambda w: pl.BlockSpec((None, tq, w), lambda bi, qi: (bi, qi, 0))
    seq = lambda w: pl.BlockSpec((None, l, w), lambda bi, qi: (bi, 0, 0))
    const = lambda shape: pl.BlockSpec(shape, lambda bi, qi: (0,) * len(shape))
    return pl.pallas_call(
        kern,
        grid=(b, l // tq),
        in_specs=[qblk(GROUP_W), seq(GROUP_W), seq(GROUP_W), seq(LANES), qblk(GROUP_W),
                  qblk(GROUP_W), qblk(d), const((2 * GROUP_W, d)), const((1, d))],
        out_specs=qblk(d),
        out_shape=jax.ShapeDtypeStruct((b, l, d), F32),
        scratch_shapes=[pltpu.VMEM((l, GROUP_W), BF16),
                        pltpu.VMEM((N_HEADS, tq, 2 * HEAD_DIM), BF16),
                        pltpu.VMEM((N_HEADS, tq, tq), F32),
                        pltpu.VMEM((N_HEADS, tq, tq), BF16),
                        pltpu.VMEM((N_HEADS, tq, 2 * HEAD_DIM), F32),
                        pltpu.VMEM((N_HEADS, tq, LANES), F32),
                        pltpu.VMEM((N_HEADS, tq, LANES), F32)],
        compiler_params=pltpu.CompilerParams(dimension_semantics=("arbitrary", "arbitrary"),
                                             vmem_limit_bytes=VMEM_LIMIT),
        name="fox_prompt",
    )(qf, kf, vf, fcol, zf, og, x, w_out, fnw)


def _out_kernel(og_ref, of_ref, x_ref, w_ref, fnw_ref, y_ref):
    o = jnp.dot(og_ref[...].astype(BF16), w_ref[0:GROUP_W, :], preferred_element_type=F32)
    o = o + jnp.dot(of_ref[...], w_ref[GROUP_W:2 * GROUP_W, :], preferred_element_type=F32)
    y = x_ref[...] + o
    var = jnp.mean(y * y, axis=-1, keepdims=True)
    y_ref[...] = y * lax.rsqrt(var + NORM_EPS) * fnw_ref[...]


def _out_call(og, of, x2d, w_out, fnw, tm):
    t, d = x2d.shape
    return pl.pallas_call(
        _out_kernel,
        grid=(t // tm,),
        in_specs=[pl.BlockSpec((tm, GROUP_W), lambda i: (i, 0)),
                  pl.BlockSpec((tm, GROUP_W), lambda i: (i, 0)),
                  pl.BlockSpec((tm, d), lambda i: (i, 0)),
                  pl.BlockSpec((2 * GROUP_W, d), lambda i: (0, 0)),
                  pl.BlockSpec((1, d), lambda i: (0, 0))],
        out_specs=pl.BlockSpec((tm, d), lambda i: (i, 0)),
        out_shape=jax.ShapeDtypeStruct((t, d), F32),
        compiler_params=pltpu.CompilerParams(dimension_semantics=("arbitrary",),
                                             vmem_limit_bytes=VMEM_LIMIT),
        name="out_proj",
    )(og, of, x2d, w_out, fnw)


def _page_copies(pt_ref, kc_ref, vc_ref, lc_ref, kbuf, vbuf, lbuf, sems, step, slot, n_pages, pg, rows):
    copies = []
    for r in range(rows):
        for p in range(n_pages):
            pid = pt_ref[step * rows + r, p]
            copies.append((pltpu.make_async_copy(kc_ref.at[pid], kbuf.at[slot, r, pl.ds(p * pg, pg)],
                                                 sems.at[0, slot]), 0))
            copies.append((pltpu.make_async_copy(vc_ref.at[pid], vbuf.at[slot, r, pl.ds(p * pg, pg)],
                                                 sems.at[1, slot]), 1))
            copies.append((pltpu.make_async_copy(lc_ref.at[pid], lbuf.at[slot, r, :, p, :], sems.at[2, slot]),
                           p % 2))
    return copies


def _split3(x):
    x1 = x.astype(BF16)
    r1 = x - x1.astype(F32)
    x2 = r1.astype(BF16)
    x3 = (r1 - x2.astype(F32)).astype(BF16)
    return x1, x2, x3


def _fox_decode_phases(pt_ref, q_ref, kn_ref, vn_ref, zf_ref, sm_ref, fbrow_ref, cums_ref,
                       kc_ref, vc_ref, lc_ref, o_ref, logf_ref, kbuf, vbuf, lbuf, sems,
                       *, n_pages, pg, l_new, rows):
    step = pl.program_id(0)
    n_steps = pl.num_programs(0)
    slot = step % 2
    copies = functools.partial(_page_copies, pt_ref, kc_ref, vc_ref, lc_ref, kbuf, vbuf, lbuf, sems,
                               n_pages=n_pages, pg=pg, rows=rows)
    nr = l_new * N_HEADS

    @pl.when(step == 0)
    def _():
        for cp, prio in copies(step=step, slot=slot):
            cp.start(priority=prio)

    @pl.when(step + 1 < n_steps)
    def _():
        for cp, prio in copies(step=step + 1, slot=1 - slot):
            cp.start(priority=prio)

    for cp, _ in copies(step=step, slot=slot):
        cp.wait()
    yield

    scale = HEAD_DIM ** -0.5
    earlier = (_iota2((n_pages, n_pages), 0) > _iota2((n_pages, n_pages), 1)).astype(F32)
    tok_valid = (_iota2((SUBLANES, 1), 0) < l_new).astype(F32)
    r_tok = _iota2((nr, SUBLANES), 0) // N_HEADS
    f_past, f_tot_row, csum, s_all, s_new = [], [], [], [], []
    for r in range(rows):
        qrows = slice(r * nr, (r + 1) * nr)
        res = jnp.zeros((3 * n_pages, 2 * pg), F32)
        for h in range(N_HEADS):
            res = res + jnp.dot(jnp.concatenate(_split3(lbuf[slot, r, h]), axis=0), cums_ref[h],
                                preferred_element_type=F32)
        res = res[0:n_pages] + res[n_pages:2 * n_pages] + res[2 * n_pages:3 * n_pages]
        within, tot = res[:, 0:pg], res[:, pg:2 * pg]
        carry = _fdot(earlier, tot)
        f_past.append(within + carry)
        f_tot_row.append(carry[n_pages - 1:n_pages, :] + tot[n_pages - 1:n_pages, :])
        sm_r = pltpu.roll(sm_ref[...], (SUBLANES - r * l_new) % SUBLANES, axis=0) if r else sm_ref[...]
        lf_col = _log_sigmoid(sm_r + fbrow_ref[...]) * tok_valid
        logf_ref[r] = lf_col
        csum.append(_fdot((_iota2((nr, SUBLANES), 1) <= r_tok).astype(F32), lf_col))
        q = q_ref[qrows, :].astype(BF16)
        s_all.append(lax.dot_general(q, kbuf[slot, r].astype(BF16), (((1,), (1,)), ((), ())),
                                     preferred_element_type=F32))
        s_new.append(lax.dot_general(q, kn_ref[qrows, :].astype(BF16), (((1,), (1,)), ((), ())),
                                     preferred_element_type=F32))
    yield
    own_lane = _iota2((nr, LANES), 1) == SM_FORGET + _iota2((nr, LANES), 0) % N_HEADS
    eye = _iota2((nr, nr), 0) == _iota2((nr, nr), 1)
    same_head = (_iota2((nr, pg), 1) % N_HEADS) == (_iota2((nr, pg), 0) % N_HEADS)
    rr, cc = _iota2((nr, nr), 0), _iota2((nr, nr), 1)
    new_ok = (rr % N_HEADS == cc % N_HEADS) & (cc // N_HEADS <= rr // N_HEADS)
    for r in range(rows):
        qrows = slice(r * nr, (r + 1) * nr)
        fq_new = jnp.sum(jnp.where(own_lane, csum[r], 0.0), axis=-1, keepdims=True)
        f_tot_col = jnp.sum(jnp.where(eye, jnp.broadcast_to(f_tot_row[r][:, 0:nr], (nr, nr)), 0.0),
                            axis=-1, keepdims=True)
        fq = fq_new + f_tot_col
        fq_row = jnp.sum(jnp.where(eye, jnp.broadcast_to(fq, (nr, nr)), 0.0), axis=0, keepdims=True)
        sp = [jnp.where(same_head, s_all[r][:, p * pg:(p + 1) * pg] * scale + (fq - f_past[r][p:p + 1, :]),
                        NEG_BIG) for p in range(n_pages)]
        sn = jnp.where(new_ok, s_new[r] * scale + (fq - fq_row), NEG_BIG)
        m_el = sp[0]
        for p in range(1, n_pages):
            m_el = jnp.maximum(m_el, sp[p])
        m = jnp.maximum(jnp.max(m_el, axis=-1, keepdims=True), jnp.max(sn, axis=-1, keepdims=True))
        pp = [jnp.exp(t - m) for t in sp]
        p_new = jnp.exp(sn - m)
        l_el = pp[0]
        for p in range(1, n_pages):
            l_el = l_el + pp[p]
        l = jnp.sum(l_el, axis=-1, keepdims=True) + jnp.sum(p_new, axis=-1, keepdims=True)
        p_all = jnp.concatenate([t.astype(BF16) for t in pp], axis=-1)
        acc = jnp.dot(p_all, vbuf[slot, r].astype(BF16), preferred_element_type=F32)
        acc = acc + jnp.dot(p_new.astype(BF16), vn_ref[qrows, :].astype(BF16), preferred_element_type=F32)
        z = zf_ref[qrows, :]
        o_ref[qrows, :] = ((acc / l) * (z * _sigmoid(z))).astype(o_ref.dtype)


def _head_cumsum_matrix(page):
    t = jnp.arange(page)[None, :, None]
    j = jnp.arange(page * N_HEADS)[None, None, :]
    h = jnp.arange(N_HEADS)[:, None, None]
    own = (j % N_HEADS) == h
    c = own & (t <= j // N_HEADS)
    b = jnp.broadcast_to(own, c.shape)
    return jnp.concatenate([c, b], axis=2).astype(BF16)


DECODE_ROWS_PER_STEP = 2
N_DECODE_INPUTS, N_DECODE_OUTPUTS, N_DECODE_SCRATCH = 10, 2, 4
N_GDN_INPUTS, N_GDN_OUTPUTS, N_GDN_SCRATCH = 8, 3, 2
N_SAMPLE_GDN_INPUTS, N_SAMPLE_GDN_OUTPUTS = 4, 3


def _decode_gdn_kernel(pt_ref, *refs, decode_kw, gdn_kw, sample_kw, n_chunks):
    take = lambda n: (refs[:n], refs[n:])
    dec_in, refs = take(N_DECODE_INPUTS)
    gdn_in, refs = take(N_GDN_INPUTS)
    smp_in, refs = take(N_SAMPLE_GDN_INPUTS)
    dec_out, refs = take(N_DECODE_OUTPUTS)
    gdn_out, refs = take(N_GDN_OUTPUTS)
    smp_out, refs = take(N_SAMPLE_GDN_OUTPUTS)
    dec_scr, refs = take(N_DECODE_SCRATCH)
    gdn_scr, smp_scr = take(N_GDN_SCRATCH)
    decode = _fox_decode_phases(pt_ref, *dec_in, *dec_out, *dec_scr, **decode_kw)
    next(decode)
    prompt = _gdn_stages(*gdn_in, *gdn_out, *gdn_scr, **gdn_kw,
                         chunk_of_step=(pl.program_id(0) % n_chunks, n_chunks))
    qkv_s, zg_s, s0_s, c0_s = smp_in
    sm_s = dec_in[4]
    sample = _gdn_stages(qkv_s, zg_s, sm_s, *gdn_in[4:8], s0_s, c0_s, *smp_out, *smp_scr, **sample_kw,
                         chunk_of_step=(jnp.int32(0), 1))
    programs = [prompt, sample, decode]
    while programs:
        for prog in list(programs):
            if next(prog, StopIteration) is StopIteration:
                programs.remove(prog)


def _decode_gdn_call(page_table, q4, kn4, vn4, zf4, sm_s, fb_row, kcache, vcache, lcache, l_new, gdn):
    b, n_pages = page_table.shape
    pg = kcache.shape[1]
    page = lcache.shape[2]
    nr = l_new * N_HEADS
    assert nr % (2 * SUBLANES) == 0 and l_new <= SUBLANES
    rps = DECODE_ROWS_PER_STEP
    assert b % rps == 0
    n_steps = b // rps
    decode_kw = dict(n_pages=n_pages, pg=pg, l_new=l_new, rows=rps)
    rows = pl.BlockSpec((rps * nr, HEAD_DIM), lambda i, pt: (i, 0))
    tok = pl.BlockSpec((rps, SUBLANES, LANES), lambda i, pt: (i, 0, 0))
    const = lambda shape: pl.BlockSpec(shape, lambda i, pt: (0,) * len(shape))
    any_spec = pl.BlockSpec(memory_space=pl.ANY)
    stok = lambda w: pl.BlockSpec((rps * l_new, w), lambda i, pt: (i, 0))
    operands = [page_table, q4, kn4, vn4, zf4, sm_s, fb_row, _head_cumsum_matrix(page), kcache, vcache, lcache]
    in_specs = [rows, rows, rows, rows, stok(LANES), const((1, LANES)), const((N_HEADS, page, 2 * pg)),
                any_spec, any_spec, any_spec]
    out_specs = [rows, tok]
    out_shape = [jax.ShapeDtypeStruct((b * nr, HEAD_DIM), BF16), jax.ShapeDtypeStruct((b, SUBLANES, LANES), F32)]
    scratch = [pltpu.VMEM((2, rps, n_pages * pg, HEAD_DIM), F32),
               pltpu.VMEM((2, rps, n_pages * pg, HEAD_DIM), F32),
               pltpu.VMEM((2, rps, N_HEADS, n_pages, page), F32),
               pltpu.SemaphoreType.DMA((3, 2))]
    h3, w_qkv, zg, sm, conv_w, alog_row, dtb_row, onw, c, (qkv_s, zg_s, s0_s, c0_s) = gdn
    assert rps * l_new == SUBLANES
    bp, l, d = h3.shape
    n_c = l // c
    nb = bp * n_c // n_steps
    assert nb >= 1 and (bp // nb) * n_c == n_steps
    blk = lambda w: pl.BlockSpec((nb, c, w), lambda i, pt: (i // n_c, i % n_c, 0))
    state = pl.BlockSpec((nb, N_HEADS, HEAD_DIM, HEAD_DIM), lambda i, pt: (i // n_c, 0, 0, 0))
    rows8 = pl.BlockSpec((nb, SUBLANES, CONV_DIM), lambda i, pt: (i // n_c, 0, 0))
    sstate = pl.BlockSpec((rps, N_HEADS, HEAD_DIM, HEAD_DIM), lambda i, pt: (i, 0, 0, 0))
    sconv = pl.BlockSpec((rps, CONV_K - 1, CONV_DIM), lambda i, pt: (i, 0, 0))
    operands += [h3, w_qkv, zg, sm, conv_w, alog_row, dtb_row, onw, qkv_s, zg_s, s0_s, c0_s]
    in_specs += [blk(d), const((d, CONV_DIM)), blk(GROUP_W), blk(LANES), const((CONV_K, CONV_DIM)),
                 const((1, LANES)), const((1, LANES)), const((1, HEAD_DIM)),
                 stok(CONV_DIM), stok(GROUP_W), sstate, sconv]
    out_specs += [blk(GROUP_W), state, rows8, stok(GROUP_W), sstate, sconv]
    out_shape += [jax.ShapeDtypeStruct((bp, l, GROUP_W), BF16),
                  jax.ShapeDtypeStruct((bp, N_HEADS, HEAD_DIM, HEAD_DIM), F32),
                  jax.ShapeDtypeStruct((bp, SUBLANES, CONV_DIM), F32),
                  jax.ShapeDtypeStruct((b * l_new, GROUP_W), F32),
                  jax.ShapeDtypeStruct((b, N_HEADS, HEAD_DIM, HEAD_DIM), F32),
                  jax.ShapeDtypeStruct((b, CONV_K - 1, CONV_DIM), F32)]
    scratch += [pltpu.VMEM((nb, c + SUBLANES, CONV_DIM), F32),
                pltpu.VMEM((nb, N_HEADS, HEAD_DIM, HEAD_DIM), F32),
                pltpu.VMEM((rps, 2 * SUBLANES, CONV_DIM), F32),
                pltpu.VMEM((rps, N_HEADS, HEAD_DIM, HEAD_DIM), F32)]
    kern = functools.partial(_decode_gdn_kernel, decode_kw=decode_kw, n_chunks=n_c,
                             gdn_kw=dict(c=c, l_valid=l, nb=nb, project=True),
                             sample_kw=dict(c=SUBLANES, l_valid=l_new, nb=rps, project=False))
    grid_spec = pltpu.PrefetchScalarGridSpec(num_scalar_prefetch=1, grid=(n_steps,), in_specs=in_specs,
                                             out_specs=out_specs, scratch_shapes=scratch)
    return pl.pallas_call(
        kern,
        grid_spec=grid_spec,
        out_shape=out_shape,
        compiler_params=pltpu.CompilerParams(dimension_semantics=("arbitrary",),
                                             vmem_limit_bytes=VMEM_LIMIT),
        name="decode_gdn",
    )(*operands)


def _gate_row(vals, offset):
    return jnp.zeros((1, LANES), F32).at[0, offset:offset + N_HEADS].set(vals.astype(F32))


def kernel(x_prompt, x_sample, cache_fox_k, cache_fox_v, cache_fox_logf, page_table, state_gdn_ssm,
           state_gdn_conv, w_in, gdn_conv_w, gdn_a_log, gdn_dt_bias, gdn_out_norm_w, fox_f_bias, w_out,
           norm_w, final_norm_w):
    bp, lp, d = x_prompt.shape
    bs, ls, _ = x_sample.shape
    depth = w_in.shape[0]
    assert depth == 1, "single-layer trunk"
    n_pool, page = cache_fox_k.shape[1], cache_fox_k.shape[2]

    w_big, w_qkv = _pack_w_call(w_in[0].T)
    w_o = w_out[0].astype(BF16)
    nw = norm_w[0].reshape(1, d)
    fnw = final_norm_w.reshape(1, d)
    conv_w = gdn_conv_w[0]
    alog_row = _gate_row(gdn_a_log[0], SM_DECAY)
    dtb_row = _gate_row(gdn_dt_bias[0], SM_DECAY)
    fb_row = _gate_row(fox_f_bias[0], SM_FORGET)
    onw = gdn_out_norm_w[0].reshape(1, HEAD_DIM)

    xp2 = x_prompt.reshape(bp * lp, d)
    hp, zg, sm, fcol, logf_t, qf, kf, vf, zf, k4, v4 = _proj_call(xp2, nw, w_big, tm=ROWS_PER_STEP_PROMPT,
                                                                  sample=False, fb_row=fb_row, seq_len=lp)
    r3 = lambda t: t.reshape(bp, lp, t.shape[-1])

    xs2 = x_sample.reshape(bs * ls, d)
    qkv_s, zg_s, sm_s, q4_s, k4_s, v4_s, z4_s = _proj_call(xs2, nw, w_big, tm=ROWS_PER_STEP_SAMPLE, sample=True)
    kcache = cache_fox_k[0].reshape(n_pool, page * N_HEADS, HEAD_DIM)
    vcache = cache_fox_v[0].reshape(n_pool, page * N_HEADS, HEAD_DIM)
    lcache = cache_fox_logf[0].transpose(0, 2, 1)
    of_s, logf_s, og_p, ssm_p, tail, og_s, ssm_s, ctail_s = _decode_gdn_call(
        page_table, q4_s, k4_s, v4_s, z4_s, sm_s, fb_row, kcache, vcache, lcache, l_new=ls,
        gdn=(r3(hp), w_qkv, r3(zg), r3(sm), conv_w, alog_row, dtb_row, onw, GDN_CHUNK,
             (qkv_s, zg_s, state_gdn_ssm[0], state_gdn_conv[0])))
    of_s2 = of_s.reshape(bs * ls, GROUP_W)
    y_s = _out_call(og_s, of_s2, xs2, w_o, fnw, tm=ROWS_PER_STEP_SAMPLE)
    y_prompt = _fox_prompt_call(r3(qf), r3(kf), r3(vf), r3(fcol), r3(zf), og_p, x_prompt, w_o, fnw,
                                tq=ROWS_PER_STEP_PROMPT)

    k_prompt = k4.reshape(1, bp, lp, N_HEADS, HEAD_DIM)
    v_prompt = v4.reshape(1, bp, lp, N_HEADS, HEAD_DIM)
    logf_prompt = logf_t.transpose(0, 2, 1).reshape(1, bp, lp, N_HEADS)
    ssm_prompt = ssm_p.reshape(1, bp, N_HEADS, HEAD_DIM, HEAD_DIM)
    conv_prompt = tail[:, SUBLANES - (CONV_K - 1):, :].reshape(1, bp, CONV_K - 1, CONV_DIM)
    y_sample = y_s.reshape(bs, ls, d)
    k_sample = k4_s.reshape(1, bs, ls, N_HEADS, HEAD_DIM)
    v_sample = v4_s.reshape(1, bs, ls, N_HEADS, HEAD_DIM)
    logf_sample = logf_s[:, :ls, SM_FORGET:SM_FORGET + N_HEADS].reshape(1, bs, ls, N_HEADS)
    ssm_sample = ssm_s.reshape(1, bs, N_HEADS, HEAD_DIM, HEAD_DIM)
    conv_sample = ctail_s.reshape(1, bs, CONV_K - 1, CONV_DIM)

    return (y_prompt, y_sample, k_prompt, v_prompt, logf_prompt, ssm_prompt, conv_prompt,
            k_sample, v_sample, logf_sample, ssm_sample, conv_sample)
```

```python
import functools
import math

import jax
import jax.numpy as jnp
from jax import lax
from jax.experimental import pallas as pl
from jax.experimental.pallas import tpu as pltpu

F32 = jnp.float32
BF16 = jnp.bfloat16

NORM_EPS = 1e-6
L2_EPS = 1e-6
HEAD_DIM = 128
N_HEADS = 4
GROUP_W = N_HEADS * HEAD_DIM
CONV_DIM = 3 * GROUP_W
CONV_K = 4
LANES = 128
SUBLANES = 8
GDN_CHUNK = 64
INV_BASE = 32
ROWS_PER_STEP_PROMPT = 512
ROWS_PER_STEP_SAMPLE = 256
SM_BETA = 0
SM_DECAY = 4
SM_FORGET = 8
VMEM_LIMIT = 56 * 1024 * 1024


def _sigmoid(x):
    return 1.0 / (1.0 + jnp.exp(-x))


def _softplus(x):
    return jnp.maximum(x, 0.0) + jnp.log(1.0 + jnp.exp(-jnp.abs(x)))


def _log_sigmoid(x):
    return -_softplus(-x)


def _bdot(a, b):
    return jnp.dot(a.astype(BF16), b.astype(BF16), preferred_element_type=F32)


def _bdot_nt(a, b):
    return lax.dot_general(a.astype(BF16), b.astype(BF16), (((1,), (1,)), ((), ())),
                           preferred_element_type=F32)


def _bdot_tn(a, b):
    return lax.dot_general(a.astype(BF16), b.astype(BF16), (((0,), (0,)), ((), ())),
                           preferred_element_type=F32)


def _fdot(a, b):
    return jnp.dot(a, b, preferred_element_type=F32, precision=lax.Precision.HIGHEST)


def _iota2(shape, dim):
    return lax.broadcasted_iota(jnp.int32, shape, dim)


W_QKV, W_ZG, W_QF, W_KF, W_VF, W_ZF, W_SM, W_END = 0, 1536, 2048, 2560, 3072, 3584, 4096, 4224
SRC_GATES_G, SRC_FOX, SRC_GATE_F, SRC_END = 2048, 2056, 4104, 4108


def _pack_w_kernel(w_ref, o_ref, qkv_ref):
    qkv_ref[...] = w_ref[W_QKV:W_ZG, :].T.astype(BF16)
    o_ref[W_QKV:W_QF, :] = w_ref[0:SRC_GATES_G, :].astype(BF16)
    o_ref[W_QF:W_SM, :] = w_ref[SRC_FOX:SRC_GATE_F, :].astype(BF16)
    n_gate = (SRC_FOX - SRC_GATES_G) + (SRC_END - SRC_GATE_F)
    gates = jnp.concatenate([w_ref[SRC_GATES_G:SRC_FOX, :], w_ref[SRC_GATE_F:SRC_END, :],
                             jnp.zeros((W_END - W_SM - n_gate, w_ref.shape[1]), F32)], axis=0)
    o_ref[W_SM:W_END, :] = gates.astype(BF16)


def _pack_w_call(w_t):
    return pl.pallas_call(
        _pack_w_kernel,
        out_shape=[jax.ShapeDtypeStruct((W_END, w_t.shape[1]), BF16),
                   jax.ShapeDtypeStruct((w_t.shape[1], CONV_DIM), BF16)],
        compiler_params=pltpu.CompilerParams(vmem_limit_bytes=VMEM_LIMIT),
        name="pack_w",
    )(w_t)


def _store_head_rows(ref, val, tm):
    for h in range(N_HEADS):
        ref[pl.ds(h, tm, stride=N_HEADS), :] = val[:, h * HEAD_DIM:(h + 1) * HEAD_DIM].astype(ref.dtype)


def _conv_silu_qkv(xbuf, cw_ref, g, rows):
    cols = slice(g * GROUP_W, (g + 1) * GROUP_W)
    x = xbuf[0:rows + SUBLANES, cols]
    y = x[SUBLANES:] * cw_ref[CONV_K - 1:CONV_K, cols]
    for j in range(CONV_K - 1):
        shifted = pltpu.roll(x, CONV_K - 1 - j, axis=0)
        y = y + shifted[SUBLANES:] * cw_ref[j:j + 1, cols]
    return y * _sigmoid(y)


def _l2_normalize(t, scale):
    return t * (lax.rsqrt(jnp.sum(t * t, axis=-1, keepdims=True) + L2_EPS) * scale)


def _proj_kernel(x_ref, nw_ref, w_ref, *refs, tm, sample, seq_tiles):
    x = x_ref[...]
    var = jnp.mean(x * x, axis=-1, keepdims=True)
    h = (x * lax.rsqrt(var + NORM_EPS) * nw_ref[...]).astype(BF16)
    seg = lambda lo, hi: lax.dot_general(h, w_ref[lo:hi, :], (((1,), (1,)), ((), ())),
                                         preferred_element_type=F32)
    if sample:
        qkv_ref, zg_ref, sm_ref, q4_ref, k4_ref, v4_ref, z4_ref = refs
        qkv_ref[...] = seg(W_QKV, W_ZG)
        zg_ref[...] = seg(W_ZG, W_QF)
        sm_ref[...] = seg(W_SM, W_END)
        _store_head_rows(q4_ref, seg(W_QF, W_KF), tm)
        _store_head_rows(k4_ref, seg(W_KF, W_VF), tm)
        _store_head_rows(v4_ref, seg(W_VF, W_ZF), tm)
        _store_head_rows(z4_ref, seg(W_ZF, W_SM), tm)
        return
    (fb_ref, h_ref, zg_ref, sm_ref, fcol_ref, logft_ref, qb_ref, kb_ref, vb_ref, zf_ref, k4_ref, v4_ref,
     carry_ref) = refs
    h_ref[...] = h

    @pl.when(pl.program_id(0) % seq_tiles == 0)
    def _():
        carry_ref[...] = jnp.zeros(carry_ref.shape, F32)

    sm = seg(W_SM, W_END)
    sm_ref[...] = sm
    zg_ref[...] = seg(W_ZG, W_QF)
    qb_ref[...] = (seg(W_QF, W_KF) * (HEAD_DIM ** -0.5)).astype(BF16)
    kf = seg(W_KF, W_VF)
    _store_head_rows(k4_ref, kf, tm)
    kb_ref[...] = kf.astype(BF16)
    vf = seg(W_VF, W_ZF)
    _store_head_rows(v4_ref, vf, tm)
    vb_ref[...] = vf.astype(BF16)
    zf_ref[...] = seg(W_ZF, W_SM)

    tri = (_iota2((LANES, LANES), 0) >= _iota2((LANES, LANES), 1)).astype(BF16)
    blocks = [slice(i * LANES, (i + 1) * LANES) for i in range(tm // LANES)]
    lf = [_log_sigmoid(sm[blk] + fb_ref[...]) for blk in blocks]
    parts = [jnp.dot(tri, jnp.concatenate(_split3(t), axis=1), preferred_element_type=F32) for t in lf]
    within = [p[:, 0:LANES] + p[:, LANES:2 * LANES] + p[:, 2 * LANES:3 * LANES] for p in parts]
    carry = carry_ref[0:1, :]
    for i, blk in enumerate(blocks):
        fcol_ref[blk, :] = within[i] + carry
        carry = carry + within[i][LANES - 1:LANES, :]
        logft_ref[:, blk] = lf[i].T[SM_FORGET:SM_FORGET + N_HEADS, :]
    carry_ref[0:1, :] = carry


def _proj_call(x2d, norm_w, w_big, tm, sample, fb_row=None, seq_len=None):
    t, d = x2d.shape
    n = w_big.shape[0]
    wide = lambda w, dt: (jax.ShapeDtypeStruct((t, w), dt), pl.BlockSpec((tm, w), lambda i: (i, 0)))
    rows4 = (jax.ShapeDtypeStruct((t * N_HEADS, HEAD_DIM), F32),
             pl.BlockSpec((tm * N_HEADS, HEAD_DIM), lambda i: (i, 0)))
    operands = [x2d, norm_w, w_big]
    in_specs = [pl.BlockSpec((tm, d), lambda i: (i, 0)),
                pl.BlockSpec((1, d), lambda i: (0, 0)),
                pl.BlockSpec((n, d), lambda i: (0, 0))]
    scratch = []
    seq_tiles = 1
    if sample:
        outs = [wide(CONV_DIM, F32), wide(GROUP_W, F32), wide(LANES, F32), rows4, rows4, rows4, rows4]
    else:
        seq_tiles = seq_len // tm
        logft = (jax.ShapeDtypeStruct((t // seq_len, N_HEADS, seq_len), F32),
                 pl.BlockSpec((None, N_HEADS, tm), lambda i: (i // seq_tiles, 0, i % seq_tiles)))
        outs = [wide(d, BF16), wide(GROUP_W, F32), wide(LANES, F32), wide(LANES, F32), logft,
                wide(GROUP_W, BF16), wide(GROUP_W, BF16), wide(GROUP_W, BF16), wide(GROUP_W, F32), rows4, rows4]
        operands.append(fb_row)
        in_specs.append(pl.BlockSpec((1, LANES), lambda i: (0, 0)))
        scratch = [pltpu.VMEM((SUBLANES, LANES), F32)]
    out_shape = [o[0] for o in outs]
    out_specs = [o[1] for o in outs]
    return pl.pallas_call(
        functools.partial(_proj_kernel, tm=tm, sample=sample, seq_tiles=seq_tiles),
        grid=(t // tm,),
        in_specs=in_specs,
        out_specs=out_specs,
        out_shape=out_shape,
        scratch_shapes=scratch,
        compiler_params=pltpu.CompilerParams(dimension_semantics=("arbitrary",),
                                             vmem_limit_bytes=VMEM_LIMIT),
        name="proj",
    )(*operands)


def _gdn_stages(*refs, c, l_valid, nb, project, chunk_of_step):
    if project:
        (h_ref, wqkv_ref, zg_ref, sm_ref, cw_ref, alog_ref, dtb_ref, onw_ref,
         og_ref, sout_ref, tail_ref, xbuf, s_scr) = refs
    else:
        (qkv_ref, zg_ref, sm_ref, cw_ref, alog_ref, dtb_ref, onw_ref, s0_ref, c0_ref,
         og_ref, sout_ref, ctail_ref, xbuf, s_scr) = refs
    ci, n_c = chunk_of_step

    @pl.when(ci == 0)
    def _():
        if project:
            xbuf[:, 0:SUBLANES, :] = jnp.zeros((nb, SUBLANES, CONV_DIM), F32)
            s_scr[...] = jnp.zeros(s_scr.shape, F32)
        else:
            xbuf[:, SUBLANES - (CONV_K - 1):SUBLANES, :] = c0_ref[...]
            s_scr[...] = s0_ref[...]

    row = _iota2((c, 1), 0) + ci * c
    valid = jnp.broadcast_to((row < l_valid).astype(F32), (c, LANES))
    tri_incl = (_iota2((c, c), 0) >= _iota2((c, c), 1))
    tri_strict = (_iota2((c, c), 0) > _iota2((c, c), 1))
    eye = (_iota2((c, c), 0) == _iota2((c, c), 1)).astype(F32)
    pad_rows = LANES - c
    sl = lambda base, h: slice(base + h * HEAD_DIM, base + (h + 1) * HEAD_DIM)

    if project:
        raw = jnp.dot(h_ref[...].reshape(nb * c, h_ref.shape[-1]), wqkv_ref[...],
                      preferred_element_type=F32)
    yield
    q, k, v, beta, gc, gc_row, gc_last = [], [], [], [], [], [], []
    own_rows = lambda t, bb: pltpu.roll(t, (c - bb * l_valid) % c, axis=0) if bb else t
    for bb in range(nb):
        if project:
            xbuf[bb, SUBLANES:SUBLANES + c, :] = raw[bb * c:(bb + 1) * c]
        else:
            xbuf[bb, SUBLANES:SUBLANES + c, :] = jnp.where(row < l_valid, own_rows(qkv_ref[...], bb), 0.0)
        yq, yk, yv = (_conv_silu_qkv(xbuf.at[bb], cw_ref, g, c) for g in range(3))
        if project:
            tail_ref[bb] = xbuf[bb, c:c + SUBLANES, :]
        else:
            last = SUBLANES + l_valid
            ctail_ref[bb] = xbuf[bb, last - (CONV_K - 1):last, :]
        xbuf[bb, 0:SUBLANES, :] = xbuf[bb, c:c + SUBLANES, :]
        sm = sm_ref[bb] if project else own_rows(sm_ref[...], bb)
        beta_t = _sigmoid(sm) * valid
        g_t = -jnp.exp(alog_ref[...]) * _softplus(sm + dtb_ref[...]) * valid
        gc_t = _fdot(tri_incl.astype(F32), g_t)
        gc_sq = jnp.concatenate([gc_t, jnp.zeros((pad_rows, LANES), F32)], axis=0) if pad_rows else gc_t
        gc_tr = gc_sq.T
        for h in range(N_HEADS):
            q.append(_l2_normalize(yq[:, sl(0, h)], HEAD_DIM ** -0.5))
            k.append(_l2_normalize(yk[:, sl(0, h)], 1.0) * valid)
            v.append(yv[:, sl(0, h)])
            beta.append(jnp.broadcast_to(beta_t[:, SM_BETA + h:SM_BETA + h + 1], (c, HEAD_DIM)))
            gc.append(jnp.broadcast_to(gc_t[:, SM_DECAY + h:SM_DECAY + h + 1], (c, HEAD_DIM)))
            gc_row.append(gc_tr[SM_DECAY + h:SM_DECAY + h + 1, 0:c])
            gc_last.append(jnp.broadcast_to(gc_t[c - 1:c, SM_DECAY + h:SM_DECAY + h + 1], (1, HEAD_DIM)))

    chains = range(nb * N_HEADS)
    decay = [jnp.where(tri_incl, jnp.exp(jnp.where(tri_incl, gc[i][:, 0:c] - gc_row[i], 0.0)), 0.0)
             for i in chains]
    kb = [k[i] * beta[i] for i in chains]
    kkqk = [_bdot_nt(jnp.concatenate([kb[i], q[i]], axis=0), k[i]) for i in chains]
    qk = [kkqk[i][c:2 * c] * decay[i] for i in chains]
    yield
    neg_a = [-jnp.where(tri_strict, kkqk[i][0:c] * decay[i], 0.0) for i in chains]
    base = min(INV_BASE, c)
    blk_r, blk_c = _iota2((c, c), 0), _iota2((c, c), 1)
    same = lambda size: (blk_r // size) == (blk_c // size)
    diag = [jnp.where(same(base), neg_a[i], 0.0) for i in chains] if base < c else neg_a
    t_inv = [eye + diag[i] for i in chains]
    pw = [_bdot(diag[i], diag[i]) for i in chains]
    yield
    n_sq = int(math.log2(base))
    for j in range(1, n_sq):
        if j < n_sq - 1:
            both = [_bdot(jnp.concatenate([t_inv[i], pw[i]], axis=0), pw[i]) for i in chains]
            t_inv = [t_inv[i] + both[i][0:c] for i in chains]
            pw = [both[i][c:2 * c] for i in chains]
        else:
            t_inv = [t_inv[i] + _bdot(t_inv[i], pw[i]) for i in chains]
        yield
    size = base
    while size < c:
        off = [jnp.where(same(2 * size) & ~same(size), neg_a[i], 0.0) for i in chains]
        right = [_bdot(off[i], t_inv[i]) for i in chains]
        yield
        t_inv = [t_inv[i] + _bdot(t_inv[i], right[i]) for i in chains]
        yield
        size *= 2
    egc = [jnp.exp(gc[i]) for i in chains]
    sol = [_bdot(t_inv[i], jnp.concatenate([v[i] * beta[i], kb[i] * egc[i]], axis=-1)) for i in chains]
    yield
    s = [s_scr[i // N_HEADS, i % N_HEADS] for i in chains]
    ws = [_bdot(jnp.concatenate([sol[i][:, HEAD_DIM:2 * HEAD_DIM], q[i] * egc[i]], axis=0), s[i])
          for i in chains]
    yield
    v_new = [sol[i][:, 0:HEAD_DIM] - ws[i][0:c] for i in chains]
    o = [ws[i][c:2 * c] + _bdot(qk[i], v_new[i]) for i in chains]
    k_dec = [k[i] * jnp.exp(gc_last[i] - gc[i]) for i in chains]
    s_new = [s[i] * jnp.exp(gc_last[i]) + _bdot_tn(k_dec[i], v_new[i]) for i in chains]
    yield
    o_tok = [None] * N_HEADS
    for i in chains:
        bb, h = i // N_HEADS, i % N_HEADS
        s_scr[bb, h] = s_new[i]
        oh = o[i] * lax.rsqrt(jnp.mean(o[i] * o[i], axis=-1, keepdims=True) + NORM_EPS) * onw_ref[...]
        if project:
            z = zg_ref[bb, :, sl(0, h)]
            og_ref[bb, :, sl(0, h)] = (oh * (z * _sigmoid(z))).astype(og_ref.dtype)
        else:
            mine = (row >= bb * l_valid) & (row < (bb + 1) * l_valid)
            placed = pltpu.roll(oh, bb * l_valid, axis=0) if bb else oh
            o_tok[h] = jnp.where(mine, placed, o_tok[h]) if bb else placed
    if not project:
        for h in range(N_HEADS):
            z = zg_ref[:, sl(0, h)]
            og_ref[:, sl(0, h)] = (o_tok[h] * (z * _sigmoid(z))).astype(og_ref.dtype)

    @pl.when(ci == n_c - 1)
    def _():
        sout_ref[...] = s_scr[...]


NEG_BIG = -1e30


def _forget_columns(f_tile, h, rows, for_keys):
    f = jnp.broadcast_to(f_tile[:, SM_FORGET + h:SM_FORGET + h + 1], (rows, LANES))
    f1, f2, f3 = (t.astype(F32) for t in _split3(-f if for_keys else f))
    lane = _iota2((rows, LANES), 1)
    base = 3 if for_keys else 0
    ones = ((lane >= 3 - base) & (lane < 6 - base)).astype(F32)
    cols = jnp.where(lane == base, f1, jnp.where(lane == base + 1, f2, jnp.where(lane == base + 2, f3, ones)))
    return cols.astype(BF16)


ROW_GROUP = 32


def _fox_prompt_kernel(q_ref, k_ref, v_ref, fcol_ref, zf_ref, og_ref, x_ref, wo_ref, fnw_ref, y_ref,
                       kx_ref, qa_ref, s_ref, p_ref, acc_ref, m_ref, a_ref, *, tq, l):
    qi = pl.program_id(1)
    heads = range(N_HEADS)
    sl = lambda h: slice(h * HEAD_DIM, (h + 1) * HEAD_DIM)
    nt = (((1,), (1,)), ((), ()))

    @pl.when(qi == 0)
    def _():
        for r in range(l // tq):
            for h in heads:
                kx_ref[r * tq:(r + 1) * tq, sl(h)] = _forget_columns(fcol_ref[r * tq:(r + 1) * tq, :], h, tq, True)

    f_q = fcol_ref[pl.ds(pl.multiple_of(qi * tq, tq), tq), :]
    for h in heads:
        qa_ref[h, :, 0:HEAD_DIM] = q_ref[:, sl(h)]
        qa_ref[h, :, HEAD_DIM:2 * HEAD_DIM] = _forget_columns(f_q, h, tq, False)
    acc_ref[...] = jnp.zeros(acc_ref.shape, F32)
    m_ref[...] = jnp.full(m_ref.shape, NEG_BIG, F32)
    ones = jnp.ones((tq, HEAD_DIM), BF16)

    half = tq // 2

    def block(ki, masked):
        start = pl.multiple_of(ki * tq, tq)
        pieces = [(slice(0, half), half), (slice(half, tq), tq)] if masked else [(slice(0, tq), tq)]
        for h in heads:
            for qr, nk in pieces:
                keys = pl.ds(start, nk)
                ka = jnp.concatenate([k_ref[keys, sl(h)], kx_ref[keys, sl(h)]], axis=1)
                s_ref[h, qr, 0:nk] = lax.dot_general(qa_ref[h, qr, :], ka, nt, preferred_element_type=F32)
        for h in heads:
            for r in range(0, tq, ROW_GROUP):
                rg = slice(r, r + ROW_GROUP)
                nk = half if (masked and r < half) else tq
                s = s_ref[h, rg, 0:nk]
                if masked:
                    keep = _iota2((ROW_GROUP, nk), 1) <= _iota2((ROW_GROUP, nk), 0) + r
                    s = jnp.where(keep, s, NEG_BIG)
                m_old = m_ref[h, rg, :]
                m_new = jnp.maximum(m_old, jnp.max(s, axis=-1, keepdims=True))
                a_ref[h, rg, :] = jnp.exp(m_old - m_new)
                m_ref[h, rg, :] = m_new
                p_ref[h, rg, 0:nk] = jnp.exp(s - jnp.concatenate([m_new] * (nk // LANES), axis=1)).astype(BF16)
        for h in heads:
            alpha = a_ref[h]
            for qr, nk in pieces:
                keys = pl.ds(start, nk)
                pv = jnp.dot(p_ref[h, qr, 0:nk], jnp.concatenate([v_ref[keys, sl(h)], ones[0:nk]], axis=1),
                             preferred_element_type=F32)
                acc_ref[h, qr, :] = acc_ref[h, qr, :] * jnp.concatenate([alpha[qr], alpha[qr]], axis=1) + pv

    def body(ki, carry):
        block(ki, False)
        return carry

    lax.fori_loop(0, qi, body, 0)
    block(qi, True)
    gated = []
    for h in heads:
        z = zf_ref[:, sl(h)]
        o = acc_ref[h, :, 0:HEAD_DIM] / acc_ref[h, :, HEAD_DIM:2 * HEAD_DIM]
        gated.append((o * (z * _sigmoid(z))).astype(BF16))
    mixed = jnp.dot(og_ref[...], wo_ref[0:GROUP_W, :], preferred_element_type=F32)
    mixed = mixed + jnp.dot(jnp.concatenate(gated, axis=1), wo_ref[GROUP_W:2 * GROUP_W, :],
                            preferred_element_type=F32)
    y = x_ref[...] + mixed
    var = jnp.mean(y * y, axis=-1, keepdims=True)
    y_ref[...] = y * lax.rsqrt(var + NORM_EPS) * fnw_ref[...]


def _fox_prompt_call(qf, kf, vf, fcol, zf, og, x, w_out, fnw, tq):
    b, l, _ = qf.shape
    d = x.shape[-1]
    kern = functools.partial(_fox_prompt_kernel, tq=tq, l=l)
    qblk = lambda w: pl.BlockSpec((None, tq, w), lambda bi, qi: (bi, qi, 0))
    seq = lambda w: pl.BlockSpec((None, l, w), lambda bi, qi: (bi, 0, 0))
    const = lambda shape: pl.BlockSpec(shape, lambda bi, qi: (0,) * len(shape))
    return pl.pallas_call(
        kern,
        grid=(b, l // tq),
        in_specs=[qblk(GROUP_W), seq(GROUP_W), seq(GROUP_W), seq(LANES), qblk(GROUP_W),
                  qblk(GROUP_W), qblk(d), const((2 * GROUP_W, d)), const((1, d))],
        out_specs=qblk(d),
        out_shape=jax.ShapeDtypeStruct((b, l, d), F32),
        scratch_shapes=[pltpu.VMEM((l, GROUP_W), BF16),
                        pltpu.VMEM((N_HEADS, tq, 2 * HEAD_DIM), BF16),
                        pltpu.VMEM((N_HEADS, tq, tq), F32),
                        pltpu.VMEM((N_HEADS, tq, tq), BF16),
                        pltpu.VMEM((N_HEADS, tq, 2 * HEAD_DIM), F32),
                        pltpu.VMEM((N_HEADS, tq, LANES), F32),
                        pltpu.VMEM((N_HEADS, tq, LANES), F32)],
        compiler_params=pltpu.CompilerParams(dimension_semantics=("arbitrary", "arbitrary"),
                                             vmem_limit_bytes=VMEM_LIMIT),
        name="fox_prompt",
    )(qf, kf, vf, fcol, zf, og, x, w_out, fnw)


def _out_kernel(og_ref, of_ref, x_ref, w_ref, fnw_ref, y_ref):
    o = jnp.dot(og_ref[...].astype(BF16), w_ref[0:GROUP_W, :], preferred_element_type=F32)
    tm = x_ref.shape[0]
    for h in range(N_HEADS):
        of_h = of_ref[pl.ds(h, tm, stride=N_HEADS), :].astype(BF16)
        o = o + jnp.dot(of_h, w_ref[GROUP_W + h * HEAD_DIM:GROUP_W + (h + 1) * HEAD_DIM, :],
                        preferred_element_type=F32)
    y = x_ref[...] + o
    var = jnp.mean(y * y, axis=-1, keepdims=True)
    y_ref[...] = y * lax.rsqrt(var + NORM_EPS) * fnw_ref[...]


def _out_call(og, of, x2d, w_out, fnw, tm):
    t, d = x2d.shape
    return pl.pallas_call(
        _out_kernel,
        grid=(t // tm,),
        in_specs=[pl.BlockSpec((tm, GROUP_W), lambda i: (i, 0)),
                  pl.BlockSpec((tm * N_HEADS, HEAD_DIM), lambda i: (i, 0)),
                  pl.BlockSpec((tm, d), lambda i: (i, 0)),
                  pl.BlockSpec((2 * GROUP_W, d), lambda i: (0, 0)),
                  pl.BlockSpec((1, d), lambda i: (0, 0))],
        out_specs=pl.BlockSpec((tm, d), lambda i: (i, 0)),
        out_shape=jax.ShapeDtypeStruct((t, d), F32),
        compiler_params=pltpu.CompilerParams(dimension_semantics=("arbitrary",),
                                             vmem_limit_bytes=VMEM_LIMIT),
        name="out_proj",
    )(og, of, x2d, w_out, fnw)


def _page_copies(pt_ref, kc_ref, vc_ref, lc_ref, kbuf, vbuf, lbuf, sems, step, slot, n_pages, pg, rows):
    copies = []
    for r in range(rows):
        for p in range(n_pages):
            pid = pt_ref[step * rows + r, p]
            copies.append((pltpu.make_async_copy(kc_ref.at[pid], kbuf.at[slot, r, pl.ds(p * pg, pg)],
                                                 sems.at[0, slot]), 0))
            copies.append((pltpu.make_async_copy(vc_ref.at[pid], vbuf.at[slot, r, pl.ds(p * pg, pg)],
                                                 sems.at[1, slot]), 1))
            copies.append((pltpu.make_async_copy(lc_ref.at[pid], lbuf.at[slot, r, :, p, :], sems.at[2, slot]),
                           p % 2))
    return copies


def _split3(x):
    x1 = x.astype(BF16)
    r1 = x - x1.astype(F32)
    x2 = r1.astype(BF16)
    x3 = (r1 - x2.astype(F32)).astype(BF16)
    return x1, x2, x3


def _fox_decode_phases(pt_ref, q_ref, kn_ref, vn_ref, zf_ref, sm_ref, fbrow_ref, cums_ref,
                       kc_ref, vc_ref, lc_ref, o_ref, logf_ref, kbuf, vbuf, lbuf, sems,
                       *, n_pages, pg, l_new, rows):
    step = pl.program_id(0)
    n_steps = pl.num_programs(0)
    slot = step % 2
    copies = functools.partial(_page_copies, pt_ref, kc_ref, vc_ref, lc_ref, kbuf, vbuf, lbuf, sems,
                               n_pages=n_pages, pg=pg, rows=rows)
    nr = l_new * N_HEADS

    @pl.when(step == 0)
    def _():
        for cp, prio in copies(step=step, slot=slot):
            cp.start(priority=prio)

    @pl.when(step + 1 < n_steps)
    def _():
        for cp, prio in copies(step=step + 1, slot=1 - slot):
            cp.start(priority=prio)

    for cp, _ in copies(step=step, slot=slot):
        cp.wait()
    yield

    scale = HEAD_DIM ** -0.5
    earlier = (_iota2((n_pages, n_pages), 0) > _iota2((n_pages, n_pages), 1)).astype(F32)
    tok_valid = (_iota2((SUBLANES, 1), 0) < l_new).astype(F32)
    r_tok = _iota2((nr, SUBLANES), 0) // N_HEADS
    f_past, f_tot_row, csum, s_all, s_new = [], [], [], [], []
    for r in range(rows):
        qrows = slice(r * nr, (r + 1) * nr)
        res = jnp.zeros((3 * n_pages, 2 * pg), F32)
        for h in range(N_HEADS):
            res = res + jnp.dot(jnp.concatenate(_split3(lbuf[slot, r, h]), axis=0), cums_ref[h],
                                preferred_element_type=F32)
        res = res[0:n_pages] + res[n_pages:2 * n_pages] + res[2 * n_pages:3 * n_pages]
        within, tot = res[:, 0:pg], res[:, pg:2 * pg]
        carry = _fdot(earlier, tot)
        f_past.append(within + carry)
        f_tot_row.append(carry[n_pages - 1:n_pages, :] + tot[n_pages - 1:n_pages, :])
        sm_r = pltpu.roll(sm_ref[...], (SUBLANES - r * l_new) % SUBLANES, axis=0) if r else sm_ref[...]
        lf_col = _log_sigmoid(sm_r + fbrow_ref[...]) * tok_valid
        logf_ref[r] = lf_col
        csum.append(_fdot((_iota2((nr, SUBLANES), 1) <= r_tok).astype(F32), lf_col))
        q = q_ref[qrows, :].astype(BF16)
        s_all.append(lax.dot_general(q, kbuf[slot, r].astype(BF16), (((1,), (1,)), ((), ())),
                                     preferred_element_type=F32))
        s_new.append(lax.dot_general(q, kn_ref[qrows, :].astype(BF16), (((1,), (1,)), ((), ())),
                                     preferred_element_type=F32))
    yield
    own_lane = _iota2((nr, LANES), 1) == SM_FORGET + _iota2((nr, LANES), 0) % N_HEADS
    eye = _iota2((nr, nr), 0) == _iota2((nr, nr), 1)
    same_head = (_iota2((nr, pg), 1) % N_HEADS) == (_iota2((nr, pg), 0) % N_HEADS)
    rr, cc = _iota2((nr, nr), 0), _iota2((nr, nr), 1)
    new_ok = (rr % N_HEADS == cc % N_HEADS) & (cc // N_HEADS <= rr // N_HEADS)
    for r in range(rows):
        qrows = slice(r * nr, (r + 1) * nr)
        fq_new = jnp.sum(jnp.where(own_lane, csum[r], 0.0), axis=-1, keepdims=True)
        f_tot_col = jnp.sum(jnp.where(eye, jnp.broadcast_to(f_tot_row[r][:, 0:nr], (nr, nr)), 0.0),
                            axis=-1, keepdims=True)
        fq = fq_new + f_tot_col
        fq_row = jnp.sum(jnp.where(eye, jnp.broadcast_to(fq, (nr, nr)), 0.0), axis=0, keepdims=True)
        sp = [jnp.where(same_head, s_all[r][:, p * pg:(p + 1) * pg] * scale + (fq - f_past[r][p:p + 1, :]),
                        NEG_BIG) for p in range(n_pages)]
        sn = jnp.where(new_ok, s_new[r] * scale + (fq - fq_row), NEG_BIG)
        m_el = sp[0]
        for p in range(1, n_pages):
            m_el = jnp.maximum(m_el, sp[p])
        m = jnp.maximum(jnp.max(m_el, axis=-1, keepdims=True), jnp.max(sn, axis=-1, keepdims=True))
        pp = [jnp.exp(t - m) for t in sp]
        p_new = jnp.exp(sn - m)
        l_el = pp[0]
        for p in range(1, n_pages):
            l_el = l_el + pp[p]
        l = jnp.sum(l_el, axis=-1, keepdims=True) + jnp.sum(p_new, axis=-1, keepdims=True)
        p_all = jnp.concatenate([t.astype(BF16) for t in pp], axis=-1)
        acc = jnp.dot(p_all, vbuf[slot, r].astype(BF16), preferred_element_type=F32)
        acc = acc + jnp.dot(p_new.astype(BF16), vn_ref[qrows, :].astype(BF16), preferred_element_type=F32)
        z = zf_ref[qrows, :]
        o_ref[qrows, :] = ((acc / l) * (z * _sigmoid(z))).astype(o_ref.dtype)


def _head_cumsum_matrix(page):
    t = jnp.arange(page)[None, :, None]
    j = jnp.arange(page * N_HEADS)[None, None, :]
    h = jnp.arange(N_HEADS)[:, None, None]
    own = (j % N_HEADS) == h
    c = own & (t <= j // N_HEADS)
    b = jnp.broadcast_to(own, c.shape)
    return jnp.concatenate([c, b], axis=2).astype(BF16)


DECODE_ROWS_PER_STEP = 2
N_DECODE_INPUTS, N_DECODE_OUTPUTS, N_DECODE_SCRATCH = 10, 2, 4
N_GDN_INPUTS, N_GDN_OUTPUTS, N_GDN_SCRATCH = 8, 3, 2
N_SAMPLE_GDN_INPUTS, N_SAMPLE_GDN_OUTPUTS = 4, 3


def _decode_gdn_kernel(pt_ref, *refs, decode_kw, gdn_kw, sample_kw, n_chunks):
    take = lambda n: (refs[:n], refs[n:])
    dec_in, refs = take(N_DECODE_INPUTS)
    gdn_in, refs = take(N_GDN_INPUTS)
    smp_in, refs = take(N_SAMPLE_GDN_INPUTS)
    dec_out, refs = take(N_DECODE_OUTPUTS)
    gdn_out, refs = take(N_GDN_OUTPUTS)
    smp_out, refs = take(N_SAMPLE_GDN_OUTPUTS)
    dec_scr, refs = take(N_DECODE_SCRATCH)
    gdn_scr, smp_scr = take(N_GDN_SCRATCH)
    decode = _fox_decode_phases(pt_ref, *dec_in, *dec_out, *dec_scr, **decode_kw)
    next(decode)
    prompt = _gdn_stages(*gdn_in, *gdn_out, *gdn_scr, **gdn_kw,
                         chunk_of_step=(pl.program_id(0) % n_chunks, n_chunks))
    qkv_s, zg_s, s0_s, c0_s = smp_in
    sm_s = dec_in[4]
    sample = _gdn_stages(qkv_s, zg_s, sm_s, *gdn_in[4:8], s0_s, c0_s, *smp_out, *smp_scr, **sample_kw,
                         chunk_of_step=(jnp.int32(0), 1))
    programs = [prompt, sample, decode]
    while programs:
        for prog in list(programs):
            if next(prog, StopIteration) is StopIteration:
                programs.remove(prog)


def _decode_gdn_call(page_table, q4, kn4, vn4, zf4, sm_s, fb_row, kcache, vcache, lcache, l_new, gdn):
    b, n_pages = page_table.shape
    pg = kcache.shape[1]
    page = lcache.shape[2]
    nr = l_new * N_HEADS
    assert nr % (2 * SUBLANES) == 0 and l_new <= SUBLANES
    rps = DECODE_ROWS_PER_STEP
    assert b % rps == 0
    n_steps = b // rps
    decode_kw = dict(n_pages=n_pages, pg=pg, l_new=l_new, rows=rps)
    rows = pl.BlockSpec((rps * nr, HEAD_DIM), lambda i, pt: (i, 0))
    tok = pl.BlockSpec((rps, SUBLANES, LANES), lambda i, pt: (i, 0, 0))
    const = lambda shape: pl.BlockSpec(shape, lambda i, pt: (0,) * len(shape))
    any_spec = pl.BlockSpec(memory_space=pl.ANY)
    stok = lambda w: pl.BlockSpec((rps * l_new, w), lambda i, pt: (i, 0))
    operands = [page_table, q4, kn4, vn4, zf4, sm_s, fb_row, _head_cumsum_matrix(page), kcache, vcache, lcache]
    in_specs = [rows, rows, rows, rows, stok(LANES), const((1, LANES)), const((N_HEADS, page, 2 * pg)),
                any_spec, any_spec, any_spec]
    out_specs = [rows, tok]
    out_shape = [jax.ShapeDtypeStruct((b * nr, HEAD_DIM), F32), jax.ShapeDtypeStruct((b, SUBLANES, LANES), F32)]
    scratch = [pltpu.VMEM((2, rps, n_pages * pg, HEAD_DIM), F32),
               pltpu.VMEM((2, rps, n_pages * pg, HEAD_DIM), F32),
               pltpu.VMEM((2, rps, N_HEADS, n_pages, page), F32),
               pltpu.SemaphoreType.DMA((3, 2))]
    h3, w_qkv, zg, sm, conv_w, alog_row, dtb_row, onw, c, (qkv_s, zg_s, s0_s, c0_s) = gdn
    assert rps * l_new == SUBLANES
    bp, l, d = h3.shape
    n_c = l // c
    nb = bp * n_c // n_steps
    assert nb >= 1 and (bp // nb) * n_c == n_steps
    blk = lambda w: pl.BlockSpec((nb, c, w), lambda i, pt: (i // n_c, i % n_c, 0))
    state = pl.BlockSpec((nb, N_HEADS, HEAD_DIM, HEAD_DIM), lambda i, pt: (i // n_c, 0, 0, 0))
    rows8 = pl.BlockSpec((nb, SUBLANES, CONV_DIM), lambda i, pt: (i // n_c, 0, 0))
    sstate = pl.BlockSpec((rps, N_HEADS, HEAD_DIM, HEAD_DIM), lambda i, pt: (i, 0, 0, 0))
    sconv = pl.BlockSpec((rps, CONV_K - 1, CONV_DIM), lambda i, pt: (i, 0, 0))
    operands += [h3, w_qkv, zg, sm, conv_w, alog_row, dtb_row, onw, qkv_s, zg_s, s0_s, c0_s]
    in_specs += [blk(d), const((d, CONV_DIM)), blk(GROUP_W), blk(LANES), const((CONV_K, CONV_DIM)),
                 const((1, LANES)), const((1, LANES)), const((1, HEAD_DIM)),
                 stok(CONV_DIM), stok(GROUP_W), sstate, sconv]
    out_specs += [blk(GROUP_W), state, rows8, stok(GROUP_W), sstate, sconv]
    out_shape += [jax.ShapeDtypeStruct((bp, l, GROUP_W), BF16),
                  jax.ShapeDtypeStruct((bp, N_HEADS, HEAD_DIM, HEAD_DIM), F32),
                  jax.ShapeDtypeStruct((bp, SUBLANES, CONV_DIM), F32),
                  jax.ShapeDtypeStruct((b * l_new, GROUP_W), F32),
                  jax.ShapeDtypeStruct((b, N_HEADS, HEAD_DIM, HEAD_DIM), F32),
                  jax.ShapeDtypeStruct((b, CONV_K - 1, CONV_DIM), F32)]
    scratch += [pltpu.VMEM((nb, c + SUBLANES, CONV_DIM), F32),
                pltpu.VMEM((nb, N_HEADS, HEAD_DIM, HEAD_DIM), F32),
                pltpu.VMEM((rps, 2 * SUBLANES, CONV_DIM), F32),
                pltpu.VMEM((rps, N_HEADS, HEAD_DIM, HEAD_DIM), F32)]
    kern = functools.partial(_decode_gdn_kernel, decode_kw=decode_kw, n_chunks=n_c,
                             gdn_kw=dict(c=c, l_valid=l, nb=nb, project=True),
                             sample_kw=dict(c=SUBLANES, l_valid=l_new, nb=rps, project=False))
    grid_spec = pltpu.PrefetchScalarGridSpec(num_scalar_prefetch=1, grid=(n_steps,), in_specs=in_specs,
                                             out_specs=out_specs, scratch_shapes=scratch)
    return pl.pallas_call(
        kern,
        grid_spec=grid_spec,
        out_shape=out_shape,
        compiler_params=pltpu.CompilerParams(dimension_semantics=("arbitrary",),
                                             vmem_limit_bytes=VMEM_LIMIT),
        name="decode_gdn",
    )(*operands)


def _gate_row(vals, offset):
    return jnp.zeros((1, LANES), F32).at[0, offset:offset + N_HEADS].set(vals.astype(F32))


def kernel(x_prompt, x_sample, cache_fox_k, cache_fox_v, cache_fox_logf, page_table, state_gdn_ssm,
           state_gdn_conv, w_in, gdn_conv_w, gdn_a_log, gdn_dt_bias, gdn_out_norm_w, fox_f_bias, w_out,
           norm_w, final_norm_w):
    bp, lp, d = x_prompt.shape
    bs, ls, _ = x_sample.shape
    depth = w_in.shape[0]
    assert depth == 1, "single-layer trunk"
    n_pool, page = cache_fox_k.shape[1], cache_fox_k.shape[2]

    w_big, w_qkv = _pack_w_call(w_in[0].T)
    w_o = w_out[0].astype(BF16)
    nw = norm_w[0].reshape(1, d)
    fnw = final_norm_w.reshape(1, d)
    conv_w = gdn_conv_w[0]
    alog_row = _gate_row(gdn_a_log[0], SM_DECAY)
    dtb_row = _gate_row(gdn_dt_bias[0], SM_DECAY)
    fb_row = _gate_row(fox_f_bias[0], SM_FORGET)
    onw = gdn_out_norm_w[0].reshape(1, HEAD_DIM)

    xp2 = x_prompt.reshape(bp * lp, d)
    hp, zg, sm, fcol, logf_t, qf, kf, vf, zf, k4, v4 = _proj_call(xp2, nw, w_big, tm=ROWS_PER_STEP_PROMPT,
                                                                  sample=False, fb_row=fb_row, seq_len=lp)
    r3 = lambda t: t.reshape(bp, lp, t.shape[-1])

    xs2 = x_sample.reshape(bs * ls, d)
    qkv_s, zg_s, sm_s, q4_s, k4_s, v4_s, z4_s = _proj_call(xs2, nw, w_big, tm=ROWS_PER_STEP_SAMPLE, sample=True)
    kcache = cache_fox_k[0].reshape(n_pool, page * N_HEADS, HEAD_DIM)
    vcache = cache_fox_v[0].reshape(n_pool, page * N_HEADS, HEAD_DIM)
    lcache = cache_fox_logf[0].transpose(0, 2, 1)
    of_s, logf_s, og_p, ssm_p, tail, og_s, ssm_s, ctail_s = _decode_gdn_call(
        page_table, q4_s, k4_s, v4_s, z4_s, sm_s, fb_row, kcache, vcache, lcache, l_new=ls,
        gdn=(r3(hp), w_qkv, r3(zg), r3(sm), conv_w, alog_row, dtb_row, onw, GDN_CHUNK,
             (qkv_s, zg_s, state_gdn_ssm[0], state_gdn_conv[0])))
    y_s = _out_call(og_s, of_s, xs2, w_o, fnw, tm=ROWS_PER_STEP_SAMPLE)
    y_prompt = _fox_prompt_call(r3(qf), r3(kf), r3(vf), r3(fcol), r3(zf), og_p, x_prompt, w_o, fnw,
                                tq=ROWS_PER_STEP_PROMPT)

    k_prompt = k4.reshape(1, bp, lp, N_HEADS, HEAD_DIM)
    v_prompt = v4.reshape(1, bp, lp, N_HEADS, HEAD_DIM)
    logf_prompt = logf_t.transpose(0, 2, 1).reshape(1, bp, lp, N_HEADS)
    ssm_prompt = ssm_p.reshape(1, bp, N_HEADS, HEAD_DIM, HEAD_DIM)
    conv_prompt = tail[:, SUBLANES - (CONV_K - 1):, :].reshape(1, bp, CONV_K - 1, CONV_DIM)
    y_sample = y_s.reshape(bs, ls, d)
    k_sample = k4_s.reshape(1, bs, ls, N_HEADS, HEAD_DIM)
    v_sample = v4_s.reshape(1, bs, ls, N_HEADS, HEAD_DIM)
    logf_sample = logf_s[:, :ls, SM_FORGET:SM_FORGET + N_HEADS].reshape(1, bs, ls, N_HEADS)
    ssm_sample = ssm_s.reshape(1, bs, N_HEADS, HEAD_DIM, HEAD_DIM)
    conv_sample = ctail_s.reshape(1, bs, CONV_K - 1, CONV_DIM)

    return (y_prompt, y_sample, k_prompt, v_prompt, logf_prompt, ssm_prompt, conv_prompt,
            k_sample, v_sample, logf_sample, ssm_sample, conv_sample)
```
